```python
import math
import jax
import jax.numpy as jnp
from jax import lax
import numpy as np

D_MODEL = 1024
BATCH = 4
SEQ = 4096
DEPTH = 1

CTX_LEN = 256
GRID_W = 64
SSM_CH_PER_GROUP = 16
SSM_GROUPS = 32
SSM_WIDTH = SSM_GROUPS * SSM_CH_PER_GROUP
SSM_STATE = 64
SSM_DT_MIN = 1e-3
SSM_DT_MAX = 1e-1
HEAD_DIM = 64
N_Q_HEADS = 8
N_KV_HEADS = 2
Q_PER_KV = N_Q_HEADS // N_KV_HEADS
ATTN_WIDTH = N_Q_HEADS * HEAD_DIM
KV_WIDTH = N_KV_HEADS * HEAD_DIM
WINDOW = 128
ATTN_BLOCK = WINDOW
ROPE_BASE = 10000.0
ROPE_PAIRS_PER_AXIS = HEAD_DIM // 4
IN_COLS = SSM_WIDTH + ATTN_WIDTH + 2 * KV_WIDTH + 2 * D_MODEL
N_EXPERTS = 32
TOP_K = 4
D_EXPERT = D_MODEL
SWIGLU_LIMIT = 7.0
SWIGLU_ALPHA = 1.702
MOE_BLOCK = 128
LN_EPS = 1e-5
DEEPNORM_ALPHA = (2 * DEPTH) ** 0.25
DEEPNORM_BETA = (8 * DEPTH) ** -0.25
NEG_INF = -1e30

kernel_name = 'hybrid_s5_swa_moe_prefix_layer'


def layer_norm(x, g, b):
    xf = x.astype(jnp.float32)
    mu = xf.mean(-1, keepdims=True)
    var = jnp.square(xf - mu).mean(-1, keepdims=True)
    y = (xf - mu) * lax.rsqrt(var + LN_EPS) * g.astype(jnp.float32) + b.astype(jnp.float32)
    return y.astype(x.dtype)


def modulate(x, shift, scale):
    return x * (1 + scale) + shift


def split_proj(p):
    i0 = SSM_WIDTH
    i1 = i0 + ATTN_WIDTH
    i2 = i1 + KV_WIDTH
    i3 = i2 + KV_WIDTH
    i4 = i3 + D_MODEL
    return jnp.split(p, [i0, i1, i2, i3, i4], axis=-1)


def to_heads(t, n_heads):
    return t.reshape(t.shape[:-1] + (n_heads, HEAD_DIM))


def to_groups(t):
    return t.reshape(t.shape[:-1] + (SSM_GROUPS, SSM_CH_PER_GROUP))


def rope_2d(x, row_ids, col_ids):
    inv = ROPE_BASE ** (-jnp.arange(ROPE_PAIRS_PER_AXIS, dtype=jnp.float32) / ROPE_PAIRS_PER_AXIS)
    ang = jnp.concatenate([row_ids.astype(jnp.float32)[:, None] * inv,
                           col_ids.astype(jnp.float32)[:, None] * inv], axis=-1)
    cos = jnp.cos(ang)[None, :, None, :].astype(x.dtype)
    sin = jnp.sin(ang)[None, :, None, :].astype(x.dtype)
    x1, x2 = jnp.split(x, 2, axis=-1)
    return jnp.concatenate([x1 * cos - x2 * sin, x2 * cos + x1 * sin], axis=-1)


def cplx_combine(e1, e2):
    a1r, a1i, b1r, b1i = e1
    a2r, a2i, b2r, b2i = e2
    return (a2r * a1r - a2i * a1i,
            a2r * a1i + a2i * a1r,
            a2r * b1r - a2i * b1i + b2r,
            a2r * b1i + a2i * b1r + b2i)


def s5_discretize(lam_re, lam_im, log_step, b_re, b_im):
    f32 = jnp.float32
    lr, li = lam_re.astype(f32), lam_im.astype(f32)
    dt = jnp.exp(log_step.astype(f32))[:, None]
    mag = jnp.exp(lr * dt)
    ar = mag * jnp.cos(li * dt)
    ai = mag * jnp.sin(li * dt)
    den = lr * lr + li * li
    cr = ((ar - 1) * lr + ai * li) / den
    ci = (ai * lr - (ar - 1) * li) / den
    br, bi = b_re.astype(f32), b_im.astype(f32)
    bbr = cr[..., None] * br - ci[..., None] * bi
    bbi = cr[..., None] * bi + ci[..., None] * br
    return ar, ai, bbr, bbi


def s5_scan(u, disc, s0r, s0i, reverse):
    ar, ai, bbr, bbi = disc
    uf = u.astype(jnp.float32)
    xr = jnp.einsum('blgh,gph->blgp', uf, bbr)
    xi = jnp.einsum('blgh,gph->blgp', uf, bbi)
    edge = u.shape[1] - 1 if reverse else 0
    xr = xr.at[:, edge].add(ar * s0r - ai * s0i)
    xi = xi.at[:, edge].add(ar * s0i + ai * s0r)
    shape = (1, u.shape[1]) + ar.shape
    _, _, sr, si = lax.associative_scan(
        cplx_combine, (jnp.broadcast_to(ar, shape), jnp.broadcast_to(ai, shape), xr, xi),
        reverse=reverse, axis=1)
    return sr, si


def s5_readout(u, sf, sb, c_re, c_im, d_skip):
    cr = c_re.astype(jnp.float32)
    ci = c_im.astype(jnp.float32)
    y = (jnp.einsum('blgp,ghp->blgh', sf[0], cr[0]) - jnp.einsum('blgp,ghp->blgh', sf[1], ci[0])
         + jnp.einsum('blgp,ghp->blgh', sb[0], cr[1]) - jnp.einsum('blgp,ghp->blgh', sb[1], ci[1])
         + d_skip.astype(jnp.float32) * u.astype(jnp.float32))
    return y.reshape(u.shape[:2] + (SSM_WIDTH,)).astype(u.dtype)


def window_attention(q, k, v, kc, vc, sink):
    bsz, seq_len = q.shape[:2]
    n_ctx = kc.shape[1]
    nb = seq_len // ATTN_BLOCK
    qb = q.reshape(bsz, nb, ATTN_BLOCK, N_KV_HEADS, Q_PER_KV, HEAD_DIM) * HEAD_DIM ** -0.5
    pad = ((0, 0), (ATTN_BLOCK, ATTN_BLOCK), (0, 0), (0, 0))

    def band(t):
        tp = jnp.pad(t, pad).reshape(bsz, nb + 2, ATTN_BLOCK, N_KV_HEADS, HEAD_DIM)
        return jnp.concatenate([tp[:, :-2], tp[:, 1:-1], tp[:, 2:]], axis=2)

    kb, vb = band(k), band(v)
    s_loc = jnp.einsum('bnqhgd,bnkhd->bhgnqk', qb, kb).astype(jnp.float32)
    s_ctx = jnp.einsum('bnqhgd,bchd->bhgnqc', qb, kc).astype(jnp.float32)
    blk = jnp.arange(nb)[:, None, None]
    qpos = blk * ATTN_BLOCK + jnp.arange(ATTN_BLOCK)[None, :, None]
    kpos = (blk - 1) * ATTN_BLOCK + jnp.arange(3 * ATTN_BLOCK)[None, None, :]
    valid = (jnp.abs(kpos - qpos) <= WINDOW) & (kpos >= 0) & (kpos < seq_len)
    s_loc = jnp.where(valid, s_loc, NEG_INF)
    snk = sink.astype(jnp.float32).reshape(N_KV_HEADS, Q_PER_KV)[None, :, :, None, None, None]
    snk = jnp.broadcast_to(snk, s_loc.shape[:-1] + (1,))
    p = jax.nn.softmax(jnp.concatenate([s_loc, s_ctx, snk], axis=-1), axis=-1).astype(v.dtype)
    p_loc = p[..., :3 * ATTN_BLOCK]
    p_ctx = p[..., 3 * ATTN_BLOCK:3 * ATTN_BLOCK + n_ctx]
    o = (jnp.einsum('bhgnqk,bnkhd->bnqhgd', p_loc, vb)
         + jnp.einsum('bhgnqc,bchd->bnqhgd', p_ctx, vc))
    return o.reshape(bsz, seq_len, ATTN_WIDTH)


def context_attention(q, k, v, sink):
    bsz, n_ctx = q.shape[:2]
    qg = q.reshape(bsz, n_ctx, N_KV_HEADS, Q_PER_KV, HEAD_DIM) * HEAD_DIM ** -0.5
    s = jnp.einsum('bqhgd,bkhd->bhgqk', qg, k).astype(jnp.float32)
    snk = sink.astype(jnp.float32).reshape(N_KV_HEADS, Q_PER_KV)[None, :, :, None, None]
    snk = jnp.broadcast_to(snk, s.shape[:-1] + (1,))
    p = jax.nn.softmax(jnp.concatenate([s, snk], axis=-1), axis=-1)[..., :n_ctx].astype(v.dtype)
    o = jnp.einsum('bhgqk,bkhd->bqhgd', p, v)
    return o.reshape(bsz, n_ctx, ATTN_WIDTH)


def merge_branches(y_ssm, o_att, g_ssm, g_att, w_glu, b_glu, w_ssm_out, w_att_out, w_o):
    z = jax.nn.gelu(y_ssm)
    z = z * jax.nn.sigmoid(z @ w_glu + b_glu)
    m = jax.nn.sigmoid(g_ssm) * (z @ w_ssm_out) + jax.nn.sigmoid(g_att) * (o_att @ w_att_out)
    return m @ w_o


def moe_ffn(h, w_router, b_router, w_gate_up, b_gate_up, w_down, b_down):
    bsz, n_tok, d = h.shape
    t = bsz * n_tok
    xf = h.reshape(t, d)
    logits = (xf @ w_router + b_router).astype(jnp.float32)
    top_v, top_i = lax.top_k(logits, TOP_K)
    gates = jax.nn.softmax(top_v, axis=-1).astype(h.dtype)
    n_rows = t * TOP_K
    e_flat = top_i.reshape(n_rows)
    tok_flat = jnp.arange(n_rows, dtype=jnp.int32) // TOP_K
    order = jnp.argsort(e_flat)
    e_sorted = e_flat[order]
    counts = jnp.bincount(e_flat, length=N_EXPERTS)
    padded = (counts + MOE_BLOCK - 1) // MOE_BLOCK * MOE_BLOCK
    start = jnp.cumsum(counts) - counts
    pad_end = jnp.cumsum(padded)
    pad_start = pad_end - padded
    dest = pad_start[e_sorted] + jnp.arange(n_rows) - start[e_sorted]
    cap = n_rows + N_EXPERTS * MOE_BLOCK
    n_blocks = cap // MOE_BLOCK
    row_tok = jnp.full((cap,), t, jnp.int32).at[dest].set(tok_flat[order])
    row_gate = jnp.zeros((cap,), h.dtype).at[dest].set(gates.reshape(n_rows)[order])
    block_exp = jnp.minimum(
        jnp.searchsorted(pad_end, jnp.arange(n_blocks) * MOE_BLOCK, side='right'), N_EXPERTS - 1)
    x_rows = jnp.concatenate([xf, jnp.zeros((1, d), h.dtype)], axis=0)[row_tok]
    x_rows = x_rows.reshape(n_blocks, MOE_BLOCK, d)

    def expert_block(args):
        xb, e = args
        gu = xb @ w_gate_up[e] + b_gate_up[e]
        glu, lin = jnp.split(gu, 2, axis=-1)
        glu = jnp.minimum(glu, SWIGLU_LIMIT)
        lin = jnp.clip(lin, -SWIGLU_LIMIT, SWIGLU_LIMIT)
        act = glu * jax.nn.sigmoid(SWIGLU_ALPHA * glu) * (lin + 1)
        return act @ w_down[e] + b_down[e]

    y_rows = lax.map(expert_block, (x_rows, block_exp)).reshape(cap, d)
    y = jnp.zeros((t + 1, d), h.dtype).at[row_tok].add(y_rows * row_gate[:, None])
    return y[:t].reshape(bsz, n_tok, d)


def setup_inputs(seed: int = 0) -> dict:
    key = jax.random.key(seed)
    ks = jax.random.split(key, 36)
    f32 = jnp.float32

    def nrm(i, shape, scale):
        return scale * jax.random.normal(ks[i], shape, f32)

    G, P, H, E, F = SSM_GROUPS, SSM_STATE, SSM_CH_PER_GROUP, N_EXPERTS, D_EXPERT
    n_idx = jnp.arange(P, dtype=f32)
    return {
        'x': nrm(0, (BATCH, SEQ, D_MODEL), 1.0),
        'c': nrm(1, (BATCH, D_MODEL), 1.0),
        'ctx': nrm(2, (BATCH, CTX_LEN, D_MODEL), 1.0),
        'c_ctx': nrm(3, (D_MODEL,), 1.0),
        'ln_in_g': 1.0 + nrm(4, (D_MODEL,), 0.02),
        'ln_in_b': nrm(5, (D_MODEL,), 0.02),
        'w_mod': nrm(6, (DEPTH, D_MODEL, 6 * D_MODEL), 0.5 * D_MODEL ** -0.5),
        'b_mod': nrm(7, (DEPTH, 6 * D_MODEL), 0.02),
        'w_in': nrm(8, (DEPTH, D_MODEL, IN_COLS), D_MODEL ** -0.5),
        'ssm_lam_re': -0.5 + nrm(9, (DEPTH, 2, G, P), 0.01),
        'ssm_lam_im': math.pi * n_idx + nrm(10, (DEPTH, 2, G, P), 0.01),
        'ssm_log_step': jax.random.uniform(ks[11], (DEPTH, 2, G), f32,
                                           minval=math.log(SSM_DT_MIN), maxval=math.log(SSM_DT_MAX)),
        'ssm_b_re': nrm(12, (DEPTH, 2, G, P, H), (2 * H) ** -0.5),
        'ssm_b_im': nrm(13, (DEPTH, 2, G, P, H), (2 * H) ** -0.5),
        'ssm_c_re': nrm(14, (DEPTH, 2, G, H, P), (2 * P) ** -0.5),
        'ssm_c_im': nrm(15, (DEPTH, 2, G, H, P), (2 * P) ** -0.5),
        'ssm_d': nrm(16, (DEPTH, G, H), 1.0),
        'w_glu': nrm(17, (DEPTH, SSM_WIDTH, SSM_WIDTH), SSM_WIDTH ** -0.5),
        'b_glu': nrm(18, (DEPTH, SSM_WIDTH), 0.02),
        'attn_sink': nrm(19, (DEPTH, N_Q_HEADS), 0.5),
        'w_ssm_out': nrm(20, (DEPTH, SSM_WIDTH, D_MODEL), SSM_WIDTH ** -0.5),
        'w_att_out': nrm(21, (DEPTH, ATTN_WIDTH, D_MODEL), ATTN_WIDTH ** -0.5),
        'w_o': nrm(22, (DEPTH, D_MODEL, D_MODEL), DEEPNORM_BETA * D_MODEL ** -0.5),
        'ln1_g': 1.0 + nrm(23, (DEPTH, D_MODEL), 0.02),
        'ln1_b': nrm(24, (DEPTH, D_MODEL), 0.02),
        'w_router': nrm(25, (DEPTH, D_MODEL, E), D_MODEL ** -0.5),
        'b_router': nrm(26, (DEPTH, E), 0.01),
        'w_gate_up': nrm(27, (DEPTH, E, D_MODEL, 2 * F), D_MODEL ** -0.5),
        'b_gate_up': nrm(28, (DEPTH, E, 2 * F), 0.01),
        'w_down': nrm(29, (DEPTH, E, F, D_MODEL), DEEPNORM_BETA * F ** -0.5),
        'b_down': nrm(30, (DEPTH, E, D_MODEL), 0.01),
        'ln2_g': 1.0 + nrm(31, (DEPTH, D_MODEL), 0.02),
        'ln2_b': nrm(32, (DEPTH, D_MODEL), 0.02),
    }


def reference(x, c, ctx, c_ctx, ln_in_g, ln_in_b, w_mod, b_mod, w_in,
              ssm_lam_re, ssm_lam_im, ssm_log_step, ssm_b_re, ssm_b_im, ssm_c_re, ssm_c_im, ssm_d,
              w_glu, b_glu, attn_sink, w_ssm_out, w_att_out, w_o, ln1_g, ln1_b,
              w_router, b_router, w_gate_up, b_gate_up, w_down, b_down, ln2_g, ln2_b):
    bsz, seq_len, _ = x.shape
    rows = seq_len // GRID_W
    row_ids = jnp.repeat(jnp.arange(rows), GRID_W)
    col_ids = jnp.tile(jnp.arange(GRID_W), rows)
    zero_state = jnp.zeros((bsz, SSM_GROUPS, SSM_STATE), jnp.float32)
    h = layer_norm(x, ln_in_g, ln_in_b)
    hc = layer_norm(ctx, ln_in_g, ln_in_b)
    for l in range(DEPTH):
        mod = jax.nn.silu(c) @ w_mod[l] + b_mod[l]
        mod_c = jax.nn.silu(c_ctx) @ w_mod[l] + b_mod[l]
        sh1, sc1, g1, sh2, sc2, g2 = [m[:, None, :] for m in jnp.split(mod, 6, axis=-1)]
        sh1c, sc1c, g1c, sh2c, sc2c, g2c = jnp.split(mod_c, 6, axis=-1)
        disc_f = s5_discretize(ssm_lam_re[l, 0], ssm_lam_im[l, 0], ssm_log_step[l, 0],
                               ssm_b_re[l, 0], ssm_b_im[l, 0])
        disc_b = s5_discretize(ssm_lam_re[l, 1], ssm_lam_im[l, 1], ssm_log_step[l, 1],
                               ssm_b_re[l, 1], ssm_b_im[l, 1])

        uc = modulate(hc, sh1c, sc1c)
        s_in_c, q_c, k_c, v_c, gs_c, ga_c = split_proj(uc @ w_in[l])
        k_c = to_heads(k_c, N_KV_HEADS)
        v_c = to_heads(v_c, N_KV_HEADS)
        ug_c = to_groups(s_in_c)
        cf = s5_scan(ug_c, disc_f, zero_state, zero_state, reverse=False)
        cb = s5_scan(ug_c, disc_b, zero_state, zero_state, reverse=True)

        u = modulate(h, sh1, sc1)
        s_in, q, k, v, gs, ga = split_proj(u @ w_in[l])
        q = rope_2d(to_heads(q, N_Q_HEADS), row_ids, col_ids)
        k = rope_2d(to_heads(k, N_KV_HEADS), row_ids, col_ids)
        o_att = window_attention(q, k, to_heads(v, N_KV_HEADS), k_c, v_c, attn_sink[l])
        ug = to_groups(s_in)
        lf = s5_scan(ug, disc_f, cf[0][:, -1], cf[1][:, -1], reverse=False)
        lb = s5_scan(ug, disc_b, cb[0][:, 0], cb[1][:, 0], reverse=True)
        y_ssm = s5_readout(ug, lf, lb, ssm_c_re[l], ssm_c_im[l], ssm_d[l])
        mix = merge_branches(y_ssm, o_att, gs, ga, w_glu[l], b_glu[l], w_ssm_out[l], w_att_out[l], w_o[l])
        h_new = layer_norm(DEEPNORM_ALPHA * h + g1 * mix, ln1_g[l], ln1_b[l])
        ffn = moe_ffn(modulate(h_new, sh2, sc2), w_router[l], b_router[l],
                      w_gate_up[l], b_gate_up[l], w_down[l], b_down[l])
        h_new = layer_norm(DEEPNORM_ALPHA * h_new + g2 * ffn, ln2_g[l], ln2_b[l])

        if l + 1 < DEPTH:
            o_att_c = context_attention(to_heads(q_c, N_Q_HEADS), k_c, v_c, attn_sink[l])
            y_ssm_c = s5_readout(ug_c, cf, cb, ssm_c_re[l], ssm_c_im[l], ssm_d[l])
            mix_c = merge_branches(y_ssm_c, o_att_c, gs_c, ga_c, w_glu[l], b_glu[l],
                                   w_ssm_out[l], w_att_out[l], w_o[l])
            hc = layer_norm(DEEPNORM_ALPHA * hc + g1c * mix_c, ln1_g[l], ln1_b[l])
            ffn_c = moe_ffn(modulate(hc, sh2c, sc2c), w_router[l], b_router[l],
                            w_gate_up[l], b_gate_up[l], w_down[l], b_down[l])
            hc = layer_norm(DEEPNORM_ALPHA * hc + g2c * ffn_c, ln2_g[l], ln2_b[l])
        h = h_new
    return h
```

```python
import functools
import math

import jax
import jax.numpy as jnp
import numpy as np
from jax import lax
from jax.experimental import pallas as pl
from jax.experimental.pallas import tpu as pltpu

F32 = jnp.float32
BF16 = jnp.bfloat16
HIGHEST = lax.Precision.HIGHEST

HEAD_DIM = 64
N_Q_HEADS = 8
N_KV_HEADS = 2
Q_PER_KV = N_Q_HEADS // N_KV_HEADS
WINDOW = 128
GRID_W = 64
ROPE_BASE = 10000.0
ROPE_PAIRS = HEAD_DIM // 4
TOP_K = 4
SWIGLU_LIMIT = 7.0
SWIGLU_ALPHA = 1.702
LN_EPS = 1e-5
NEG_INF = -1e30

LANES = 128
SUBLANES = 8
VMEM_LIMIT = 56 * 1024 * 1024

S5_CHUNK = 8


def _cparams(sem):
    return pltpu.CompilerParams(dimension_semantics=sem, vmem_limit_bytes=VMEM_LIMIT)


def _sigmoid(x):
    return 1.0 / (1.0 + jnp.exp(-x))


def _layer_norm(x, g, b):
    mu = jnp.mean(x, axis=-1, keepdims=True)
    xc = x - mu
    var = jnp.mean(xc * xc, axis=-1, keepdims=True)
    return xc * lax.rsqrt(var + LN_EPS) * g + b


def _dot(a, b):
    return jnp.dot(a, b, preferred_element_type=F32)


def _mod_kernel(c_ref, w_ref, b_ref, o_ref):
    c = c_ref[...]
    a = c * _sigmoid(c)
    o_ref[...] = jnp.dot(a, w_ref[...], preferred_element_type=F32, precision=HIGHEST) + b_ref[...]


def _mod_vectors(c_rows, w_mod, b_mod):
    d = c_rows.shape[1]
    n = w_mod.shape[1]
    return pl.pallas_call(
        _mod_kernel,
        grid=(n // d,),
        in_specs=[
            pl.BlockSpec((SUBLANES, d), lambda i: (0, 0)),
            pl.BlockSpec((d, d), lambda i: (0, i)),
            pl.BlockSpec((1, d), lambda i: (0, i)),
        ],
        out_specs=pl.BlockSpec((SUBLANES, d), lambda i: (0, i)),
        out_shape=jax.ShapeDtypeStruct((SUBLANES, n), F32),
        compiler_params=_cparams(("arbitrary",)),
        name="mod",
    )(c_rows, w_mod, b_mod)


def _rope(t, cos, sin):
    n = t.shape[1]
    reps = n // LANES
    c = jnp.concatenate([cos] * reps, axis=1) if reps > 1 else cos
    s = jnp.concatenate([sin] * reps, axis=1) if reps > 1 else sin
    half = HEAD_DIM // 2
    upper = pltpu.roll(t, n - half, axis=1)
    lower = pltpu.roll(t, half, axis=1)
    lane = lax.broadcasted_iota(jnp.int32, t.shape, 1)
    partner = jnp.where((lane & half) == 0, upper, lower)
    return t * c + partner * s


def _inproj_kernel(*refs, latent, ssm_w, attn_w, kv2_w, d_model):
    if latent:
        (x_ref, g_ref, b_ref, sh_ref, sc_ref, w_ref, cos_ref, sin_ref,
         s_ref, q_ref, k_ref, v_ref, gs_ref, ga_ref) = refs
    else:
        x_ref, g_ref, b_ref, sh_ref, sc_ref, w_ref, s_ref, k_ref, v_ref = refs
    h = _layer_norm(x_ref[...], g_ref[...], b_ref[...])
    u = (h * (1.0 + sc_ref[...]) + sh_ref[...]).astype(BF16)
    col = 0
    s = _dot(u, w_ref[:, col:col + ssm_w])
    for lb in range(ssm_w // LANES):
        s_ref[lb] = s[:, lb * LANES:(lb + 1) * LANES]
    col += ssm_w
    if latent:
        q = _dot(u, w_ref[:, col:col + attn_w])
        q = _rope(q, cos_ref[...], sin_ref[...]) * (HEAD_DIM ** -0.5)
        q_ref[...] = q.astype(BF16)
        col += attn_w
    k = _dot(u, w_ref[:, col:col + kv2_w])
    if latent:
        k = _rope(k, cos_ref[...], sin_ref[...])
    k_ref[...] = k.astype(BF16)
    col += kv2_w
    v_ref[...] = _dot(u, w_ref[:, col:col + kv2_w]).astype(BF16)
    col += kv2_w
    if latent:
        gs_ref[...] = _sigmoid(_dot(u, w_ref[:, col:col + d_model])).astype(BF16)
        col += d_model
        ga_ref[...] = _sigmoid(_dot(u, w_ref[:, col:col + d_model])).astype(BF16)


def _inproj(x2, ln_g, ln_b, mod3, w_cat, cos_t, sin_t, *, latent, rows_per_batch, tm, ctx_mod_row,
            ssm_w, attn_w, kv2_w):
    t, d = x2.shape
    tiles_per_batch = rows_per_batch // tm
    nlb = ssm_w // LANES

    def brow(i):
        return i // tiles_per_batch if ctx_mod_row is None else ctx_mod_row

    in_specs = [
        pl.BlockSpec((tm, d), lambda i: (i, 0)),
        pl.BlockSpec((1, d), lambda i: (0, 0)),
        pl.BlockSpec((1, d), lambda i: (0, 0)),
        pl.BlockSpec((None, 1, d), lambda i: (brow(i), 0, 0)),
        pl.BlockSpec((None, 1, d), lambda i: (brow(i), 0, 1)),
        pl.BlockSpec(w_cat.shape, lambda i: (0, 0)),
    ]
    args = [x2, ln_g, ln_b, mod3, mod3, w_cat]
    out_specs = [pl.BlockSpec((nlb, tm, LANES), lambda i: (0, i, 0))]
    out_shape = [jax.ShapeDtypeStruct((nlb, t, LANES), F32)]
    if latent:
        in_specs += [
            pl.BlockSpec((tm, LANES), lambda i: (i % tiles_per_batch, 0)),
            pl.BlockSpec((tm, LANES), lambda i: (i % tiles_per_batch, 0)),
        ]
        args += [cos_t, sin_t]
        out_specs.append(pl.BlockSpec((tm, attn_w), lambda i: (i, 0)))
        out_shape.append(jax.ShapeDtypeStruct((t, attn_w), BF16))
    out_specs += [pl.BlockSpec((tm, kv2_w), lambda i: (i, 0))] * 2
    out_shape += [jax.ShapeDtypeStruct((t, kv2_w), BF16)] * 2
    if latent:
        out_specs += [pl.BlockSpec((tm, d), lambda i: (i, 0))] * 2
        out_shape += [jax.ShapeDtypeStruct((t, d), BF16)] * 2
    return pl.pallas_call(
        functools.partial(_inproj_kernel, latent=latent, ssm_w=ssm_w, attn_w=attn_w, kv2_w=kv2_w,
                          d_model=d),
        grid=(t // tm,),
        in_specs=in_specs,
        out_specs=out_specs,
        out_shape=out_shape,
        compiler_params=_cparams(("arbitrary",)),
        name="inproj_latent" if latent else "inproj_ctx",
    )(*args)


def _s5_weights(lam_re, lam_im, log_step, b_re, b_im, c_re, c_im, reverse):
    ch = S5_CHUNK
    g, p = lam_re.shape
    hh = b_re.shape[-1]
    gpb = LANES // hh
    nlb = g // gpb
    lr, li = lam_re.astype(F32), lam_im.astype(F32)
    dt = jnp.exp(log_step.astype(F32))[:, None]
    jj = jnp.arange(ch + 1, dtype=F32)[:, None, None]
    mag = jnp.exp(jj * lr * dt)
    pr = mag * jnp.cos(jj * li * dt)
    pi = mag * jnp.sin(jj * li * dt)
    ar, ai = pr[1], pi[1]
    den = lr * lr + li * li
    cr = ((ar - 1) * lr + ai * li) / den
    ci = (ai * lr - (ar - 1) * li) / den
    br, bi = b_re.astype(F32), b_im.astype(F32)
    bbr = cr[..., None] * br - ci[..., None] * bi
    bbi = cr[..., None] * bi + ci[..., None] * br
    ccr, cci = c_re.astype(F32), c_im.astype(F32)
    er = ccr[None] * pr[:, :, None, :] - cci[None] * pi[:, :, None, :]
    ei = ccr[None] * pi[:, :, None, :] + cci[None] * pr[:, :, None, :]
    bbr_t = jnp.swapaxes(bbr, 1, 2)
    bbi_t = jnp.swapaxes(bbi, 1, 2)
    m = jnp.sum(er[:ch, :, :, None, :] * bbr_t[None, :, None, :, :]
                - ei[:ch, :, :, None, :] * bbi_t[None, :, None, :, :], axis=-1)
    m = jnp.concatenate([m, jnp.zeros_like(m[:1])], axis=0)
    kk = np.arange(ch)[:, None]
    ii = np.arange(ch)[None, :]
    lag = (kk - ii) if reverse else (ii - kk)
    lag = np.where(lag >= 0, lag, ch)
    kin = m[lag]
    kin = kin.reshape(ch, ch, nlb, gpb, hh, hh)
    eye = jnp.eye(gpb, dtype=F32)
    kin = jnp.transpose(kin, (2, 0, 3, 5, 1, 4))
    kin = kin[:, :, :, :, :, None, :] * eye[None, None, :, None, None, :, None]
    kin = kin.reshape(nlb, ch * LANES, ch * LANES)
    rk = np.arange(ch) if reverse else (ch - 1 - np.arange(ch))
    apr, api = pr[rk], pi[rk]
    sr = apr[..., None] * bbr[None] - api[..., None] * bbi[None]
    si = apr[..., None] * bbi[None] + api[..., None] * bbr[None]
    ws = jnp.stack([sr, si], axis=0)
    ws = ws.reshape(2, ch, nlb, gpb, p, hh)
    ws = jnp.transpose(ws, (2, 1, 3, 5, 0, 4))
    ws = ws[:, :, :, :, :, None, :] * eye[None, None, :, None, None, :, None]
    ws = ws.reshape(nlb, ch * LANES, 2 * gpb * p)
    ex = (ch - np.arange(ch)) if reverse else (np.arange(ch) + 1)
    wo = jnp.stack([er[ex], -ei[ex]], axis=0)
    wo = wo.reshape(2, ch, nlb, gpb, hh, p)
    wo = jnp.transpose(wo, (2, 0, 3, 5, 1, 4))
    wo = wo[:, :, :, :, :, None, :] * eye[None, None, :, None, None, :, None]
    wo = wo.reshape(nlb, 2 * gpb * p, ch * LANES)
    a_chunk = jnp.stack([pr[ch].reshape(nlb, gpb * p), pi[ch].reshape(nlb, gpb * p)], axis=1)
    return kin.astype(BF16), ws.astype(BF16), wo.astype(BF16), a_chunk


def _s5_kernel(u_ref, s0_ref, a_ref, kin_ref, ws_ref, wo_ref, d_ref, y_ref, sfin_ref,
               z_scr, sin_scr, carry_scr, *, reverse, nb, cc, add_skip):
    j = pl.program_id(1)

    @pl.when(j == 0)
    def _():
        carry_scr[...] = s0_ref[...]

    width = u_ref.shape[-1]
    sw = a_ref.shape[-1]
    ns = sw // LANES
    u = u_ref[...].reshape(nb * cc, width)
    ub = u.astype(BF16)
    z = _dot(ub, ws_ref[...])
    for s in range(2 * ns):
        z_scr[s] = z[:, s * LANES:(s + 1) * LANES]
    ar = [a_ref[0:1, s * LANES:(s + 1) * LANES] for s in range(ns)]
    ai = [a_ref[1:2, s * LANES:(s + 1) * LANES] for s in range(ns)]

    def body(c, st):
        cidx = (cc - 1 - c) if reverse else c
        rows = pl.ds(cidx, nb, stride=cc)
        new = [None] * (2 * ns)
        for s in range(ns):
            sr, si = st[s], st[ns + s]
            sin_scr[s, rows, :] = sr
            sin_scr[ns + s, rows, :] = si
            new[s] = ar[s] * sr - ai[s] * si + z_scr[s, rows, :]
            new[ns + s] = ar[s] * si + ai[s] * sr + z_scr[ns + s, rows, :]
        return tuple(new)

    st0 = tuple(carry_scr[:, s * LANES:(s + 1) * LANES] for s in range(2 * ns))
    st = lax.fori_loop(0, cc, body, st0)
    s_fin = jnp.concatenate(st, axis=1)
    carry_scr[...] = s_fin
    sfin_ref[...] = s_fin
    s_in = jnp.concatenate([sin_scr[s] for s in range(2 * ns)], axis=1)
    y = _dot(ub, kin_ref[...]) + _dot(s_in.astype(BF16), wo_ref[...])
    if add_skip:
        y = y + u * d_ref[...]
    y_ref[...] = y.reshape(nb, cc, width)


def _s5_scan(u4, s0, weights, d_tile, *, reverse, cc, add_skip):
    kin, ws, wo, a_chunk = weights
    nlb, nb, nch, width = u4.shape
    sw2 = ws.shape[-1]
    nj = nch // cc

    def jm(j):
        return (nj - 1 - j) if reverse else j

    return pl.pallas_call(
        functools.partial(_s5_kernel, reverse=reverse, nb=nb, cc=cc, add_skip=add_skip),
        grid=(nlb, nj),
        in_specs=[
            pl.BlockSpec((None, nb, cc, width), lambda l, j: (l, 0, jm(j), 0)),
            pl.BlockSpec((None, nb, sw2), lambda l, j: (l, 0, 0)),
            pl.BlockSpec((None, 2, sw2 // 2), lambda l, j: (l, 0, 0)),
            pl.BlockSpec((None, width, width), lambda l, j: (l, 0, 0)),
            pl.BlockSpec((None, width, sw2), lambda l, j: (l, 0, 0)),
            pl.BlockSpec((None, sw2, width), lambda l, j: (l, 0, 0)),
            pl.BlockSpec((None, 1, width), lambda l, j: (l, 0, 0)),
        ],
        out_specs=[
            pl.BlockSpec((None, nb, cc, width), lambda l, j: (l, 0, jm(j), 0)),
            pl.BlockSpec((None, nb, sw2), lambda l, j: (l, 0, 0)),
        ],
        out_shape=[
            jax.ShapeDtypeStruct(u4.shape, F32),
            jax.ShapeDtypeStruct((nlb, nb, sw2), F32),
        ],
        scratch_shapes=[
            pltpu.VMEM((sw2 // LANES, nb * cc, LANES), F32),
            pltpu.VMEM((sw2 // LANES, nb * cc, LANES), F32),
            pltpu.VMEM((nb, sw2), F32),
        ],
        compiler_params=_cparams(("arbitrary", "arbitrary")),
        name="s5_bwd" if reverse else "s5_fwd",
    )(u4, s0, a_chunk, kin, ws, wo, d_tile)


def _attn_kernel(sink_ref, q_ref, kp_ref, ko_ref, kn_ref, vp_ref, vo_ref, vn_ref, kc_ref, vc_ref,
                 o_ref, *, nblk, blk):
    n = pl.program_id(1)
    n_ctx = kc_ref.shape[0]
    n_loc = 3 * blk
    qi = lax.broadcasted_iota(jnp.int32, (blk, n_loc + n_ctx), 0)
    kj = lax.broadcasted_iota(jnp.int32, (blk, n_loc + n_ctx), 1)
    rel = kj - qi
    lo = jnp.where(n == 0, blk, 0)
    hi = jnp.where(n == nblk - 1, 2 * blk, n_loc)
    valid = ((rel >= blk - WINDOW) & (rel <= blk + WINDOW) & (kj >= lo) & (kj < hi)) | (kj >= n_loc)
    lane = lax.broadcasted_iota(jnp.int32, (blk, LANES), 1)
    low_half = lane < HEAD_DIM
    outs = []
    for hk in range(N_KV_HEADS):
        cs = slice(hk * LANES, (hk + 1) * LANES)
        kcat = jnp.concatenate([kp_ref[:, cs], ko_ref[:, cs], kn_ref[:, cs], kc_ref[:, cs]], axis=0)
        vcat = jnp.concatenate([vp_ref[:, cs], vo_ref[:, cs], vn_ref[:, cs], vc_ref[:, cs]], axis=0)
        for hq in range(Q_PER_KV):
            h = hk * Q_PER_KV + hq
            qb = q_ref[:, (h // 2) * LANES:(h // 2 + 1) * LANES]
            qm = jnp.where(low_half if h % 2 == 0 else jnp.logical_not(low_half), qb, jnp.zeros_like(qb))
            s = lax.dot_general(qm, kcat, (((1,), (1,)), ((), ())), preferred_element_type=F32)
            s = jnp.where(valid, s, NEG_INF)
            snk = sink_ref[h]
            mx = jnp.maximum(jnp.max(s, axis=1, keepdims=True), snk)
            p = jnp.exp(s - mx)
            den = jnp.sum(p, axis=1, keepdims=True) + jnp.exp(snk - mx)
            o = _dot(p.astype(BF16), vcat) / den
            outs.append(o)
    blocks = [jnp.where(low_half, outs[2 * m], outs[2 * m + 1]) for m in range(N_Q_HEADS // 2)]
    o_ref[...] = jnp.concatenate(blocks, axis=1).astype(BF16)


def _attention(q, kd, vd, kcd, vcd, sink, *, bsz, seq_len, n_ctx):
    blk = WINDOW
    nblk = seq_len // blk
    aw = q.shape[1]
    kw = kd.shape[1]

    def qmap(b, n):
        return (b * nblk + n, 0)

    def pmap(b, n):
        return (b * nblk + jnp.maximum(n - 1, 0), 0)

    def nmap(b, n):
        return (b * nblk + jnp.minimum(n + 1, nblk - 1), 0)

    kv_spec = lambda f: pl.BlockSpec((blk, kw), f)
    return pl.pallas_call(
        functools.partial(_attn_kernel, nblk=nblk, blk=blk),
        grid=(bsz, nblk),
        in_specs=[
            pl.BlockSpec(memory_space=pltpu.SMEM),
            pl.BlockSpec((blk, aw), qmap),
            kv_spec(pmap), kv_spec(qmap), kv_spec(nmap),
            kv_spec(pmap), kv_spec(qmap), kv_spec(nmap),
            pl.BlockSpec((n_ctx, kw), lambda b, n: (b, 0)),
            pl.BlockSpec((n_ctx, kw), lambda b, n: (b, 0)),
        ],
        out_specs=pl.BlockSpec((blk, aw), qmap),
        out_shape=jax.ShapeDtypeStruct(q.shape, BF16),
        compiler_params=_cparams(("arbitrary", "arbitrary")),
        name="attn",
    )(sink, q, kd, kd, kd, vd, vd, vd, kcd, vcd)


def _gelu_tanh(x):
    return 0.5 * x * (1.0 + jnp.tanh(math.sqrt(2.0 / math.pi) * (x + 0.044715 * (x * x * x))))


def _merge_kernel(yf_ref, yb_ref, o_ref, gs_ref, ga_ref, x_ref, lng_ref, lnb_ref, g1_ref, sh2_ref, sc2_ref,
                  wglu_ref, bglu_ref, wso_ref, wao_ref, wo_ref, l1g_ref, l1b_ref, wr_ref, br_ref,
                  h1_ref, idx_ref, gate_ref, rank_ref, cnt_ref, cnt_scr, *, alpha, n_exp):
    i = pl.program_id(0)

    @pl.when(i == 0)
    def _():
        cnt_scr[...] = jnp.zeros_like(cnt_scr)

    nlb = yf_ref.shape[0]
    y = jnp.concatenate([yf_ref[lb] + yb_ref[lb] for lb in range(nlb)], axis=1)
    z = _gelu_tanh(y)
    z = z * _sigmoid(_dot(z.astype(BF16), wglu_ref[...]) + bglu_ref[...])
    m = (gs_ref[...].astype(F32) * _dot(z.astype(BF16), wso_ref[...])
         + ga_ref[...].astype(F32) * _dot(o_ref[...], wao_ref[...]))
    mix = _dot(m.astype(BF16), wo_ref[...])
    h = _layer_norm(x_ref[...], lng_ref[...], lnb_ref[...])
    h1 = _layer_norm(alpha * h + g1_ref[...] * mix, l1g_ref[...], l1b_ref[...])
    h1_ref[...] = h1
    xm = h1 * (1.0 + sc2_ref[...]) + sh2_ref[...]
    logits = jnp.dot(xm, wr_ref[...], preferred_element_type=F32, precision=HIGHEST) + br_ref[...]
    tm = logits.shape[0]
    lane = lax.broadcasted_iota(jnp.int32, (tm, n_exp), 1)
    work = logits
    vals, sels = [], []
    for _ in range(TOP_K):
        mx = jnp.max(work, axis=1, keepdims=True)
        sel = jnp.min(jnp.where(work == mx, lane, n_exp), axis=1, keepdims=True)
        vals.append(mx)
        sels.append(sel)
        work = jnp.where(lane == sel, -jnp.inf, work)
    exps = [jnp.exp(v - vals[0]) for v in vals]
    den = exps[0] + exps[1] + exps[2] + exps[3]
    onehot = jnp.zeros((tm, n_exp), F32)
    for sel in sels:
        onehot = onehot + (lane == sel).astype(F32)
    ri = lax.broadcasted_iota(jnp.int32, (tm, tm), 0)
    ci = lax.broadcasted_iota(jnp.int32, (tm, tm), 1)
    tri = jnp.where(ci < ri, 1.0, 0.0).astype(BF16)
    rank = _dot(tri, onehot.astype(BF16)) + cnt_scr[...]
    lane_k = lax.broadcasted_iota(jnp.int32, (tm, TOP_K), 1)
    idx_o = jnp.zeros((tm, TOP_K), jnp.int32)
    gate_o = jnp.zeros((tm, TOP_K), F32)
    rank_o = jnp.zeros((tm, TOP_K), jnp.int32)
    for k in range(TOP_K):
        rk = jnp.sum(jnp.where(lane == sels[k], rank, 0.0), axis=1, keepdims=True).astype(jnp.int32)
        idx_o = jnp.where(lane_k == k, sels[k], idx_o)
        gate_o = jnp.where(lane_k == k, exps[k] / den, gate_o)
        rank_o = jnp.where(lane_k == k, rk, rank_o)
    idx_ref[...] = idx_o
    gate_ref[...] = gate_o
    rank_ref[...] = rank_o
    cnt = cnt_scr[...] + jnp.sum(onehot, axis=0, keepdims=True)
    cnt_scr[...] = cnt
    cnt_ref[...] = cnt


def _merge(yf, yb, o_att, sgs, sga, x2, ln_g, ln_b, mod3, w_glu, b_glu, w_so, w_ao, w_o, l1g, l1b,
           w_r, b_r, *, rows_per_batch, tm, alpha):
    t, d = x2.shape
    nlb = yf.shape[0]
    sw = nlb * LANES
    n_exp = w_r.shape[1]
    tpb = rows_per_batch // tm
    row = lambda i: (i, 0)
    const = lambda i: (0, 0)

    def modspec(chunk):
        return pl.BlockSpec((None, 1, d), lambda i: (i // tpb, 0, chunk))

    return pl.pallas_call(
        functools.partial(_merge_kernel, alpha=alpha, n_exp=n_exp),
        grid=(t // tm,),
        in_specs=[
            pl.BlockSpec((nlb, tm, LANES), lambda i: (0, i, 0)),
            pl.BlockSpec((nlb, tm, LANES), lambda i: (0, i, 0)),
            pl.BlockSpec((tm, o_att.shape[1]), row),
            pl.BlockSpec((tm, d), row),
            pl.BlockSpec((tm, d), row),
            pl.BlockSpec((tm, d), row),
            pl.BlockSpec((1, d), const),
            pl.BlockSpec((1, d), const),
            modspec(2), modspec(3), modspec(4),
            pl.BlockSpec(w_glu.shape, const),
            pl.BlockSpec((1, sw), const),
            pl.BlockSpec(w_so.shape, const),
            pl.BlockSpec(w_ao.shape, const),
            pl.BlockSpec(w_o.shape, const),
            pl.BlockSpec((1, d), const),
            pl.BlockSpec((1, d), const),
            pl.BlockSpec(w_r.shape, const),
            pl.BlockSpec((1, n_exp), const),
        ],
        out_specs=[
            pl.BlockSpec((tm, d), row),
            pl.BlockSpec((tm, TOP_K), row),
            pl.BlockSpec((tm, TOP_K), row),
            pl.BlockSpec((tm, TOP_K), row),
            pl.BlockSpec((1, n_exp), const),
        ],
        out_shape=[
            jax.ShapeDtypeStruct((t, d), F32),
            jax.ShapeDtypeStruct((t, TOP_K), jnp.int32),
            jax.ShapeDtypeStruct((t, TOP_K), F32),
            jax.ShapeDtypeStruct((t, TOP_K), jnp.int32),
            jax.ShapeDtypeStruct((1, n_exp), F32),
        ],
        scratch_shapes=[pltpu.VMEM((1, n_exp), F32)],
        compiler_params=_cparams(("arbitrary",)),
        name="merge",
    )(yf, yb, o_att, sgs, sga, x2, ln_g, ln_b, mod3, mod3, mod3, w_glu, b_glu, w_so, w_ao, w_o,
      l1g, l1b, w_r, b_r)


def _row_copy_out(src, dst, pos_ref, sem, t, k):
    return pltpu.make_async_copy(src.at[pl.ds(t, 1)], dst.at[pl.ds(pos_ref[t * TOP_K + k], 1)], sem)


def _dispatch_kernel(pos_ref, h1_ref, sh2_ref, sc2_ref, xrows_ref, xm_scr, sem):
    tm = h1_ref.shape[0]
    xm_scr[...] = h1_ref[...] * (1.0 + sc2_ref[...]) + sh2_ref[...]

    def start(t, carry):
        for k in range(TOP_K):
            _row_copy_out(xm_scr, xrows_ref, pos_ref, sem, t, k).start()
        return carry

    def wait(t, carry):
        for k in range(TOP_K):
            _row_copy_out(xm_scr, xrows_ref, pos_ref, sem, t, k).wait()
        return carry

    lax.fori_loop(0, tm, start, 0)
    lax.fori_loop(0, tm, wait, 0)


def _dispatch(pos_flat, h1, mod3, *, rows_per_batch, tm):
    t, d = h1.shape
    tpb = rows_per_batch // tm
    return pl.pallas_call(
        _dispatch_kernel,
        grid=(t // tm,),
        in_specs=[
            pl.BlockSpec((tm * TOP_K,), lambda i: (i,), memory_space=pltpu.SMEM),
            pl.BlockSpec((tm, d), lambda i: (i, 0)),
            pl.BlockSpec((None, 1, d), lambda i: (i // tpb, 0, 3)),
            pl.BlockSpec((None, 1, d), lambda i: (i // tpb, 0, 4)),
        ],
        out_specs=pl.BlockSpec(memory_space=pl.ANY),
        out_shape=jax.ShapeDtypeStruct((t * TOP_K, d), F32),
        scratch_shapes=[pltpu.VMEM((tm, d), F32), pltpu.SemaphoreType.DMA(())],
        compiler_params=_cparams(("arbitrary",)),
        name="dispatch",
    )(pos_flat, h1, mod3, mod3)


def _experts_kernel(tile_ref, exp_ref, lo_ref, hi_ref, x_ref, wgu_ref, bgu_ref, wd_ref, bd_ref, y_ref,
                    wgu_scr, wd_scr, *, rows):
    w = pl.program_id(0)
    prev = jnp.maximum(w - 1, 0)
    e_new = (w == 0) | (exp_ref[w] != exp_ref[prev])
    t_new = (w == 0) | (tile_ref[w] != tile_ref[prev])
    lo = lo_ref[w]
    hi = hi_ref[w]
    f = wd_ref.shape[0]

    @pl.when(e_new)
    def _():
        wgu_scr[...] = wgu_ref[...].astype(BF16)
        wd_scr[...] = wd_ref[...].astype(BF16)

    @pl.when(t_new)
    def _():
        y_ref[...] = jnp.zeros_like(y_ref)

    @pl.when(hi > lo)
    def _():
        gu = _dot(x_ref[...].astype(BF16), wgu_scr[...]) + bgu_ref[...]
        glu = jnp.minimum(gu[:, :f], SWIGLU_LIMIT)
        lin = jnp.clip(gu[:, f:], -SWIGLU_LIMIT, SWIGLU_LIMIT)
        act = glu * _sigmoid(SWIGLU_ALPHA * glu) * (lin + 1.0)
        y = _dot(act.astype(BF16), wd_scr[...]) + bd_ref[...]
        r = tile_ref[w] * rows + lax.broadcasted_iota(jnp.int32, (rows, 1), 0)
        y_ref[...] = jnp.where((r >= lo) & (r < hi), y, y_ref[...])


def _experts(work, xrows, w_gu, b_gu, w_d, b_d, *, rows):
    tile_id, exp_id, lo, hi = work
    n, d = xrows.shape
    n_exp, _, f2 = w_gu.shape
    f = w_d.shape[1]
    grid_spec = pltpu.PrefetchScalarGridSpec(
        num_scalar_prefetch=4,
        grid=(tile_id.shape[0],),
        in_specs=[
            pl.BlockSpec((rows, d), lambda w, ti, ex, lo, hi: (ti[w], 0)),
            pl.BlockSpec((None, d, f2), lambda w, ti, ex, lo, hi: (ex[w], 0, 0)),
            pl.BlockSpec((None, 1, f2), lambda w, ti, ex, lo, hi: (ex[w], 0, 0)),
            pl.BlockSpec((None, f, d), lambda w, ti, ex, lo, hi: (ex[w], 0, 0)),
            pl.BlockSpec((None, 1, d), lambda w, ti, ex, lo, hi: (ex[w], 0, 0)),
        ],
        out_specs=pl.BlockSpec((rows, d), lambda w, ti, ex, lo, hi: (ti[w], 0)),
        scratch_shapes=[pltpu.VMEM((d, f2), BF16), pltpu.VMEM((f, d), BF16)],
    )
    return pl.pallas_call(
        functools.partial(_experts_kernel, rows=rows),
        grid_spec=grid_spec,
        out_shape=jax.ShapeDtypeStruct((n, d), F32),
        compiler_params=_cparams(("arbitrary",)),
        name="experts",
    )(tile_id, exp_id, lo, hi, xrows, w_gu, b_gu.reshape(n_exp, 1, f2), w_d, b_d.reshape(n_exp, 1, d))


def _row_copy_in(src, dst, pos_ref, sem, t, k):
    return pltpu.make_async_copy(src.at[pl.ds(pos_ref[t * TOP_K + k], 1)], dst.at[k, pl.ds(t, 1)], sem)


def _combine_kernel(pos_ref, h1_ref, gate_ref, g2_ref, lg_ref, lb_ref, yrows_ref, o_ref, buf, sem, *, alpha):
    tm = h1_ref.shape[0]

    def start(t, carry):
        for k in range(TOP_K):
            _row_copy_in(yrows_ref, buf, pos_ref, sem, t, k).start()
        return carry

    def wait(t, carry):
        for k in range(TOP_K):
            _row_copy_in(yrows_ref, buf, pos_ref, sem, t, k).wait()
        return carry

    lax.fori_loop(0, tm, start, 0)
    lax.fori_loop(0, tm, wait, 0)
    gates = gate_ref[...]
    ffn = gates[:, 0:1] * buf[0]
    for k in range(1, TOP_K):
        ffn = ffn + gates[:, k:k + 1] * buf[k]
    o_ref[...] = _layer_norm(alpha * h1_ref[...] + g2_ref[...] * ffn, lg_ref[...], lb_ref[...])


def _combine(pos_flat, h1, gates, mod3, l2g, l2b, yrows, *, rows_per_batch, tm, alpha):
    t, d = h1.shape
    tpb = rows_per_batch // tm
    return pl.pallas_call(
        functools.partial(_combine_kernel, alpha=alpha),
        grid=(t // tm,),
        in_specs=[
            pl.BlockSpec((tm * TOP_K,), lambda i: (i,), memory_space=pltpu.SMEM),
            pl.BlockSpec((tm, d), lambda i: (i, 0)),
            pl.BlockSpec((tm, TOP_K), lambda i: (i, 0)),
            pl.BlockSpec((None, 1, d), lambda i: (i // tpb, 0, 5)),
            pl.BlockSpec((1, d), lambda i: (0, 0)),
            pl.BlockSpec((1, d), lambda i: (0, 0)),
            pl.BlockSpec(memory_space=pl.ANY),
        ],
        out_specs=pl.BlockSpec((tm, d), lambda i: (i, 0)),
        out_shape=jax.ShapeDtypeStruct((t, d), F32),
        scratch_shapes=[pltpu.VMEM((TOP_K, tm, d), F32), pltpu.SemaphoreType.DMA(())],
        compiler_params=_cparams(("arbitrary",)),
        name="combine",
    )(pos_flat, h1, gates, mod3, l2g, l2b, yrows)


def _work_list(counts, n_rows, rows):
    n_exp = counts.shape[0]
    n_tiles = n_rows // rows
    n_work = n_tiles + n_exp - 1
    def count_le(sorted_vals, q):
        return jnp.sum((sorted_vals[None, :] <= q[:, None]).astype(jnp.int32), axis=1)

    ends = jnp.cumsum(counts)
    starts = ends - counts
    tile_lo = jnp.arange(n_tiles, dtype=jnp.int32) * rows
    e_lo = count_le(ends, tile_lo)
    e_hi = count_le(ends, tile_lo + rows - 1)
    per_tile = e_hi - e_lo + 1
    off_end = jnp.cumsum(per_tile)
    off = off_end - per_tile
    w = jnp.arange(n_work, dtype=jnp.int32)
    tile = jnp.minimum(count_le(off_end, w), n_tiles - 1)
    exp = jnp.minimum(e_lo[tile] + (w - off[tile]), n_exp - 1).astype(jnp.int32)
    live = w < off_end[-1]
    lo = jnp.maximum(starts[exp], tile * rows)
    hi = jnp.minimum(ends[exp], (tile + 1) * rows)
    hi = jnp.where(live, hi, lo)
    return tile, exp, lo.astype(jnp.int32), hi.astype(jnp.int32), starts


def kernel(x, c, ctx, c_ctx, ln_in_g, ln_in_b, w_mod, b_mod, w_in, ssm_lam_re, ssm_lam_im, ssm_log_step, ssm_b_re, ssm_b_im, ssm_c_re, ssm_c_im, ssm_d, w_glu, b_glu, attn_sink, w_ssm_out, w_att_out, w_o, ln1_g, ln1_b, w_router, b_router, w_gate_up, b_gate_up, w_down, b_down, ln2_g, ln2_b):
    bsz, seq_len, d = x.shape
    n_ctx = ctx.shape[1]
    depth = w_mod.shape[0]
    assert depth == 1, "single-layer kernel"
    g_ssm, h_ssm = ssm_d.shape[1:]
    ssm_w = g_ssm * h_ssm
    attn_w = N_Q_HEADS * HEAD_DIM
    kv_w = N_KV_HEADS * HEAD_DIM
    kv2_w = 2 * kv_w
    nlb = ssm_w // LANES
    alpha = (2.0 * depth) ** 0.25
    t = bsz * seq_len
    assert bsz + 1 <= SUBLANES

    row2 = lambda a: a.reshape(1, -1)

    c_rows = jnp.concatenate([c, c_ctx[None], jnp.zeros((SUBLANES - bsz - 1, d), F32)], axis=0)
    mod = _mod_vectors(c_rows, w_mod[0], row2(b_mod[0]))
    mod3 = mod.reshape(SUBLANES, 1, 6 * d)

    wi = w_in[0]
    i0, i1 = ssm_w, ssm_w + attn_w
    i2, i3 = i1 + kv_w, i1 + 2 * kv_w
    i4 = i3 + d

    def dup_heads(wkv):
        parts = []
        for hk in range(N_KV_HEADS):
            blk = wkv[:, hk * HEAD_DIM:(hk + 1) * HEAD_DIM]
            parts += [blk, blk]
        return jnp.concatenate(parts, axis=1)

    w_s, w_q = wi[:, :i0], wi[:, i0:i1]
    w_k2, w_v2 = dup_heads(wi[:, i1:i2]), dup_heads(wi[:, i2:i3])
    w_lat = jnp.concatenate([w_s, w_q, w_k2, w_v2, wi[:, i3:i4], wi[:, i4:]], axis=1).astype(BF16)
    w_ctx = jnp.concatenate([w_s, w_k2, w_v2], axis=1).astype(BF16)

    pos = jnp.arange(seq_len)
    inv = ROPE_BASE ** (-jnp.arange(ROPE_PAIRS, dtype=F32) / ROPE_PAIRS)
    ang = jnp.concatenate([(pos // GRID_W).astype(F32)[:, None] * inv,
                           (pos % GRID_W).astype(F32)[:, None] * inv], axis=-1)
    cos_t = jnp.tile(jnp.cos(ang), (1, 2 * LANES // HEAD_DIM))
    sin_h = jnp.sin(ang)
    sin_t = jnp.tile(jnp.concatenate([-sin_h, sin_h], axis=-1), (1, LANES // HEAD_DIM))

    g_in, b_in = row2(ln_in_g), row2(ln_in_b)
    x2 = x.reshape(t, d)
    ctx2 = ctx.reshape(bsz * n_ctx, d)
    s_c, k_c, v_c = _inproj(ctx2, g_in, b_in, mod3, w_ctx, None, None, latent=False,
                            rows_per_batch=n_ctx, tm=n_ctx, ctx_mod_row=bsz,
                            ssm_w=ssm_w, attn_w=attn_w, kv2_w=kv2_w)
    s_l, q_l, k_l, v_l, sgs, sga = _inproj(x2, g_in, b_in, mod3, w_lat, cos_t, sin_t, latent=True,
                                           rows_per_batch=seq_len, tm=512, ctx_mod_row=None,
                                           ssm_w=ssm_w, attn_w=attn_w, kv2_w=kv2_w)

    wf = _s5_weights(ssm_lam_re[0, 0], ssm_lam_im[0, 0], ssm_log_step[0, 0], ssm_b_re[0, 0], ssm_b_im[0, 0],
                     ssm_c_re[0, 0], ssm_c_im[0, 0], reverse=False)
    wb = _s5_weights(ssm_lam_re[0, 1], ssm_lam_im[0, 1], ssm_log_step[0, 1], ssm_b_re[0, 1], ssm_b_im[0, 1],
                     ssm_c_re[0, 1], ssm_c_im[0, 1], reverse=True)
    width = S5_CHUNK * LANES
    d_tile = jnp.tile(ssm_d[0].astype(F32).reshape(nlb, 1, LANES), (1, 1, S5_CHUNK))
    sw2 = wf[1].shape[-1]
    zero_state = jnp.zeros((nlb, bsz, sw2), F32)
    uc4 = s_c.reshape(nlb, bsz, n_ctx // S5_CHUNK, width)
    ul4 = s_l.reshape(nlb, bsz, seq_len // S5_CHUNK, width)
    _, sf0 = _s5_scan(uc4, zero_state, wf, d_tile, reverse=False, cc=n_ctx // S5_CHUNK, add_skip=False)
    _, sb0 = _s5_scan(uc4, zero_state, wb, d_tile, reverse=True, cc=n_ctx // S5_CHUNK, add_skip=False)
    yf4, _ = _s5_scan(ul4, sf0, wf, d_tile, reverse=False, cc=128, add_skip=True)
    yb4, _ = _s5_scan(ul4, sb0, wb, d_tile, reverse=True, cc=128, add_skip=False)
    yf = yf4.reshape(nlb, t, LANES)
    yb = yb4.reshape(nlb, t, LANES)

    o_att = _attention(q_l, k_l, v_l, k_c, v_c, attn_sink[0].astype(F32), bsz=bsz, seq_len=seq_len, n_ctx=n_ctx)

    h1, top_i, gates, rank, counts = _merge(
        yf, yb, o_att, sgs, sga, x2, g_in, b_in, mod3, w_glu[0].astype(BF16), row2(b_glu[0]),
        w_ssm_out[0].astype(BF16), w_att_out[0].astype(BF16), w_o[0].astype(BF16), row2(ln1_g[0]), row2(ln1_b[0]),
        w_router[0], row2(b_router[0]), rows_per_batch=seq_len, tm=256, alpha=alpha)

    rows = 256
    n_rows = t * TOP_K
    tile_id, exp_id, lo, hi, starts = _work_list(counts[0].astype(jnp.int32), n_rows, rows)
    pos_flat = (starts[top_i] + rank).reshape(n_rows).astype(jnp.int32)
    xrows = _dispatch(pos_flat, h1, mod3, rows_per_batch=seq_len, tm=256)
    yrows = _experts((tile_id, exp_id, lo, hi), xrows, w_gate_up[0], b_gate_up[0], w_down[0], b_down[0], rows=rows)
    out = _combine(pos_flat, h1, gates, mod3, row2(ln2_g[0]), row2(ln2_b[0]), yrows,
                   rows_per_batch=seq_len, tm=256, alpha=alpha)
    return out.reshape(bsz, seq_len, d)
```

```python
import functools
import math

import jax
import jax.numpy as jnp
import numpy as np
from jax import lax
from jax.experimental import pallas as pl
from jax.experimental.pallas import tpu as pltpu

F32 = jnp.float32
BF16 = jnp.bfloat16
HIGHEST = lax.Precision.HIGHEST

HEAD_DIM = 64
N_Q_HEADS = 8
N_KV_HEADS = 2
Q_PER_KV = N_Q_HEADS // N_KV_HEADS
WINDOW = 128
GRID_W = 64
ROPE_BASE = 10000.0
ROPE_PAIRS = HEAD_DIM // 4
TOP_K = 4
SWIGLU_LIMIT = 7.0
SWIGLU_ALPHA = 1.702
LN_EPS = 1e-5
NEG_INF = -1e30

LANES = 128
SUBLANES = 8
ROW_TILES = 8
VMEM_LIMIT = 56 * 1024 * 1024
N_DMA_PRIORITIES = 2
DMA_LOOP_UNROLL = 8

S5_CHUNK = 8
S5_ROW_PAD = SUBLANES


def _cparams(sem):
    return pltpu.CompilerParams(dimension_semantics=sem, vmem_limit_bytes=VMEM_LIMIT)


def _sigmoid(x):
    return 1.0 / (1.0 + jnp.exp(-x))


def _layer_norm(x, g, b):
    mu = jnp.mean(x, axis=-1, keepdims=True)
    xc = x - mu
    var = jnp.mean(xc * xc, axis=-1, keepdims=True)
    return xc * lax.rsqrt(var + LN_EPS) * g + b


def _dot(a, b):
    return jnp.dot(a, b, preferred_element_type=F32)


def _mod_kernel(c_ref, w_ref, b_ref, o_ref):
    c = c_ref[...]
    a = c * _sigmoid(c)
    o_ref[...] = jnp.dot(a, w_ref[...], preferred_element_type=F32, precision=HIGHEST) + b_ref[...]


def _mod_vectors(c_rows, w_mod, b_mod):
    d = c_rows.shape[1]
    n = w_mod.shape[1]
    return pl.pallas_call(
        _mod_kernel,
        grid=(n // d,),
        in_specs=[
            pl.BlockSpec((SUBLANES, d), lambda i: (0, 0)),
            pl.BlockSpec((d, d), lambda i: (0, i)),
            pl.BlockSpec((1, d), lambda i: (0, i)),
        ],
        out_specs=pl.BlockSpec((SUBLANES, d), lambda i: (0, i)),
        out_shape=jax.ShapeDtypeStruct((SUBLANES, n), F32),
        compiler_params=_cparams(("arbitrary",)),
        name="mod",
    )(c_rows, w_mod, b_mod)


def _rope(t, cos, sin):
    n = t.shape[1]
    reps = n // LANES
    c = jnp.concatenate([cos] * reps, axis=1) if reps > 1 else cos
    s = jnp.concatenate([sin] * reps, axis=1) if reps > 1 else sin
    half = HEAD_DIM // 2
    upper = pltpu.roll(t, n - half, axis=1)
    lower = pltpu.roll(t, half, axis=1)
    lane = lax.broadcasted_iota(jnp.int32, t.shape, 1)
    partner = jnp.where((lane & half) == 0, upper, lower)
    return t * c + partner * s


def _inproj_kernel(*refs, latent, ssm_w, attn_w, kv2_w, d_model):
    if latent:
        (x_ref, g_ref, b_ref, sh_ref, sc_ref, w_ref, cos_ref, sin_ref,
         s_ref, q_ref, k_ref, v_ref, gs_ref, ga_ref) = refs
    else:
        x_ref, g_ref, b_ref, sh_ref, sc_ref, w_ref, s_ref, k_ref, v_ref = refs
    h = _layer_norm(x_ref[...], g_ref[...], b_ref[...])
    u = (h * (1.0 + sc_ref[...]) + sh_ref[...]).astype(BF16)
    col = 0
    s = _dot(u, w_ref[:, col:col + ssm_w])
    for lb in range(ssm_w // LANES):
        s_ref[lb] = s[:, lb * LANES:(lb + 1) * LANES]
    col += ssm_w
    if latent:
        q = _dot(u, w_ref[:, col:col + attn_w])
        q = _rope(q, cos_ref[...], sin_ref[...]) * (HEAD_DIM ** -0.5)
        q_ref[...] = q.astype(BF16)
        col += attn_w
    k = _dot(u, w_ref[:, col:col + kv2_w])
    if latent:
        k = _rope(k, cos_ref[...], sin_ref[...])
    k_ref[...] = k.astype(BF16)
    col += kv2_w
    v_ref[...] = _dot(u, w_ref[:, col:col + kv2_w]).astype(BF16)
    col += kv2_w
    if latent:
        gs_ref[...] = _sigmoid(_dot(u, w_ref[:, col:col + d_model])).astype(BF16)
        col += d_model
        ga_ref[...] = _sigmoid(_dot(u, w_ref[:, col:col + d_model])).astype(BF16)


def _inproj(x2, ln_g, ln_b, mod3, w_cat, cos_t, sin_t, *, latent, rows_per_batch, tm, ctx_mod_row,
            ssm_w, attn_w, kv2_w):
    t, d = x2.shape
    tiles_per_batch = rows_per_batch // tm
    nlb = ssm_w // LANES

    def brow(i):
        return i // tiles_per_batch if ctx_mod_row is None else ctx_mod_row

    in_specs = [
        pl.BlockSpec((tm, d), lambda i: (i, 0)),
        pl.BlockSpec((1, d), lambda i: (0, 0)),
        pl.BlockSpec((1, d), lambda i: (0, 0)),
        pl.BlockSpec((None, 1, d), lambda i: (brow(i), 0, 0)),
        pl.BlockSpec((None, 1, d), lambda i: (brow(i), 0, 1)),
        pl.BlockSpec(w_cat.shape, lambda i: (0, 0)),
    ]
    args = [x2, ln_g, ln_b, mod3, mod3, w_cat]
    out_specs = [pl.BlockSpec((nlb, tm, LANES), lambda i: (0, i, 0))]
    out_shape = [jax.ShapeDtypeStruct((nlb, t, LANES), F32)]
    if latent:
        in_specs += [
            pl.BlockSpec((tm, LANES), lambda i: (i % tiles_per_batch, 0)),
            pl.BlockSpec((tm, LANES), lambda i: (i % tiles_per_batch, 0)),
        ]
        args += [cos_t, sin_t]
        out_specs.append(pl.BlockSpec((tm, attn_w), lambda i: (i, 0)))
        out_shape.append(jax.ShapeDtypeStruct((t, attn_w), BF16))
    out_specs += [pl.BlockSpec((tm, kv2_w), lambda i: (i, 0))] * 2
    out_shape += [jax.ShapeDtypeStruct((t, kv2_w), BF16)] * 2
    if latent:
        out_specs += [pl.BlockSpec((tm, d), lambda i: (i, 0))] * 2
        out_shape += [jax.ShapeDtypeStruct((t, d), BF16)] * 2
    return pl.pallas_call(
        functools.partial(_inproj_kernel, latent=latent, ssm_w=ssm_w, attn_w=attn_w, kv2_w=kv2_w,
                          d_model=d),
        grid=(t // tm,),
        in_specs=in_specs,
        out_specs=out_specs,
        out_shape=out_shape,
        compiler_params=_cparams(("arbitrary",)),
        name="inproj_latent" if latent else "inproj_ctx",
    )(*args)


def _s5_weights(lam_re, lam_im, log_step, b_re, b_im, c_re, c_im, reverse):
    ch = S5_CHUNK
    g, p = lam_re.shape
    hh = b_re.shape[-1]
    gpb = LANES // hh
    nlb = g // gpb
    lr, li = lam_re.astype(F32), lam_im.astype(F32)
    dt = jnp.exp(log_step.astype(F32))[:, None]
    jj = jnp.arange(ch + 1, dtype=F32)[:, None, None]
    mag = jnp.exp(jj * lr * dt)
    pr = mag * jnp.cos(jj * li * dt)
    pi = mag * jnp.sin(jj * li * dt)
    ar, ai = pr[1], pi[1]
    den = lr * lr + li * li
    cr = ((ar - 1) * lr + ai * li) / den
    ci = (ai * lr - (ar - 1) * li) / den
    br, bi = b_re.astype(F32), b_im.astype(F32)
    bbr = cr[..., None] * br - ci[..., None] * bi
    bbi = cr[..., None] * bi + ci[..., None] * br
    ccr, cci = c_re.astype(F32), c_im.astype(F32)
    bbr_t = jnp.swapaxes(bbr, 1, 2)
    bbi_t = jnp.swapaxes(bbi, 1, 2)
    ccr_t = jnp.swapaxes(ccr, 1, 2)
    cci_t = jnp.swapaxes(cci, 1, 2)

    def expand(tbl, n_copies, rows_per_group, cols_per_group):
        r, w = tbl.shape[-2:]
        sel = np.tile(np.eye(w, dtype=np.float32), (1, n_copies))
        keep = (np.arange(r)[:, None] // rows_per_group) == (np.arange(n_copies * w)[None, :] // cols_per_group)
        wide = jnp.einsum("...rw,wc->...rc", tbl, jnp.asarray(sel), precision=HIGHEST)
        return wide * jnp.asarray(keep.astype(np.float32))

    er = ccr[None] * pr[:ch, :, None, :] - cci[None] * pi[:ch, :, None, :]
    ei = ccr[None] * pi[:ch, :, None, :] + cci[None] * pr[:ch, :, None, :]
    m = jnp.sum(er[:, :, None, :, :] * bbr_t[None, :, :, None, :]
                - ei[:, :, None, :, :] * bbi_t[None, :, :, None, :], axis=-1)
    bd = expand(m.reshape(ch, nlb, LANES, hh), gpb, hh, hh)
    bd = jnp.concatenate([bd, jnp.zeros_like(bd[:1])], axis=0)
    kk = np.arange(ch)[:, None]
    ii = np.arange(ch)[None, :]
    lag = (kk - ii) if reverse else (ii - kk)
    lag = np.where(lag >= 0, lag, ch)
    kin = jnp.transpose(bd[lag], (2, 0, 3, 1, 4)).reshape(nlb, ch * LANES, ch * LANES)
    rk = np.arange(ch) if reverse else (ch - 1 - np.arange(ch))
    apr, api = pr[rk][:, :, None, :], pi[rk][:, :, None, :]
    sr = apr * bbr_t[None] - api * bbi_t[None]
    si = apr * bbi_t[None] + api * bbr_t[None]
    ws = jnp.stack([sr, si], axis=0).reshape(2, ch, nlb, LANES, p)
    ws = expand(ws, gpb, hh, p)
    ws = jnp.transpose(ws, (2, 1, 3, 0, 4)).reshape(nlb, ch * LANES, 2 * gpb * p)
    ex = (ch - np.arange(ch)) if reverse else (np.arange(ch) + 1)
    epr, epi = pr[ex][:, :, :, None], pi[ex][:, :, :, None]
    wo_r = ccr_t[None] * epr - cci_t[None] * epi
    wo_i = -(ccr_t[None] * epi + cci_t[None] * epr)
    wo = jnp.stack([wo_r, wo_i], axis=0).reshape(2, ch, nlb, gpb * p, hh)
    wo = expand(wo, gpb, p, hh)
    wo = jnp.transpose(wo, (2, 0, 3, 1, 4)).reshape(nlb, 2 * gpb * p, ch * LANES)
    a_chunk = jnp.stack([pr[ch].reshape(nlb, gpb * p), pi[ch].reshape(nlb, gpb * p)], axis=1)
    return kin.astype(BF16), ws.astype(BF16), wo.astype(BF16), a_chunk


def _s5_kernel(u_ref, s0_ref, a_ref, kin_ref, ws_ref, wo_ref, d_ref, y_ref, sfin_ref,
               z_scr, sin_scr, carry_scr, *, reverse, nb, cc, add_skip):
    j = pl.program_id(1)

    @pl.when(j == 0)
    def _():
        carry_scr[...] = s0_ref[...]

    width = u_ref.shape[-1]
    sw = a_ref.shape[-1]
    ns = sw // LANES
    u = u_ref[...].reshape(nb * cc, width)
    ub = u.astype(BF16)
    z = _dot(ub, ws_ref[...])
    ccp = cc + S5_ROW_PAD
    for s in range(2 * ns):
        for b in range(nb):
            z_scr[s, b * ccp:b * ccp + cc, :] = z[b * cc:(b + 1) * cc, s * LANES:(s + 1) * LANES]
    ar = [a_ref[0:1, s * LANES:(s + 1) * LANES] for s in range(ns)]
    ai = [a_ref[1:2, s * LANES:(s + 1) * LANES] for s in range(ns)]

    def body(c, st):
        cidx = (cc - 1 - c) if reverse else c
        rows = pl.ds(cidx, nb, stride=ccp)
        new = [None] * (2 * ns)
        for s in range(ns):
            sr, si = st[s], st[ns + s]
            sin_scr[s, rows, :] = sr
            sin_scr[ns + s, rows, :] = si
            new[s] = ar[s] * sr - ai[s] * si + z_scr[s, rows, :]
            new[ns + s] = ar[s] * si + ai[s] * sr + z_scr[ns + s, rows, :]
        return tuple(new)

    st0 = tuple(carry_scr[:, s * LANES:(s + 1) * LANES] for s in range(2 * ns))
    st = lax.fori_loop(0, cc, body, st0)
    s_fin = jnp.concatenate(st, axis=1)
    carry_scr[...] = s_fin
    sfin_ref[...] = s_fin
    s_in = jnp.concatenate(
        [jnp.concatenate([sin_scr[s, b * ccp:b * ccp + cc, :] for b in range(nb)], axis=0)
         for s in range(2 * ns)], axis=1)
    y = _dot(ub, kin_ref[...]) + _dot(s_in.astype(BF16), wo_ref[...])
    if add_skip:
        y = y + u * d_ref[...]
    y_ref[...] = y.reshape(nb, cc, width)


def _s5_scan(u4, s0, weights, d_tile, *, reverse, cc, add_skip):
    kin, ws, wo, a_chunk = weights
    nlb, nb, nch, width = u4.shape
    sw2 = ws.shape[-1]
    nj = nch // cc

    def jm(j):
        return (nj - 1 - j) if reverse else j

    return pl.pallas_call(
        functools.partial(_s5_kernel, reverse=reverse, nb=nb, cc=cc, add_skip=add_skip),
        grid=(nlb, nj),
        in_specs=[
            pl.BlockSpec((None, nb, cc, width), lambda l, j: (l, 0, jm(j), 0)),
            pl.BlockSpec((None, nb, sw2), lambda l, j: (l, 0, 0)),
            pl.BlockSpec((None, 2, sw2 // 2), lambda l, j: (l, 0, 0)),
            pl.BlockSpec((None, width, width), lambda l, j: (l, 0, 0)),
            pl.BlockSpec((None, width, sw2), lambda l, j: (l, 0, 0)),
            pl.BlockSpec((None, sw2, width), lambda l, j: (l, 0, 0)),
            pl.BlockSpec((None, 1, width), lambda l, j: (l, 0, 0)),
        ],
        out_specs=[
            pl.BlockSpec((None, nb, cc, width), lambda l, j: (l, 0, jm(j), 0)),
            pl.BlockSpec((None, nb, sw2), lambda l, j: (l, 0, 0)),
        ],
        out_shape=[
            jax.ShapeDtypeStruct(u4.shape, F32),
            jax.ShapeDtypeStruct((nlb, nb, sw2), F32),
        ],
        scratch_shapes=[
            pltpu.VMEM((sw2 // LANES, nb * (cc + S5_ROW_PAD), LANES), F32),
            pltpu.VMEM((sw2 // LANES, nb * (cc + S5_ROW_PAD), LANES), F32),
            pltpu.VMEM((nb, sw2), F32),
        ],
        compiler_params=_cparams(("arbitrary", "arbitrary")),
        name="s5_bwd" if reverse else "s5_fwd",
    )(u4, s0, a_chunk, kin, ws, wo, d_tile)


def _attn_kernel(sink_ref, q_ref, kp_ref, ko_ref, kn_ref, vp_ref, vo_ref, vn_ref, kc_ref, vc_ref,
                 o_ref, *, nblk, blk):
    n = pl.program_id(1)
    n_ctx = kc_ref.shape[0]
    n_loc = 3 * blk
    qi = lax.broadcasted_iota(jnp.int32, (blk, n_loc + n_ctx), 0)
    kj = lax.broadcasted_iota(jnp.int32, (blk, n_loc + n_ctx), 1)
    rel = kj - qi
    lo = jnp.where(n == 0, blk, 0)
    hi = jnp.where(n == nblk - 1, 2 * blk, n_loc)
    valid = ((rel >= blk - WINDOW) & (rel <= blk + WINDOW) & (kj >= lo) & (kj < hi)) | (kj >= n_loc)
    lane = lax.broadcasted_iota(jnp.int32, (blk, LANES), 1)
    low_half = lane < HEAD_DIM
    outs = []
    for hk in range(N_KV_HEADS):
        cs = slice(hk * LANES, (hk + 1) * LANES)
        kcat = jnp.concatenate([kp_ref[:, cs], ko_ref[:, cs], kn_ref[:, cs], kc_ref[:, cs]], axis=0)
        vcat = jnp.concatenate([vp_ref[:, cs], vo_ref[:, cs], vn_ref[:, cs], vc_ref[:, cs]], axis=0)
        for hq in range(Q_PER_KV):
            h = hk * Q_PER_KV + hq
            qb = q_ref[:, (h // 2) * LANES:(h // 2 + 1) * LANES]
            qm = jnp.where(low_half if h % 2 == 0 else jnp.logical_not(low_half), qb, jnp.zeros_like(qb))
            s = lax.dot_general(qm, kcat, (((1,), (1,)), ((), ())), preferred_element_type=F32)
            s = jnp.where(valid, s, NEG_INF)
            snk = sink_ref[h]
            mx = jnp.maximum(jnp.max(s, axis=1, keepdims=True), snk)
            p = jnp.exp(s - mx)
            den = jnp.sum(p, axis=1, keepdims=True) + jnp.exp(snk - mx)
            o = _dot(p.astype(BF16), vcat) / den
            outs.append(o)
    blocks = [jnp.where(low_half, outs[2 * m], outs[2 * m + 1]) for m in range(N_Q_HEADS // 2)]
    o_ref[...] = jnp.concatenate(blocks, axis=1).astype(BF16)


def _attention(q, kd, vd, kcd, vcd, sink, *, bsz, seq_len, n_ctx):
    blk = WINDOW
    nblk = seq_len // blk
    aw = q.shape[1]
    kw = kd.shape[1]

    def qmap(b, n):
        return (b * nblk + n, 0)

    def pmap(b, n):
        return (b * nblk + jnp.maximum(n - 1, 0), 0)

    def nmap(b, n):
        return (b * nblk + jnp.minimum(n + 1, nblk - 1), 0)

    kv_spec = lambda f: pl.BlockSpec((blk, kw), f)
    return pl.pallas_call(
        functools.partial(_attn_kernel, nblk=nblk, blk=blk),
        grid=(bsz, nblk),
        in_specs=[
            pl.BlockSpec(memory_space=pltpu.SMEM),
            pl.BlockSpec((blk, aw), qmap),
            kv_spec(pmap), kv_spec(qmap), kv_spec(nmap),
            kv_spec(pmap), kv_spec(qmap), kv_spec(nmap),
            pl.BlockSpec((n_ctx, kw), lambda b, n: (b, 0)),
            pl.BlockSpec((n_ctx, kw), lambda b, n: (b, 0)),
        ],
        out_specs=pl.BlockSpec((blk, aw), qmap),
        out_shape=jax.ShapeDtypeStruct(q.shape, BF16),
        compiler_params=_cparams(("arbitrary", "arbitrary")),
        name="attn",
    )(sink, q, kd, kd, kd, vd, vd, vd, kcd, vcd)


def _gelu_tanh(x):
    return 0.5 * x * (1.0 + jnp.tanh(math.sqrt(2.0 / math.pi) * (x + 0.044715 * (x * x * x))))


def _merge_kernel(yf_ref, yb_ref, o_ref, gs_ref, ga_ref, x_ref, lng_ref, lnb_ref, g1_ref, sh2_ref, sc2_ref,
                  wglu_ref, bglu_ref, wso_ref, wao_ref, wo_ref, l1g_ref, l1b_ref, wr_ref, br_ref,
                  h1_ref, idx_ref, gate_ref, rank_ref, cnt_ref, cnt_scr, *, alpha, n_exp):
    i = pl.program_id(0)

    @pl.when(i == 0)
    def _():
        cnt_scr[...] = jnp.zeros_like(cnt_scr)

    nlb = yf_ref.shape[0]
    y = jnp.concatenate([yf_ref[lb] + yb_ref[lb] for lb in range(nlb)], axis=1)
    z = _gelu_tanh(y)
    z = z * _sigmoid(_dot(z.astype(BF16), wglu_ref[...]) + bglu_ref[...])
    m = (gs_ref[...].astype(F32) * _dot(z.astype(BF16), wso_ref[...])
         + ga_ref[...].astype(F32) * _dot(o_ref[...], wao_ref[...]))
    mix = _dot(m.astype(BF16), wo_ref[...])
    h = _layer_norm(x_ref[...], lng_ref[...], lnb_ref[...])
    h1 = _layer_norm(alpha * h + g1_ref[...] * mix, l1g_ref[...], l1b_ref[...])
    h1_ref[...] = h1
    xm = h1 * (1.0 + sc2_ref[...]) + sh2_ref[...]
    logits = jnp.dot(xm, wr_ref[...], preferred_element_type=F32, precision=HIGHEST) + br_ref[...]
    tm = logits.shape[0]
    lane = lax.broadcasted_iota(jnp.int32, (tm, n_exp), 1)
    work = logits
    vals, sels = [], []
    for _ in range(TOP_K):
        mx = jnp.max(work, axis=1, keepdims=True)
        sel = jnp.min(jnp.where(work == mx, lane, n_exp), axis=1, keepdims=True)
        vals.append(mx)
        sels.append(sel)
        work = jnp.where(lane == sel, -jnp.inf, work)
    exps = [jnp.exp(v - vals[0]) for v in vals]
    den = exps[0] + exps[1] + exps[2] + exps[3]
    onehot = jnp.zeros((tm, n_exp), F32)
    for sel in sels:
        onehot = onehot + (lane == sel).astype(F32)
    ri = lax.broadcasted_iota(jnp.int32, (tm, tm), 0)
    ci = lax.broadcasted_iota(jnp.int32, (tm, tm), 1)
    tri = jnp.where(ci < ri, 1.0, 0.0).astype(BF16)
    rank = _dot(tri, onehot.astype(BF16)) + cnt_scr[...]
    lane_k = lax.broadcasted_iota(jnp.int32, (tm, TOP_K), 1)
    idx_o = jnp.zeros((tm, TOP_K), jnp.int32)
    gate_o = jnp.zeros((tm, TOP_K), F32)
    rank_o = jnp.zeros((tm, TOP_K), jnp.int32)
    for k in range(TOP_K):
        rk = jnp.sum(jnp.where(lane == sels[k], rank, 0.0), axis=1, keepdims=True).astype(jnp.int32)
        idx_o = jnp.where(lane_k == k, sels[k], idx_o)
        gate_o = jnp.where(lane_k == k, exps[k] / den, gate_o)
        rank_o = jnp.where(lane_k == k, rk, rank_o)
    idx_ref[...] = idx_o
    gate_ref[...] = gate_o
    rank_ref[...] = rank_o
    cnt = cnt_scr[...] + jnp.sum(onehot, axis=0, keepdims=True)
    cnt_scr[...] = cnt
    cnt_ref[...] = cnt


def _merge(yf, yb, o_att, sgs, sga, x2, ln_g, ln_b, mod3, w_glu, b_glu, w_so, w_ao, w_o, l1g, l1b,
           w_r, b_r, *, rows_per_batch, tm, alpha):
    t, d = x2.shape
    nlb = yf.shape[0]
    sw = nlb * LANES
    n_exp = w_r.shape[1]
    tpb = rows_per_batch // tm
    row = lambda i: (i, 0)
    const = lambda i: (0, 0)

    def modspec(chunk):
        return pl.BlockSpec((None, 1, d), lambda i: (i // tpb, 0, chunk))

    return pl.pallas_call(
        functools.partial(_merge_kernel, alpha=alpha, n_exp=n_exp),
        grid=(t // tm,),
        in_specs=[
            pl.BlockSpec((nlb, tm, LANES), lambda i: (0, i, 0)),
            pl.BlockSpec((nlb, tm, LANES), lambda i: (0, i, 0)),
            pl.BlockSpec((tm, o_att.shape[1]), row),
            pl.BlockSpec((tm, d), row),
            pl.BlockSpec((tm, d), row),
            pl.BlockSpec((tm, d), row),
            pl.BlockSpec((1, d), const),
            pl.BlockSpec((1, d), const),
            modspec(2), modspec(3), modspec(4),
            pl.BlockSpec(w_glu.shape, const),
            pl.BlockSpec((1, sw), const),
            pl.BlockSpec(w_so.shape, const),
            pl.BlockSpec(w_ao.shape, const),
            pl.BlockSpec(w_o.shape, const),
            pl.BlockSpec((1, d), const),
            pl.BlockSpec((1, d), const),
            pl.BlockSpec(w_r.shape, const),
            pl.BlockSpec((1, n_exp), const),
        ],
        out_specs=[
            pl.BlockSpec((tm, d), row),
            pl.BlockSpec((tm, TOP_K), row),
            pl.BlockSpec((tm, TOP_K), row),
            pl.BlockSpec((tm, TOP_K), row),
            pl.BlockSpec((1, n_exp), const),
        ],
        out_shape=[
            jax.ShapeDtypeStruct((t, d), F32),
            jax.ShapeDtypeStruct((t, TOP_K), jnp.int32),
            jax.ShapeDtypeStruct((t, TOP_K), F32),
            jax.ShapeDtypeStruct((t, TOP_K), jnp.int32),
            jax.ShapeDtypeStruct((1, n_exp), F32),
        ],
        scratch_shapes=[pltpu.VMEM((1, n_exp), F32)],
        compiler_params=_cparams(("arbitrary",)),
        name="merge",
    )(yf, yb, o_att, sgs, sga, x2, ln_g, ln_b, mod3, mod3, mod3, w_glu, b_glu, w_so, w_ao, w_o,
      l1g, l1b, w_r, b_r)


def _row_tile(ref, r):
    return ref.at[pl.ds(pl.multiple_of(r * ROW_TILES, ROW_TILES), ROW_TILES)]


def _store_row_tiles(ref, val):
    rows = val.shape[0]
    for l in range(ROW_TILES):
        ref[pl.ds(l, rows, stride=ROW_TILES), :] = val[:, l * LANES:(l + 1) * LANES]


def _load_row_tiles(ref, row0, rows):
    return jnp.concatenate(
        [ref[pl.ds(row0 * ROW_TILES + l, rows, stride=ROW_TILES), :] for l in range(ROW_TILES)], axis=1)


def _row_copy_out(src, dst, pos_ref, sem, t, k):
    return pltpu.make_async_copy(_row_tile(src, t), _row_tile(dst, pos_ref[t * TOP_K + k]), sem)


def _dispatch_kernel(pos_ref, prev_pos_ref, h1_ref, sh2_ref, sc2_ref, xrows_ref, xm_scr, sems):
    i = pl.program_id(0)
    n = pl.num_programs(0)
    tm = h1_ref.shape[0]
    slot = i % 2
    xm = h1_ref[...] * (1.0 + sc2_ref[...]) + sh2_ref[...]
    _store_row_tiles(xm_scr.at[slot], xm)

    def copies(slot_, idx_ref, op):
        def body(t, carry):
            for k in range(TOP_K):
                op(_row_copy_out(xm_scr.at[slot_], xrows_ref, idx_ref, sems.at[slot_], t, k), k)
            return carry
        lax.fori_loop(0, tm, body, 0, unroll=DMA_LOOP_UNROLL)

    start = lambda cp, k: cp.start(priority=k % N_DMA_PRIORITIES)
    wait = lambda cp, k: cp.wait()
    for s in range(2):
        @pl.when(slot == s)
        def _():
            copies(s, pos_ref, start)

    for s in range(2):
        @pl.when((slot == 1 - s) & (i > 0))
        def _():
            copies(s, prev_pos_ref, wait)

        @pl.when((slot == s) & (i == n - 1))
        def _():
            copies(s, pos_ref, wait)


def _dispatch(pos_flat, h1, mod3, *, rows_per_batch, tm):
    t, d = h1.shape
    tpb = rows_per_batch // tm
    return pl.pallas_call(
        _dispatch_kernel,
        grid=(t // tm,),
        in_specs=[
            pl.BlockSpec((tm * TOP_K,), lambda i: (i,), memory_space=pltpu.SMEM),
            pl.BlockSpec((tm * TOP_K,), lambda i: (jnp.maximum(i - 1, 0),), memory_space=pltpu.SMEM),
            pl.BlockSpec((tm, d), lambda i: (i, 0)),
            pl.BlockSpec((None, 1, d), lambda i: (i // tpb, 0, 3)),
            pl.BlockSpec((None, 1, d), lambda i: (i // tpb, 0, 4)),
        ],
        out_specs=pl.BlockSpec(memory_space=pl.ANY),
        out_shape=jax.ShapeDtypeStruct((t * TOP_K * ROW_TILES, LANES), F32),
        scratch_shapes=[pltpu.VMEM((2, tm * ROW_TILES, LANES), F32), pltpu.SemaphoreType.DMA((2,))],
        compiler_params=_cparams(("arbitrary",)),
        name="dispatch",
    )(pos_flat, pos_flat, h1, mod3, mod3)


def _experts_kernel(tile_ref, exp_ref, lo_ref, hi_ref, x_ref, wgu_ref, bgu_ref, wd_ref, bd_ref, y_ref,
                    wgu_scr, wd_scr, *, rows):
    w = pl.program_id(0)
    prev = jnp.maximum(w - 1, 0)
    e_new = (w == 0) | (exp_ref[w] != exp_ref[prev])
    t_new = (w == 0) | (tile_ref[w] != tile_ref[prev])
    lo = lo_ref[w]
    hi = hi_ref[w]
    f = wd_ref.shape[0]

    @pl.when(e_new)
    def _():
        wgu_scr[...] = wgu_ref[...].astype(BF16)
        wd_scr[...] = wd_ref[...].astype(BF16)

    row0 = tile_ref[w] * rows
    whole = (lo <= row0) & (hi >= row0 + rows)

    @pl.when(t_new & jnp.logical_not(whole))
    def _():
        y_ref[...] = jnp.zeros_like(y_ref)

    @pl.when(hi > lo)
    def _():
        x = _load_row_tiles(x_ref, 0, rows)
        gu = _dot(x.astype(BF16), wgu_scr[...]) + bgu_ref[...]
        glu = jnp.minimum(gu[:, :f], SWIGLU_LIMIT)
        lin = jnp.clip(gu[:, f:], -SWIGLU_LIMIT, SWIGLU_LIMIT)
        act = glu * _sigmoid(SWIGLU_ALPHA * glu) * (lin + 1.0)
        y = _dot(act.astype(BF16), wd_scr[...]) + bd_ref[...]

        @pl.when(whole)
        def _():
            _store_row_tiles(y_ref, y)

        @pl.when(jnp.logical_not(whole))
        def _():
            r = row0 + lax.broadcasted_iota(jnp.int32, (rows, 1), 0)
            _store_row_tiles(y_ref, jnp.where((r >= lo) & (r < hi), y, _load_row_tiles(y_ref, 0, rows)))


def _experts(work, xrows, w_gu, b_gu, w_d, b_d, *, rows):
    tile_id, exp_id, lo, hi = work
    n_exp, d, f2 = w_gu.shape
    f = w_d.shape[1]
    grid_spec = pltpu.PrefetchScalarGridSpec(
        num_scalar_prefetch=4,
        grid=(tile_id.shape[0],),
        in_specs=[
            pl.BlockSpec((rows * ROW_TILES, LANES), lambda w, ti, ex, lo, hi: (ti[w], 0)),
            pl.BlockSpec((None, d, f2), lambda w, ti, ex, lo, hi: (ex[w], 0, 0)),
            pl.BlockSpec((None, 1, f2), lambda w, ti, ex, lo, hi: (ex[w], 0, 0)),
            pl.BlockSpec((None, f, d), lambda w, ti, ex, lo, hi: (ex[w], 0, 0)),
            pl.BlockSpec((None, 1, d), lambda w, ti, ex, lo, hi: (ex[w], 0, 0)),
        ],
        out_specs=pl.BlockSpec((rows * ROW_TILES, LANES), lambda w, ti, ex, lo, hi: (ti[w], 0)),
        scratch_shapes=[pltpu.VMEM((d, f2), BF16), pltpu.VMEM((f, d), BF16)],
    )
    return pl.pallas_call(
        functools.partial(_experts_kernel, rows=rows),
        grid_spec=grid_spec,
        out_shape=jax.ShapeDtypeStruct(xrows.shape, F32),
        compiler_params=_cparams(("arbitrary",)),
        name="experts",
    )(tile_id, exp_id, lo, hi, xrows, w_gu, b_gu.reshape(n_exp, 1, f2), w_d, b_d.reshape(n_exp, 1, d))


def _row_copy_in(src, dst, pos_ref, sem, t, k, tm):
    return pltpu.make_async_copy(_row_tile(src, pos_ref[t * TOP_K + k]), _row_tile(dst, k * tm + t), sem)


def _combine_kernel(pos_ref, next_pos_ref, h1_ref, gate_ref, g2_ref, lg_ref, lb_ref, yrows_ref, o_ref,
                    buf, sems, *, alpha):
    i = pl.program_id(0)
    n = pl.num_programs(0)
    tm = h1_ref.shape[0]
    slot = i % 2

    def copies(slot_, idx_ref, op):
        def body(t, carry):
            for k in range(TOP_K):
                op(_row_copy_in(yrows_ref, buf.at[slot_], idx_ref, sems.at[slot_], t, k, tm), k)
            return carry
        lax.fori_loop(0, tm, body, 0, unroll=DMA_LOOP_UNROLL)

    start = lambda cp, k: cp.start(priority=k % N_DMA_PRIORITIES)
    wait = lambda cp, k: cp.wait()

    @pl.when(i == 0)
    def _():
        copies(0, pos_ref, start)

    for s in range(2):
        @pl.when((slot == 1 - s) & (i + 1 < n))
        def _():
            copies(s, next_pos_ref, start)

    def reduce(s):
        copies(s, pos_ref, wait)
        gates = gate_ref[...]
        ffn = gates[:, 0:1] * _load_row_tiles(buf.at[s], 0, tm)
        for k in range(1, TOP_K):
            ffn = ffn + gates[:, k:k + 1] * _load_row_tiles(buf.at[s], k * tm, tm)
        o_ref[...] = _layer_norm(alpha * h1_ref[...] + g2_ref[...] * ffn, lg_ref[...], lb_ref[...])

    for s in range(2):
        @pl.when(slot == s)
        def _():
            reduce(s)


def _combine(pos_flat, h1, gates, mod3, l2g, l2b, yrows, *, rows_per_batch, tm, alpha):
    t, d = h1.shape
    tpb = rows_per_batch // tm
    n_tiles = t // tm
    return pl.pallas_call(
        functools.partial(_combine_kernel, alpha=alpha),
        grid=(n_tiles,),
        in_specs=[
            pl.BlockSpec((tm * TOP_K,), lambda i: (i,), memory_space=pltpu.SMEM),
            pl.BlockSpec((tm * TOP_K,), lambda i: (jnp.minimum(i + 1, n_tiles - 1),), memory_space=pltpu.SMEM),
            pl.BlockSpec((tm, d), lambda i: (i, 0)),
            pl.BlockSpec((tm, TOP_K), lambda i: (i, 0)),
            pl.BlockSpec((None, 1, d), lambda i: (i // tpb, 0, 5)),
            pl.BlockSpec((1, d), lambda i: (0, 0)),
            pl.BlockSpec((1, d), lambda i: (0, 0)),
            pl.BlockSpec(memory_space=pl.ANY),
        ],
        out_specs=pl.BlockSpec((tm, d), lambda i: (i, 0)),
        out_shape=jax.ShapeDtypeStruct((t, d), F32),
        scratch_shapes=[pltpu.VMEM((2, TOP_K * tm * ROW_TILES, LANES), F32), pltpu.SemaphoreType.DMA((2,))],
        compiler_params=_cparams(("arbitrary",)),
        name="combine",
    )(pos_flat, pos_flat, h1, gates, mod3, l2g, l2b, yrows)


def _work_list(counts, n_rows, rows):
    n_exp = counts.shape[0]
    n_tiles = n_rows // rows
    n_work = n_tiles + n_exp - 1
    def count_le(sorted_vals, q):
        return jnp.sum((sorted_vals[None, :] <= q[:, None]).astype(jnp.int32), axis=1)

    ends = jnp.cumsum(counts)
    starts = ends - counts
    tile_lo = jnp.arange(n_tiles, dtype=jnp.int32) * rows
    e_lo = count_le(ends, tile_lo)
    e_hi = count_le(ends, tile_lo + rows - 1)
    per_tile = e_hi - e_lo + 1
    off_end = jnp.cumsum(per_tile)
    off = off_end - per_tile
    w = jnp.arange(n_work, dtype=jnp.int32)
    tile = jnp.minimum(count_le(off_end, w), n_tiles - 1)
    exp = jnp.minimum(e_lo[tile] + (w - off[tile]), n_exp - 1).astype(jnp.int32)
    live = w < off_end[-1]
    lo = jnp.maximum(starts[exp], tile * rows)
    hi = jnp.minimum(ends[exp], (tile + 1) * rows)
    hi = jnp.where(live, hi, lo)
    return tile, exp, lo.astype(jnp.int32), hi.astype(jnp.int32), starts


def kernel(x, c, ctx, c_ctx, ln_in_g, ln_in_b, w_mod, b_mod, w_in, ssm_lam_re, ssm_lam_im, ssm_log_step, ssm_b_re, ssm_b_im, ssm_c_re, ssm_c_im, ssm_d, w_glu, b_glu, attn_sink, w_ssm_out, w_att_out, w_o, ln1_g, ln1_b, w_router, b_router, w_gate_up, b_gate_up, w_down, b_down, ln2_g, ln2_b):
    bsz, seq_len, d = x.shape
    n_ctx = ctx.shape[1]
    depth = w_mod.shape[0]
    assert depth == 1, "single-layer kernel"
    g_ssm, h_ssm = ssm_d.shape[1:]
    ssm_w = g_ssm * h_ssm
    attn_w = N_Q_HEADS * HEAD_DIM
    kv_w = N_KV_HEADS * HEAD_DIM
    kv2_w = 2 * kv_w
    nlb = ssm_w // LANES
    alpha = (2.0 * depth) ** 0.25
    t = bsz * seq_len
    assert bsz + 1 <= SUBLANES
    assert d == ROW_TILES * LANES and ROW_TILES == SUBLANES, "a token row must be exactly one (8, 128) tile"

    row2 = lambda a: a.reshape(1, -1)

    c_rows = jnp.concatenate([c, c_ctx[None], jnp.zeros((SUBLANES - bsz - 1, d), F32)], axis=0)
    mod = _mod_vectors(c_rows, w_mod[0], row2(b_mod[0]))
    mod3 = mod.reshape(SUBLANES, 1, 6 * d)

    wi = w_in[0]
    i0, i1 = ssm_w, ssm_w + attn_w
    i2, i3 = i1 + kv_w, i1 + 2 * kv_w
    i4 = i3 + d

    def dup_heads(wkv):
        parts = []
        for hk in range(N_KV_HEADS):
            blk = wkv[:, hk * HEAD_DIM:(hk + 1) * HEAD_DIM]
            parts += [blk, blk]
        return jnp.concatenate(parts, axis=1)

    w_s, w_q = wi[:, :i0], wi[:, i0:i1]
    w_k2, w_v2 = dup_heads(wi[:, i1:i2]), dup_heads(wi[:, i2:i3])
    w_lat = jnp.concatenate([w_s, w_q, w_k2, w_v2, wi[:, i3:i4], wi[:, i4:]], axis=1).astype(BF16)
    w_ctx = jnp.concatenate([w_s, w_k2, w_v2], axis=1).astype(BF16)

    pos = jnp.arange(seq_len)
    inv = ROPE_BASE ** (-jnp.arange(ROPE_PAIRS, dtype=F32) / ROPE_PAIRS)
    ang = jnp.concatenate([(pos // GRID_W).astype(F32)[:, None] * inv,
                           (pos % GRID_W).astype(F32)[:, None] * inv], axis=-1)
    cos_t = jnp.tile(jnp.cos(ang), (1, 2 * LANES // HEAD_DIM))
    sin_h = jnp.sin(ang)
    sin_t = jnp.tile(jnp.concatenate([-sin_h, sin_h], axis=-1), (1, LANES // HEAD_DIM))

    g_in, b_in = row2(ln_in_g), row2(ln_in_b)
    x2 = x.reshape(t, d)
    ctx2 = ctx.reshape(bsz * n_ctx, d)
    s_c, k_c, v_c = _inproj(ctx2, g_in, b_in, mod3, w_ctx, None, None, latent=False,
                            rows_per_batch=n_ctx, tm=n_ctx, ctx_mod_row=bsz,
                            ssm_w=ssm_w, attn_w=attn_w, kv2_w=kv2_w)
    s_l, q_l, k_l, v_l, sgs, sga = _inproj(x2, g_in, b_in, mod3, w_lat, cos_t, sin_t, latent=True,
                                           rows_per_batch=seq_len, tm=512, ctx_mod_row=None,
                                           ssm_w=ssm_w, attn_w=attn_w, kv2_w=kv2_w)

    wf = _s5_weights(ssm_lam_re[0, 0], ssm_lam_im[0, 0], ssm_log_step[0, 0], ssm_b_re[0, 0], ssm_b_im[0, 0],
                     ssm_c_re[0, 0], ssm_c_im[0, 0], reverse=False)
    wb = _s5_weights(ssm_lam_re[0, 1], ssm_lam_im[0, 1], ssm_log_step[0, 1], ssm_b_re[0, 1], ssm_b_im[0, 1],
                     ssm_c_re[0, 1], ssm_c_im[0, 1], reverse=True)
    width = S5_CHUNK * LANES
    d_tile = jnp.tile(ssm_d[0].astype(F32).reshape(nlb, 1, LANES), (1, 1, S5_CHUNK))
    sw2 = wf[1].shape[-1]
    zero_state = jnp.zeros((nlb, bsz, sw2), F32)
    uc4 = s_c.reshape(nlb, bsz, n_ctx // S5_CHUNK, width)
    ul4 = s_l.reshape(nlb, bsz, seq_len // S5_CHUNK, width)
    _, sf0 = _s5_scan(uc4, zero_state, wf, d_tile, reverse=False, cc=n_ctx // S5_CHUNK, add_skip=False)
    _, sb0 = _s5_scan(uc4, zero_state, wb, d_tile, reverse=True, cc=n_ctx // S5_CHUNK, add_skip=False)
    yf4, _ = _s5_scan(ul4, sf0, wf, d_tile, reverse=False, cc=128, add_skip=True)
    yb4, _ = _s5_scan(ul4, sb0, wb, d_tile, reverse=True, cc=128, add_skip=False)
    yf = yf4.reshape(nlb, t, LANES)
    yb = yb4.reshape(nlb, t, LANES)

    o_att = _attention(q_l, k_l, v_l, k_c, v_c, attn_sink[0].astype(F32), bsz=bsz, seq_len=seq_len, n_ctx=n_ctx)

    h1, top_i, gates, rank, counts = _merge(
        yf, yb, o_att, sgs, sga, x2, g_in, b_in, mod3, w_glu[0].astype(BF16), row2(b_glu[0]),
        w_ssm_out[0].astype(BF16), w_att_out[0].astype(BF16), w_o[0].astype(BF16), row2(ln1_g[0]), row2(ln1_b[0]),
        w_router[0], row2(b_router[0]), rows_per_batch=seq_len, tm=256, alpha=alpha)

    rows = 256
    n_rows = t * TOP_K
    tile_id, exp_id, lo, hi, starts = _work_list(counts[0].astype(jnp.int32), n_rows, rows)
    pos_flat = (starts[top_i] + rank).reshape(n_rows).astype(jnp.int32)
    xrows = _dispatch(pos_flat, h1, mod3, rows_per_batch=seq_len, tm=256)
    yrows = _experts((tile_id, exp_id, lo, hi), xrows, w_gate_up[0], b_gate_up[0], w_down[0], b_down[0], rows=rows)
    out = _combine(pos_flat, h1, gates, mod3, row2(ln2_g[0]), row2(ln2_b[0]), yrows,
                   rows_per_batch=seq_len, tm=256, alpha=alpha)
    return out.reshape(bsz, seq_len, d)
```

```python
import functools
import math

import jax
import jax.numpy as jnp
import numpy as np
from jax import lax
from jax.experimental import pallas as pl
from jax.experimental.pallas import tpu as pltpu

F32 = jnp.float32
BF16 = jnp.bfloat16
HIGHEST = lax.Precision.HIGHEST

HEAD_DIM = 64
N_Q_HEADS = 8
N_KV_HEADS = 2
Q_PER_KV = N_Q_HEADS // N_KV_HEADS
WINDOW = 128
GRID_W = 64
ROPE_BASE = 10000.0
ROPE_PAIRS = HEAD_DIM // 4
TOP_K = 4
SWIGLU_LIMIT = 7.0
SWIGLU_ALPHA = 1.702
LN_EPS = 1e-5
NEG_INF = -1e30

LANES = 128
SUBLANES = 8
ROW_TILES = 8
VMEM_LIMIT = 56 * 1024 * 1024
N_DMA_PRIORITIES = 2
DMA_LOOP_UNROLL = 8

S5_CHUNK = 8
S5_ROW_PAD = SUBLANES


def _cparams(sem):
    return pltpu.CompilerParams(dimension_semantics=sem, vmem_limit_bytes=VMEM_LIMIT)


def _sigmoid(x):
    return 1.0 / (1.0 + jnp.exp(-x))


def _layer_norm(x, g, b):
    mu = jnp.mean(x, axis=-1, keepdims=True)
    xc = x - mu
    var = jnp.mean(xc * xc, axis=-1, keepdims=True)
    return xc * lax.rsqrt(var + LN_EPS) * g + b


def _dot(a, b):
    return jnp.dot(a, b, preferred_element_type=F32)


def _mod_kernel(c_ref, w_ref, b_ref, o_ref):
    c = c_ref[...]
    a = c * _sigmoid(c)
    o_ref[...] = jnp.dot(a, w_ref[...], preferred_element_type=F32, precision=HIGHEST) + b_ref[...]


def _mod_vectors(c_rows, w_mod, b_mod):
    d = c_rows.shape[1]
    n = w_mod.shape[1]
    return pl.pallas_call(
        _mod_kernel,
        grid=(n // d,),
        in_specs=[
            pl.BlockSpec((SUBLANES, d), lambda i: (0, 0)),
            pl.BlockSpec((d, d), lambda i: (0, i)),
            pl.BlockSpec((1, d), lambda i: (0, i)),
        ],
        out_specs=pl.BlockSpec((SUBLANES, d), lambda i: (0, i)),
        out_shape=jax.ShapeDtypeStruct((SUBLANES, n), F32),
        compiler_params=_cparams(("arbitrary",)),
        name="mod",
    )(c_rows, w_mod, b_mod)


def _rope(t, cos, sin):
    n = t.shape[1]
    reps = n // LANES
    c = jnp.concatenate([cos] * reps, axis=1) if reps > 1 else cos
    s = jnp.concatenate([sin] * reps, axis=1) if reps > 1 else sin
    half = HEAD_DIM // 2
    upper = pltpu.roll(t, n - half, axis=1)
    lower = pltpu.roll(t, half, axis=1)
    lane = lax.broadcasted_iota(jnp.int32, t.shape, 1)
    partner = jnp.where((lane & half) == 0, upper, lower)
    return t * c + partner * s


def _inproj_kernel(*refs, latent, ssm_w, attn_w, kv2_w, d_model):
    if latent:
        (x_ref, g_ref, b_ref, sh_ref, sc_ref, w_ref, cos_ref, sin_ref,
         s_ref, q_ref, k_ref, v_ref, gs_ref, ga_ref) = refs
    else:
        x_ref, g_ref, b_ref, sh_ref, sc_ref, w_ref, s_ref, k_ref, v_ref = refs
    h = _layer_norm(x_ref[...], g_ref[...], b_ref[...])
    u = (h * (1.0 + sc_ref[...]) + sh_ref[...]).astype(BF16)
    col = 0
    s = _dot(u, w_ref[:, col:col + ssm_w])
    for lb in range(ssm_w // LANES):
        s_ref[lb] = s[:, lb * LANES:(lb + 1) * LANES]
    col += ssm_w
    if latent:
        q = _dot(u, w_ref[:, col:col + attn_w])
        q = _rope(q, cos_ref[...], sin_ref[...]) * (HEAD_DIM ** -0.5)
        q_ref[...] = q.astype(BF16)
        col += attn_w
    k = _dot(u, w_ref[:, col:col + kv2_w])
    if latent:
        k = _rope(k, cos_ref[...], sin_ref[...])
    k_ref[...] = k.astype(BF16)
    col += kv2_w
    v_ref[...] = _dot(u, w_ref[:, col:col + kv2_w]).astype(BF16)
    col += kv2_w
    if latent:
        gs_ref[...] = _sigmoid(_dot(u, w_ref[:, col:col + d_model])).astype(BF16)
        col += d_model
        ga_ref[...] = _sigmoid(_dot(u, w_ref[:, col:col + d_model])).astype(BF16)


def _inproj(x2, ln_g, ln_b, mod3, w_cat, cos_t, sin_t, *, latent, rows_per_batch, tm, ctx_mod_row,
            ssm_w, attn_w, kv2_w):
    t, d = x2.shape
    tiles_per_batch = rows_per_batch // tm
    nlb = ssm_w // LANES

    def brow(i):
        return i // tiles_per_batch if ctx_mod_row is None else ctx_mod_row

    in_specs = [
        pl.BlockSpec((tm, d), lambda i: (i, 0)),
        pl.BlockSpec((1, d), lambda i: (0, 0)),
        pl.BlockSpec((1, d), lambda i: (0, 0)),
        pl.BlockSpec((None, 1, d), lambda i: (brow(i), 0, 0)),
        pl.BlockSpec((None, 1, d), lambda i: (brow(i), 0, 1)),
        pl.BlockSpec(w_cat.shape, lambda i: (0, 0)),
    ]
    args = [x2, ln_g, ln_b, mod3, mod3, w_cat]
    out_specs = [pl.BlockSpec((nlb, tm, LANES), lambda i: (0, i, 0))]
    out_shape = [jax.ShapeDtypeStruct((nlb, t, LANES), F32)]
    if latent:
        in_specs += [
            pl.BlockSpec((tm, LANES), lambda i: (i % tiles_per_batch, 0)),
            pl.BlockSpec((tm, LANES), lambda i: (i % tiles_per_batch, 0)),
        ]
        args += [cos_t, sin_t]
        out_specs.append(pl.BlockSpec((tm, attn_w), lambda i: (i, 0)))
        out_shape.append(jax.ShapeDtypeStruct((t, attn_w), BF16))
    out_specs += [pl.BlockSpec((tm, kv2_w), lambda i: (i, 0))] * 2
    out_shape += [jax.ShapeDtypeStruct((t, kv2_w), BF16)] * 2
    if latent:
        out_specs += [pl.BlockSpec((tm, d), lambda i: (i, 0))] * 2
        out_shape += [jax.ShapeDtypeStruct((t, d), BF16)] * 2
    return pl.pallas_call(
        functools.partial(_inproj_kernel, latent=latent, ssm_w=ssm_w, attn_w=attn_w, kv2_w=kv2_w,
                          d_model=d),
        grid=(t // tm,),
        in_specs=in_specs,
        out_specs=out_specs,
        out_shape=out_shape,
        compiler_params=_cparams(("arbitrary",)),
        name="inproj_latent" if latent else "inproj_ctx",
    )(*args)


def _lane_expand(tbl, n_copies, rows_per_group, cols_per_group):
    r, w = tbl.shape
    wide_w = n_copies * w
    log2 = lambda v: int(v).bit_length() - 1
    assert all(v == 1 << log2(v) for v in (w, rows_per_group, cols_per_group))
    sel = ((lax.broadcasted_iota(jnp.int32, (w, wide_w), 1) & (w - 1))
           == lax.broadcasted_iota(jnp.int32, (w, wide_w), 0)).astype(F32)
    wide = jnp.dot(tbl, sel, preferred_element_type=F32, precision=HIGHEST)
    keep = ((lax.broadcasted_iota(jnp.int32, (r, wide_w), 0) >> log2(rows_per_group))
            == (lax.broadcasted_iota(jnp.int32, (r, wide_w), 1) >> log2(cols_per_group)))
    return jnp.where(keep, wide, 0.0)


def _s5_assemble_kernel(m_ref, ws_ref, wo_ref, kin_out, ws_out, wo_out, *, reverse):
    ch, _, hh = m_ref.shape
    p = ws_ref.shape[-1]
    gpb = LANES // hh
    sw = gpb * p
    bd = [_lane_expand(m_ref[j], gpb, hh, hh).astype(BF16) for j in range(ch)]
    zero = jnp.zeros((LANES, LANES), BF16)
    for k in range(ch):
        for i in range(ch):
            lag = (k - i) if reverse else (i - k)
            kin_out[k * LANES:(k + 1) * LANES, i * LANES:(i + 1) * LANES] = bd[lag] if lag >= 0 else zero
    for ri in range(2):
        for k in range(ch):
            ws_out[k * LANES:(k + 1) * LANES, ri * sw:(ri + 1) * sw] = (
                _lane_expand(ws_ref[ri, k], gpb, hh, p).astype(BF16))
        for i in range(ch):
            wo_out[ri * sw:(ri + 1) * sw, i * LANES:(i + 1) * LANES] = (
                _lane_expand(wo_ref[ri, i], gpb, p, hh).astype(BF16))


def _s5_assemble(m_tbl, ws_tbl, wo_tbl, *, reverse):
    ch, nlb, _, hh = m_tbl.shape
    p = ws_tbl.shape[-1]
    sw = (LANES // hh) * p
    width = ch * LANES
    return pl.pallas_call(
        functools.partial(_s5_assemble_kernel, reverse=reverse),
        grid=(nlb,),
        in_specs=[
            pl.BlockSpec((ch, None, LANES, hh), lambda l: (0, l, 0, 0)),
            pl.BlockSpec((2, ch, None, LANES, p), lambda l: (0, 0, l, 0, 0)),
            pl.BlockSpec((2, ch, None, sw, hh), lambda l: (0, 0, l, 0, 0)),
        ],
        out_specs=[
            pl.BlockSpec((None, width, width), lambda l: (l, 0, 0)),
            pl.BlockSpec((None, width, 2 * sw), lambda l: (l, 0, 0)),
            pl.BlockSpec((None, 2 * sw, width), lambda l: (l, 0, 0)),
        ],
        out_shape=[
            jax.ShapeDtypeStruct((nlb, width, width), BF16),
            jax.ShapeDtypeStruct((nlb, width, 2 * sw), BF16),
            jax.ShapeDtypeStruct((nlb, 2 * sw, width), BF16),
        ],
        compiler_params=_cparams(("arbitrary",)),
        name="s5_assemble_bwd" if reverse else "s5_assemble_fwd",
    )(m_tbl, ws_tbl, wo_tbl)


def _s5_weights(lam_re, lam_im, log_step, b_re, b_im, c_re, c_im, reverse):
    ch = S5_CHUNK
    g, p = lam_re.shape
    hh = b_re.shape[-1]
    gpb = LANES // hh
    nlb = g // gpb
    lr, li = lam_re.astype(F32), lam_im.astype(F32)
    dt = jnp.exp(log_step.astype(F32))[:, None]
    jj = jnp.arange(ch + 1, dtype=F32)[:, None, None]
    mag = jnp.exp(jj * lr * dt)
    pr = mag * jnp.cos(jj * li * dt)
    pi = mag * jnp.sin(jj * li * dt)
    ar, ai = pr[1], pi[1]
    den = lr * lr + li * li
    cr = ((ar - 1) * lr + ai * li) / den
    ci = (ai * lr - (ar - 1) * li) / den
    br, bi = b_re.astype(F32), b_im.astype(F32)
    bbr = cr[..., None] * br - ci[..., None] * bi
    bbi = cr[..., None] * bi + ci[..., None] * br
    ccr, cci = c_re.astype(F32), c_im.astype(F32)
    bbr_t = jnp.swapaxes(bbr, 1, 2)
    bbi_t = jnp.swapaxes(bbi, 1, 2)
    ccr_t = jnp.swapaxes(ccr, 1, 2)
    cci_t = jnp.swapaxes(cci, 1, 2)

    er = ccr[None] * pr[:ch, :, None, :] - cci[None] * pi[:ch, :, None, :]
    ei = ccr[None] * pi[:ch, :, None, :] + cci[None] * pr[:ch, :, None, :]
    m = jnp.sum(er[:, :, None, :, :] * bbr_t[None, :, :, None, :]
                - ei[:, :, None, :, :] * bbi_t[None, :, :, None, :], axis=-1)
    m_tbl = m.reshape(ch, nlb, LANES, hh)
    rk = np.arange(ch) if reverse else (ch - 1 - np.arange(ch))
    apr, api = pr[rk][:, :, None, :], pi[rk][:, :, None, :]
    sr = apr * bbr_t[None] - api * bbi_t[None]
    si = apr * bbi_t[None] + api * bbr_t[None]
    ws_tbl = jnp.stack([sr, si], axis=0).reshape(2, ch, nlb, LANES, p)
    ex = (ch - np.arange(ch)) if reverse else (np.arange(ch) + 1)
    epr, epi = pr[ex][:, :, :, None], pi[ex][:, :, :, None]
    wo_r = ccr_t[None] * epr - cci_t[None] * epi
    wo_i = -(ccr_t[None] * epi + cci_t[None] * epr)
    wo_tbl = jnp.stack([wo_r, wo_i], axis=0).reshape(2, ch, nlb, gpb * p, hh)
    kin, ws, wo = _s5_assemble(m_tbl, ws_tbl, wo_tbl, reverse=reverse)
    a_chunk = jnp.stack([pr[ch].reshape(nlb, gpb * p), pi[ch].reshape(nlb, gpb * p)], axis=1)
    return kin, ws, wo, a_chunk


def _s5_kernel(u_ref, s0_ref, a_ref, kin_ref, ws_ref, wo_ref, d_ref, y_ref, sfin_ref,
               z_scr, sin_scr, carry_scr, *, reverse, nb, cc, add_skip):
    j = pl.program_id(1)

    @pl.when(j == 0)
    def _():
        carry_scr[...] = s0_ref[...]

    ch = S5_CHUNK
    width = ch * LANES
    sw = a_ref.shape[-1]
    ns = sw // LANES
    u = jnp.concatenate(
        [jnp.concatenate([u_ref[b, pl.ds(k, cc, stride=ch), :] for b in range(nb)], axis=0)
         for k in range(ch)], axis=1)
    ub = u.astype(BF16)
    z = _dot(ub, ws_ref[...])
    ccp = cc + S5_ROW_PAD
    for s in range(2 * ns):
        for b in range(nb):
            z_scr[s, b * ccp:b * ccp + cc, :] = z[b * cc:(b + 1) * cc, s * LANES:(s + 1) * LANES]
    ar = [a_ref[0:1, s * LANES:(s + 1) * LANES] for s in range(ns)]
    ai = [a_ref[1:2, s * LANES:(s + 1) * LANES] for s in range(ns)]

    def body(c, st):
        cidx = (cc - 1 - c) if reverse else c
        rows = pl.ds(cidx, nb, stride=ccp)
        new = [None] * (2 * ns)
        for s in range(ns):
            sr, si = st[s], st[ns + s]
            sin_scr[s, rows, :] = sr
            sin_scr[ns + s, rows, :] = si
            new[s] = ar[s] * sr - ai[s] * si + z_scr[s, rows, :]
            new[ns + s] = ar[s] * si + ai[s] * sr + z_scr[ns + s, rows, :]
        return tuple(new)

    st0 = tuple(carry_scr[:, s * LANES:(s + 1) * LANES] for s in range(2 * ns))
    st = lax.fori_loop(0, cc, body, st0)
    s_fin = jnp.concatenate(st, axis=1)
    carry_scr[...] = s_fin
    sfin_ref[...] = s_fin
    s_in = jnp.concatenate(
        [jnp.concatenate([sin_scr[s, b * ccp:b * ccp + cc, :] for b in range(nb)], axis=0)
         for s in range(2 * ns)], axis=1)
    y = _dot(ub, kin_ref[...]) + _dot(s_in.astype(BF16), wo_ref[...])
    if add_skip:
        y = y + u * d_ref[...]
    for b in range(nb):
        for i in range(ch):
            y_ref[b, pl.ds(i, cc, stride=ch), :] = y[b * cc:(b + 1) * cc, i * LANES:(i + 1) * LANES]


def _s5_scan(u4, s0, weights, d_tile, *, reverse, cc, add_skip):
    kin, ws, wo, a_chunk = weights
    nlb, nb, n_steps, _ = u4.shape
    width = S5_CHUNK * LANES
    sw2 = ws.shape[-1]
    nj = n_steps // (cc * S5_CHUNK)

    def jm(j):
        return (nj - 1 - j) if reverse else j

    return pl.pallas_call(
        functools.partial(_s5_kernel, reverse=reverse, nb=nb, cc=cc, add_skip=add_skip),
        grid=(nlb, nj),
        in_specs=[
            pl.BlockSpec((None, nb, cc * S5_CHUNK, LANES), lambda l, j: (l, 0, jm(j), 0)),
            pl.BlockSpec((None, nb, sw2), lambda l, j: (l, 0, 0)),
            pl.BlockSpec((None, 2, sw2 // 2), lambda l, j: (l, 0, 0)),
            pl.BlockSpec((None, width, width), lambda l, j: (l, 0, 0)),
            pl.BlockSpec((None, width, sw2), lambda l, j: (l, 0, 0)),
            pl.BlockSpec((None, sw2, width), lambda l, j: (l, 0, 0)),
            pl.BlockSpec((None, 1, width), lambda l, j: (l, 0, 0)),
        ],
        out_specs=[
            pl.BlockSpec((None, nb, cc * S5_CHUNK, LANES), lambda l, j: (l, 0, jm(j), 0)),
            pl.BlockSpec((None, nb, sw2), lambda l, j: (l, 0, 0)),
        ],
        out_shape=[
            jax.ShapeDtypeStruct(u4.shape, F32),
            jax.ShapeDtypeStruct((nlb, nb, sw2), F32),
        ],
        scratch_shapes=[
            pltpu.VMEM((sw2 // LANES, nb * (cc + S5_ROW_PAD), LANES), F32),
            pltpu.VMEM((sw2 // LANES, nb * (cc + S5_ROW_PAD), LANES), F32),
            pltpu.VMEM((nb, sw2), F32),
        ],
        compiler_params=_cparams(("arbitrary", "arbitrary")),
        name="s5_bwd" if reverse else "s5_fwd",
    )(u4, s0, a_chunk, kin, ws, wo, d_tile)


def _attn_kernel(sink_ref, q_ref, kp_ref, ko_ref, kn_ref, vp_ref, vo_ref, vn_ref, kc_ref, vc_ref,
                 o_ref, *, nblk, blk):
    n = pl.program_id(1)
    n_ctx = kc_ref.shape[0]
    n_loc = 3 * blk
    qi = lax.broadcasted_iota(jnp.int32, (blk, n_loc + n_ctx), 0)
    kj = lax.broadcasted_iota(jnp.int32, (blk, n_loc + n_ctx), 1)
    rel = kj - qi
    lo = jnp.where(n == 0, blk, 0)
    hi = jnp.where(n == nblk - 1, 2 * blk, n_loc)
    valid = ((rel >= blk - WINDOW) & (rel <= blk + WINDOW) & (kj >= lo) & (kj < hi)) | (kj >= n_loc)
    lane = lax.broadcasted_iota(jnp.int32, (blk, LANES), 1)
    low_half = lane < HEAD_DIM
    outs = []
    for hk in range(N_KV_HEADS):
        cs = slice(hk * LANES, (hk + 1) * LANES)
        kcat = jnp.concatenate([kp_ref[:, cs], ko_ref[:, cs], kn_ref[:, cs], kc_ref[:, cs]], axis=0)
        vcat = jnp.concatenate([vp_ref[:, cs], vo_ref[:, cs], vn_ref[:, cs], vc_ref[:, cs]], axis=0)
        for hq in range(Q_PER_KV):
            h = hk * Q_PER_KV + hq
            qb = q_ref[:, (h // 2) * LANES:(h // 2 + 1) * LANES]
            qm = jnp.where(low_half if h % 2 == 0 else jnp.logical_not(low_half), qb, jnp.zeros_like(qb))
            s = lax.dot_general(qm, kcat, (((1,), (1,)), ((), ())), preferred_element_type=F32)
            s = jnp.where(valid, s, NEG_INF)
            snk = sink_ref[h]
            mx = jnp.maximum(jnp.max(s, axis=1, keepdims=True), snk)
            p = jnp.exp(s - mx)
            den = jnp.sum(p, axis=1, keepdims=True) + jnp.exp(snk - mx)
            o = _dot(p.astype(BF16), vcat) / den
            outs.append(o)
    blocks = [jnp.where(low_half, outs[2 * m], outs[2 * m + 1]) for m in range(N_Q_HEADS // 2)]
    o_ref[...] = jnp.concatenate(blocks, axis=1).astype(BF16)


def _attention(q, kd, vd, kcd, vcd, sink, *, bsz, seq_len, n_ctx):
    blk = WINDOW
    nblk = seq_len // blk
    aw = q.shape[1]
    kw = kd.shape[1]

    def qmap(b, n):
        return (b * nblk + n, 0)

    def pmap(b, n):
        return (b * nblk + jnp.maximum(n - 1, 0), 0)

    def nmap(b, n):
        return (b * nblk + jnp.minimum(n + 1, nblk - 1), 0)

    kv_spec = lambda f: pl.BlockSpec((blk, kw), f)
    return pl.pallas_call(
        functools.partial(_attn_kernel, nblk=nblk, blk=blk),
        grid=(bsz, nblk),
        in_specs=[
            pl.BlockSpec(memory_space=pltpu.SMEM),
            pl.BlockSpec((blk, aw), qmap),
            kv_spec(pmap), kv_spec(qmap), kv_spec(nmap),
            kv_spec(pmap), kv_spec(qmap), kv_spec(nmap),
            pl.BlockSpec((n_ctx, kw), lambda b, n: (b, 0)),
            pl.BlockSpec((n_ctx, kw), lambda b, n: (b, 0)),
        ],
        out_specs=pl.BlockSpec((blk, aw), qmap),
        out_shape=jax.ShapeDtypeStruct(q.shape, BF16),
        compiler_params=_cparams(("arbitrary", "arbitrary")),
        name="attn",
    )(sink, q, kd, kd, kd, vd, vd, vd, kcd, vcd)


def _gelu_tanh(x):
    return 0.5 * x * (1.0 + jnp.tanh(math.sqrt(2.0 / math.pi) * (x + 0.044715 * (x * x * x))))


def _flatten_slots(v):
    tm, nk = v.shape
    per_row = LANES // nk
    log2 = lambda x: int(x).bit_length() - 1
    assert nk == 1 << log2(nk) and tm % per_row == 0
    spread = ((lax.broadcasted_iota(jnp.int32, (nk, LANES), 1) & (nk - 1))
              == lax.broadcasted_iota(jnp.int32, (nk, LANES), 0)).astype(F32)
    wide = jnp.dot(v, spread, preferred_element_type=F32, precision=HIGHEST)
    t_id = lax.broadcasted_iota(jnp.int32, (tm, LANES), 0)
    c_id = lax.broadcasted_iota(jnp.int32, (tm, LANES), 1)
    wide = jnp.where((c_id >> log2(nk)) == (t_id & (per_row - 1)), wide, 0.0)
    group = ((lax.broadcasted_iota(jnp.int32, (tm // per_row, tm), 1) >> log2(per_row))
             == lax.broadcasted_iota(jnp.int32, (tm // per_row, tm), 0)).astype(F32)
    return jnp.dot(group, wide, preferred_element_type=F32, precision=HIGHEST)


def _merge_kernel(yf_ref, yb_ref, o_ref, gs_ref, ga_ref, x_ref, lng_ref, lnb_ref, g1_ref, sh2_ref, sc2_ref,
                  wglu_ref, bglu_ref, wso_ref, wao_ref, wo_ref, l1g_ref, l1b_ref, wr_ref, br_ref,
                  h1_ref, idx_ref, gate_ref, rank_ref, cnt_ref, cnt_scr, *, alpha, n_exp):
    i = pl.program_id(0)

    @pl.when(i == 0)
    def _():
        cnt_scr[...] = jnp.zeros_like(cnt_scr)

    nlb = yf_ref.shape[0]
    y = jnp.concatenate([yf_ref[lb] + yb_ref[lb] for lb in range(nlb)], axis=1)
    z = _gelu_tanh(y)
    z = z * _sigmoid(_dot(z.astype(BF16), wglu_ref[...]) + bglu_ref[...])
    m = (gs_ref[...].astype(F32) * _dot(z.astype(BF16), wso_ref[...])
         + ga_ref[...].astype(F32) * _dot(o_ref[...], wao_ref[...]))
    mix = _dot(m.astype(BF16), wo_ref[...])
    h = _layer_norm(x_ref[...], lng_ref[...], lnb_ref[...])
    h1 = _layer_norm(alpha * h + g1_ref[...] * mix, l1g_ref[...], l1b_ref[...])
    h1_ref[...] = h1
    xm = h1 * (1.0 + sc2_ref[...]) + sh2_ref[...]
    logits = jnp.dot(xm, wr_ref[...], preferred_element_type=F32, precision=HIGHEST) + br_ref[...]
    tm = logits.shape[0]
    lane = lax.broadcasted_iota(jnp.int32, (tm, n_exp), 1)
    work = logits
    vals, sels = [], []
    for _ in range(TOP_K):
        mx = jnp.max(work, axis=1, keepdims=True)
        sel = jnp.min(jnp.where(work == mx, lane, n_exp), axis=1, keepdims=True)
        vals.append(mx)
        sels.append(sel)
        work = jnp.where(lane == sel, -jnp.inf, work)
    exps = [jnp.exp(v - vals[0]) for v in vals]
    den = exps[0] + exps[1] + exps[2] + exps[3]
    onehot = jnp.zeros((tm, n_exp), F32)
    for sel in sels:
        onehot = onehot + (lane == sel).astype(F32)
    ri = lax.broadcasted_iota(jnp.int32, (tm, tm), 0)
    ci = lax.broadcasted_iota(jnp.int32, (tm, tm), 1)
    tri = jnp.where(ci < ri, 1.0, 0.0).astype(BF16)
    rank = _dot(tri, onehot.astype(BF16)) + cnt_scr[...]
    lane_k = lax.broadcasted_iota(jnp.int32, (tm, TOP_K), 1)
    idx_o = jnp.zeros((tm, TOP_K), F32)
    gate_o = jnp.zeros((tm, TOP_K), F32)
    rank_o = jnp.zeros((tm, TOP_K), F32)
    for k in range(TOP_K):
        rk = jnp.sum(jnp.where(lane == sels[k], rank, 0.0), axis=1, keepdims=True)
        idx_o = jnp.where(lane_k == k, sels[k].astype(F32), idx_o)
        gate_o = jnp.where(lane_k == k, exps[k] / den, gate_o)
        rank_o = jnp.where(lane_k == k, rk, rank_o)
    gate_ref[...] = gate_o
    idx_ref[...] = _flatten_slots(idx_o).astype(jnp.int32)
    rank_ref[...] = _flatten_slots(rank_o).astype(jnp.int32)
    cnt = cnt_scr[...] + jnp.sum(onehot, axis=0, keepdims=True)
    cnt_scr[...] = cnt
    cnt_ref[...] = cnt


def _merge(yf, yb, o_att, sgs, sga, x2, ln_g, ln_b, mod3, w_glu, b_glu, w_so, w_ao, w_o, l1g, l1b,
           w_r, b_r, *, rows_per_batch, tm, alpha):
    t, d = x2.shape
    nlb = yf.shape[0]
    sw = nlb * LANES
    n_exp = w_r.shape[1]
    tpb = rows_per_batch // tm
    row = lambda i: (i, 0)
    const = lambda i: (0, 0)

    def modspec(chunk):
        return pl.BlockSpec((None, 1, d), lambda i: (i // tpb, 0, chunk))

    return pl.pallas_call(
        functools.partial(_merge_kernel, alpha=alpha, n_exp=n_exp),
        grid=(t // tm,),
        in_specs=[
            pl.BlockSpec((nlb, tm, LANES), lambda i: (0, i, 0)),
            pl.BlockSpec((nlb, tm, LANES), lambda i: (0, i, 0)),
            pl.BlockSpec((tm, o_att.shape[1]), row),
            pl.BlockSpec((tm, d), row),
            pl.BlockSpec((tm, d), row),
            pl.BlockSpec((tm, d), row),
            pl.BlockSpec((1, d), const),
            pl.BlockSpec((1, d), const),
            modspec(2), modspec(3), modspec(4),
            pl.BlockSpec(w_glu.shape, const),
            pl.BlockSpec((1, sw), const),
            pl.BlockSpec(w_so.shape, const),
            pl.BlockSpec(w_ao.shape, const),
            pl.BlockSpec(w_o.shape, const),
            pl.BlockSpec((1, d), const),
            pl.BlockSpec((1, d), const),
            pl.BlockSpec(w_r.shape, const),
            pl.BlockSpec((1, n_exp), const),
        ],
        out_specs=[
            pl.BlockSpec((tm, d), row),
            pl.BlockSpec((tm * TOP_K // LANES, LANES), row),
            pl.BlockSpec((tm, TOP_K), row),
            pl.BlockSpec((tm * TOP_K // LANES, LANES), row),
            pl.BlockSpec((1, n_exp), const),
        ],
        out_shape=[
            jax.ShapeDtypeStruct((t, d), F32),
            jax.ShapeDtypeStruct((t * TOP_K // LANES, LANES), jnp.int32),
            jax.ShapeDtypeStruct((t, TOP_K), F32),
            jax.ShapeDtypeStruct((t * TOP_K // LANES, LANES), jnp.int32),
            jax.ShapeDtypeStruct((1, n_exp), F32),
        ],
        scratch_shapes=[pltpu.VMEM((1, n_exp), F32)],
        compiler_params=_cparams(("arbitrary",)),
        name="merge",
    )(yf, yb, o_att, sgs, sga, x2, ln_g, ln_b, mod3, mod3, mod3, w_glu, b_glu, w_so, w_ao, w_o,
      l1g, l1b, w_r, b_r)


def _row_tile(ref, r):
    return ref.at[pl.ds(pl.multiple_of(r * ROW_TILES, ROW_TILES), ROW_TILES)]


def _store_row_tiles(ref, val):
    rows = val.shape[0]
    for l in range(ROW_TILES):
        ref[pl.ds(l, rows, stride=ROW_TILES), :] = val[:, l * LANES:(l + 1) * LANES]


def _load_row_tiles(ref, row0, rows):
    return jnp.concatenate(
        [ref[pl.ds(row0 * ROW_TILES + l, rows, stride=ROW_TILES), :] for l in range(ROW_TILES)], axis=1)


def _row_copy_out(src, dst, pos_ref, sem, t, k):
    return pltpu.make_async_copy(_row_tile(src, t), _row_tile(dst, pos_ref[t * TOP_K + k]), sem)


def _dispatch_kernel(pos_ref, prev_pos_ref, h1_ref, sh2_ref, sc2_ref, xrows_ref, xm_scr, sems):
    i = pl.program_id(0)
    n = pl.num_programs(0)
    tm = h1_ref.shape[0]
    slot = i % 2
    xm = h1_ref[...] * (1.0 + sc2_ref[...]) + sh2_ref[...]
    _store_row_tiles(xm_scr.at[slot], xm)

    def copies(slot_, idx_ref, op):
        def body(t, carry):
            for k in range(TOP_K):
                op(_row_copy_out(xm_scr.at[slot_], xrows_ref, idx_ref, sems.at[slot_], t, k), k)
            return carry
        lax.fori_loop(0, tm, body, 0, unroll=DMA_LOOP_UNROLL)

    start = lambda cp, k: cp.start(priority=k % N_DMA_PRIORITIES)
    wait = lambda cp, k: cp.wait()
    for s in range(2):
        @pl.when(slot == s)
        def _():
            copies(s, pos_ref, start)

    for s in range(2):
        @pl.when((slot == 1 - s) & (i > 0))
        def _():
            copies(s, prev_pos_ref, wait)

        @pl.when((slot == s) & (i == n - 1))
        def _():
            copies(s, pos_ref, wait)


def _dispatch(pos_flat, h1, mod3, *, rows_per_batch, tm):
    t, d = h1.shape
    tpb = rows_per_batch // tm
    return pl.pallas_call(
        _dispatch_kernel,
        grid=(t // tm,),
        in_specs=[
            pl.BlockSpec((tm * TOP_K,), lambda i: (i,), memory_space=pltpu.SMEM),
            pl.BlockSpec((tm * TOP_K,), lambda i: (jnp.maximum(i - 1, 0),), memory_space=pltpu.SMEM),
            pl.BlockSpec((tm, d), lambda i: (i, 0)),
            pl.BlockSpec((None, 1, d), lambda i: (i // tpb, 0, 3)),
            pl.BlockSpec((None, 1, d), lambda i: (i // tpb, 0, 4)),
        ],
        out_specs=pl.BlockSpec(memory_space=pl.ANY),
        out_shape=jax.ShapeDtypeStruct((t * TOP_K * ROW_TILES, LANES), F32),
        scratch_shapes=[pltpu.VMEM((2, tm * ROW_TILES, LANES), F32), pltpu.SemaphoreType.DMA((2,))],
        compiler_params=_cparams(("arbitrary",)),
        name="dispatch",
    )(pos_flat, pos_flat, h1, mod3, mod3)


def _experts_kernel(tile_ref, exp_ref, lo_ref, hi_ref, nxt_ref, x_ref, wgu_hbm, bgu_ref, wd_hbm, bd_ref, y_ref,
                    wgu_f32, wd_f32, wgu_scr, wd_scr, sems, *, rows):
    w = pl.program_id(0)
    prev = jnp.maximum(w - 1, 0)
    e_new = (w == 0) | (exp_ref[w] != exp_ref[prev])
    t_new = (w == 0) | (tile_ref[w] != tile_ref[prev])
    lo = lo_ref[w]
    hi = hi_ref[w]
    f = wd_scr.shape[0]

    def weight_copies(e):
        return (pltpu.make_async_copy(wgu_hbm.at[e], wgu_f32, sems.at[0]),
                pltpu.make_async_copy(wd_hbm.at[e], wd_f32, sems.at[1]))

    @pl.when(w == 0)
    def _():
        for cp in weight_copies(exp_ref[0]):
            cp.start()

    @pl.when(e_new)
    def _():
        for cp in weight_copies(exp_ref[w]):
            cp.wait()
        wgu_scr[...] = wgu_f32[...].astype(BF16)
        wd_scr[...] = wd_f32[...].astype(BF16)

        @pl.when(nxt_ref[w] >= 0)
        def _():
            for cp in weight_copies(nxt_ref[w]):
                cp.start()

    row0 = tile_ref[w] * rows
    whole = (lo <= row0) & (hi >= row0 + rows)

    @pl.when(t_new & jnp.logical_not(whole))
    def _():
        y_ref[...] = jnp.zeros_like(y_ref)

    @pl.when(hi > lo)
    def _():
        x = _load_row_tiles(x_ref, 0, rows)
        gu = _dot(x.astype(BF16), wgu_scr[...]) + bgu_ref[...]
        glu = jnp.minimum(gu[:, :f], SWIGLU_LIMIT)
        lin = jnp.clip(gu[:, f:], -SWIGLU_LIMIT, SWIGLU_LIMIT)
        act = glu * _sigmoid(SWIGLU_ALPHA * glu) * (lin + 1.0)
        y = _dot(act.astype(BF16), wd_scr[...]) + bd_ref[...]

        @pl.when(whole)
        def _():
            _store_row_tiles(y_ref, y)

        @pl.when(jnp.logical_not(whole))
        def _():
            r = row0 + lax.broadcasted_iota(jnp.int32, (rows, 1), 0)
            _store_row_tiles(y_ref, jnp.where((r >= lo) & (r < hi), y, _load_row_tiles(y_ref, 0, rows)))


def _experts(work, xrows, w_gu, b_gu, w_d, b_d, *, rows):
    tile_id, exp_id, lo, hi, nxt = work
    n_exp, d, f2 = w_gu.shape
    f = w_d.shape[1]
    grid_spec = pltpu.PrefetchScalarGridSpec(
        num_scalar_prefetch=5,
        grid=(tile_id.shape[0],),
        in_specs=[
            pl.BlockSpec((rows * ROW_TILES, LANES), lambda w, ti, ex, lo, hi, nx: (ti[w], 0)),
            pl.BlockSpec(memory_space=pl.ANY),
            pl.BlockSpec((None, 1, f2), lambda w, ti, ex, lo, hi, nx: (ex[w], 0, 0)),
            pl.BlockSpec(memory_space=pl.ANY),
            pl.BlockSpec((None, 1, d), lambda w, ti, ex, lo, hi, nx: (ex[w], 0, 0)),
        ],
        out_specs=pl.BlockSpec((rows * ROW_TILES, LANES), lambda w, ti, ex, lo, hi, nx: (ti[w], 0)),
        scratch_shapes=[
            pltpu.VMEM((d, f2), F32), pltpu.VMEM((f, d), F32),
            pltpu.VMEM((d, f2), BF16), pltpu.VMEM((f, d), BF16),
            pltpu.SemaphoreType.DMA((2,)),
        ],
    )
    return pl.pallas_call(
        functools.partial(_experts_kernel, rows=rows),
        grid_spec=grid_spec,
        out_shape=jax.ShapeDtypeStruct(xrows.shape, F32),
        compiler_params=_cparams(("arbitrary",)),
        name="experts",
    )(tile_id, exp_id, lo, hi, nxt, xrows, w_gu, b_gu.reshape(n_exp, 1, f2), w_d, b_d.reshape(n_exp, 1, d))


def _row_copy_in(src, dst, pos_ref, sem, t, k, tm):
    return pltpu.make_async_copy(_row_tile(src, pos_ref[t * TOP_K + k]), _row_tile(dst, k * tm + t), sem)


def _combine_kernel(pos_ref, next_pos_ref, h1_ref, gate_ref, g2_ref, lg_ref, lb_ref, yrows_ref, o_ref,
                    buf, sems, *, alpha):
    i = pl.program_id(0)
    n = pl.num_programs(0)
    tm = h1_ref.shape[0]
    slot = i % 2

    def copies(slot_, idx_ref, op):
        def body(t, carry):
            for k in range(TOP_K):
                op(_row_copy_in(yrows_ref, buf.at[slot_], idx_ref, sems.at[slot_], t, k, tm), k)
            return carry
        lax.fori_loop(0, tm, body, 0, unroll=DMA_LOOP_UNROLL)

    start = lambda cp, k: cp.start(priority=k % N_DMA_PRIORITIES)
    wait = lambda cp, k: cp.wait()

    @pl.when(i == 0)
    def _():
        copies(0, pos_ref, start)

    for s in range(2):
        @pl.when((slot == 1 - s) & (i + 1 < n))
        def _():
            copies(s, next_pos_ref, start)

    def reduce(s):
        copies(s, pos_ref, wait)
        gates = gate_ref[...]
        ffn = gates[:, 0:1] * _load_row_tiles(buf.at[s], 0, tm)
        for k in range(1, TOP_K):
            ffn = ffn + gates[:, k:k + 1] * _load_row_tiles(buf.at[s], k * tm, tm)
        o_ref[...] = _layer_norm(alpha * h1_ref[...] + g2_ref[...] * ffn, lg_ref[...], lb_ref[...])

    for s in range(2):
        @pl.when(slot == s)
        def _():
            reduce(s)


def _combine(pos_flat, h1, gates, mod3, l2g, l2b, yrows, *, rows_per_batch, tm, alpha):
    t, d = h1.shape
    tpb = rows_per_batch // tm
    n_tiles = t // tm
    return pl.pallas_call(
        functools.partial(_combine_kernel, alpha=alpha),
        grid=(n_tiles,),
        in_specs=[
            pl.BlockSpec((tm * TOP_K,), lambda i: (i,), memory_space=pltpu.SMEM),
            pl.BlockSpec((tm * TOP_K,), lambda i: (jnp.minimum(i + 1, n_tiles - 1),), memory_space=pltpu.SMEM),
            pl.BlockSpec((tm, d), lambda i: (i, 0)),
            pl.BlockSpec((tm, TOP_K), lambda i: (i, 0)),
            pl.BlockSpec((None, 1, d), lambda i: (i // tpb, 0, 5)),
            pl.BlockSpec((1, d), lambda i: (0, 0)),
            pl.BlockSpec((1, d), lambda i: (0, 0)),
            pl.BlockSpec(memory_space=pl.ANY),
        ],
        out_specs=pl.BlockSpec((tm, d), lambda i: (i, 0)),
        out_shape=jax.ShapeDtypeStruct((t, d), F32),
        scratch_shapes=[pltpu.VMEM((2, TOP_K * tm * ROW_TILES, LANES), F32), pltpu.SemaphoreType.DMA((2,))],
        compiler_params=_cparams(("arbitrary",)),
        name="combine",
    )(pos_flat, pos_flat, h1, gates, mod3, l2g, l2b, yrows)


def _work_list(counts, n_rows, rows):
    n_exp = counts.shape[0]
    n_tiles = n_rows // rows
    n_work = n_tiles + n_exp - 1
    def count_le(sorted_vals, q):
        return jnp.sum((sorted_vals[None, :] <= q[:, None]).astype(jnp.int32), axis=1)

    ends = jnp.cumsum(counts)
    starts = ends - counts
    tile_lo = jnp.arange(n_tiles, dtype=jnp.int32) * rows
    e_lo = count_le(ends, tile_lo)
    e_hi = count_le(ends, tile_lo + rows - 1)
    per_tile = e_hi - e_lo + 1
    off_end = jnp.cumsum(per_tile)
    off = off_end - per_tile
    w = jnp.arange(n_work, dtype=jnp.int32)
    tile = jnp.minimum(count_le(off_end, w), n_tiles - 1)
    exp = jnp.minimum(e_lo[tile] + (w - off[tile]), n_exp - 1).astype(jnp.int32)
    live = w < off_end[-1]
    lo = jnp.maximum(starts[exp], tile * rows)
    hi = jnp.minimum(ends[exp], (tile + 1) * rows)
    hi = jnp.where(live, hi, lo)
    ids = jnp.arange(n_exp, dtype=jnp.int32)
    later = (ids[None, :] > ids[:, None]) & (counts[None, :] > 0)
    nxt_e = jnp.min(jnp.where(later, ids[None, :], n_exp), axis=1)
    nxt_e = jnp.where(nxt_e < n_exp, nxt_e, -1).astype(jnp.int32)
    return tile, exp, lo.astype(jnp.int32), hi.astype(jnp.int32), nxt_e[exp], starts


def kernel(x, c, ctx, c_ctx, ln_in_g, ln_in_b, w_mod, b_mod, w_in, ssm_lam_re, ssm_lam_im, ssm_log_step, ssm_b_re, ssm_b_im, ssm_c_re, ssm_c_im, ssm_d, w_glu, b_glu, attn_sink, w_ssm_out, w_att_out, w_o, ln1_g, ln1_b, w_router, b_router, w_gate_up, b_gate_up, w_down, b_down, ln2_g, ln2_b):
    bsz, seq_len, d = x.shape
    n_ctx = ctx.shape[1]
    depth = w_mod.shape[0]
    assert depth == 1, "single-layer kernel"
    g_ssm, h_ssm = ssm_d.shape[1:]
    ssm_w = g_ssm * h_ssm
    attn_w = N_Q_HEADS * HEAD_DIM
    kv_w = N_KV_HEADS * HEAD_DIM
    kv2_w = 2 * kv_w
    nlb = ssm_w // LANES
    alpha = (2.0 * depth) ** 0.25
    t = bsz * seq_len
    assert bsz + 1 <= SUBLANES
    assert d == ROW_TILES * LANES and ROW_TILES == SUBLANES, "a token row must be exactly one (8, 128) tile"

    row2 = lambda a: a.reshape(1, -1)

    c_rows = jnp.concatenate([c, c_ctx[None], jnp.zeros((SUBLANES - bsz - 1, d), F32)], axis=0)
    mod = _mod_vectors(c_rows, w_mod[0], row2(b_mod[0]))
    mod3 = mod.reshape(SUBLANES, 1, 6 * d)

    wi = w_in[0]
    i0, i1 = ssm_w, ssm_w + attn_w
    i2, i3 = i1 + kv_w, i1 + 2 * kv_w
    i4 = i3 + d

    def dup_heads(wkv):
        parts = []
        for hk in range(N_KV_HEADS):
            blk = wkv[:, hk * HEAD_DIM:(hk + 1) * HEAD_DIM]
            parts += [blk, blk]
        return jnp.concatenate(parts, axis=1)

    w_s, w_q = wi[:, :i0], wi[:, i0:i1]
    w_k2, w_v2 = dup_heads(wi[:, i1:i2]), dup_heads(wi[:, i2:i3])
    w_lat = jnp.concatenate([w_s, w_q, w_k2, w_v2, wi[:, i3:i4], wi[:, i4:]], axis=1).astype(BF16)
    w_ctx = jnp.concatenate([w_s, w_k2, w_v2], axis=1).astype(BF16)

    pos = jnp.arange(seq_len)
    inv = ROPE_BASE ** (-jnp.arange(ROPE_PAIRS, dtype=F32) / ROPE_PAIRS)
    ang = jnp.concatenate([(pos // GRID_W).astype(F32)[:, None] * inv,
                           (pos % GRID_W).astype(F32)[:, None] * inv], axis=-1)
    cos_t = jnp.tile(jnp.cos(ang), (1, 2 * LANES // HEAD_DIM))
    sin_h = jnp.sin(ang)
    sin_t = jnp.tile(jnp.concatenate([-sin_h, sin_h], axis=-1), (1, LANES // HEAD_DIM))

    g_in, b_in = row2(ln_in_g), row2(ln_in_b)
    x2 = x.reshape(t, d)
    ctx2 = ctx.reshape(bsz * n_ctx, d)
    s_c, k_c, v_c = _inproj(ctx2, g_in, b_in, mod3, w_ctx, None, None, latent=False,
                            rows_per_batch=n_ctx, tm=n_ctx, ctx_mod_row=bsz,
                            ssm_w=ssm_w, attn_w=attn_w, kv2_w=kv2_w)
    s_l, q_l, k_l, v_l, sgs, sga = _inproj(x2, g_in, b_in, mod3, w_lat, cos_t, sin_t, latent=True,
                                           rows_per_batch=seq_len, tm=512, ctx_mod_row=None,
                                           ssm_w=ssm_w, attn_w=attn_w, kv2_w=kv2_w)

    wf = _s5_weights(ssm_lam_re[0, 0], ssm_lam_im[0, 0], ssm_log_step[0, 0], ssm_b_re[0, 0], ssm_b_im[0, 0],
                     ssm_c_re[0, 0], ssm_c_im[0, 0], reverse=False)
    wb = _s5_weights(ssm_lam_re[0, 1], ssm_lam_im[0, 1], ssm_log_step[0, 1], ssm_b_re[0, 1], ssm_b_im[0, 1],
                     ssm_c_re[0, 1], ssm_c_im[0, 1], reverse=True)
    d_tile = jnp.tile(ssm_d[0].astype(F32).reshape(nlb, 1, LANES), (1, 1, S5_CHUNK))
    sw2 = wf[1].shape[-1]
    zero_state = jnp.zeros((nlb, bsz, sw2), F32)
    uc4 = s_c.reshape(nlb, bsz, n_ctx, LANES)
    ul4 = s_l.reshape(nlb, bsz, seq_len, LANES)
    _, sf0 = _s5_scan(uc4, zero_state, wf, d_tile, reverse=False, cc=n_ctx // S5_CHUNK, add_skip=False)
    _, sb0 = _s5_scan(uc4, zero_state, wb, d_tile, reverse=True, cc=n_ctx // S5_CHUNK, add_skip=False)
    yf4, _ = _s5_scan(ul4, sf0, wf, d_tile, reverse=False, cc=128, add_skip=True)
    yb4, _ = _s5_scan(ul4, sb0, wb, d_tile, reverse=True, cc=128, add_skip=False)
    yf = yf4.reshape(nlb, t, LANES)
    yb = yb4.reshape(nlb, t, LANES)

    o_att = _attention(q_l, k_l, v_l, k_c, v_c, attn_sink[0].astype(F32), bsz=bsz, seq_len=seq_len, n_ctx=n_ctx)

    h1, top_i, gates, rank, counts = _merge(
        yf, yb, o_att, sgs, sga, x2, g_in, b_in, mod3, w_glu[0].astype(BF16), row2(b_glu[0]),
        w_ssm_out[0].astype(BF16), w_att_out[0].astype(BF16), w_o[0].astype(BF16), row2(ln1_g[0]), row2(ln1_b[0]),
        w_router[0], row2(b_router[0]), rows_per_batch=seq_len, tm=256, alpha=alpha)

    rows = 256
    n_rows = t * TOP_K
    tile_id, exp_id, lo, hi, nxt, starts = _work_list(counts[0].astype(jnp.int32), n_rows, rows)
    pos_flat = (starts[top_i] + rank).reshape(n_rows).astype(jnp.int32)
    xrows = _dispatch(pos_flat, h1, mod3, rows_per_batch=seq_len, tm=256)
    yrows = _experts((tile_id, exp_id, lo, hi, nxt), xrows, w_gate_up[0], b_gate_up[0], w_down[0], b_down[0],
                     rows=rows)
    out = _combine(pos_flat, h1, gates, mod3, row2(ln2_g[0]), row2(ln2_b[0]), yrows,
                   rows_per_batch=seq_len, tm=256, alpha=alpha)
    return out.reshape(bsz, seq_len, d)
```

```python
import functools
import math

import jax
import jax.numpy as jnp
import numpy as np
from jax import lax
from jax.experimental import pallas as pl
from jax.experimental.pallas import tpu as pltpu

F32 = jnp.float32
BF16 = jnp.bfloat16
HIGHEST = lax.Precision.HIGHEST

HEAD_DIM = 64
N_Q_HEADS = 8
N_KV_HEADS = 2
Q_PER_KV = N_Q_HEADS // N_KV_HEADS
WINDOW = 128
GRID_W = 64
ROPE_BASE = 10000.0
ROPE_PAIRS = HEAD_DIM // 4
TOP_K = 4
SWIGLU_LIMIT = 7.0
SWIGLU_ALPHA = 1.702
LN_EPS = 1e-5
NEG_INF = -1e30

LANES = 128
SUBLANES = 8
ROW_TILES = 8
VMEM_LIMIT = 56 * 1024 * 1024
N_DMA_PRIORITIES = 2
DMA_LOOP_UNROLL = 8

S5_CHUNK = 8
S5_ROW_PAD = SUBLANES


def _cparams(sem):
    return pltpu.CompilerParams(dimension_semantics=sem, vmem_limit_bytes=VMEM_LIMIT)


def _sigmoid(x):
    return 1.0 / (1.0 + jnp.exp(-x))


def _layer_norm(x, g, b):
    mu = jnp.mean(x, axis=-1, keepdims=True)
    xc = x - mu
    var = jnp.mean(xc * xc, axis=-1, keepdims=True)
    return xc * lax.rsqrt(var + LN_EPS) * g + b


def _dot(a, b):
    return jnp.dot(a, b, preferred_element_type=F32)


def _mod_kernel(c_ref, w_ref, b_ref, o_ref):
    c = c_ref[...]
    a = c * _sigmoid(c)
    o_ref[...] = jnp.dot(a, w_ref[...], preferred_element_type=F32, precision=HIGHEST) + b_ref[...]


def _mod_vectors(c_rows, w_mod, b_mod):
    d = c_rows.shape[1]
    n = w_mod.shape[1]
    return pl.pallas_call(
        _mod_kernel,
        grid=(n // d,),
        in_specs=[
            pl.BlockSpec((SUBLANES, d), lambda i: (0, 0)),
            pl.BlockSpec((d, d), lambda i: (0, i)),
            pl.BlockSpec((1, d), lambda i: (0, i)),
        ],
        out_specs=pl.BlockSpec((SUBLANES, d), lambda i: (0, i)),
        out_shape=jax.ShapeDtypeStruct((SUBLANES, n), F32),
        compiler_params=_cparams(("arbitrary",)),
        name="mod",
    )(c_rows, w_mod, b_mod)


def _rope(t, cos, sin):
    n = t.shape[1]
    reps = n // LANES
    c = jnp.concatenate([cos] * reps, axis=1) if reps > 1 else cos
    s = jnp.concatenate([sin] * reps, axis=1) if reps > 1 else sin
    half = HEAD_DIM // 2
    upper = pltpu.roll(t, n - half, axis=1)
    lower = pltpu.roll(t, half, axis=1)
    lane = lax.broadcasted_iota(jnp.int32, t.shape, 1)
    partner = jnp.where((lane & half) == 0, upper, lower)
    return t * c + partner * s


def _inproj_kernel(*refs, latent, ssm_w, attn_w, kv2_w, d_model):
    if latent:
        (x_ref, g_ref, b_ref, sh_ref, sc_ref, w_ref, cos_ref, sin_ref,
         s_ref, q_ref, k_ref, v_ref, gs_ref, ga_ref) = refs
    else:
        x_ref, g_ref, b_ref, sh_ref, sc_ref, w_ref, s_ref, k_ref, v_ref = refs
    h = _layer_norm(x_ref[...], g_ref[...], b_ref[...])
    u = (h * (1.0 + sc_ref[...]) + sh_ref[...]).astype(BF16)
    col = 0
    s = _dot(u, w_ref[:, col:col + ssm_w])
    for lb in range(ssm_w // LANES):
        s_ref[lb] = s[:, lb * LANES:(lb + 1) * LANES]
    col += ssm_w
    if latent:
        q = _dot(u, w_ref[:, col:col + attn_w])
        q = _rope(q, cos_ref[...], sin_ref[...]) * (HEAD_DIM ** -0.5)
        q_ref[...] = q.astype(BF16)
        col += attn_w
    k = _dot(u, w_ref[:, col:col + kv2_w])
    if latent:
        k = _rope(k, cos_ref[...], sin_ref[...])
    k_ref[...] = k.astype(BF16)
    col += kv2_w
    v_ref[...] = _dot(u, w_ref[:, col:col + kv2_w]).astype(BF16)
    col += kv2_w
    if latent:
        gs_ref[...] = _sigmoid(_dot(u, w_ref[:, col:col + d_model])).astype(BF16)
        col += d_model
        ga_ref[...] = _sigmoid(_dot(u, w_ref[:, col:col + d_model])).astype(BF16)


def _inproj(x2, ln_g, ln_b, mod3, w_cat, cos_t, sin_t, *, latent, rows_per_batch, tm, ctx_mod_row,
            ssm_w, attn_w, kv2_w):
    t, d = x2.shape
    tiles_per_batch = rows_per_batch // tm
    nlb = ssm_w // LANES

    def brow(i):
        return i // tiles_per_batch if ctx_mod_row is None else ctx_mod_row

    in_specs = [
        pl.BlockSpec((tm, d), lambda i: (i, 0)),
        pl.BlockSpec((1, d), lambda i: (0, 0)),
        pl.BlockSpec((1, d), lambda i: (0, 0)),
        pl.BlockSpec((None, 1, d), lambda i: (brow(i), 0, 0)),
        pl.BlockSpec((None, 1, d), lambda i: (brow(i), 0, 1)),
        pl.BlockSpec(w_cat.shape, lambda i: (0, 0)),
    ]
    args = [x2, ln_g, ln_b, mod3, mod3, w_cat]
    out_specs = [pl.BlockSpec((nlb, tm, LANES), lambda i: (0, i, 0))]
    out_shape = [jax.ShapeDtypeStruct((nlb, t, LANES), F32)]
    if latent:
        in_specs += [
            pl.BlockSpec((tm, LANES), lambda i: (i % tiles_per_batch, 0)),
            pl.BlockSpec((tm, LANES), lambda i: (i % tiles_per_batch, 0)),
        ]
        args += [cos_t, sin_t]
        out_specs.append(pl.BlockSpec((tm, attn_w), lambda i: (i, 0)))
        out_shape.append(jax.ShapeDtypeStruct((t, attn_w), BF16))
    out_specs += [pl.BlockSpec((tm, kv2_w), lambda i: (i, 0))] * 2
    out_shape += [jax.ShapeDtypeStruct((t, kv2_w), BF16)] * 2
    if latent:
        out_specs += [pl.BlockSpec((tm, d), lambda i: (i, 0))] * 2
        out_shape += [jax.ShapeDtypeStruct((t, d), BF16)] * 2
    return pl.pallas_call(
        functools.partial(_inproj_kernel, latent=latent, ssm_w=ssm_w, attn_w=attn_w, kv2_w=kv2_w,
                          d_model=d),
        grid=(t // tm,),
        in_specs=in_specs,
        out_specs=out_specs,
        out_shape=out_shape,
        compiler_params=_cparams(("arbitrary",)),
        name="inproj_latent" if latent else "inproj_ctx",
    )(*args)


def _lane_expand(tbl, n_copies, rows_per_group, cols_per_group):
    r, w = tbl.shape
    wide_w = n_copies * w
    log2 = lambda v: int(v).bit_length() - 1
    assert all(v == 1 << log2(v) for v in (w, rows_per_group, cols_per_group))
    sel = ((lax.broadcasted_iota(jnp.int32, (w, wide_w), 1) & (w - 1))
           == lax.broadcasted_iota(jnp.int32, (w, wide_w), 0)).astype(F32)
    wide = jnp.dot(tbl, sel, preferred_element_type=F32, precision=HIGHEST)
    keep = ((lax.broadcasted_iota(jnp.int32, (r, wide_w), 0) >> log2(rows_per_group))
            == (lax.broadcasted_iota(jnp.int32, (r, wide_w), 1) >> log2(cols_per_group)))
    return jnp.where(keep, wide, 0.0)


def _s5_assemble_kernel(m_ref, ws_ref, wo_ref, kin_out, ws_out, wo_out, *, reverse):
    ch, _, hh = m_ref.shape
    p = ws_ref.shape[-1]
    gpb = LANES // hh
    sw = gpb * p
    bd = [_lane_expand(m_ref[j], gpb, hh, hh).astype(BF16) for j in range(ch)]
    zero = jnp.zeros((LANES, LANES), BF16)
    for k in range(ch):
        for i in range(ch):
            lag = (k - i) if reverse else (i - k)
            kin_out[k * LANES:(k + 1) * LANES, i * LANES:(i + 1) * LANES] = bd[lag] if lag >= 0 else zero
    for ri in range(2):
        for k in range(ch):
            ws_out[k * LANES:(k + 1) * LANES, ri * sw:(ri + 1) * sw] = (
                _lane_expand(ws_ref[ri, k], gpb, hh, p).astype(BF16))
        for i in range(ch):
            wo_out[ri * sw:(ri + 1) * sw, i * LANES:(i + 1) * LANES] = (
                _lane_expand(wo_ref[ri, i], gpb, p, hh).astype(BF16))


def _s5_assemble(m_tbl, ws_tbl, wo_tbl, *, reverse):
    ch, nlb, _, hh = m_tbl.shape
    p = ws_tbl.shape[-1]
    sw = (LANES // hh) * p
    width = ch * LANES
    return pl.pallas_call(
        functools.partial(_s5_assemble_kernel, reverse=reverse),
        grid=(nlb,),
        in_specs=[
            pl.BlockSpec((ch, None, LANES, hh), lambda l: (0, l, 0, 0)),
            pl.BlockSpec((2, ch, None, LANES, p), lambda l: (0, 0, l, 0, 0)),
            pl.BlockSpec((2, ch, None, sw, hh), lambda l: (0, 0, l, 0, 0)),
        ],
        out_specs=[
            pl.BlockSpec((None, width, width), lambda l: (l, 0, 0)),
            pl.BlockSpec((None, width, 2 * sw), lambda l: (l, 0, 0)),
            pl.BlockSpec((None, 2 * sw, width), lambda l: (l, 0, 0)),
        ],
        out_shape=[
            jax.ShapeDtypeStruct((nlb, width, width), BF16),
            jax.ShapeDtypeStruct((nlb, width, 2 * sw), BF16),
            jax.ShapeDtypeStruct((nlb, 2 * sw, width), BF16),
        ],
        compiler_params=_cparams(("arbitrary",)),
        name="s5_assemble_bwd" if reverse else "s5_assemble_fwd",
    )(m_tbl, ws_tbl, wo_tbl)


def _s5_weights(lam_re, lam_im, log_step, b_re, b_im, c_re, c_im, reverse):
    ch = S5_CHUNK
    g, p = lam_re.shape
    hh = b_re.shape[-1]
    gpb = LANES // hh
    nlb = g // gpb
    lr, li = lam_re.astype(F32), lam_im.astype(F32)
    dt = jnp.exp(log_step.astype(F32))[:, None]
    jj = jnp.arange(ch + 1, dtype=F32)[:, None, None]
    mag = jnp.exp(jj * lr * dt)
    pr = mag * jnp.cos(jj * li * dt)
    pi = mag * jnp.sin(jj * li * dt)
    ar, ai = pr[1], pi[1]
    den = lr * lr + li * li
    cr = ((ar - 1) * lr + ai * li) / den
    ci = (ai * lr - (ar - 1) * li) / den
    br, bi = b_re.astype(F32), b_im.astype(F32)
    bbr = cr[..., None] * br - ci[..., None] * bi
    bbi = cr[..., None] * bi + ci[..., None] * br
    ccr, cci = c_re.astype(F32), c_im.astype(F32)
    bbr_t = jnp.swapaxes(bbr, 1, 2)
    bbi_t = jnp.swapaxes(bbi, 1, 2)
    ccr_t = jnp.swapaxes(ccr, 1, 2)
    cci_t = jnp.swapaxes(cci, 1, 2)

    er = ccr[None] * pr[:ch, :, None, :] - cci[None] * pi[:ch, :, None, :]
    ei = ccr[None] * pi[:ch, :, None, :] + cci[None] * pr[:ch, :, None, :]
    m = jnp.sum(er[:, :, None, :, :] * bbr_t[None, :, :, None, :]
                - ei[:, :, None, :, :] * bbi_t[None, :, :, None, :], axis=-1)
    m_tbl = m.reshape(ch, nlb, LANES, hh)
    rk = np.arange(ch) if reverse else (ch - 1 - np.arange(ch))
    apr, api = pr[rk][:, :, None, :], pi[rk][:, :, None, :]
    sr = apr * bbr_t[None] - api * bbi_t[None]
    si = apr * bbi_t[None] + api * bbr_t[None]
    ws_tbl = jnp.stack([sr, si], axis=0).reshape(2, ch, nlb, LANES, p)
    ex = (ch - np.arange(ch)) if reverse else (np.arange(ch) + 1)
    epr, epi = pr[ex][:, :, :, None], pi[ex][:, :, :, None]
    wo_r = ccr_t[None] * epr - cci_t[None] * epi
    wo_i = -(ccr_t[None] * epi + cci_t[None] * epr)
    wo_tbl = jnp.stack([wo_r, wo_i], axis=0).reshape(2, ch, nlb, gpb * p, hh)
    kin, ws, wo = _s5_assemble(m_tbl, ws_tbl, wo_tbl, reverse=reverse)
    a_chunk = jnp.stack([pr[ch].reshape(nlb, gpb * p), pi[ch].reshape(nlb, gpb * p)], axis=1)
    return kin, ws, wo, a_chunk


def _s5_kernel(u_ref, s0_ref, a_ref, kin_ref, ws_ref, wo_ref, d_ref, y_ref, sfin_ref,
               z_scr, sin_scr, carry_scr, *, reverse, nb, cc, add_skip):
    j = pl.program_id(1)

    @pl.when(j == 0)
    def _():
        carry_scr[...] = s0_ref[...]

    ch = S5_CHUNK
    width = ch * LANES
    sw = a_ref.shape[-1]
    ns = sw // LANES
    u = jnp.concatenate(
        [jnp.concatenate([u_ref[b, pl.ds(k, cc, stride=ch), :] for b in range(nb)], axis=0)
         for k in range(ch)], axis=1)
    ub = u.astype(BF16)
    z = _dot(ub, ws_ref[...])
    ccp = cc + S5_ROW_PAD
    for s in range(2 * ns):
        for b in range(nb):
            z_scr[s, b * ccp:b * ccp + cc, :] = z[b * cc:(b + 1) * cc, s * LANES:(s + 1) * LANES]
    ar = [a_ref[0:1, s * LANES:(s + 1) * LANES] for s in range(ns)]
    ai = [a_ref[1:2, s * LANES:(s + 1) * LANES] for s in range(ns)]

    def body(c, st):
        cidx = (cc - 1 - c) if reverse else c
        rows = pl.ds(cidx, nb, stride=ccp)
        new = [None] * (2 * ns)
        for s in range(ns):
            sr, si = st[s], st[ns + s]
            sin_scr[s, rows, :] = sr
            sin_scr[ns + s, rows, :] = si
            new[s] = ar[s] * sr - ai[s] * si + z_scr[s, rows, :]
            new[ns + s] = ar[s] * si + ai[s] * sr + z_scr[ns + s, rows, :]
        return tuple(new)

    st0 = tuple(carry_scr[:, s * LANES:(s + 1) * LANES] for s in range(2 * ns))
    st = lax.fori_loop(0, cc, body, st0)
    s_fin = jnp.concatenate(st, axis=1)
    carry_scr[...] = s_fin
    sfin_ref[...] = s_fin
    s_in = jnp.concatenate(
        [jnp.concatenate([sin_scr[s, b * ccp:b * ccp + cc, :] for b in range(nb)], axis=0)
         for s in range(2 * ns)], axis=1)
    y = _dot(ub, kin_ref[...]) + _dot(s_in.astype(BF16), wo_ref[...])
    if add_skip:
        y = y + u * d_ref[...]
    for b in range(nb):
        for i in range(ch):
            y_ref[b, pl.ds(i, cc, stride=ch), :] = y[b * cc:(b + 1) * cc, i * LANES:(i + 1) * LANES]


def _s5_scan(u4, s0, weights, d_tile, *, reverse, cc, add_skip):
    kin, ws, wo, a_chunk = weights
    nlb, nb, n_steps, _ = u4.shape
    width = S5_CHUNK * LANES
    sw2 = ws.shape[-1]
    nj = n_steps // (cc * S5_CHUNK)

    def jm(j):
        return (nj - 1 - j) if reverse else j

    return pl.pallas_call(
        functools.partial(_s5_kernel, reverse=reverse, nb=nb, cc=cc, add_skip=add_skip),
        grid=(nlb, nj),
        in_specs=[
            pl.BlockSpec((None, nb, cc * S5_CHUNK, LANES), lambda l, j: (l, 0, jm(j), 0)),
            pl.BlockSpec((None, nb, sw2), lambda l, j: (l, 0, 0)),
            pl.BlockSpec((None, 2, sw2 // 2), lambda l, j: (l, 0, 0)),
            pl.BlockSpec((None, width, width), lambda l, j: (l, 0, 0)),
            pl.BlockSpec((None, width, sw2), lambda l, j: (l, 0, 0)),
            pl.BlockSpec((None, sw2, width), lambda l, j: (l, 0, 0)),
            pl.BlockSpec((None, 1, width), lambda l, j: (l, 0, 0)),
        ],
        out_specs=[
            pl.BlockSpec((None, nb, cc * S5_CHUNK, LANES), lambda l, j: (l, 0, jm(j), 0)),
            pl.BlockSpec((None, nb, sw2), lambda l, j: (l, 0, 0)),
        ],
        out_shape=[
            jax.ShapeDtypeStruct(u4.shape, F32),
            jax.ShapeDtypeStruct((nlb, nb, sw2), F32),
        ],
        scratch_shapes=[
            pltpu.VMEM((sw2 // LANES, nb * (cc + S5_ROW_PAD), LANES), F32),
            pltpu.VMEM((sw2 // LANES, nb * (cc + S5_ROW_PAD), LANES), F32),
            pltpu.VMEM((nb, sw2), F32),
        ],
        compiler_params=_cparams(("arbitrary", "arbitrary")),
        name="s5_bwd" if reverse else "s5_fwd",
    )(u4, s0, a_chunk, kin, ws, wo, d_tile)


def _attn_kernel(sink_ref, q_ref, kp_ref, ko_ref, kn_ref, vp_ref, vo_ref, vn_ref, kc_ref, vc_ref,
                 o_ref, *, nblk, blk):
    n = pl.program_id(1)
    n_ctx = kc_ref.shape[0]
    n_loc = 3 * blk
    nrow = Q_PER_KV * blk
    row = lax.broadcasted_iota(jnp.int32, (nrow, n_loc + n_ctx), 0)
    qi = row & (blk - 1)
    kj = lax.broadcasted_iota(jnp.int32, (nrow, n_loc + n_ctx), 1)
    rel = kj - qi
    lo = jnp.where(n == 0, blk, 0)
    hi = jnp.where(n == nblk - 1, 2 * blk, n_loc)
    valid = ((rel >= blk - WINDOW) & (rel <= blk + WINDOW) & (kj >= lo) & (kj < hi)) | (kj >= n_loc)
    lane = lax.broadcasted_iota(jnp.int32, (blk, LANES), 1)
    low_half = lane < HEAD_DIM
    row1 = lax.broadcasted_iota(jnp.int32, (nrow, 1), 0)
    kvs = range(N_KV_HEADS)
    each = lambda f: [f(hk) for hk in kvs]
    cs = [slice(hk * LANES, (hk + 1) * LANES) for hk in kvs]
    kcat = each(lambda g: jnp.concatenate([kp_ref[:, cs[g]], ko_ref[:, cs[g]], kn_ref[:, cs[g]], kc_ref[:, cs[g]]], axis=0))
    vcat = each(lambda g: jnp.concatenate([vp_ref[:, cs[g]], vo_ref[:, cs[g]], vn_ref[:, cs[g]], vc_ref[:, cs[g]]], axis=0))

    def stacked_q(g):
        qms = []
        for hq in range(Q_PER_KV):
            h = g * Q_PER_KV + hq
            qb = q_ref[:, (h // 2) * LANES:(h // 2 + 1) * LANES]
            qms.append(jnp.where(low_half if h % 2 == 0 else jnp.logical_not(low_half), qb, jnp.zeros_like(qb)))
        return jnp.concatenate(qms, axis=0)

    def stacked_sink(g):
        snk = jnp.zeros((nrow, 1), F32)
        for hq in range(Q_PER_KV):
            snk = jnp.where((row1 >= hq * blk) & (row1 < (hq + 1) * blk), sink_ref[g * Q_PER_KV + hq], snk)
        return snk

    qs = each(stacked_q)
    snk = each(stacked_sink)
    s = each(lambda g: lax.dot_general(qs[g], kcat[g], (((1,), (1,)), ((), ())), preferred_element_type=F32))
    s = each(lambda g: jnp.where(valid, s[g], NEG_INF))
    mx = each(lambda g: jnp.maximum(jnp.max(s[g], axis=1, keepdims=True), snk[g]))
    p = each(lambda g: jnp.exp(s[g] - mx[g]))
    den = each(lambda g: jnp.sum(p[g], axis=1, keepdims=True) + jnp.exp(snk[g] - mx[g]))
    o = each(lambda g: _dot(p[g].astype(BF16), vcat[g]) / den[g])
    outs = [o[g][hq * blk:(hq + 1) * blk] for g in kvs for hq in range(Q_PER_KV)]
    blocks = [jnp.where(low_half, outs[2 * m], outs[2 * m + 1]) for m in range(N_Q_HEADS // 2)]
    o_ref[...] = jnp.concatenate(blocks, axis=1).astype(BF16)


def _attention(q, kd, vd, kcd, vcd, sink, *, bsz, seq_len, n_ctx):
    blk = WINDOW
    nblk = seq_len // blk
    aw = q.shape[1]
    kw = kd.shape[1]

    def qmap(b, n):
        return (b * nblk + n, 0)

    def pmap(b, n):
        return (b * nblk + jnp.maximum(n - 1, 0), 0)

    def nmap(b, n):
        return (b * nblk + jnp.minimum(n + 1, nblk - 1), 0)

    kv_spec = lambda f: pl.BlockSpec((blk, kw), f)
    return pl.pallas_call(
        functools.partial(_attn_kernel, nblk=nblk, blk=blk),
        grid=(bsz, nblk),
        in_specs=[
            pl.BlockSpec(memory_space=pltpu.SMEM),
            pl.BlockSpec((blk, aw), qmap),
            kv_spec(pmap), kv_spec(qmap), kv_spec(nmap),
            kv_spec(pmap), kv_spec(qmap), kv_spec(nmap),
            pl.BlockSpec((n_ctx, kw), lambda b, n: (b, 0)),
            pl.BlockSpec((n_ctx, kw), lambda b, n: (b, 0)),
        ],
        out_specs=pl.BlockSpec((blk, aw), qmap),
        out_shape=jax.ShapeDtypeStruct(q.shape, BF16),
        compiler_params=_cparams(("arbitrary", "arbitrary")),
        name="attn",
    )(sink, q, kd, kd, kd, vd, vd, vd, kcd, vcd)


def _gelu_tanh(x):
    return 0.5 * x * (1.0 + jnp.tanh(math.sqrt(2.0 / math.pi) * (x + 0.044715 * (x * x * x))))


def _flatten_slots(v):
    tm, nk = v.shape
    per_row = LANES // nk
    log2 = lambda x: int(x).bit_length() - 1
    assert nk == 1 << log2(nk) and tm % per_row == 0
    spread = ((lax.broadcasted_iota(jnp.int32, (nk, LANES), 1) & (nk - 1))
              == lax.broadcasted_iota(jnp.int32, (nk, LANES), 0)).astype(F32)
    wide = jnp.dot(v, spread, preferred_element_type=F32, precision=HIGHEST)
    t_id = lax.broadcasted_iota(jnp.int32, (tm, LANES), 0)
    c_id = lax.broadcasted_iota(jnp.int32, (tm, LANES), 1)
    wide = jnp.where((c_id >> log2(nk)) == (t_id & (per_row - 1)), wide, 0.0)
    group = ((lax.broadcasted_iota(jnp.int32, (tm // per_row, tm), 1) >> log2(per_row))
             == lax.broadcasted_iota(jnp.int32, (tm // per_row, tm), 0)).astype(F32)
    return jnp.dot(group, wide, preferred_element_type=F32, precision=HIGHEST)


def _merge_kernel(yf_ref, yb_ref, o_ref, gs_ref, ga_ref, x_ref, lng_ref, lnb_ref, g1_ref, sh2_ref, sc2_ref,
                  wglu_ref, bglu_ref, wso_ref, wao_ref, wo_ref, l1g_ref, l1b_ref, wr_ref, br_ref,
                  h1_ref, idx_ref, gate_ref, rank_ref, cnt_ref, cnt_scr, *, alpha, n_exp, sub):
    i = pl.program_id(0)

    @pl.when(i == 0)
    def _():
        cnt_scr[...] = jnp.zeros_like(cnt_scr)

    nlb = yf_ref.shape[0]
    tm = x_ref.shape[0]
    lane = lax.broadcasted_iota(jnp.int32, (sub, n_exp), 1)
    lane_k = lax.broadcasted_iota(jnp.int32, (sub, TOP_K), 1)
    ri = lax.broadcasted_iota(jnp.int32, (sub, sub), 0)
    ci = lax.broadcasted_iota(jnp.int32, (sub, sub), 1)
    tri = jnp.where(ci < ri, 1.0, 0.0).astype(BF16)
    slots = sub * TOP_K // LANES
    cnt = cnt_scr[...]
    parts = range(tm // sub)
    rs = [slice(p * sub, (p + 1) * sub) for p in parts]
    each = lambda f: [f(p) for p in parts]
    y = each(lambda p: jnp.concatenate([yf_ref[lb, rs[p], :] + yb_ref[lb, rs[p], :] for lb in range(nlb)], axis=1))
    z = each(lambda p: _gelu_tanh(y[p]))
    zg = each(lambda p: _dot(z[p].astype(BF16), wglu_ref[...]) + bglu_ref[...])
    z = each(lambda p: z[p] * _sigmoid(zg[p]))
    ms = each(lambda p: _dot(z[p].astype(BF16), wso_ref[...]))
    ma = each(lambda p: _dot(o_ref[rs[p], :], wao_ref[...]))
    m = each(lambda p: gs_ref[rs[p], :].astype(F32) * ms[p] + ga_ref[rs[p], :].astype(F32) * ma[p])
    mix = each(lambda p: _dot(m[p].astype(BF16), wo_ref[...]))
    h = each(lambda p: _layer_norm(x_ref[rs[p], :], lng_ref[...], lnb_ref[...]))
    h1 = each(lambda p: _layer_norm(alpha * h[p] + g1_ref[...] * mix[p], l1g_ref[...], l1b_ref[...]))
    for p in parts:
        h1_ref[rs[p], :] = h1[p]
    xm = each(lambda p: h1[p] * (1.0 + sc2_ref[...]) + sh2_ref[...])
    work = each(lambda p: jnp.dot(xm[p], wr_ref[...], preferred_element_type=F32, precision=HIGHEST) + br_ref[...])
    vals, sels = [], []
    for _ in range(TOP_K):
        mx = each(lambda p: jnp.max(work[p], axis=1, keepdims=True))
        sel = each(lambda p: jnp.min(jnp.where(work[p] == mx[p], lane, n_exp), axis=1, keepdims=True))
        work = each(lambda p: jnp.where(lane == sel[p], -jnp.inf, work[p]))
        vals.append(mx)
        sels.append(sel)
    exps = [each(lambda p: jnp.exp(vals[k][p] - vals[0][p])) for k in range(TOP_K)]
    den = each(lambda p: exps[0][p] + exps[1][p] + exps[2][p] + exps[3][p])
    onehot = each(lambda p: sum((lane == sels[k][p]).astype(F32) for k in range(TOP_K)))
    prefix = each(lambda p: _dot(tri, onehot[p].astype(BF16)))
    for p in parts:
        rank = prefix[p] + cnt
        idx_o = jnp.zeros((sub, TOP_K), F32)
        gate_o = jnp.zeros((sub, TOP_K), F32)
        rank_o = jnp.zeros((sub, TOP_K), F32)
        for k in range(TOP_K):
            rk = jnp.sum(jnp.where(lane == sels[k][p], rank, 0.0), axis=1, keepdims=True)
            idx_o = jnp.where(lane_k == k, sels[k][p].astype(F32), idx_o)
            gate_o = jnp.where(lane_k == k, exps[k][p] / den[p], gate_o)
            rank_o = jnp.where(lane_k == k, rk, rank_o)
        gate_ref[rs[p], :] = gate_o
        idx_ref[p * slots:(p + 1) * slots, :] = _flatten_slots(idx_o).astype(jnp.int32)
        rank_ref[p * slots:(p + 1) * slots, :] = _flatten_slots(rank_o).astype(jnp.int32)
        cnt = cnt + jnp.sum(onehot[p], axis=0, keepdims=True)
    cnt_scr[...] = cnt
    cnt_ref[...] = cnt


def _merge(yf, yb, o_att, sgs, sga, x2, ln_g, ln_b, mod3, w_glu, b_glu, w_so, w_ao, w_o, l1g, l1b,
           w_r, b_r, *, rows_per_batch, tm, sub, alpha):
    t, d = x2.shape
    nlb = yf.shape[0]
    sw = nlb * LANES
    n_exp = w_r.shape[1]
    tpb = rows_per_batch // tm
    row = lambda i: (i, 0)
    const = lambda i: (0, 0)

    def modspec(chunk):
        return pl.BlockSpec((None, 1, d), lambda i: (i // tpb, 0, chunk))

    return pl.pallas_call(
        functools.partial(_merge_kernel, alpha=alpha, n_exp=n_exp, sub=sub),
        grid=(t // tm,),
        in_specs=[
            pl.BlockSpec((nlb, tm, LANES), lambda i: (0, i, 0)),
            pl.BlockSpec((nlb, tm, LANES), lambda i: (0, i, 0)),
            pl.BlockSpec((tm, o_att.shape[1]), row),
            pl.BlockSpec((tm, d), row),
            pl.BlockSpec((tm, d), row),
            pl.BlockSpec((tm, d), row),
            pl.BlockSpec((1, d), const),
            pl.BlockSpec((1, d), const),
            modspec(2), modspec(3), modspec(4),
            pl.BlockSpec(w_glu.shape, const),
            pl.BlockSpec((1, sw), const),
            pl.BlockSpec(w_so.shape, const),
            pl.BlockSpec(w_ao.shape, const),
            pl.BlockSpec(w_o.shape, const),
            pl.BlockSpec((1, d), const),
            pl.BlockSpec((1, d), const),
            pl.BlockSpec(w_r.shape, const),
            pl.BlockSpec((1, n_exp), const),
        ],
        out_specs=[
            pl.BlockSpec((tm, d), row),
            pl.BlockSpec((tm * TOP_K // LANES, LANES), row),
            pl.BlockSpec((tm, TOP_K), row),
            pl.BlockSpec((tm * TOP_K // LANES, LANES), row),
            pl.BlockSpec((1, n_exp), const),
        ],
        out_shape=[
            jax.ShapeDtypeStruct((t, d), F32),
            jax.ShapeDtypeStruct((t * TOP_K // LANES, LANES), jnp.int32),
            jax.ShapeDtypeStruct((t, TOP_K), F32),
            jax.ShapeDtypeStruct((t * TOP_K // LANES, LANES), jnp.int32),
            jax.ShapeDtypeStruct((1, n_exp), F32),
        ],
        scratch_shapes=[pltpu.VMEM((1, n_exp), F32)],
        compiler_params=_cparams(("arbitrary",)),
        name="merge",
    )(yf, yb, o_att, sgs, sga, x2, ln_g, ln_b, mod3, mod3, mod3, w_glu, b_glu, w_so, w_ao, w_o,
      l1g, l1b, w_r, b_r)


def _row_tile(ref, r):
    return ref.at[pl.ds(pl.multiple_of(r * ROW_TILES, ROW_TILES), ROW_TILES)]


def _store_row_tiles(ref, val):
    rows = val.shape[0]
    for l in range(ROW_TILES):
        ref[pl.ds(l, rows, stride=ROW_TILES), :] = val[:, l * LANES:(l + 1) * LANES]


def _load_row_tiles(ref, row0, rows):
    return jnp.concatenate(
        [ref[pl.ds(row0 * ROW_TILES + l, rows, stride=ROW_TILES), :] for l in range(ROW_TILES)], axis=1)


def _row_copy_out(src, dst, pos_ref, sem, t, k):
    return pltpu.make_async_copy(_row_tile(src, t), _row_tile(dst, pos_ref[t * TOP_K + k]), sem)


def _dispatch_kernel(pos_ref, prev_pos_ref, h1_ref, sh2_ref, sc2_ref, xrows_ref, xm_scr, sems):
    i = pl.program_id(0)
    n = pl.num_programs(0)
    tm = h1_ref.shape[0]
    slot = i % 2
    xm = h1_ref[...] * (1.0 + sc2_ref[...]) + sh2_ref[...]
    _store_row_tiles(xm_scr.at[slot], xm)

    def copies(slot_, idx_ref, op):
        def body(t, carry):
            for k in range(TOP_K):
                op(_row_copy_out(xm_scr.at[slot_], xrows_ref, idx_ref, sems.at[slot_], t, k), k)
            return carry
        lax.fori_loop(0, tm, body, 0, unroll=DMA_LOOP_UNROLL)

    start = lambda cp, k: cp.start(priority=k % N_DMA_PRIORITIES)
    wait = lambda cp, k: cp.wait()
    for s in range(2):
        @pl.when(slot == s)
        def _():
            copies(s, pos_ref, start)

    for s in range(2):
        @pl.when((slot == 1 - s) & (i > 0))
        def _():
            copies(s, prev_pos_ref, wait)

        @pl.when((slot == s) & (i == n - 1))
        def _():
            copies(s, pos_ref, wait)


def _dispatch(pos_flat, h1, mod3, *, rows_per_batch, tm):
    t, d = h1.shape
    tpb = rows_per_batch // tm
    return pl.pallas_call(
        _dispatch_kernel,
        grid=(t // tm,),
        in_specs=[
            pl.BlockSpec((tm * TOP_K,), lambda i: (i,), memory_space=pltpu.SMEM),
            pl.BlockSpec((tm * TOP_K,), lambda i: (jnp.maximum(i - 1, 0),), memory_space=pltpu.SMEM),
            pl.BlockSpec((tm, d), lambda i: (i, 0)),
            pl.BlockSpec((None, 1, d), lambda i: (i // tpb, 0, 3)),
            pl.BlockSpec((None, 1, d), lambda i: (i // tpb, 0, 4)),
        ],
        out_specs=pl.BlockSpec(memory_space=pl.ANY),
        out_shape=jax.ShapeDtypeStruct((t * TOP_K * ROW_TILES, LANES), F32),
        scratch_shapes=[pltpu.VMEM((2, tm * ROW_TILES, LANES), F32), pltpu.SemaphoreType.DMA((2,))],
        compiler_params=_cparams(("arbitrary",)),
        name="dispatch",
    )(pos_flat, pos_flat, h1, mod3, mod3)


def _experts_kernel(tile_ref, exp_ref, lo_ref, hi_ref, nxt_ref, x_ref, wgu_hbm, bgu_ref, wd_hbm, bd_ref, y_ref,
                    wgu_f32, wd_f32, wgu_scr, wd_scr, sems, *, rows):
    w = pl.program_id(0)
    prev = jnp.maximum(w - 1, 0)
    e_new = (w == 0) | (exp_ref[w] != exp_ref[prev])
    t_new = (w == 0) | (tile_ref[w] != tile_ref[prev])
    lo = lo_ref[w]
    hi = hi_ref[w]
    f = wd_scr.shape[0]

    def weight_copies(e):
        return (pltpu.make_async_copy(wgu_hbm.at[e], wgu_f32, sems.at[0]),
                pltpu.make_async_copy(wd_hbm.at[e], wd_f32, sems.at[1]))

    @pl.when(w == 0)
    def _():
        for cp in weight_copies(exp_ref[0]):
            cp.start()

    @pl.when(e_new)
    def _():
        for cp in weight_copies(exp_ref[w]):
            cp.wait()
        wgu_scr[...] = wgu_f32[...].astype(BF16)
        wd_scr[...] = wd_f32[...].astype(BF16)

        @pl.when(nxt_ref[w] >= 0)
        def _():
            for cp in weight_copies(nxt_ref[w]):
                cp.start()

    row0 = tile_ref[w] * rows
    whole = (lo <= row0) & (hi >= row0 + rows)

    @pl.when(t_new & jnp.logical_not(whole))
    def _():
        y_ref[...] = jnp.zeros_like(y_ref)

    @pl.when(hi > lo)
    def _():
        x = _load_row_tiles(x_ref, 0, rows)
        gu = _dot(x.astype(BF16), wgu_scr[...]) + bgu_ref[...]
        glu = jnp.minimum(gu[:, :f], SWIGLU_LIMIT)
        lin = jnp.clip(gu[:, f:], -SWIGLU_LIMIT, SWIGLU_LIMIT)
        act = glu * _sigmoid(SWIGLU_ALPHA * glu) * (lin + 1.0)
        y = _dot(act.astype(BF16), wd_scr[...]) + bd_ref[...]

        @pl.when(whole)
        def _():
            _store_row_tiles(y_ref, y)

        @pl.when(jnp.logical_not(whole))
        def _():
            r = row0 + lax.broadcasted_iota(jnp.int32, (rows, 1), 0)
            _store_row_tiles(y_ref, jnp.where((r >= lo) & (r < hi), y, _load_row_tiles(y_ref, 0, rows)))


def _experts(work, xrows, w_gu, b_gu, w_d, b_d, *, rows):
    tile_id, exp_id, lo, hi, nxt = work
    n_exp, d, f2 = w_gu.shape
    f = w_d.shape[1]
    grid_spec = pltpu.PrefetchScalarGridSpec(
        num_scalar_prefetch=5,
        grid=(tile_id.shape[0],),
        in_specs=[
            pl.BlockSpec((rows * ROW_TILES, LANES), lambda w, ti, ex, lo, hi, nx: (ti[w], 0)),
            pl.BlockSpec(memory_space=pl.ANY),
            pl.BlockSpec((None, 1, f2), lambda w, ti, ex, lo, hi, nx: (ex[w], 0, 0)),
            pl.BlockSpec(memory_space=pl.ANY),
            pl.BlockSpec((None, 1, d), lambda w, ti, ex, lo, hi, nx: (ex[w], 0, 0)),
        ],
        out_specs=pl.BlockSpec((rows * ROW_TILES, LANES), lambda w, ti, ex, lo, hi, nx: (ti[w], 0)),
        scratch_shapes=[
            pltpu.VMEM((d, f2), F32), pltpu.VMEM((f, d), F32),
            pltpu.VMEM((d, f2), BF16), pltpu.VMEM((f, d), BF16),
            pltpu.SemaphoreType.DMA((2,)),
        ],
    )
    return pl.pallas_call(
        functools.partial(_experts_kernel, rows=rows),
        grid_spec=grid_spec,
        out_shape=jax.ShapeDtypeStruct(xrows.shape, F32),
        compiler_params=_cparams(("arbitrary",)),
        name="experts",
    )(tile_id, exp_id, lo, hi, nxt, xrows, w_gu, b_gu.reshape(n_exp, 1, f2), w_d, b_d.reshape(n_exp, 1, d))


def _row_copy_in(src, dst, pos_ref, sem, t, k, tm):
    return pltpu.make_async_copy(_row_tile(src, pos_ref[t * TOP_K + k]), _row_tile(dst, k * tm + t), sem)


def _combine_kernel(pos_ref, next_pos_ref, h1_ref, gate_ref, g2_ref, lg_ref, lb_ref, yrows_ref, o_ref,
                    buf, sems, *, alpha):
    i = pl.program_id(0)
    n = pl.num_programs(0)
    tm = h1_ref.shape[0]
    slot = i % 2

    def copies(slot_, idx_ref, op):
        def body(t, carry):
            for k in range(TOP_K):
                op(_row_copy_in(yrows_ref, buf.at[slot_], idx_ref, sems.at[slot_], t, k, tm), k)
            return carry
        lax.fori_loop(0, tm, body, 0, unroll=DMA_LOOP_UNROLL)

    start = lambda cp, k: cp.start(priority=k % N_DMA_PRIORITIES)
    wait = lambda cp, k: cp.wait()

    @pl.when(i == 0)
    def _():
        copies(0, pos_ref, start)

    for s in range(2):
        @pl.when((slot == 1 - s) & (i + 1 < n))
        def _():
            copies(s, next_pos_ref, start)

    def reduce(s):
        copies(s, pos_ref, wait)
        gates = gate_ref[...]
        ffn = gates[:, 0:1] * _load_row_tiles(buf.at[s], 0, tm)
        for k in range(1, TOP_K):
            ffn = ffn + gates[:, k:k + 1] * _load_row_tiles(buf.at[s], k * tm, tm)
        o_ref[...] = _layer_norm(alpha * h1_ref[...] + g2_ref[...] * ffn, lg_ref[...], lb_ref[...])

    for s in range(2):
        @pl.when(slot == s)
        def _():
            reduce(s)


def _combine(pos_flat, h1, gates, mod3, l2g, l2b, yrows, *, rows_per_batch, tm, alpha):
    t, d = h1.shape
    tpb = rows_per_batch // tm
    n_tiles = t // tm
    return pl.pallas_call(
        functools.partial(_combine_kernel, alpha=alpha),
        grid=(n_tiles,),
        in_specs=[
            pl.BlockSpec((tm * TOP_K,), lambda i: (i,), memory_space=pltpu.SMEM),
            pl.BlockSpec((tm * TOP_K,), lambda i: (jnp.minimum(i + 1, n_tiles - 1),), memory_space=pltpu.SMEM),
            pl.BlockSpec((tm, d), lambda i: (i, 0)),
            pl.BlockSpec((tm, TOP_K), lambda i: (i, 0)),
            pl.BlockSpec((None, 1, d), lambda i: (i // tpb, 0, 5)),
            pl.BlockSpec((1, d), lambda i: (0, 0)),
            pl.BlockSpec((1, d), lambda i: (0, 0)),
            pl.BlockSpec(memory_space=pl.ANY),
        ],
        out_specs=pl.BlockSpec((tm, d), lambda i: (i, 0)),
        out_shape=jax.ShapeDtypeStruct((t, d), F32),
        scratch_shapes=[pltpu.VMEM((2, TOP_K * tm * ROW_TILES, LANES), F32), pltpu.SemaphoreType.DMA((2,))],
        compiler_params=_cparams(("arbitrary",)),
        name="combine",
    )(pos_flat, pos_flat, h1, gates, mod3, l2g, l2b, yrows)


def _work_list(counts, n_rows, rows):
    n_exp = counts.shape[0]
    n_tiles = n_rows // rows
    n_work = n_tiles + n_exp - 1
    def count_le(sorted_vals, q):
        return jnp.sum((sorted_vals[None, :] <= q[:, None]).astype(jnp.int32), axis=1)

    ends = jnp.cumsum(counts)
    starts = ends - counts
    tile_lo = jnp.arange(n_tiles, dtype=jnp.int32) * rows
    e_lo = count_le(ends, tile_lo)
    e_hi = count_le(ends, tile_lo + rows - 1)
    per_tile = e_hi - e_lo + 1
    off_end = jnp.cumsum(per_tile)
    off = off_end - per_tile
    w = jnp.arange(n_work, dtype=jnp.int32)
    tile = jnp.minimum(count_le(off_end, w), n_tiles - 1)
    exp = jnp.minimum(e_lo[tile] + (w - off[tile]), n_exp - 1).astype(jnp.int32)
    live = w < off_end[-1]
    lo = jnp.maximum(starts[exp], tile * rows)
    hi = jnp.minimum(ends[exp], (tile + 1) * rows)
    hi = jnp.where(live, hi, lo)
    ids = jnp.arange(n_exp, dtype=jnp.int32)
    later = (ids[None, :] > ids[:, None]) & (counts[None, :] > 0)
    nxt_e = jnp.min(jnp.where(later, ids[None, :], n_exp), axis=1)
    nxt_e = jnp.where(nxt_e < n_exp, nxt_e, -1).astype(jnp.int32)
    return tile, exp, lo.astype(jnp.int32), hi.astype(jnp.int32), nxt_e[exp], starts


def kernel(x, c, ctx, c_ctx, ln_in_g, ln_in_b, w_mod, b_mod, w_in, ssm_lam_re, ssm_lam_im, ssm_log_step, ssm_b_re, ssm_b_im, ssm_c_re, ssm_c_im, ssm_d, w_glu, b_glu, attn_sink, w_ssm_out, w_att_out, w_o, ln1_g, ln1_b, w_router, b_router, w_gate_up, b_gate_up, w_down, b_down, ln2_g, ln2_b):
    bsz, seq_len, d = x.shape
    n_ctx = ctx.shape[1]
    depth = w_mod.shape[0]
    assert depth == 1, "single-layer kernel"
    g_ssm, h_ssm = ssm_d.shape[1:]
    ssm_w = g_ssm * h_ssm
    attn_w = N_Q_HEADS * HEAD_DIM
    kv_w = N_KV_HEADS * HEAD_DIM
    kv2_w = 2 * kv_w
    nlb = ssm_w // LANES
    alpha = (2.0 * depth) ** 0.25
    t = bsz * seq_len
    assert bsz + 1 <= SUBLANES
    assert d == ROW_TILES * LANES and ROW_TILES == SUBLANES, "a token row must be exactly one (8, 128) tile"

    row2 = lambda a: a.reshape(1, -1)

    c_rows = jnp.concatenate([c, c_ctx[None], jnp.zeros((SUBLANES - bsz - 1, d), F32)], axis=0)
    mod = _mod_vectors(c_rows, w_mod[0], row2(b_mod[0]))
    mod3 = mod.reshape(SUBLANES, 1, 6 * d)

    wi = w_in[0]
    i0, i1 = ssm_w, ssm_w + attn_w
    i2, i3 = i1 + kv_w, i1 + 2 * kv_w
    i4 = i3 + d

    def dup_heads(wkv):
        parts = []
        for hk in range(N_KV_HEADS):
            blk = wkv[:, hk * HEAD_DIM:(hk + 1) * HEAD_DIM]
            parts += [blk, blk]
        return jnp.concatenate(parts, axis=1)

    w_s, w_q = wi[:, :i0], wi[:, i0:i1]
    w_k2, w_v2 = dup_heads(wi[:, i1:i2]), dup_heads(wi[:, i2:i3])
    w_lat = jnp.concatenate([w_s, w_q, w_k2, w_v2, wi[:, i3:i4], wi[:, i4:]], axis=1).astype(BF16)
    w_ctx = jnp.concatenate([w_s, w_k2, w_v2], axis=1).astype(BF16)

    pos = jnp.arange(seq_len)
    inv = ROPE_BASE ** (-jnp.arange(ROPE_PAIRS, dtype=F32) / ROPE_PAIRS)
    ang = jnp.concatenate([(pos // GRID_W).astype(F32)[:, None] * inv,
                           (pos % GRID_W).astype(F32)[:, None] * inv], axis=-1)
    cos_t = jnp.tile(jnp.cos(ang), (1, 2 * LANES // HEAD_DIM))
    sin_h = jnp.sin(ang)
    sin_t = jnp.tile(jnp.concatenate([-sin_h, sin_h], axis=-1), (1, LANES // HEAD_DIM))

    g_in, b_in = row2(ln_in_g), row2(ln_in_b)
    x2 = x.reshape(t, d)
    ctx2 = ctx.reshape(bsz * n_ctx, d)
    s_c, k_c, v_c = _inproj(ctx2, g_in, b_in, mod3, w_ctx, None, None, latent=False,
                            rows_per_batch=n_ctx, tm=n_ctx, ctx_mod_row=bsz,
                            ssm_w=ssm_w, attn_w=attn_w, kv2_w=kv2_w)
    s_l, q_l, k_l, v_l, sgs, sga = _inproj(x2, g_in, b_in, mod3, w_lat, cos_t, sin_t, latent=True,
                                           rows_per_batch=seq_len, tm=512, ctx_mod_row=None,
                                           ssm_w=ssm_w, attn_w=attn_w, kv2_w=kv2_w)

    wf = _s5_weights(ssm_lam_re[0, 0], ssm_lam_im[0, 0], ssm_log_step[0, 0], ssm_b_re[0, 0], ssm_b_im[0, 0],
                     ssm_c_re[0, 0], ssm_c_im[0, 0], reverse=False)
    wb = _s5_weights(ssm_lam_re[0, 1], ssm_lam_im[0, 1], ssm_log_step[0, 1], ssm_b_re[0, 1], ssm_b_im[0, 1],
                     ssm_c_re[0, 1], ssm_c_im[0, 1], reverse=True)
    d_tile = jnp.tile(ssm_d[0].astype(F32).reshape(nlb, 1, LANES), (1, 1, S5_CHUNK))
    sw2 = wf[1].shape[-1]
    zero_state = jnp.zeros((nlb, bsz, sw2), F32)
    uc4 = s_c.reshape(nlb, bsz, n_ctx, LANES)
    ul4 = s_l.reshape(nlb, bsz, seq_len, LANES)
    _, sf0 = _s5_scan(uc4, zero_state, wf, d_tile, reverse=False, cc=n_ctx // S5_CHUNK, add_skip=False)
    _, sb0 = _s5_scan(uc4, zero_state, wb, d_tile, reverse=True, cc=n_ctx // S5_CHUNK, add_skip=False)
    yf4, _ = _s5_scan(ul4, sf0, wf, d_tile, reverse=False, cc=128, add_skip=True)
    yb4, _ = _s5_scan(ul4, sb0, wb, d_tile, reverse=True, cc=128, add_skip=False)
    yf = yf4.reshape(nlb, t, LANES)
    yb = yb4.reshape(nlb, t, LANES)

    o_att = _attention(q_l, k_l, v_l, k_c, v_c, attn_sink[0].astype(F32), bsz=bsz, seq_len=seq_len, n_ctx=n_ctx)

    h1, top_i, gates, rank, counts = _merge(
        yf, yb, o_att, sgs, sga, x2, g_in, b_in, mod3, w_glu[0].astype(BF16), row2(b_glu[0]),
        w_ssm_out[0].astype(BF16), w_att_out[0].astype(BF16), w_o[0].astype(BF16), row2(ln1_g[0]), row2(ln1_b[0]),
        w_router[0], row2(b_router[0]), rows_per_batch=seq_len, tm=1024, sub=256, alpha=alpha)

    rows = 256
    n_rows = t * TOP_K
    tile_id, exp_id, lo, hi, nxt, starts = _work_list(counts[0].astype(jnp.int32), n_rows, rows)
    pos_flat = (starts[top_i] + rank).reshape(n_rows).astype(jnp.int32)
    xrows = _dispatch(pos_flat, h1, mod3, rows_per_batch=seq_len, tm=256)
    yrows = _experts((tile_id, exp_id, lo, hi, nxt), xrows, w_gate_up[0], b_gate_up[0], w_down[0], b_down[0],
                     rows=rows)
    out = _combine(pos_flat, h1, gates, mod3, row2(ln2_g[0]), row2(ln2_b[0]), yrows,
                   rows_per_batch=seq_len, tm=256, alpha=alpha)
    return out.reshape(bsz, seq_len, d)
```

```python
import functools
import math

import jax
import jax.numpy as jnp
import numpy as np
from jax import lax
from jax.experimental import pallas as pl
from jax.experimental.pallas import tpu as pltpu

F32 = jnp.float32
BF16 = jnp.bfloat16
HIGHEST = lax.Precision.HIGHEST

HEAD_DIM = 64
N_Q_HEADS = 8
N_KV_HEADS = 2
Q_PER_KV = N_Q_HEADS // N_KV_HEADS
WINDOW = 128
GRID_W = 64
ROPE_BASE = 10000.0
ROPE_PAIRS = HEAD_DIM // 4
TOP_K = 4
SWIGLU_LIMIT = 7.0
SWIGLU_ALPHA = 1.702
LN_EPS = 1e-5
NEG_INF = -1e30

LANES = 128
SUBLANES = 8
ROW_TILES = 8
VMEM_LIMIT = 56 * 1024 * 1024
N_DMA_PRIORITIES = 2
DMA_LOOP_UNROLL = 8

S5_CHUNK = 8
S5_ROW_PAD = SUBLANES


def _cparams(sem):
    return pltpu.CompilerParams(dimension_semantics=sem, vmem_limit_bytes=VMEM_LIMIT)


def _sigmoid(x):
    return 1.0 / (1.0 + jnp.exp(-x))


def _layer_norm(x, g, b):
    mu = jnp.mean(x, axis=-1, keepdims=True)
    xc = x - mu
    var = jnp.mean(xc * xc, axis=-1, keepdims=True)
    return xc * lax.rsqrt(var + LN_EPS) * g + b


def _dot(a, b):
    return jnp.dot(a, b, preferred_element_type=F32)


def _dot_3pass(a, b):
    a_hi = a.astype(BF16)
    b_hi = b.astype(BF16)
    a_lo = (a - a_hi.astype(F32)).astype(BF16)
    b_lo = (b - b_hi.astype(F32)).astype(BF16)
    return _dot(a_hi, b_hi) + (_dot(a_hi, b_lo) + _dot(a_lo, b_hi))


def _mod_kernel(c_ref, w_ref, b_ref, o_ref):
    c = c_ref[...]
    a = c * _sigmoid(c)
    o_ref[...] = jnp.dot(a, w_ref[...], preferred_element_type=F32, precision=HIGHEST) + b_ref[...]


def _mod_vectors(c_rows, w_mod, b_mod):
    d = c_rows.shape[1]
    n = w_mod.shape[1]
    return pl.pallas_call(
        _mod_kernel,
        grid=(n // d,),
        in_specs=[
            pl.BlockSpec((SUBLANES, d), lambda i: (0, 0)),
            pl.BlockSpec((d, d), lambda i: (0, i)),
            pl.BlockSpec((1, d), lambda i: (0, i)),
        ],
        out_specs=pl.BlockSpec((SUBLANES, d), lambda i: (0, i)),
        out_shape=jax.ShapeDtypeStruct((SUBLANES, n), F32),
        compiler_params=_cparams(("arbitrary",)),
        name="mod",
    )(c_rows, w_mod, b_mod)


def _rope(t, cos, sin):
    n = t.shape[1]
    reps = n // LANES
    c = jnp.concatenate([cos] * reps, axis=1) if reps > 1 else cos
    s = jnp.concatenate([sin] * reps, axis=1) if reps > 1 else sin
    half = HEAD_DIM // 2
    upper = pltpu.roll(t, n - half, axis=1)
    lower = pltpu.roll(t, half, axis=1)
    lane = lax.broadcasted_iota(jnp.int32, t.shape, 1)
    partner = jnp.where((lane & half) == 0, upper, lower)
    return t * c + partner * s


def _inproj_kernel(*refs, latent, ssm_w, attn_w, kv2_w, d_model):
    if latent:
        (x_ref, g_ref, b_ref, sh_ref, sc_ref, w_ref, wkv_ref, cos_ref, sin_ref,
         s_ref, q_ref, k_ref, v_ref, gs_ref, ga_ref) = refs
    else:
        x_ref, g_ref, b_ref, sh_ref, sc_ref, w_ref, wkv_ref, s_ref, k_ref, v_ref = refs
    h = _layer_norm(x_ref[...], g_ref[...], b_ref[...])
    u = (h * (1.0 + sc_ref[...]) + sh_ref[...]).astype(BF16)
    s = _dot(u, w_ref[:, 0:ssm_w])
    for lb in range(ssm_w // LANES):
        s_ref[lb] = s[:, lb * LANES:(lb + 1) * LANES]
    if latent:
        q = _dot(u, w_ref[:, ssm_w:ssm_w + attn_w])
        q = _rope(q, cos_ref[...], sin_ref[...]) * (HEAD_DIM ** -0.5)
        q_ref[...] = q.astype(BF16)
    k = _dot(u, wkv_ref[:, 0:kv2_w])
    if latent:
        k = _rope(k, cos_ref[...], sin_ref[...])
    k_ref[...] = k.astype(BF16)
    v_ref[...] = _dot(u, wkv_ref[:, kv2_w:2 * kv2_w]).astype(BF16)
    if latent:
        col = ssm_w + attn_w + kv2_w
        gs_ref[...] = _sigmoid(_dot(u, w_ref[:, col:col + d_model])).astype(BF16)
        ga_ref[...] = _sigmoid(_dot(u, w_ref[:, col + d_model:col + 2 * d_model])).astype(BF16)


def _inproj(x2, ln_g, ln_b, mod3, w_all, w_kv2, cos_t, sin_t, *, latent, rows_per_batch, tm, ctx_mod_row,
            ssm_w, attn_w, kv2_w):
    t, d = x2.shape
    tiles_per_batch = rows_per_batch // tm
    nlb = ssm_w // LANES

    def brow(i):
        return i // tiles_per_batch if ctx_mod_row is None else ctx_mod_row

    in_specs = [
        pl.BlockSpec((tm, d), lambda i: (i, 0)),
        pl.BlockSpec((1, d), lambda i: (0, 0)),
        pl.BlockSpec((1, d), lambda i: (0, 0)),
        pl.BlockSpec((None, 1, d), lambda i: (brow(i), 0, 0)),
        pl.BlockSpec((None, 1, d), lambda i: (brow(i), 0, 1)),
        pl.BlockSpec(w_all.shape, lambda i: (0, 0)),
        pl.BlockSpec(w_kv2.shape, lambda i: (0, 0)),
    ]
    args = [x2, ln_g, ln_b, mod3, mod3, w_all, w_kv2]
    out_specs = [pl.BlockSpec((nlb, tm, LANES), lambda i: (0, i, 0))]
    out_shape = [jax.ShapeDtypeStruct((nlb, t, LANES), F32)]
    if latent:
        in_specs += [
            pl.BlockSpec((tm, LANES), lambda i: (i % tiles_per_batch, 0)),
            pl.BlockSpec((tm, LANES), lambda i: (i % tiles_per_batch, 0)),
        ]
        args += [cos_t, sin_t]
        out_specs.append(pl.BlockSpec((tm, attn_w), lambda i: (i, 0)))
        out_shape.append(jax.ShapeDtypeStruct((t, attn_w), BF16))
    out_specs += [pl.BlockSpec((tm, kv2_w), lambda i: (i, 0))] * 2
    out_shape += [jax.ShapeDtypeStruct((t, kv2_w), BF16)] * 2
    if latent:
        out_specs += [pl.BlockSpec((tm, d), lambda i: (i, 0))] * 2
        out_shape += [jax.ShapeDtypeStruct((t, d), BF16)] * 2
    return pl.pallas_call(
        functools.partial(_inproj_kernel, latent=latent, ssm_w=ssm_w, attn_w=attn_w, kv2_w=kv2_w,
                          d_model=d),
        grid=(t // tm,),
        in_specs=in_specs,
        out_specs=out_specs,
        out_shape=out_shape,
        compiler_params=_cparams(("arbitrary",)),
        name="inproj_latent" if latent else "inproj_ctx",
    )(*args)


def _lane_expand(tbl, n_copies, rows_per_group, cols_per_group):
    r, w = tbl.shape
    wide_w = n_copies * w
    log2 = lambda v: int(v).bit_length() - 1
    assert all(v == 1 << log2(v) for v in (w, rows_per_group, cols_per_group))
    sel = ((lax.broadcasted_iota(jnp.int32, (w, wide_w), 1) & (w - 1))
           == lax.broadcasted_iota(jnp.int32, (w, wide_w), 0)).astype(BF16)
    wide = jnp.dot(tbl.astype(BF16), sel, preferred_element_type=F32)
    keep = ((lax.broadcasted_iota(jnp.int32, (r, wide_w), 0) >> log2(rows_per_group))
            == (lax.broadcasted_iota(jnp.int32, (r, wide_w), 1) >> log2(cols_per_group)))
    return jnp.where(keep, wide, 0.0)


def _s5_assemble_kernel(m_ref, ws_ref, wo_ref, kin_out, ws_out, wo_out, *, reverse):
    ch, _, hh = m_ref.shape
    p = ws_ref.shape[-1]
    gpb = LANES // hh
    sw = gpb * p
    bd = [_lane_expand(m_ref[j], gpb, hh, hh).astype(BF16) for j in range(ch)]
    zero = jnp.zeros((LANES, LANES), BF16)
    for k in range(ch):
        for i in range(ch):
            lag = (k - i) if reverse else (i - k)
            kin_out[k * LANES:(k + 1) * LANES, i * LANES:(i + 1) * LANES] = bd[lag] if lag >= 0 else zero
    for ri in range(2):
        for k in range(ch):
            ws_out[k * LANES:(k + 1) * LANES, ri * sw:(ri + 1) * sw] = (
                _lane_expand(ws_ref[ri, k], gpb, hh, p).astype(BF16))
        for i in range(ch):
            wo_out[ri * sw:(ri + 1) * sw, i * LANES:(i + 1) * LANES] = (
                _lane_expand(wo_ref[ri, i], gpb, p, hh).astype(BF16))


def _s5_assemble(m_tbl, ws_tbl, wo_tbl, *, reverse):
    ch, nlb, _, hh = m_tbl.shape
    p = ws_tbl.shape[-1]
    sw = (LANES // hh) * p
    width = ch * LANES
    return pl.pallas_call(
        functools.partial(_s5_assemble_kernel, reverse=reverse),
        grid=(nlb,),
        in_specs=[
            pl.BlockSpec((ch, None, LANES, hh), lambda l: (0, l, 0, 0)),
            pl.BlockSpec((2, ch, None, LANES, p), lambda l: (0, 0, l, 0, 0)),
            pl.BlockSpec((2, ch, None, sw, hh), lambda l: (0, 0, l, 0, 0)),
        ],
        out_specs=[
            pl.BlockSpec((None, width, width), lambda l: (l, 0, 0)),
            pl.BlockSpec((None, width, 2 * sw), lambda l: (l, 0, 0)),
            pl.BlockSpec((None, 2 * sw, width), lambda l: (l, 0, 0)),
        ],
        out_shape=[
            jax.ShapeDtypeStruct((nlb, width, width), BF16),
            jax.ShapeDtypeStruct((nlb, width, 2 * sw), BF16),
            jax.ShapeDtypeStruct((nlb, 2 * sw, width), BF16),
        ],
        compiler_params=_cparams(("arbitrary",)),
        name="s5_assemble_bwd" if reverse else "s5_assemble_fwd",
    )(m_tbl, ws_tbl, wo_tbl)


def _s5_weights(lam_re, lam_im, log_step, b_re, b_im, c_re, c_im, reverse):
    ch = S5_CHUNK
    g, p = lam_re.shape
    hh = b_re.shape[-1]
    gpb = LANES // hh
    nlb = g // gpb
    lr, li = lam_re.astype(F32), lam_im.astype(F32)
    dt = jnp.exp(log_step.astype(F32))[:, None]
    jj = jnp.arange(ch + 1, dtype=F32)[:, None, None]
    mag = jnp.exp(jj * lr * dt)
    pr = mag * jnp.cos(jj * li * dt)
    pi = mag * jnp.sin(jj * li * dt)
    ar, ai = pr[1], pi[1]
    den = lr * lr + li * li
    cr = ((ar - 1) * lr + ai * li) / den
    ci = (ai * lr - (ar - 1) * li) / den
    br, bi = b_re.astype(F32), b_im.astype(F32)
    bbr = cr[..., None] * br - ci[..., None] * bi
    bbi = cr[..., None] * bi + ci[..., None] * br
    ccr, cci = c_re.astype(F32), c_im.astype(F32)
    bbr_t = jnp.swapaxes(bbr, 1, 2)
    bbi_t = jnp.swapaxes(bbi, 1, 2)
    ccr_t = jnp.swapaxes(ccr, 1, 2)
    cci_t = jnp.swapaxes(cci, 1, 2)

    er = ccr[None] * pr[:ch, :, None, :] - cci[None] * pi[:ch, :, None, :]
    ei = ccr[None] * pi[:ch, :, None, :] + cci[None] * pr[:ch, :, None, :]
    m = jnp.sum(er[:, :, None, :, :] * bbr_t[None, :, :, None, :]
                - ei[:, :, None, :, :] * bbi_t[None, :, :, None, :], axis=-1)
    m_tbl = m.reshape(ch, nlb, LANES, hh)
    rk = np.arange(ch) if reverse else (ch - 1 - np.arange(ch))
    apr, api = pr[rk][:, :, None, :], pi[rk][:, :, None, :]
    sr = apr * bbr_t[None] - api * bbi_t[None]
    si = apr * bbi_t[None] + api * bbr_t[None]
    ws_tbl = jnp.stack([sr, si], axis=0).reshape(2, ch, nlb, LANES, p)
    ex = (ch - np.arange(ch)) if reverse else (np.arange(ch) + 1)
    epr, epi = pr[ex][:, :, :, None], pi[ex][:, :, :, None]
    wo_r = ccr_t[None] * epr - cci_t[None] * epi
    wo_i = -(ccr_t[None] * epi + cci_t[None] * epr)
    wo_tbl = jnp.stack([wo_r, wo_i], axis=0).reshape(2, ch, nlb, gpb * p, hh)
    kin, ws, wo = _s5_assemble(m_tbl, ws_tbl, wo_tbl, reverse=reverse)
    a_chunk = jnp.stack([pr[ch].reshape(nlb, gpb * p), pi[ch].reshape(nlb, gpb * p)], axis=1)
    return kin, ws, wo, a_chunk


def _s5_kernel(u_ref, s0_ref, a_ref, kin_ref, ws_ref, wo_ref, d_ref, y_ref, sfin_ref,
               z_scr, sin_scr, carry_scr, *, reverse, nb, cc, add_skip):
    j = pl.program_id(1)

    @pl.when(j == 0)
    def _():
        carry_scr[...] = s0_ref[...]

    ch = S5_CHUNK
    width = ch * LANES
    sw = a_ref.shape[-1]
    ns = sw // LANES
    u = jnp.concatenate(
        [jnp.concatenate([u_ref[b, pl.ds(k, cc, stride=ch), :] for b in range(nb)], axis=0)
         for k in range(ch)], axis=1)
    ub = u.astype(BF16)
    z = _dot(ub, ws_ref[...])
    ccp = cc + S5_ROW_PAD
    for s in range(2 * ns):
        for b in range(nb):
            z_scr[s, b * ccp:b * ccp + cc, :] = z[b * cc:(b + 1) * cc, s * LANES:(s + 1) * LANES]
    ar = [a_ref[0:1, s * LANES:(s + 1) * LANES] for s in range(ns)]
    ai = [a_ref[1:2, s * LANES:(s + 1) * LANES] for s in range(ns)]

    def body(c, st):
        cidx = (cc - 1 - c) if reverse else c
        rows = pl.ds(cidx, nb, stride=ccp)
        new = [None] * (2 * ns)
        for s in range(ns):
            sr, si = st[s], st[ns + s]
            sin_scr[s, rows, :] = sr
            sin_scr[ns + s, rows, :] = si
            new[s] = ar[s] * sr - ai[s] * si + z_scr[s, rows, :]
            new[ns + s] = ar[s] * si + ai[s] * sr + z_scr[ns + s, rows, :]
        return tuple(new)

    st0 = tuple(carry_scr[:, s * LANES:(s + 1) * LANES] for s in range(2 * ns))
    st = lax.fori_loop(0, cc, body, st0)
    s_fin = jnp.concatenate(st, axis=1)
    carry_scr[...] = s_fin
    sfin_ref[...] = s_fin
    s_in = jnp.concatenate(
        [jnp.concatenate([sin_scr[s, b * ccp:b * ccp + cc, :] for b in range(nb)], axis=0)
         for s in range(2 * ns)], axis=1)
    y = _dot(ub, kin_ref[...]) + _dot(s_in.astype(BF16), wo_ref[...])
    if add_skip:
        y = y + u * d_ref[...]
    for b in range(nb):
        for i in range(ch):
            y_ref[b, pl.ds(i, cc, stride=ch), :] = y[b * cc:(b + 1) * cc, i * LANES:(i + 1) * LANES]


def _s5_scan(u4, s0, weights, d_tile, *, reverse, cc, add_skip):
    kin, ws, wo, a_chunk = weights
    nlb, nb, n_steps, _ = u4.shape
    width = S5_CHUNK * LANES
    sw2 = ws.shape[-1]
    nj = n_steps // (cc * S5_CHUNK)

    def jm(j):
        return (nj - 1 - j) if reverse else j

    return pl.pallas_call(
        functools.partial(_s5_kernel, reverse=reverse, nb=nb, cc=cc, add_skip=add_skip),
        grid=(nlb, nj),
        in_specs=[
            pl.BlockSpec((None, nb, cc * S5_CHUNK, LANES), lambda l, j: (l, 0, jm(j), 0)),
            pl.BlockSpec((None, nb, sw2), lambda l, j: (l, 0, 0)),
            pl.BlockSpec((None, 2, sw2 // 2), lambda l, j: (l, 0, 0)),
            pl.BlockSpec((None, width, width), lambda l, j: (l, 0, 0)),
            pl.BlockSpec((None, width, sw2), lambda l, j: (l, 0, 0)),
            pl.BlockSpec((None, sw2, width), lambda l, j: (l, 0, 0)),
            pl.BlockSpec((None, 1, width), lambda l, j: (l, 0, 0)),
        ],
        out_specs=[
            pl.BlockSpec((None, nb, cc * S5_CHUNK, LANES), lambda l, j: (l, 0, jm(j), 0)),
            pl.BlockSpec((None, nb, sw2), lambda l, j: (l, 0, 0)),
        ],
        out_shape=[
            jax.ShapeDtypeStruct(u4.shape, F32),
            jax.ShapeDtypeStruct((nlb, nb, sw2), F32),
        ],
        scratch_shapes=[
            pltpu.VMEM((sw2 // LANES, nb * (cc + S5_ROW_PAD), LANES), F32),
            pltpu.VMEM((sw2 // LANES, nb * (cc + S5_ROW_PAD), LANES), F32),
            pltpu.VMEM((nb, sw2), F32),
        ],
        compiler_params=_cparams(("arbitrary", "arbitrary")),
        name="s5_bwd" if reverse else "s5_fwd",
    )(u4, s0, a_chunk, kin, ws, wo, d_tile)


def _attn_kernel(sink_ref, q_ref, kp_ref, ko_ref, kn_ref, vp_ref, vo_ref, vn_ref, kc_ref, vc_ref,
                 o_ref, *, nblk, blk, qblocks):
    n = pl.program_id(1)
    n_ctx = kc_ref.shape[0]
    n_loc = 3 * blk
    nrow = Q_PER_KV * blk
    row = lax.broadcasted_iota(jnp.int32, (nrow, n_loc + n_ctx), 0)
    qi = row & (blk - 1)
    kj = lax.broadcasted_iota(jnp.int32, (nrow, n_loc + n_ctx), 1)
    rel = kj - qi
    in_window = (rel >= blk - WINDOW) & (rel <= blk + WINDOW)
    lane = lax.broadcasted_iota(jnp.int32, (blk, LANES), 1)
    low_half = lane < HEAD_DIM
    row1 = lax.broadcasted_iota(jnp.int32, (nrow, 1), 0)

    def valid_mask(j):
        gblk = n * qblocks + j
        lo = jnp.where(gblk == 0, blk, 0)
        hi = jnp.where(gblk == nblk - 1, 2 * blk, n_loc)
        return (in_window & (kj >= lo) & (kj < hi)) | (kj >= n_loc)

    valid = [valid_mask(j) for j in range(qblocks)]
    parts = [(j, g) for j in range(qblocks) for g in range(N_KV_HEADS)]
    each = lambda f: [f(j, g) for j, g in parts]
    at = {pg: i for i, pg in enumerate(parts)}

    def key_blocks(p_ref, own_ref, n_ref, c_ref, j, g):
        cs = slice(g * LANES, (g + 1) * LANES)
        blocks = [p_ref[:, cs]] + [own_ref[i * blk:(i + 1) * blk, cs] for i in range(qblocks)] + [n_ref[:, cs]]
        return jnp.concatenate(blocks[j:j + 3] + [c_ref[:, cs]], axis=0)

    kcat = each(lambda j, g: key_blocks(kp_ref, ko_ref, kn_ref, kc_ref, j, g))
    vcat = each(lambda j, g: key_blocks(vp_ref, vo_ref, vn_ref, vc_ref, j, g))

    def stacked_q(j, g):
        qms = []
        for hq in range(Q_PER_KV):
            h = g * Q_PER_KV + hq
            qb = q_ref[j * blk:(j + 1) * blk, (h // 2) * LANES:(h // 2 + 1) * LANES]
            qms.append(jnp.where(low_half if h % 2 == 0 else jnp.logical_not(low_half), qb, jnp.zeros_like(qb)))
        return jnp.concatenate(qms, axis=0)

    def stacked_sink(j, g):
        snk = jnp.zeros((nrow, 1), F32)
        for hq in range(Q_PER_KV):
            snk = jnp.where((row1 >= hq * blk) & (row1 < (hq + 1) * blk), sink_ref[g * Q_PER_KV + hq], snk)
        return snk

    qs = each(stacked_q)
    snk = each(stacked_sink)
    s = each(lambda j, g: lax.dot_general(qs[at[j, g]], kcat[at[j, g]], (((1,), (1,)), ((), ())),
                                          preferred_element_type=F32))
    s = each(lambda j, g: jnp.where(valid[j], s[at[j, g]], NEG_INF))
    mx = each(lambda j, g: jnp.maximum(jnp.max(s[at[j, g]], axis=1, keepdims=True), snk[at[j, g]]))
    p = each(lambda j, g: jnp.exp(s[at[j, g]] - mx[at[j, g]]))
    den = each(lambda j, g: jnp.sum(p[at[j, g]], axis=1, keepdims=True) + jnp.exp(snk[at[j, g]] - mx[at[j, g]]))
    o = each(lambda j, g: _dot(p[at[j, g]].astype(BF16), vcat[at[j, g]]) / den[at[j, g]])
    for j in range(qblocks):
        outs = [o[at[j, g]][hq * blk:(hq + 1) * blk] for g in range(N_KV_HEADS) for hq in range(Q_PER_KV)]
        blocks = [jnp.where(low_half, outs[2 * m], outs[2 * m + 1]) for m in range(N_Q_HEADS // 2)]
        o_ref[j * blk:(j + 1) * blk, :] = jnp.concatenate(blocks, axis=1).astype(BF16)


def _attention(q, kd, vd, kcd, vcd, sink, *, bsz, seq_len, n_ctx, qblocks):
    blk = WINDOW
    nblk = seq_len // blk
    nstep = nblk // qblocks
    aw = q.shape[1]
    kw = kd.shape[1]

    def qmap(b, n):
        return (b * nstep + n, 0)

    def pmap(b, n):
        return (b * nblk + jnp.maximum(n * qblocks - 1, 0), 0)

    def nmap(b, n):
        return (b * nblk + jnp.minimum((n + 1) * qblocks, nblk - 1), 0)

    edge = lambda f: pl.BlockSpec((blk, kw), f)
    own = pl.BlockSpec((qblocks * blk, kw), qmap)
    return pl.pallas_call(
        functools.partial(_attn_kernel, nblk=nblk, blk=blk, qblocks=qblocks),
        grid=(bsz, nstep),
        in_specs=[
            pl.BlockSpec(memory_space=pltpu.SMEM),
            pl.BlockSpec((qblocks * blk, aw), qmap),
            edge(pmap), own, edge(nmap),
            edge(pmap), own, edge(nmap),
            pl.BlockSpec((n_ctx, kw), lambda b, n: (b, 0)),
            pl.BlockSpec((n_ctx, kw), lambda b, n: (b, 0)),
        ],
        out_specs=pl.BlockSpec((qblocks * blk, aw), qmap),
        out_shape=jax.ShapeDtypeStruct(q.shape, BF16),
        compiler_params=_cparams(("arbitrary", "arbitrary")),
        name="attn",
    )(sink, q, kd, kd, kd, vd, vd, vd, kcd, vcd)


def _gelu_tanh(x):
    return 0.5 * x * (1.0 + jnp.tanh(math.sqrt(2.0 / math.pi) * (x + 0.044715 * (x * x * x))))


def _flatten_slots(v, max_value):
    tm, nk = v.shape
    per_row = LANES // nk
    log2 = lambda x: int(x).bit_length() - 1
    assert nk == 1 << log2(nk) and tm % per_row == 0
    spread = ((lax.broadcasted_iota(jnp.int32, (nk, LANES), 1) & (nk - 1))
              == lax.broadcasted_iota(jnp.int32, (nk, LANES), 0)).astype(BF16)
    t_id = lax.broadcasted_iota(jnp.int32, (tm, LANES), 0)
    c_id = lax.broadcasted_iota(jnp.int32, (tm, LANES), 1)
    own = (c_id >> log2(nk)) == (t_id & (per_row - 1))
    group = ((lax.broadcasted_iota(jnp.int32, (tm // per_row, tm), 1) >> log2(per_row))
             == lax.broadcasted_iota(jnp.int32, (tm // per_row, tm), 0)).astype(BF16)
    digit_bits = 7
    out = jnp.zeros((tm // per_row, LANES), jnp.int32)
    for shift in range(0, max(int(max_value).bit_length(), 1), digit_bits):
        digit = ((v >> shift) & ((1 << digit_bits) - 1)).astype(F32).astype(BF16)
        wide = jnp.dot(digit, spread, preferred_element_type=F32)
        wide = jnp.where(own, wide, 0.0).astype(BF16)
        out = out + (jnp.dot(group, wide, preferred_element_type=F32).astype(jnp.int32) << shift)
    return out


def _merge_kernel(yf_ref, yb_ref, o_ref, gs_ref, ga_ref, x_ref, lng_ref, lnb_ref, g1_ref, sh2_ref, sc2_ref,
                  wglu_ref, bglu_ref, wso_ref, wao_ref, wo_ref, l1g_ref, l1b_ref, wr_ref, br_ref,
                  h1_ref, idx_ref, gate_ref, rank_ref, cnt_ref, cnt_scr, *, alpha, n_exp, sub, n_tokens):
    i = pl.program_id(0)

    @pl.when(i == 0)
    def _():
        cnt_scr[...] = jnp.zeros_like(cnt_scr)

    nlb = yf_ref.shape[0]
    tm = x_ref.shape[0]
    lane = lax.broadcasted_iota(jnp.int32, (sub, n_exp), 1)
    lane_k = lax.broadcasted_iota(jnp.int32, (sub, TOP_K), 1)
    ri = lax.broadcasted_iota(jnp.int32, (sub, sub), 0)
    ci = lax.broadcasted_iota(jnp.int32, (sub, sub), 1)
    tri = jnp.where(ci < ri, 1.0, 0.0).astype(BF16)
    slots = sub * TOP_K // LANES
    cnt = cnt_scr[...]
    parts = range(tm // sub)
    rs = [slice(p * sub, (p + 1) * sub) for p in parts]
    each = lambda f: [f(p) for p in parts]
    y = each(lambda p: jnp.concatenate([yf_ref[lb, rs[p], :] + yb_ref[lb, rs[p], :] for lb in range(nlb)], axis=1))
    z = each(lambda p: _gelu_tanh(y[p]))
    zg = each(lambda p: _dot(z[p].astype(BF16), wglu_ref[...]) + bglu_ref[...])
    z = each(lambda p: z[p] * _sigmoid(zg[p]))
    ms = each(lambda p: _dot(z[p].astype(BF16), wso_ref[...]))
    ma = each(lambda p: _dot(o_ref[rs[p], :], wao_ref[...]))
    m = each(lambda p: gs_ref[rs[p], :].astype(F32) * ms[p] + ga_ref[rs[p], :].astype(F32) * ma[p])
    mix = each(lambda p: _dot(m[p].astype(BF16), wo_ref[...]))
    h = each(lambda p: _layer_norm(x_ref[rs[p], :], lng_ref[...], lnb_ref[...]))
    h1 = each(lambda p: _layer_norm(alpha * h[p] + g1_ref[...] * mix[p], l1g_ref[...], l1b_ref[...]))
    for p in parts:
        h1_ref[rs[p], :] = h1[p]
    xm = each(lambda p: h1[p] * (1.0 + sc2_ref[...]) + sh2_ref[...])
    work = each(lambda p: _dot_3pass(xm[p], wr_ref[...]) + br_ref[...])
    vals, sels = [], []
    for _ in range(TOP_K):
        mx = each(lambda p: jnp.max(work[p], axis=1, keepdims=True))
        sel = each(lambda p: jnp.min(jnp.where(work[p] == mx[p], lane, n_exp), axis=1, keepdims=True))
        work = each(lambda p: jnp.where(lane == sel[p], -jnp.inf, work[p]))
        vals.append(mx)
        sels.append(sel)
    exps = [each(lambda p: jnp.exp(vals[k][p] - vals[0][p])) for k in range(TOP_K)]
    den = each(lambda p: exps[0][p] + exps[1][p] + exps[2][p] + exps[3][p])
    onehot = each(lambda p: sum((lane == sels[k][p]).astype(F32) for k in range(TOP_K)))
    prefix = each(lambda p: _dot(tri, onehot[p].astype(BF16)))
    for p in parts:
        rank = prefix[p] + cnt
        idx_o = jnp.zeros((sub, TOP_K), jnp.int32)
        gate_o = jnp.zeros((sub, TOP_K), F32)
        rank_o = jnp.zeros((sub, TOP_K), jnp.int32)
        for k in range(TOP_K):
            rk = jnp.sum(jnp.where(lane == sels[k][p], rank, 0.0), axis=1, keepdims=True).astype(jnp.int32)
            idx_o = jnp.where(lane_k == k, sels[k][p], idx_o)
            gate_o = jnp.where(lane_k == k, exps[k][p] / den[p], gate_o)
            rank_o = jnp.where(lane_k == k, rk, rank_o)
        gate_ref[rs[p], :] = gate_o
        idx_ref[p * slots:(p + 1) * slots, :] = _flatten_slots(idx_o, n_exp - 1)
        rank_ref[p * slots:(p + 1) * slots, :] = _flatten_slots(rank_o, n_tokens - 1)
        cnt = cnt + jnp.sum(onehot[p], axis=0, keepdims=True)
    cnt_scr[...] = cnt
    cnt_ref[...] = cnt


def _merge(yf, yb, o_att, sgs, sga, x2, ln_g, ln_b, mod3, w_glu, b_glu, w_so, w_ao, w_o, l1g, l1b,
           w_r, b_r, *, rows_per_batch, tm, sub, alpha):
    t, d = x2.shape
    nlb = yf.shape[0]
    sw = nlb * LANES
    n_exp = w_r.shape[1]
    tpb = rows_per_batch // tm
    row = lambda i: (i, 0)
    const = lambda i: (0, 0)

    def modspec(chunk):
        return pl.BlockSpec((None, 1, d), lambda i: (i // tpb, 0, chunk))

    return pl.pallas_call(
        functools.partial(_merge_kernel, alpha=alpha, n_exp=n_exp, sub=sub, n_tokens=t),
        grid=(t // tm,),
        in_specs=[
            pl.BlockSpec((nlb, tm, LANES), lambda i: (0, i, 0)),
            pl.BlockSpec((nlb, tm, LANES), lambda i: (0, i, 0)),
            pl.BlockSpec((tm, o_att.shape[1]), row),
            pl.BlockSpec((tm, d), row),
            pl.BlockSpec((tm, d), row),
            pl.BlockSpec((tm, d), row),
            pl.BlockSpec((1, d), const),
            pl.BlockSpec((1, d), const),
            modspec(2), modspec(3), modspec(4),
            pl.BlockSpec(w_glu.shape, const),
            pl.BlockSpec((1, sw), const),
            pl.BlockSpec(w_so.shape, const),
            pl.BlockSpec(w_ao.shape, const),
            pl.BlockSpec(w_o.shape, const),
            pl.BlockSpec((1, d), const),
            pl.BlockSpec((1, d), const),
            pl.BlockSpec(w_r.shape, const),
            pl.BlockSpec((1, n_exp), const),
        ],
        out_specs=[
            pl.BlockSpec((tm, d), row),
            pl.BlockSpec((tm * TOP_K // LANES, LANES), row),
            pl.BlockSpec((tm, TOP_K), row),
            pl.BlockSpec((tm * TOP_K // LANES, LANES), row),
            pl.BlockSpec((1, n_exp), const),
        ],
        out_shape=[
            jax.ShapeDtypeStruct((t, d), F32),
            jax.ShapeDtypeStruct((t * TOP_K // LANES, LANES), jnp.int32),
            jax.ShapeDtypeStruct((t, TOP_K), F32),
            jax.ShapeDtypeStruct((t * TOP_K // LANES, LANES), jnp.int32),
            jax.ShapeDtypeStruct((1, n_exp), F32),
        ],
        scratch_shapes=[pltpu.VMEM((1, n_exp), F32)],
        compiler_params=_cparams(("arbitrary",)),
        name="merge",
    )(yf, yb, o_att, sgs, sga, x2, ln_g, ln_b, mod3, mod3, mod3, w_glu, b_glu, w_so, w_ao, w_o,
      l1g, l1b, w_r, b_r)


def _row_tile(ref, r):
    return ref.at[pl.ds(pl.multiple_of(r * ROW_TILES, ROW_TILES), ROW_TILES)]


def _store_row_tiles(ref, val):
    rows = val.shape[0]
    for l in range(ROW_TILES):
        ref[pl.ds(l, rows, stride=ROW_TILES), :] = val[:, l * LANES:(l + 1) * LANES]


def _load_row_tiles(ref, row0, rows):
    return jnp.concatenate(
        [ref[pl.ds(row0 * ROW_TILES + l, rows, stride=ROW_TILES), :] for l in range(ROW_TILES)], axis=1)


def _row_copy_out(src, dst, pos_ref, sem, t, k):
    return pltpu.make_async_copy(_row_tile(src, t), _row_tile(dst, pos_ref[t * TOP_K + k]), sem)


def _dispatch_kernel(pos_ref, prev_pos_ref, h1_ref, sh2_ref, sc2_ref, xrows_ref, xm_scr, sems):
    i = pl.program_id(0)
    n = pl.num_programs(0)
    tm = h1_ref.shape[0]
    slot = i % 2
    xm = h1_ref[...] * (1.0 + sc2_ref[...]) + sh2_ref[...]
    _store_row_tiles(xm_scr.at[slot], xm)

    def copies(slot_, idx_ref, op):
        def body(t, carry):
            for k in range(TOP_K):
                op(_row_copy_out(xm_scr.at[slot_], xrows_ref, idx_ref, sems.at[slot_], t, k), k)
            return carry
        lax.fori_loop(0, tm, body, 0, unroll=DMA_LOOP_UNROLL)

    start = lambda cp, k: cp.start(priority=k % N_DMA_PRIORITIES)
    wait = lambda cp, k: cp.wait()
    for s in range(2):
        @pl.when(slot == s)
        def _():
            copies(s, pos_ref, start)

    for s in range(2):
        @pl.when((slot == 1 - s) & (i > 0))
        def _():
            copies(s, prev_pos_ref, wait)

        @pl.when((slot == s) & (i == n - 1))
        def _():
            copies(s, pos_ref, wait)


def _dispatch(pos_flat, h1, mod3, *, rows_per_batch, tm):
    t, d = h1.shape
    tpb = rows_per_batch // tm
    return pl.pallas_call(
        _dispatch_kernel,
        grid=(t // tm,),
        in_specs=[
            pl.BlockSpec((tm * TOP_K,), lambda i: (i,), memory_space=pltpu.SMEM),
            pl.BlockSpec((tm * TOP_K,), lambda i: (jnp.maximum(i - 1, 0),), memory_space=pltpu.SMEM),
            pl.BlockSpec((tm, d), lambda i: (i, 0)),
            pl.BlockSpec((None, 1, d), lambda i: (i // tpb, 0, 3)),
            pl.BlockSpec((None, 1, d), lambda i: (i // tpb, 0, 4)),
        ],
        out_specs=pl.BlockSpec(memory_space=pl.ANY),
        out_shape=jax.ShapeDtypeStruct((t * TOP_K * ROW_TILES, LANES), F32),
        scratch_shapes=[pltpu.VMEM((2, tm * ROW_TILES, LANES), F32), pltpu.SemaphoreType.DMA((2,))],
        compiler_params=_cparams(("arbitrary",)),
        name="dispatch",
    )(pos_flat, pos_flat, h1, mod3, mod3)


def _experts_kernel(tile_ref, exp_ref, lo_ref, hi_ref, nxt_ref, x_ref, wgu_hbm, bgu_ref, wd_hbm, bd_ref, y_ref,
                    wgu_f32, wd_f32, wgu_scr, wd_scr, sems, *, rows, sub_rows):
    w = pl.program_id(0)
    prev = jnp.maximum(w - 1, 0)
    e_new = (w == 0) | (exp_ref[w] != exp_ref[prev])
    t_new = (w == 0) | (tile_ref[w] != tile_ref[prev])
    lo = lo_ref[w]
    hi = hi_ref[w]
    f = wd_scr.shape[0]

    def weight_copies(e):
        return (pltpu.make_async_copy(wgu_hbm.at[e], wgu_f32, sems.at[0]),
                pltpu.make_async_copy(wd_hbm.at[e], wd_f32, sems.at[1]))

    @pl.when(w == 0)
    def _():
        for cp in weight_copies(exp_ref[0]):
            cp.start()

    @pl.when(e_new)
    def _():
        for cp in weight_copies(exp_ref[w]):
            cp.wait()
        wgu_scr[...] = wgu_f32[...].astype(BF16)
        wd_scr[...] = wd_f32[...].astype(BF16)

        @pl.when(nxt_ref[w] >= 0)
        def _():
            for cp in weight_copies(nxt_ref[w]):
                cp.start()

    row0 = tile_ref[w] * rows
    whole = (lo <= row0) & (hi >= row0 + rows)

    @pl.when(t_new & jnp.logical_not(whole))
    def _():
        y_ref[...] = jnp.zeros_like(y_ref)

    def sub_block(sb):
        s_lo = row0 + sb * sub_rows
        s_hi = s_lo + sub_rows
        y_sub = y_ref.at[pl.ds(sb * sub_rows * ROW_TILES, sub_rows * ROW_TILES)]

        @pl.when((hi > s_lo) & (lo < s_hi))
        def _():
            x = _load_row_tiles(x_ref, sb * sub_rows, sub_rows)
            gu = _dot(x.astype(BF16), wgu_scr[...]) + bgu_ref[...]
            glu = jnp.minimum(gu[:, :f], SWIGLU_LIMIT)
            lin = jnp.clip(gu[:, f:], -SWIGLU_LIMIT, SWIGLU_LIMIT)
            act = glu * _sigmoid(SWIGLU_ALPHA * glu) * (lin + 1.0)
            y = _dot(act.astype(BF16), wd_scr[...]) + bd_ref[...]
            sub_whole = (lo <= s_lo) & (hi >= s_hi)

            @pl.when(sub_whole)
            def _():
                _store_row_tiles(y_sub, y)

            @pl.when(jnp.logical_not(sub_whole))
            def _():
                r = s_lo + lax.broadcasted_iota(jnp.int32, (sub_rows, 1), 0)
                _store_row_tiles(y_sub, jnp.where((r >= lo) & (r < hi), y, _load_row_tiles(y_sub, 0, sub_rows)))

    for sb in range(rows // sub_rows):
        sub_block(sb)


def _experts(work, xrows, w_gu, b_gu, w_d, b_d, *, rows, sub_rows):
    tile_id, exp_id, lo, hi, nxt = work
    n_exp, d, f2 = w_gu.shape
    f = w_d.shape[1]
    grid_spec = pltpu.PrefetchScalarGridSpec(
        num_scalar_prefetch=5,
        grid=(tile_id.shape[0],),
        in_specs=[
            pl.BlockSpec((rows * ROW_TILES, LANES), lambda w, ti, ex, lo, hi, nx: (ti[w], 0)),
            pl.BlockSpec(memory_space=pl.ANY),
            pl.BlockSpec((None, 1, f2), lambda w, ti, ex, lo, hi, nx: (ex[w], 0, 0)),
            pl.BlockSpec(memory_space=pl.ANY),
            pl.BlockSpec((None, 1, d), lambda w, ti, ex, lo, hi, nx: (ex[w], 0, 0)),
        ],
        out_specs=pl.BlockSpec((rows * ROW_TILES, LANES), lambda w, ti, ex, lo, hi, nx: (ti[w], 0)),
        scratch_shapes=[
            pltpu.VMEM((d, f2), F32), pltpu.VMEM((f, d), F32),
            pltpu.VMEM((d, f2), BF16), pltpu.VMEM((f, d), BF16),
            pltpu.SemaphoreType.DMA((2,)),
        ],
    )
    return pl.pallas_call(
        functools.partial(_experts_kernel, rows=rows, sub_rows=sub_rows),
        grid_spec=grid_spec,
        out_shape=jax.ShapeDtypeStruct(xrows.shape, F32),
        compiler_params=_cparams(("arbitrary",)),
        name="experts",
    )(tile_id, exp_id, lo, hi, nxt, xrows, w_gu, b_gu.reshape(n_exp, 1, f2), w_d, b_d.reshape(n_exp, 1, d))


def _row_copy_in(src, dst, pos_ref, sem, t, k, tm):
    return pltpu.make_async_copy(_row_tile(src, pos_ref[t * TOP_K + k]), _row_tile(dst, k * tm + t), sem)


def _combine_kernel(pos_ref, next_pos_ref, h1_ref, gate_ref, g2_ref, lg_ref, lb_ref, yrows_ref, o_ref,
                    buf, sems, *, alpha):
    i = pl.program_id(0)
    n = pl.num_programs(0)
    tm = h1_ref.shape[0]
    slot = i % 2

    def copies(slot_, idx_ref, op):
        def body(t, carry):
            for k in range(TOP_K):
                op(_row_copy_in(yrows_ref, buf.at[slot_], idx_ref, sems.at[slot_], t, k, tm), k)
            return carry
        lax.fori_loop(0, tm, body, 0, unroll=DMA_LOOP_UNROLL)

    start = lambda cp, k: cp.start(priority=k % N_DMA_PRIORITIES)
    wait = lambda cp, k: cp.wait()

    @pl.when(i == 0)
    def _():
        copies(0, pos_ref, start)

    for s in range(2):
        @pl.when((slot == 1 - s) & (i + 1 < n))
        def _():
            copies(s, next_pos_ref, start)

    def reduce(s):
        copies(s, pos_ref, wait)
        gates = gate_ref[...]
        ffn = gates[:, 0:1] * _load_row_tiles(buf.at[s], 0, tm)
        for k in range(1, TOP_K):
            ffn = ffn + gates[:, k:k + 1] * _load_row_tiles(buf.at[s], k * tm, tm)
        o_ref[...] = _layer_norm(alpha * h1_ref[...] + g2_ref[...] * ffn, lg_ref[...], lb_ref[...])

    for s in range(2):
        @pl.when(slot == s)
        def _():
            reduce(s)


def _combine(pos_flat, h1, gates, mod3, l2g, l2b, yrows, *, rows_per_batch, tm, alpha):
    t, d = h1.shape
    tpb = rows_per_batch // tm
    n_tiles = t // tm
    return pl.pallas_call(
        functools.partial(_combine_kernel, alpha=alpha),
        grid=(n_tiles,),
        in_specs=[
            pl.BlockSpec((tm * TOP_K,), lambda i: (i,), memory_space=pltpu.SMEM),
            pl.BlockSpec((tm * TOP_K,), lambda i: (jnp.minimum(i + 1, n_tiles - 1),), memory_space=pltpu.SMEM),
            pl.BlockSpec((tm, d), lambda i: (i, 0)),
            pl.BlockSpec((tm, TOP_K), lambda i: (i, 0)),
            pl.BlockSpec((None, 1, d), lambda i: (i // tpb, 0, 5)),
            pl.BlockSpec((1, d), lambda i: (0, 0)),
            pl.BlockSpec((1, d), lambda i: (0, 0)),
            pl.BlockSpec(memory_space=pl.ANY),
        ],
        out_specs=pl.BlockSpec((tm, d), lambda i: (i, 0)),
        out_shape=jax.ShapeDtypeStruct((t, d), F32),
        scratch_shapes=[pltpu.VMEM((2, TOP_K * tm * ROW_TILES, LANES), F32), pltpu.SemaphoreType.DMA((2,))],
        compiler_params=_cparams(("arbitrary",)),
        name="combine",
    )(pos_flat, pos_flat, h1, gates, mod3, l2g, l2b, yrows)


def _plan_kernel(cnt_ref, idx_ref, rank_ref, pos_ref, tile_out, exp_out, lo_out, hi_out, nxt_out,
                 start_s, end_s, next_s, *, rows, n_tiles):
    n_exp = cnt_ref.shape[0]
    n_work = tile_out.shape[0]

    def cumulate(e, acc):
        start_s[e] = acc
        end_s[e] = acc + cnt_ref[e]
        return acc + cnt_ref[e]

    lax.fori_loop(0, n_exp, cumulate, 0)

    def next_nonempty(i, cur):
        e = n_exp - 1 - i
        next_s[e] = cur
        return jnp.where(cnt_ref[e] > 0, e, cur)

    first = lax.fori_loop(0, n_exp, next_nonempty, -1)

    def item(w, carry):
        tile, e = carry
        live = tile < n_tiles
        tl = jnp.minimum(tile, n_tiles - 1)
        tile_lo = tl * rows
        tile_hi = tile_lo + rows
        lo = jnp.maximum(start_s[e], tile_lo)
        hi = jnp.where(live, jnp.minimum(end_s[e], tile_hi), lo)
        tile_out[w] = tl
        exp_out[w] = e
        lo_out[w] = lo
        hi_out[w] = hi
        nxt_out[w] = next_s[e]
        tile_done = live & (end_s[e] >= tile_hi)
        expert_done = live & (end_s[e] <= tile_hi) & (next_s[e] >= 0)
        return jnp.where(tile_done, tile + 1, tile), jnp.where(expert_done, next_s[e], e)

    lax.fori_loop(0, n_work, item, (0, first))

    idx = idx_ref[...]
    base = jnp.zeros(idx.shape, jnp.int32)
    for e in range(n_exp):
        base = jnp.where(idx == e, start_s[e], base)
    pos_ref[...] = base + rank_ref[...]


def _plan(counts, idx_flat, rank_flat, *, rows):
    n_exp = counts.shape[0]
    n_rows = idx_flat.shape[0] * idx_flat.shape[1]
    n_tiles = n_rows // rows
    n_work = n_tiles + n_exp - 1
    smem = pl.BlockSpec(memory_space=pltpu.SMEM)
    vmem = pl.BlockSpec(memory_space=pltpu.VMEM)
    return pl.pallas_call(
        functools.partial(_plan_kernel, rows=rows, n_tiles=n_tiles),
        in_specs=[smem, vmem, vmem],
        out_specs=[vmem] + [smem] * 5,
        out_shape=[jax.ShapeDtypeStruct(idx_flat.shape, jnp.int32)]
        + [jax.ShapeDtypeStruct((n_work,), jnp.int32)] * 5,
        scratch_shapes=[pltpu.SMEM((n_exp,), jnp.int32)] * 3,
        name="plan",
    )(counts, idx_flat, rank_flat)


def kernel(x, c, ctx, c_ctx, ln_in_g, ln_in_b, w_mod, b_mod, w_in, ssm_lam_re, ssm_lam_im, ssm_log_step, ssm_b_re, ssm_b_im, ssm_c_re, ssm_c_im, ssm_d, w_glu, b_glu, attn_sink, w_ssm_out, w_att_out, w_o, ln1_g, ln1_b, w_router, b_router, w_gate_up, b_gate_up, w_down, b_down, ln2_g, ln2_b):
    bsz, seq_len, d = x.shape
    n_ctx = ctx.shape[1]
    depth = w_mod.shape[0]
    assert depth == 1, "single-layer kernel"
    g_ssm, h_ssm = ssm_d.shape[1:]
    ssm_w = g_ssm * h_ssm
    attn_w = N_Q_HEADS * HEAD_DIM
    kv_w = N_KV_HEADS * HEAD_DIM
    kv2_w = 2 * kv_w
    nlb = ssm_w // LANES
    alpha = (2.0 * depth) ** 0.25
    t = bsz * seq_len
    assert bsz + 1 <= SUBLANES
    assert d == ROW_TILES * LANES and ROW_TILES == SUBLANES, "a token row must be exactly one (8, 128) tile"

    row2 = lambda a: a.reshape(1, -1)

    c_rows = jnp.concatenate([c, c_ctx[None], jnp.zeros((SUBLANES - bsz - 1, d), F32)], axis=0)
    mod = _mod_vectors(c_rows, w_mod[0], row2(b_mod[0]))
    mod3 = mod.reshape(SUBLANES, 1, 6 * d)

    wi = w_in[0]
    i1 = ssm_w + attn_w
    i2, i3 = i1 + kv_w, i1 + 2 * kv_w
    assert wi.shape[1] == i3 + 2 * d

    def dup_heads(wkv):
        parts = []
        for hk in range(N_KV_HEADS):
            blk = wkv[:, hk * HEAD_DIM:(hk + 1) * HEAD_DIM]
            parts += [blk, blk]
        return jnp.concatenate(parts, axis=1)

    w_all = wi.astype(BF16)
    w_kv2 = jnp.concatenate([dup_heads(wi[:, i1:i2]), dup_heads(wi[:, i2:i3])], axis=1).astype(BF16)

    pos = jnp.arange(seq_len)
    inv = ROPE_BASE ** (-jnp.arange(ROPE_PAIRS, dtype=F32) / ROPE_PAIRS)
    ang = jnp.concatenate([(pos // GRID_W).astype(F32)[:, None] * inv,
                           (pos % GRID_W).astype(F32)[:, None] * inv], axis=-1)
    cos_t = jnp.tile(jnp.cos(ang), (1, 2 * LANES // HEAD_DIM))
    sin_h = jnp.sin(ang)
    sin_t = jnp.tile(jnp.concatenate([-sin_h, sin_h], axis=-1), (1, LANES // HEAD_DIM))

    g_in, b_in = row2(ln_in_g), row2(ln_in_b)
    x2 = x.reshape(t, d)
    ctx2 = ctx.reshape(bsz * n_ctx, d)
    s_c, k_c, v_c = _inproj(ctx2, g_in, b_in, mod3, w_all, w_kv2, None, None, latent=False,
                            rows_per_batch=n_ctx, tm=n_ctx, ctx_mod_row=bsz,
                            ssm_w=ssm_w, attn_w=attn_w, kv2_w=kv2_w)
    s_l, q_l, k_l, v_l, sgs, sga = _inproj(x2, g_in, b_in, mod3, w_all, w_kv2, cos_t, sin_t, latent=True,
                                           rows_per_batch=seq_len, tm=512, ctx_mod_row=None,
                                           ssm_w=ssm_w, attn_w=attn_w, kv2_w=kv2_w)

    wf = _s5_weights(ssm_lam_re[0, 0], ssm_lam_im[0, 0], ssm_log_step[0, 0], ssm_b_re[0, 0], ssm_b_im[0, 0],
                     ssm_c_re[0, 0], ssm_c_im[0, 0], reverse=False)
    wb = _s5_weights(ssm_lam_re[0, 1], ssm_lam_im[0, 1], ssm_log_step[0, 1], ssm_b_re[0, 1], ssm_b_im[0, 1],
                     ssm_c_re[0, 1], ssm_c_im[0, 1], reverse=True)
    d_tile = jnp.tile(ssm_d[0].astype(F32).reshape(nlb, 1, LANES), (1, 1, S5_CHUNK))
    sw2 = wf[1].shape[-1]
    zero_state = jnp.zeros((nlb, bsz, sw2), F32)
    uc4 = s_c.reshape(nlb, bsz, n_ctx, LANES)
    ul4 = s_l.reshape(nlb, bsz, seq_len, LANES)
    _, sf0 = _s5_scan(uc4, zero_state, wf, d_tile, reverse=False, cc=n_ctx // S5_CHUNK, add_skip=False)
    _, sb0 = _s5_scan(uc4, zero_state, wb, d_tile, reverse=True, cc=n_ctx // S5_CHUNK, add_skip=False)
    yf4, _ = _s5_scan(ul4, sf0, wf, d_tile, reverse=False, cc=128, add_skip=True)
    yb4, _ = _s5_scan(ul4, sb0, wb, d_tile, reverse=True, cc=128, add_skip=False)
    yf = yf4.reshape(nlb, t, LANES)
    yb = yb4.reshape(nlb, t, LANES)

    o_att = _attention(q_l, k_l, v_l, k_c, v_c, attn_sink[0].astype(F32), bsz=bsz, seq_len=seq_len, n_ctx=n_ctx,
                       qblocks=2)

    h1, top_i, gates, rank, counts = _merge(
        yf, yb, o_att, sgs, sga, x2, g_in, b_in, mod3, w_glu[0].astype(BF16), row2(b_glu[0]),
        w_ssm_out[0].astype(BF16), w_att_out[0].astype(BF16), w_o[0].astype(BF16), row2(ln1_g[0]), row2(ln1_b[0]),
        w_router[0], row2(b_router[0]), rows_per_batch=seq_len, tm=1024, sub=256, alpha=alpha)

    rows, sub_rows = 512, 256
    n_rows = t * TOP_K
    pos, tile_id, exp_id, lo, hi, nxt = _plan(counts[0].astype(jnp.int32), top_i, rank, rows=rows)
    pos_flat = pos.reshape(n_rows)
    xrows = _dispatch(pos_flat, h1, mod3, rows_per_batch=seq_len, tm=256)
    yrows = _experts((tile_id, exp_id, lo, hi, nxt), xrows, w_gate_up[0], b_gate_up[0], w_down[0], b_down[0],
                     rows=rows, sub_rows=sub_rows)
    out = _combine(pos_flat, h1, gates, mod3, row2(ln2_g[0]), row2(ln2_b[0]), yrows,
                   rows_per_batch=seq_len, tm=256, alpha=alpha)
    return out.reshape(bsz, seq_len, d)
```

```python
import functools
import math

import jax
import jax.numpy as jnp
import numpy as np
from jax import lax
from jax.experimental import pallas as pl
from jax.experimental.pallas import tpu as pltpu

F32 = jnp.float32
BF16 = jnp.bfloat16
HIGHEST = lax.Precision.HIGHEST

HEAD_DIM = 64
N_Q_HEADS = 8
N_KV_HEADS = 2
Q_PER_KV = N_Q_HEADS // N_KV_HEADS
WINDOW = 128
GRID_W = 64
ROPE_BASE = 10000.0
ROPE_PAIRS = HEAD_DIM // 4
TOP_K = 4
SWIGLU_LIMIT = 7.0
SWIGLU_ALPHA = 1.702
LN_EPS = 1e-5
NEG_INF = -1e30

LANES = 128
SUBLANES = 8
MXU_DIM = 256
ROW_TILES = 8
VMEM_LIMIT = 56 * 1024 * 1024
N_DMA_PRIORITIES = 2
DMA_LOOP_UNROLL = 8

S5_CHUNK = 8
S5_ROW_PAD = SUBLANES


def _cparams(sem):
    return pltpu.CompilerParams(dimension_semantics=sem, vmem_limit_bytes=VMEM_LIMIT)


def _sigmoid(x):
    return 1.0 / (1.0 + jnp.exp(-x))


def _layer_norm(x, g, b):
    mu = jnp.mean(x, axis=-1, keepdims=True)
    xc = x - mu
    var = jnp.mean(xc * xc, axis=-1, keepdims=True)
    return xc * lax.rsqrt(var + LN_EPS) * g + b


def _dot(a, b):
    return jnp.dot(a, b, preferred_element_type=F32)


def _dot_3pass(a, b):
    a_hi = a.astype(BF16)
    b_hi = b.astype(BF16)
    a_lo = (a - a_hi.astype(F32)).astype(BF16)
    b_lo = (b - b_hi.astype(F32)).astype(BF16)
    return _dot(a_hi, b_hi) + (_dot(a_hi, b_lo) + _dot(a_lo, b_hi))


def _mod_kernel(c_ref, w_ref, b_ref, o_ref):
    c = c_ref[...]
    a = c * _sigmoid(c)
    o_ref[...] = jnp.dot(a, w_ref[...], preferred_element_type=F32, precision=HIGHEST) + b_ref[...]


def _mod_vectors(c_rows, w_mod, b_mod):
    d = c_rows.shape[1]
    n = w_mod.shape[1]
    return pl.pallas_call(
        _mod_kernel,
        grid=(n // d,),
        in_specs=[
            pl.BlockSpec((SUBLANES, d), lambda i: (0, 0)),
            pl.BlockSpec((d, d), lambda i: (0, i)),
            pl.BlockSpec((1, d), lambda i: (0, i)),
        ],
        out_specs=pl.BlockSpec((SUBLANES, d), lambda i: (0, i)),
        out_shape=jax.ShapeDtypeStruct((SUBLANES, n), F32),
        compiler_params=_cparams(("arbitrary",)),
        name="mod",
    )(c_rows, w_mod, b_mod)


def _rope(t, cos, sin):
    n = t.shape[1]
    reps = n // LANES
    c = jnp.concatenate([cos] * reps, axis=1) if reps > 1 else cos
    s = jnp.concatenate([sin] * reps, axis=1) if reps > 1 else sin
    half = HEAD_DIM // 2
    upper = pltpu.roll(t, n - half, axis=1)
    lower = pltpu.roll(t, half, axis=1)
    lane = lax.broadcasted_iota(jnp.int32, t.shape, 1)
    partner = jnp.where((lane & half) == 0, upper, lower)
    return t * c + partner * s


def _inproj_kernel(*refs, latent, ssm_w, attn_w, kv2_w, d_model, sub):
    if latent:
        (x_ref, g_ref, b_ref, sh_ref, sc_ref, w_ref, wkv_ref, cos_ref, sin_ref,
         s_ref, q_ref, k_ref, v_ref, gs_ref, ga_ref) = refs
    else:
        x_ref, g_ref, b_ref, sh_ref, sc_ref, w_ref, wkv_ref, s_ref, k_ref, v_ref = refs
    tm = x_ref.shape[0]
    parts = range(tm // sub)
    rs = [slice(p * sub, (p + 1) * sub) for p in parts]
    each = lambda f: [f(p) for p in parts]
    h = each(lambda p: _layer_norm(x_ref[rs[p], :], g_ref[...], b_ref[...]))
    u = each(lambda p: (h[p] * (1.0 + sc_ref[...]) + sh_ref[...]).astype(BF16))
    s = each(lambda p: _dot(u[p], w_ref[:, 0:ssm_w]))
    for p in parts:
        for lb in range(ssm_w // LANES):
            s_ref[lb, rs[p], :] = s[p][:, lb * LANES:(lb + 1) * LANES]
    if latent:
        q = each(lambda p: _dot(u[p], w_ref[:, ssm_w:ssm_w + attn_w]))
        q = each(lambda p: _rope(q[p], cos_ref[rs[p], :], sin_ref[rs[p], :]) * (HEAD_DIM ** -0.5))
        for p in parts:
            q_ref[rs[p], :] = q[p].astype(BF16)
    k = each(lambda p: _dot(u[p], wkv_ref[:, 0:kv2_w]))
    if latent:
        k = each(lambda p: _rope(k[p], cos_ref[rs[p], :], sin_ref[rs[p], :]))
    v = each(lambda p: _dot(u[p], wkv_ref[:, kv2_w:2 * kv2_w]))
    for p in parts:
        k_ref[rs[p], :] = k[p].astype(BF16)
        v_ref[rs[p], :] = v[p].astype(BF16)
    if latent:
        col = ssm_w + attn_w + kv2_w
        gs = each(lambda p: _dot(u[p], w_ref[:, col:col + d_model]))
        ga = each(lambda p: _dot(u[p], w_ref[:, col + d_model:col + 2 * d_model]))
        for p in parts:
            gs_ref[rs[p], :] = _sigmoid(gs[p]).astype(BF16)
            ga_ref[rs[p], :] = _sigmoid(ga[p]).astype(BF16)


def _inproj(x2, ln_g, ln_b, mod3, w_all, w_kv2, cos_t, sin_t, *, latent, rows_per_batch, tm, sub, ctx_mod_row,
            ssm_w, attn_w, kv2_w):
    t, d = x2.shape
    tiles_per_batch = rows_per_batch // tm
    nlb = ssm_w // LANES

    def brow(i):
        return i // tiles_per_batch if ctx_mod_row is None else ctx_mod_row

    in_specs = [
        pl.BlockSpec((tm, d), lambda i: (i, 0)),
        pl.BlockSpec((1, d), lambda i: (0, 0)),
        pl.BlockSpec((1, d), lambda i: (0, 0)),
        pl.BlockSpec((None, 1, d), lambda i: (brow(i), 0, 0)),
        pl.BlockSpec((None, 1, d), lambda i: (brow(i), 0, 1)),
        pl.BlockSpec(w_all.shape, lambda i: (0, 0)),
        pl.BlockSpec(w_kv2.shape, lambda i: (0, 0)),
    ]
    args = [x2, ln_g, ln_b, mod3, mod3, w_all, w_kv2]
    out_specs = [pl.BlockSpec((nlb, tm, LANES), lambda i: (0, i, 0))]
    out_shape = [jax.ShapeDtypeStruct((nlb, t, LANES), F32)]
    if latent:
        in_specs += [
            pl.BlockSpec((tm, LANES), lambda i: (i % tiles_per_batch, 0)),
            pl.BlockSpec((tm, LANES), lambda i: (i % tiles_per_batch, 0)),
        ]
        args += [cos_t, sin_t]
        out_specs.append(pl.BlockSpec((tm, attn_w), lambda i: (i, 0)))
        out_shape.append(jax.ShapeDtypeStruct((t, attn_w), BF16))
    out_specs += [pl.BlockSpec((tm, kv2_w), lambda i: (i, 0))] * 2
    out_shape += [jax.ShapeDtypeStruct((t, kv2_w), BF16)] * 2
    if latent:
        out_specs += [pl.BlockSpec((tm, d), lambda i: (i, 0))] * 2
        out_shape += [jax.ShapeDtypeStruct((t, d), BF16)] * 2
    return pl.pallas_call(
        functools.partial(_inproj_kernel, latent=latent, ssm_w=ssm_w, attn_w=attn_w, kv2_w=kv2_w,
                          d_model=d, sub=sub),
        grid=(t // tm,),
        in_specs=in_specs,
        out_specs=out_specs,
        out_shape=out_shape,
        compiler_params=_cparams(("arbitrary",)),
        name="inproj_latent" if latent else "inproj_ctx",
    )(*args)


def _lane_expand(tbl, n_copies, rows_per_group, cols_per_group):
    r, w = tbl.shape
    wide_w = n_copies * w
    log2 = lambda v: int(v).bit_length() - 1
    assert all(v == 1 << log2(v) for v in (w, rows_per_group, cols_per_group))
    sel = ((lax.broadcasted_iota(jnp.int32, (w, wide_w), 1) & (w - 1))
           == lax.broadcasted_iota(jnp.int32, (w, wide_w), 0)).astype(BF16)
    wide = jnp.dot(tbl.astype(BF16), sel, preferred_element_type=F32)
    keep = ((lax.broadcasted_iota(jnp.int32, (r, wide_w), 0) >> log2(rows_per_group))
            == (lax.broadcasted_iota(jnp.int32, (r, wide_w), 1) >> log2(cols_per_group)))
    return jnp.where(keep, wide, 0.0)


def _s5_assemble_kernel(m_ref, ws_ref, wo_ref, kin_out, ws_out, wo_out, *, reverse):
    ch, _, hh = m_ref.shape
    p = ws_ref.shape[-1]
    gpb = LANES // hh
    sw = gpb * p
    bd = [_lane_expand(m_ref[j], gpb, hh, hh).astype(BF16) for j in range(ch)]
    zero = jnp.zeros((LANES, LANES), BF16)
    for k in range(ch):
        for i in range(ch):
            lag = (k - i) if reverse else (i - k)
            kin_out[k * LANES:(k + 1) * LANES, i * LANES:(i + 1) * LANES] = bd[lag] if lag >= 0 else zero
    for ri in range(2):
        for k in range(ch):
            ws_out[k * LANES:(k + 1) * LANES, ri * sw:(ri + 1) * sw] = (
                _lane_expand(ws_ref[ri, k], gpb, hh, p).astype(BF16))
        for i in range(ch):
            wo_out[ri * sw:(ri + 1) * sw, i * LANES:(i + 1) * LANES] = (
                _lane_expand(wo_ref[ri, i], gpb, p, hh).astype(BF16))


def _s5_assemble(m_tbl, ws_tbl, wo_tbl, *, reverse):
    ch, nlb, _, hh = m_tbl.shape
    p = ws_tbl.shape[-1]
    sw = (LANES // hh) * p
    width = ch * LANES
    return pl.pallas_call(
        functools.partial(_s5_assemble_kernel, reverse=reverse),
        grid=(nlb,),
        in_specs=[
            pl.BlockSpec((ch, None, LANES, hh), lambda l: (0, l, 0, 0)),
            pl.BlockSpec((2, ch, None, LANES, p), lambda l: (0, 0, l, 0, 0)),
            pl.BlockSpec((2, ch, None, sw, hh), lambda l: (0, 0, l, 0, 0)),
        ],
        out_specs=[
            pl.BlockSpec((None, width, width), lambda l: (l, 0, 0)),
            pl.BlockSpec((None, width, 2 * sw), lambda l: (l, 0, 0)),
            pl.BlockSpec((None, 2 * sw, width), lambda l: (l, 0, 0)),
        ],
        out_shape=[
            jax.ShapeDtypeStruct((nlb, width, width), BF16),
            jax.ShapeDtypeStruct((nlb, width, 2 * sw), BF16),
            jax.ShapeDtypeStruct((nlb, 2 * sw, width), BF16),
        ],
        compiler_params=_cparams(("arbitrary",)),
        name="s5_assemble_bwd" if reverse else "s5_assemble_fwd",
    )(m_tbl, ws_tbl, wo_tbl)


def _s5_weights(lam_re, lam_im, log_step, b_re, b_im, c_re, c_im, reverse):
    ch = S5_CHUNK
    g, p = lam_re.shape
    hh = b_re.shape[-1]
    gpb = LANES // hh
    nlb = g // gpb
    lr, li = lam_re.astype(F32), lam_im.astype(F32)
    dt = jnp.exp(log_step.astype(F32))[:, None]
    jj = jnp.arange(ch + 1, dtype=F32)[:, None, None]
    mag = jnp.exp(jj * lr * dt)
    pr = mag * jnp.cos(jj * li * dt)
    pi = mag * jnp.sin(jj * li * dt)
    ar, ai = pr[1], pi[1]
    den = lr * lr + li * li
    cr = ((ar - 1) * lr + ai * li) / den
    ci = (ai * lr - (ar - 1) * li) / den
    br, bi = b_re.astype(F32), b_im.astype(F32)
    bbr = cr[..., None] * br - ci[..., None] * bi
    bbi = cr[..., None] * bi + ci[..., None] * br
    ccr, cci = c_re.astype(F32), c_im.astype(F32)
    bbr_t = jnp.swapaxes(bbr, 1, 2)
    bbi_t = jnp.swapaxes(bbi, 1, 2)
    ccr_t = jnp.swapaxes(ccr, 1, 2)
    cci_t = jnp.swapaxes(cci, 1, 2)

    er = ccr[None] * pr[:ch, :, None, :] - cci[None] * pi[:ch, :, None, :]
    ei = ccr[None] * pi[:ch, :, None, :] + cci[None] * pr[:ch, :, None, :]
    m = jnp.sum(er[:, :, None, :, :] * bbr_t[None, :, :, None, :]
                - ei[:, :, None, :, :] * bbi_t[None, :, :, None, :], axis=-1)
    m_tbl = m.reshape(ch, nlb, LANES, hh)
    rk = np.arange(ch) if reverse else (ch - 1 - np.arange(ch))
    apr, api = pr[rk][:, :, None, :], pi[rk][:, :, None, :]
    sr = apr * bbr_t[None] - api * bbi_t[None]
    si = apr * bbi_t[None] + api * bbr_t[None]
    ws_tbl = jnp.stack([sr, si], axis=0).reshape(2, ch, nlb, LANES, p)
    ex = (ch - np.arange(ch)) if reverse else (np.arange(ch) + 1)
    epr, epi = pr[ex][:, :, :, None], pi[ex][:, :, :, None]
    wo_r = ccr_t[None] * epr - cci_t[None] * epi
    wo_i = -(ccr_t[None] * epi + cci_t[None] * epr)
    wo_tbl = jnp.stack([wo_r, wo_i], axis=0).reshape(2, ch, nlb, gpb * p, hh)
    kin, ws, wo = _s5_assemble(m_tbl, ws_tbl, wo_tbl, reverse=reverse)
    a_chunk = jnp.stack([pr[ch].reshape(nlb, gpb * p), pi[ch].reshape(nlb, gpb * p)], axis=1)
    return kin, ws, wo, a_chunk


def _s5_kernel(u_ref, s0_ref, a_ref, kin_ref, ws_ref, wo_ref, d_ref, y_ref, sfin_ref,
               z_scr, sin_scr, carry_scr, *, reverse, nb, cc, add_skip):
    j = pl.program_id(1)

    @pl.when(j == 0)
    def _():
        carry_scr[...] = s0_ref[...]

    ch = S5_CHUNK
    width = ch * LANES
    sw = a_ref.shape[-1]
    ns = sw // LANES
    u = jnp.concatenate(
        [jnp.concatenate([u_ref[b, pl.ds(k, cc, stride=ch), :] for b in range(nb)], axis=0)
         for k in range(ch)], axis=1)
    ub = u.astype(BF16)
    z = _dot(ub, ws_ref[...])
    ccp = cc + S5_ROW_PAD
    for s in range(2 * ns):
        for b in range(nb):
            z_scr[s, b * ccp:b * ccp + cc, :] = z[b * cc:(b + 1) * cc, s * LANES:(s + 1) * LANES]
    ar = [a_ref[0:1, s * LANES:(s + 1) * LANES] for s in range(ns)]
    ai = [a_ref[1:2, s * LANES:(s + 1) * LANES] for s in range(ns)]

    def body(c, st):
        cidx = (cc - 1 - c) if reverse else c
        rows = pl.ds(cidx, nb, stride=ccp)
        new = [None] * (2 * ns)
        for s in range(ns):
            sr, si = st[s], st[ns + s]
            sin_scr[s, rows, :] = sr
            sin_scr[ns + s, rows, :] = si
            new[s] = ar[s] * sr - ai[s] * si + z_scr[s, rows, :]
            new[ns + s] = ar[s] * si + ai[s] * sr + z_scr[ns + s, rows, :]
        return tuple(new)

    st0 = tuple(carry_scr[:, s * LANES:(s + 1) * LANES] for s in range(2 * ns))
    st = lax.fori_loop(0, cc, body, st0)
    s_fin = jnp.concatenate(st, axis=1)
    carry_scr[...] = s_fin
    sfin_ref[...] = s_fin
    s_in = jnp.concatenate(
        [jnp.concatenate([sin_scr[s, b * ccp:b * ccp + cc, :] for b in range(nb)], axis=0)
         for s in range(2 * ns)], axis=1)
    intra = []
    for c0 in range(0, width, MXU_DIM):
        ks = slice(c0, width) if reverse else slice(0, c0 + MXU_DIM)
        intra.append(_dot(ub[:, ks], kin_ref[ks, c0:c0 + MXU_DIM]))
    y = jnp.concatenate(intra, axis=1) + _dot(s_in.astype(BF16), wo_ref[...])
    if add_skip:
        y = y + u * d_ref[...]
    for b in range(nb):
        for i in range(ch):
            y_ref[b, pl.ds(i, cc, stride=ch), :] = y[b * cc:(b + 1) * cc, i * LANES:(i + 1) * LANES]


def _s5_scan(u4, s0, weights, d_tile, *, reverse, cc, add_skip):
    kin, ws, wo, a_chunk = weights
    nlb, nb, n_steps, _ = u4.shape
    width = S5_CHUNK * LANES
    sw2 = ws.shape[-1]
    nj = n_steps // (cc * S5_CHUNK)

    def jm(j):
        return (nj - 1 - j) if reverse else j

    return pl.pallas_call(
        functools.partial(_s5_kernel, reverse=reverse, nb=nb, cc=cc, add_skip=add_skip),
        grid=(nlb, nj),
        in_specs=[
            pl.BlockSpec((None, nb, cc * S5_CHUNK, LANES), lambda l, j: (l, 0, jm(j), 0)),
            pl.BlockSpec((None, nb, sw2), lambda l, j: (l, 0, 0)),
            pl.BlockSpec((None, 2, sw2 // 2), lambda l, j: (l, 0, 0)),
            pl.BlockSpec((None, width, width), lambda l, j: (l, 0, 0)),
            pl.BlockSpec((None, width, sw2), lambda l, j: (l, 0, 0)),
            pl.BlockSpec((None, sw2, width), lambda l, j: (l, 0, 0)),
            pl.BlockSpec((None, 1, width), lambda l, j: (l, 0, 0)),
        ],
        out_specs=[
            pl.BlockSpec((None, nb, cc * S5_CHUNK, LANES), lambda l, j: (l, 0, jm(j), 0)),
            pl.BlockSpec((None, nb, sw2), lambda l, j: (l, 0, 0)),
        ],
        out_shape=[
            jax.ShapeDtypeStruct(u4.shape, F32),
            jax.ShapeDtypeStruct((nlb, nb, sw2), F32),
        ],
        scratch_shapes=[
            pltpu.VMEM((sw2 // LANES, nb * (cc + S5_ROW_PAD), LANES), F32),
            pltpu.VMEM((sw2 // LANES, nb * (cc + S5_ROW_PAD), LANES), F32),
            pltpu.VMEM((nb, sw2), F32),
        ],
        compiler_params=_cparams(("arbitrary", "arbitrary")),
        name="s5_bwd" if reverse else "s5_fwd",
    )(u4, s0, a_chunk, kin, ws, wo, d_tile)


def _attn_kernel(sink_ref, q_ref, kp_ref, ko_ref, kn_ref, vp_ref, vo_ref, vn_ref, kc_ref, vc_ref,
                 o_ref, *, nblk, blk, qblocks):
    n = pl.program_id(1)
    n_ctx = kc_ref.shape[0]
    n_loc = 3 * blk
    nrow = Q_PER_KV * blk
    row = lax.broadcasted_iota(jnp.int32, (nrow, n_loc), 0)
    qi = row & (blk - 1)
    kj = lax.broadcasted_iota(jnp.int32, (nrow, n_loc), 1)
    rel = kj - qi
    in_window = (rel >= blk - WINDOW) & (rel <= blk + WINDOW)
    lane = lax.broadcasted_iota(jnp.int32, (blk, LANES), 1)
    low_half = lane < HEAD_DIM
    row1 = lax.broadcasted_iota(jnp.int32, (nrow, 1), 0)

    def valid_mask(j):
        gblk = n * qblocks + j
        lo = jnp.where(gblk == 0, blk, 0)
        hi = jnp.where(gblk == nblk - 1, 2 * blk, n_loc)
        return in_window & (kj >= lo) & (kj < hi)

    valid = [valid_mask(j) for j in range(qblocks)]
    parts = [(j, g) for j in range(qblocks) for g in range(N_KV_HEADS)]
    each = lambda f: [f(j, g) for j, g in parts]
    at = {pg: i for i, pg in enumerate(parts)}

    def key_blocks(p_ref, own_ref, n_ref, c_ref, j, g):
        cs = slice(g * LANES, (g + 1) * LANES)
        blocks = [p_ref[:, cs]] + [own_ref[i * blk:(i + 1) * blk, cs] for i in range(qblocks)] + [n_ref[:, cs]]
        return jnp.concatenate(blocks[j:j + 3] + [c_ref[:, cs]], axis=0)

    kcat = each(lambda j, g: key_blocks(kp_ref, ko_ref, kn_ref, kc_ref, j, g))
    vcat = each(lambda j, g: key_blocks(vp_ref, vo_ref, vn_ref, vc_ref, j, g))

    def stacked_q(j, g):
        qms = []
        for hq in range(Q_PER_KV):
            h = g * Q_PER_KV + hq
            qb = q_ref[j * blk:(j + 1) * blk, (h // 2) * LANES:(h // 2 + 1) * LANES]
            qms.append(jnp.where(low_half if h % 2 == 0 else jnp.logical_not(low_half), qb, jnp.zeros_like(qb)))
        return jnp.concatenate(qms, axis=0)

    def stacked_sink(j, g):
        snk = jnp.zeros((nrow, 1), F32)
        for hq in range(Q_PER_KV):
            snk = jnp.where((row1 >= hq * blk) & (row1 < (hq + 1) * blk), sink_ref[g * Q_PER_KV + hq], snk)
        return snk

    qs = each(stacked_q)
    snk = each(stacked_sink)
    s = each(lambda j, g: lax.dot_general(qs[at[j, g]], kcat[at[j, g]], (((1,), (1,)), ((), ())),
                                          preferred_element_type=F32))
    s = each(lambda j, g: jnp.concatenate(
        [jnp.where(valid[j], s[at[j, g]][:, :n_loc], NEG_INF), s[at[j, g]][:, n_loc:]], axis=1))
    mx = each(lambda j, g: jnp.maximum(jnp.max(s[at[j, g]], axis=1, keepdims=True), snk[at[j, g]]))
    p = each(lambda j, g: jnp.exp(s[at[j, g]] - mx[at[j, g]]))
    den = each(lambda j, g: jnp.sum(p[at[j, g]], axis=1, keepdims=True) + jnp.exp(snk[at[j, g]] - mx[at[j, g]]))
    o = each(lambda j, g: _dot(p[at[j, g]].astype(BF16), vcat[at[j, g]]) / den[at[j, g]])
    for j in range(qblocks):
        outs = [o[at[j, g]][hq * blk:(hq + 1) * blk] for g in range(N_KV_HEADS) for hq in range(Q_PER_KV)]
        blocks = [jnp.where(low_half, outs[2 * m], outs[2 * m + 1]) for m in range(N_Q_HEADS // 2)]
        o_ref[j * blk:(j + 1) * blk, :] = jnp.concatenate(blocks, axis=1).astype(BF16)


def _attention(q, kd, vd, kcd, vcd, sink, *, bsz, seq_len, n_ctx, qblocks):
    blk = WINDOW
    nblk = seq_len // blk
    nstep = nblk // qblocks
    aw = q.shape[1]
    kw = kd.shape[1]

    def qmap(b, n):
        return (b * nstep + n, 0)

    def pmap(b, n):
        return (b * nblk + jnp.maximum(n * qblocks - 1, 0), 0)

    def nmap(b, n):
        return (b * nblk + jnp.minimum((n + 1) * qblocks, nblk - 1), 0)

    edge = lambda f: pl.BlockSpec((blk, kw), f)
    own = pl.BlockSpec((qblocks * blk, kw), qmap)
    return pl.pallas_call(
        functools.partial(_attn_kernel, nblk=nblk, blk=blk, qblocks=qblocks),
        grid=(bsz, nstep),
        in_specs=[
            pl.BlockSpec(memory_space=pltpu.SMEM),
            pl.BlockSpec((qblocks * blk, aw), qmap),
            edge(pmap), own, edge(nmap),
            edge(pmap), own, edge(nmap),
            pl.BlockSpec((n_ctx, kw), lambda b, n: (b, 0)),
            pl.BlockSpec((n_ctx, kw), lambda b, n: (b, 0)),
        ],
        out_specs=pl.BlockSpec((qblocks * blk, aw), qmap),
        out_shape=jax.ShapeDtypeStruct(q.shape, BF16),
        compiler_params=_cparams(("arbitrary", "arbitrary")),
        name="attn",
    )(sink, q, kd, kd, kd, vd, vd, vd, kcd, vcd)


def _gelu_tanh(x):
    return 0.5 * x * (1.0 + jnp.tanh(math.sqrt(2.0 / math.pi) * (x + 0.044715 * (x * x * x))))


def _flatten_slots(v, max_value):
    tm, nk = v.shape
    per_row = LANES // nk
    log2 = lambda x: int(x).bit_length() - 1
    assert nk == 1 << log2(nk) and tm % per_row == 0
    spread = ((lax.broadcasted_iota(jnp.int32, (nk, LANES), 1) & (nk - 1))
              == lax.broadcasted_iota(jnp.int32, (nk, LANES), 0)).astype(BF16)
    t_id = lax.broadcasted_iota(jnp.int32, (tm, LANES), 0)
    c_id = lax.broadcasted_iota(jnp.int32, (tm, LANES), 1)
    own = (c_id >> log2(nk)) == (t_id & (per_row - 1))
    group = ((lax.broadcasted_iota(jnp.int32, (tm // per_row, tm), 1) >> log2(per_row))
             == lax.broadcasted_iota(jnp.int32, (tm // per_row, tm), 0)).astype(BF16)
    digit_bits = 7
    out = jnp.zeros((tm // per_row, LANES), jnp.int32)
    for shift in range(0, max(int(max_value).bit_length(), 1), digit_bits):
        digit = ((v >> shift) & ((1 << digit_bits) - 1)).astype(F32).astype(BF16)
        wide = jnp.dot(digit, spread, preferred_element_type=F32)
        wide = jnp.where(own, wide, 0.0).astype(BF16)
        out = out + (jnp.dot(group, wide, preferred_element_type=F32).astype(jnp.int32) << shift)
    return out


def _merge_kernel(yf_ref, yb_ref, o_ref, gs_ref, ga_ref, x_ref, lng_ref, lnb_ref, g1_ref, sh2_ref, sc2_ref,
                  wglu_ref, bglu_ref, wso_ref, wao_ref, wo_ref, l1g_ref, l1b_ref, wr_ref, br_ref,
                  h1_ref, idx_ref, gate_ref, rank_ref, cnt_ref, cnt_scr, *, alpha, n_exp, sub, n_tokens):
    i = pl.program_id(0)

    @pl.when(i == 0)
    def _():
        cnt_scr[...] = jnp.zeros_like(cnt_scr)

    nlb = yf_ref.shape[0]
    tm = x_ref.shape[0]
    lane = lax.broadcasted_iota(jnp.int32, (sub, n_exp), 1)
    lane_k = lax.broadcasted_iota(jnp.int32, (sub, TOP_K), 1)
    ri = lax.broadcasted_iota(jnp.int32, (sub, sub), 0)
    ci = lax.broadcasted_iota(jnp.int32, (sub, sub), 1)
    tri = jnp.where(ci < ri, 1.0, 0.0).astype(BF16)
    slots = sub * TOP_K // LANES
    cnt = cnt_scr[...]
    parts = range(tm // sub)
    rs = [slice(p * sub, (p + 1) * sub) for p in parts]
    each = lambda f: [f(p) for p in parts]
    y = each(lambda p: jnp.concatenate([yf_ref[lb, rs[p], :] + yb_ref[lb, rs[p], :] for lb in range(nlb)], axis=1))
    z = each(lambda p: _gelu_tanh(y[p]))
    zg = each(lambda p: _dot(z[p].astype(BF16), wglu_ref[...]) + bglu_ref[...])
    z = each(lambda p: z[p] * _sigmoid(zg[p]))
    ms = each(lambda p: _dot(z[p].astype(BF16), wso_ref[...]))
    ma = each(lambda p: _dot(o_ref[rs[p], :], wao_ref[...]))
    m = each(lambda p: gs_ref[rs[p], :].astype(F32) * ms[p] + ga_ref[rs[p], :].astype(F32) * ma[p])
    mix = each(lambda p: _dot(m[p].astype(BF16), wo_ref[...]))
    h = each(lambda p: _layer_norm(x_ref[rs[p], :], lng_ref[...], lnb_ref[...]))
    h1 = each(lambda p: _layer_norm(alpha * h[p] + g1_ref[...] * mix[p], l1g_ref[...], l1b_ref[...]))
    for p in parts:
        h1_ref[rs[p], :] = h1[p]
    xm = each(lambda p: h1[p] * (1.0 + sc2_ref[...]) + sh2_ref[...])
    work = each(lambda p: _dot_3pass(xm[p], wr_ref[...]) + br_ref[...])
    vals, sels = [], []
    for _ in range(TOP_K):
        mx = each(lambda p: jnp.max(work[p], axis=1, keepdims=True))
        sel = each(lambda p: jnp.min(jnp.where(work[p] == mx[p], lane, n_exp), axis=1, keepdims=True))
        work = each(lambda p: jnp.where(lane == sel[p], -jnp.inf, work[p]))
        vals.append(mx)
        sels.append(sel)
    exps = [each(lambda p: jnp.exp(vals[k][p] - vals[0][p])) for k in range(TOP_K)]
    den = each(lambda p: exps[0][p] + exps[1][p] + exps[2][p] + exps[3][p])
    onehot = each(lambda p: sum((lane == sels[k][p]).astype(F32) for k in range(TOP_K)))
    prefix = each(lambda p: _dot(tri, onehot[p].astype(BF16)))
    for p in parts:
        rank = prefix[p] + cnt
        idx_o = jnp.zeros((sub, TOP_K), jnp.int32)
        gate_o = jnp.zeros((sub, TOP_K), F32)
        rank_o = jnp.zeros((sub, TOP_K), jnp.int32)
        for k in range(TOP_K):
            rk = jnp.sum(jnp.where(lane == sels[k][p], rank, 0.0), axis=1, keepdims=True).astype(jnp.int32)
            idx_o = jnp.where(lane_k == k, sels[k][p], idx_o)
            gate_o = jnp.where(lane_k == k, exps[k][p] / den[p], gate_o)
            rank_o = jnp.where(lane_k == k, rk, rank_o)
        gate_ref[rs[p], :] = gate_o
        idx_ref[p * slots:(p + 1) * slots, :] = _flatten_slots(idx_o, n_exp - 1)
        rank_ref[p * slots:(p + 1) * slots, :] = _flatten_slots(rank_o, n_tokens - 1)
        cnt = cnt + jnp.sum(onehot[p], axis=0, keepdims=True)
    cnt_scr[...] = cnt
    cnt_ref[...] = cnt


def _merge(yf, yb, o_att, sgs, sga, x2, ln_g, ln_b, mod3, w_glu, b_glu, w_so, w_ao, w_o, l1g, l1b,
           w_r, b_r, *, rows_per_batch, tm, sub, alpha):
    t, d = x2.shape
    nlb = yf.shape[0]
    sw = nlb * LANES
    n_exp = w_r.shape[1]
    tpb = rows_per_batch // tm
    row = lambda i: (i, 0)
    const = lambda i: (0, 0)

    def modspec(chunk):
        return pl.BlockSpec((None, 1, d), lambda i: (i // tpb, 0, chunk))

    return pl.pallas_call(
        functools.partial(_merge_kernel, alpha=alpha, n_exp=n_exp, sub=sub, n_tokens=t),
        grid=(t // tm,),
        in_specs=[
            pl.BlockSpec((nlb, tm, LANES), lambda i: (0, i, 0)),
            pl.BlockSpec((nlb, tm, LANES), lambda i: (0, i, 0)),
            pl.BlockSpec((tm, o_att.shape[1]), row),
            pl.BlockSpec((tm, d), row),
            pl.BlockSpec((tm, d), row),
            pl.BlockSpec((tm, d), row),
            pl.BlockSpec((1, d), const),
            pl.BlockSpec((1, d), const),
            modspec(2), modspec(3), modspec(4),
            pl.BlockSpec(w_glu.shape, const),
            pl.BlockSpec((1, sw), const),
            pl.BlockSpec(w_so.shape, const),
            pl.BlockSpec(w_ao.shape, const),
            pl.BlockSpec(w_o.shape, const),
            pl.BlockSpec((1, d), const),
            pl.BlockSpec((1, d), const),
            pl.BlockSpec(w_r.shape, const),
            pl.BlockSpec((1, n_exp), const),
        ],
        out_specs=[
            pl.BlockSpec((tm, d), row),
            pl.BlockSpec((tm * TOP_K // LANES, LANES), row),
            pl.BlockSpec((tm, TOP_K), row),
            pl.BlockSpec((tm * TOP_K // LANES, LANES), row),
            pl.BlockSpec((1, n_exp), const),
        ],
        out_shape=[
            jax.ShapeDtypeStruct((t, d), F32),
            jax.ShapeDtypeStruct((t * TOP_K // LANES, LANES), jnp.int32),
            jax.ShapeDtypeStruct((t, TOP_K), F32),
            jax.ShapeDtypeStruct((t * TOP_K // LANES, LANES), jnp.int32),
            jax.ShapeDtypeStruct((1, n_exp), F32),
        ],
        scratch_shapes=[pltpu.VMEM((1, n_exp), F32)],
        compiler_params=_cparams(("arbitrary",)),
        name="merge",
    )(yf, yb, o_att, sgs, sga, x2, ln_g, ln_b, mod3, mod3, mod3, w_glu, b_glu, w_so, w_ao, w_o,
      l1g, l1b, w_r, b_r)


def _row_tile(ref, r):
    return ref.at[pl.ds(pl.multiple_of(r * ROW_TILES, ROW_TILES), ROW_TILES)]


def _store_row_tiles(ref, val):
    rows = val.shape[0]
    for l in range(ROW_TILES):
        ref[pl.ds(l, rows, stride=ROW_TILES), :] = val[:, l * LANES:(l + 1) * LANES]


def _load_row_tiles(ref, row0, rows):
    return jnp.concatenate(
        [ref[pl.ds(row0 * ROW_TILES + l, rows, stride=ROW_TILES), :] for l in range(ROW_TILES)], axis=1)


def _row_copy_out(src, dst, pos_ref, sem, t, k):
    return pltpu.make_async_copy(_row_tile(src, t), _row_tile(dst, pos_ref[t * TOP_K + k]), sem)


def _dispatch_kernel(pos_ref, prev_pos_ref, h1_ref, sh2_ref, sc2_ref, xrows_ref, xm_scr, sems):
    i = pl.program_id(0)
    n = pl.num_programs(0)
    tm = h1_ref.shape[0]
    slot = i % 2
    xm = h1_ref[...] * (1.0 + sc2_ref[...]) + sh2_ref[...]
    _store_row_tiles(xm_scr.at[slot], xm)

    def copies(slot_, idx_ref, op):
        def body(t, carry):
            for k in range(TOP_K):
                op(_row_copy_out(xm_scr.at[slot_], xrows_ref, idx_ref, sems.at[slot_], t, k), k)
            return carry
        lax.fori_loop(0, tm, body, 0, unroll=DMA_LOOP_UNROLL)

    start = lambda cp, k: cp.start(priority=k % N_DMA_PRIORITIES)
    wait = lambda cp, k: cp.wait()
    for s in range(2):
        @pl.when(slot == s)
        def _():
            copies(s, pos_ref, start)

    for s in range(2):
        @pl.when((slot == 1 - s) & (i > 0))
        def _():
            copies(s, prev_pos_ref, wait)

        @pl.when((slot == s) & (i == n - 1))
        def _():
            copies(s, pos_ref, wait)


def _dispatch(pos_flat, h1, mod3, *, rows_per_batch, tm):
    t, d = h1.shape
    tpb = rows_per_batch // tm
    return pl.pallas_call(
        _dispatch_kernel,
        grid=(t // tm,),
        in_specs=[
            pl.BlockSpec((tm * TOP_K,), lambda i: (i,), memory_space=pltpu.SMEM),
            pl.BlockSpec((tm * TOP_K,), lambda i: (jnp.maximum(i - 1, 0),), memory_space=pltpu.SMEM),
            pl.BlockSpec((tm, d), lambda i: (i, 0)),
            pl.BlockSpec((None, 1, d), lambda i: (i // tpb, 0, 3)),
            pl.BlockSpec((None, 1, d), lambda i: (i // tpb, 0, 4)),
        ],
        out_specs=pl.BlockSpec(memory_space=pl.ANY),
        out_shape=jax.ShapeDtypeStruct((t * TOP_K * ROW_TILES, LANES), F32),
        scratch_shapes=[pltpu.VMEM((2, tm * ROW_TILES, LANES), F32), pltpu.SemaphoreType.DMA((2,))],
        compiler_params=_cparams(("arbitrary",)),
        name="dispatch",
    )(pos_flat, pos_flat, h1, mod3, mod3)


def _experts_kernel(tile_ref, exp_ref, lo_ref, hi_ref, nxt_ref, x_ref, wgu_hbm, bgu_ref, wd_hbm, bd_ref, y_ref,
                    wgu_f32, wd_f32, wgu_scr, wd_scr, sems, *, rows, sub_rows):
    w = pl.program_id(0)
    prev = jnp.maximum(w - 1, 0)
    e_new = (w == 0) | (exp_ref[w] != exp_ref[prev])
    t_new = (w == 0) | (tile_ref[w] != tile_ref[prev])
    lo = lo_ref[w]
    hi = hi_ref[w]
    f = wd_scr.shape[0]

    def weight_copies(e):
        return (pltpu.make_async_copy(wgu_hbm.at[e], wgu_f32, sems.at[0]),
                pltpu.make_async_copy(wd_hbm.at[e], wd_f32, sems.at[1]))

    @pl.when(w == 0)
    def _():
        for cp in weight_copies(exp_ref[0]):
            cp.start()

    @pl.when(e_new)
    def _():
        for cp in weight_copies(exp_ref[w]):
            cp.wait()
        wgu_scr[...] = wgu_f32[...].astype(BF16)
        wd_scr[...] = wd_f32[...].astype(BF16)

        @pl.when(nxt_ref[w] >= 0)
        def _():
            for cp in weight_copies(nxt_ref[w]):
                cp.start()

    row0 = tile_ref[w] * rows
    whole = (lo <= row0) & (hi >= row0 + rows)

    @pl.when(t_new & jnp.logical_not(whole))
    def _():
        y_ref[...] = jnp.zeros_like(y_ref)

    def expert_mlp(x):
        gu = _dot(x.astype(BF16), wgu_scr[...]) + bgu_ref[...]
        glu = jnp.minimum(gu[:, :f], SWIGLU_LIMIT)
        lin = jnp.clip(gu[:, f:], -SWIGLU_LIMIT, SWIGLU_LIMIT)
        act = glu * _sigmoid(SWIGLU_ALPHA * glu) * (lin + 1.0)
        return _dot(act.astype(BF16), wd_scr[...]) + bd_ref[...]

    def sub_block(sb):
        s_lo = row0 + sb * sub_rows
        s_hi = s_lo + sub_rows
        y_sub = y_ref.at[pl.ds(sb * sub_rows * ROW_TILES, sub_rows * ROW_TILES)]

        @pl.when((hi > s_lo) & (lo < s_hi))
        def _():
            y = expert_mlp(_load_row_tiles(x_ref, sb * sub_rows, sub_rows))
            sub_whole = (lo <= s_lo) & (hi >= s_hi)

            @pl.when(sub_whole)
            def _():
                _store_row_tiles(y_sub, y)

            @pl.when(jnp.logical_not(sub_whole))
            def _():
                r = s_lo + lax.broadcasted_iota(jnp.int32, (sub_rows, 1), 0)
                _store_row_tiles(y_sub, jnp.where((r >= lo) & (r < hi), y, _load_row_tiles(y_sub, 0, sub_rows)))

    @pl.when(whole)
    def _():
        _store_row_tiles(y_ref, expert_mlp(_load_row_tiles(x_ref, 0, rows)))

    @pl.when(jnp.logical_not(whole))
    def _():
        for sb in range(rows // sub_rows):
            sub_block(sb)


def _experts(work, xrows, w_gu, b_gu, w_d, b_d, *, rows, sub_rows):
    tile_id, exp_id, lo, hi, nxt = work
    n_exp, d, f2 = w_gu.shape
    f = w_d.shape[1]
    grid_spec = pltpu.PrefetchScalarGridSpec(
        num_scalar_prefetch=5,
        grid=(tile_id.shape[0],),
        in_specs=[
            pl.BlockSpec((rows * ROW_TILES, LANES), lambda w, ti, ex, lo, hi, nx: (ti[w], 0)),
            pl.BlockSpec(memory_space=pl.ANY),
            pl.BlockSpec((None, 1, f2), lambda w, ti, ex, lo, hi, nx: (ex[w], 0, 0)),
            pl.BlockSpec(memory_space=pl.ANY),
            pl.BlockSpec((None, 1, d), lambda w, ti, ex, lo, hi, nx: (ex[w], 0, 0)),
        ],
        out_specs=pl.BlockSpec((rows * ROW_TILES, LANES), lambda w, ti, ex, lo, hi, nx: (ti[w], 0)),
        scratch_shapes=[
            pltpu.VMEM((d, f2), F32), pltpu.VMEM((f, d), F32),
            pltpu.VMEM((d, f2), BF16), pltpu.VMEM((f, d), BF16),
            pltpu.SemaphoreType.DMA((2,)),
        ],
    )
    return pl.pallas_call(
        functools.partial(_experts_kernel, rows=rows, sub_rows=sub_rows),
        grid_spec=grid_spec,
        out_shape=jax.ShapeDtypeStruct(xrows.shape, F32),
        compiler_params=_cparams(("arbitrary",)),
        name="experts",
    )(tile_id, exp_id, lo, hi, nxt, xrows, w_gu, b_gu.reshape(n_exp, 1, f2), w_d, b_d.reshape(n_exp, 1, d))


def _row_copy_in(src, dst, pos_ref, sem, t, k, tm):
    return pltpu.make_async_copy(_row_tile(src, pos_ref[t * TOP_K + k]), _row_tile(dst, k * tm + t), sem)


def _combine_kernel(pos_ref, next_pos_ref, h1_ref, gate_ref, g2_ref, lg_ref, lb_ref, yrows_ref, o_ref,
                    buf, sems, *, alpha):
    i = pl.program_id(0)
    n = pl.num_programs(0)
    tm = h1_ref.shape[0]
    slot = i % 2

    def copies(slot_, idx_ref, op):
        def body(t, carry):
            for k in range(TOP_K):
                op(_row_copy_in(yrows_ref, buf.at[slot_], idx_ref, sems.at[slot_], t, k, tm), k)
            return carry
        lax.fori_loop(0, tm, body, 0, unroll=DMA_LOOP_UNROLL)

    start = lambda cp, k: cp.start(priority=k % N_DMA_PRIORITIES)
    wait = lambda cp, k: cp.wait()

    @pl.when(i == 0)
    def _():
        copies(0, pos_ref, start)

    for s in range(2):
        @pl.when((slot == 1 - s) & (i + 1 < n))
        def _():
            copies(s, next_pos_ref, start)

    def reduce(s):
        copies(s, pos_ref, wait)
        gates = gate_ref[...]
        ffn = gates[:, 0:1] * _load_row_tiles(buf.at[s], 0, tm)
        for k in range(1, TOP_K):
            ffn = ffn + gates[:, k:k + 1] * _load_row_tiles(buf.at[s], k * tm, tm)
        o_ref[...] = _layer_norm(alpha * h1_ref[...] + g2_ref[...] * ffn, lg_ref[...], lb_ref[...])

    for s in range(2):
        @pl.when(slot == s)
        def _():
            reduce(s)


def _combine(pos_flat, h1, gates, mod3, l2g, l2b, yrows, *, rows_per_batch, tm, alpha):
    t, d = h1.shape
    tpb = rows_per_batch // tm
    n_tiles = t // tm
    return pl.pallas_call(
        functools.partial(_combine_kernel, alpha=alpha),
        grid=(n_tiles,),
        in_specs=[
            pl.BlockSpec((tm * TOP_K,), lambda i: (i,), memory_space=pltpu.SMEM),
            pl.BlockSpec((tm * TOP_K,), lambda i: (jnp.minimum(i + 1, n_tiles - 1),), memory_space=pltpu.SMEM),
            pl.BlockSpec((tm, d), lambda i: (i, 0)),
            pl.BlockSpec((tm, TOP_K), lambda i: (i, 0)),
            pl.BlockSpec((None, 1, d), lambda i: (i // tpb, 0, 5)),
            pl.BlockSpec((1, d), lambda i: (0, 0)),
            pl.BlockSpec((1, d), lambda i: (0, 0)),
            pl.BlockSpec(memory_space=pl.ANY),
        ],
        out_specs=pl.BlockSpec((tm, d), lambda i: (i, 0)),
        out_shape=jax.ShapeDtypeStruct((t, d), F32),
        scratch_shapes=[pltpu.VMEM((2, TOP_K * tm * ROW_TILES, LANES), F32), pltpu.SemaphoreType.DMA((2,))],
        compiler_params=_cparams(("arbitrary",)),
        name="combine",
    )(pos_flat, pos_flat, h1, gates, mod3, l2g, l2b, yrows)


def _plan_kernel(cnt_ref, idx_ref, rank_ref, pos_ref, tile_out, exp_out, lo_out, hi_out, nxt_out,
                 start_s, end_s, next_s, *, rows, n_tiles):
    n_exp = cnt_ref.shape[0]
    n_work = tile_out.shape[0]

    def cumulate(e, acc):
        start_s[e] = acc
        end_s[e] = acc + cnt_ref[e]
        return acc + cnt_ref[e]

    lax.fori_loop(0, n_exp, cumulate, 0)

    def next_nonempty(i, cur):
        e = n_exp - 1 - i
        next_s[e] = cur
        return jnp.where(cnt_ref[e] > 0, e, cur)

    first = lax.fori_loop(0, n_exp, next_nonempty, -1)

    def item(w, carry):
        tile, e = carry
        live = tile < n_tiles
        tl = jnp.minimum(tile, n_tiles - 1)
        tile_lo = tl * rows
        tile_hi = tile_lo + rows
        lo = jnp.maximum(start_s[e], tile_lo)
        hi = jnp.where(live, jnp.minimum(end_s[e], tile_hi), lo)
        tile_out[w] = tl
        exp_out[w] = e
        lo_out[w] = lo
        hi_out[w] = hi
        nxt_out[w] = next_s[e]
        tile_done = live & (end_s[e] >= tile_hi)
        expert_done = live & (end_s[e] <= tile_hi) & (next_s[e] >= 0)
        return jnp.where(tile_done, tile + 1, tile), jnp.where(expert_done, next_s[e], e)

    lax.fori_loop(0, n_work, item, (0, first))

    idx = idx_ref[...]
    base = jnp.zeros(idx.shape, jnp.int32)
    for e in range(n_exp):
        base = jnp.where(idx == e, start_s[e], base)
    pos_ref[...] = base + rank_ref[...]


def _plan(counts, idx_flat, rank_flat, *, rows):
    n_exp = counts.shape[0]
    n_rows = idx_flat.shape[0] * idx_flat.shape[1]
    n_tiles = n_rows // rows
    n_work = n_tiles + n_exp - 1
    smem = pl.BlockSpec(memory_space=pltpu.SMEM)
    vmem = pl.BlockSpec(memory_space=pltpu.VMEM)
    return pl.pallas_call(
        functools.partial(_plan_kernel, rows=rows, n_tiles=n_tiles),
        in_specs=[smem, vmem, vmem],
        out_specs=[vmem] + [smem] * 5,
        out_shape=[jax.ShapeDtypeStruct(idx_flat.shape, jnp.int32)]
        + [jax.ShapeDtypeStruct((n_work,), jnp.int32)] * 5,
        scratch_shapes=[pltpu.SMEM((n_exp,), jnp.int32)] * 3,
        name="plan",
    )(counts, idx_flat, rank_flat)


def kernel(x, c, ctx, c_ctx, ln_in_g, ln_in_b, w_mod, b_mod, w_in, ssm_lam_re, ssm_lam_im, ssm_log_step, ssm_b_re, ssm_b_im, ssm_c_re, ssm_c_im, ssm_d, w_glu, b_glu, attn_sink, w_ssm_out, w_att_out, w_o, ln1_g, ln1_b, w_router, b_router, w_gate_up, b_gate_up, w_down, b_down, ln2_g, ln2_b):
    bsz, seq_len, d = x.shape
    n_ctx = ctx.shape[1]
    depth = w_mod.shape[0]
    assert depth == 1, "single-layer kernel"
    g_ssm, h_ssm = ssm_d.shape[1:]
    ssm_w = g_ssm * h_ssm
    attn_w = N_Q_HEADS * HEAD_DIM
    kv_w = N_KV_HEADS * HEAD_DIM
    kv2_w = 2 * kv_w
    nlb = ssm_w // LANES
    alpha = (2.0 * depth) ** 0.25
    t = bsz * seq_len
    assert bsz + 1 <= SUBLANES
    assert d == ROW_TILES * LANES and ROW_TILES == SUBLANES, "a token row must be exactly one (8, 128) tile"

    row2 = lambda a: a.reshape(1, -1)

    c_rows = jnp.concatenate([c, c_ctx[None], jnp.zeros((SUBLANES - bsz - 1, d), F32)], axis=0)
    mod = _mod_vectors(c_rows, w_mod[0], row2(b_mod[0]))
    mod3 = mod.reshape(SUBLANES, 1, 6 * d)

    wi = w_in[0]
    i1 = ssm_w + attn_w
    i2, i3 = i1 + kv_w, i1 + 2 * kv_w
    assert wi.shape[1] == i3 + 2 * d

    def dup_heads(wkv):
        parts = []
        for hk in range(N_KV_HEADS):
            blk = wkv[:, hk * HEAD_DIM:(hk + 1) * HEAD_DIM]
            parts += [blk, blk]
        return jnp.concatenate(parts, axis=1)

    w_all = wi.astype(BF16)
    w_kv2 = jnp.concatenate([dup_heads(wi[:, i1:i2]), dup_heads(wi[:, i2:i3])], axis=1).astype(BF16)

    pos = jnp.arange(seq_len)
    inv = ROPE_BASE ** (-jnp.arange(ROPE_PAIRS, dtype=F32) / ROPE_PAIRS)
    ang = jnp.concatenate([(pos // GRID_W).astype(F32)[:, None] * inv,
                           (pos % GRID_W).astype(F32)[:, None] * inv], axis=-1)
    cos_t = jnp.tile(jnp.cos(ang), (1, 2 * LANES // HEAD_DIM))
    sin_h = jnp.sin(ang)
    sin_t = jnp.tile(jnp.concatenate([-sin_h, sin_h], axis=-1), (1, LANES // HEAD_DIM))

    g_in, b_in = row2(ln_in_g), row2(ln_in_b)
    x2 = x.reshape(t, d)
    ctx2 = ctx.reshape(bsz * n_ctx, d)
    s_c, k_c, v_c = _inproj(ctx2, g_in, b_in, mod3, w_all, w_kv2, None, None, latent=False,
                            rows_per_batch=n_ctx, tm=n_ctx, sub=n_ctx, ctx_mod_row=bsz,
                            ssm_w=ssm_w, attn_w=attn_w, kv2_w=kv2_w)
    s_l, q_l, k_l, v_l, sgs, sga = _inproj(x2, g_in, b_in, mod3, w_all, w_kv2, cos_t, sin_t, latent=True,
                                           rows_per_batch=seq_len, tm=1024, sub=512, ctx_mod_row=None,
                                           ssm_w=ssm_w, attn_w=attn_w, kv2_w=kv2_w)

    wf = _s5_weights(ssm_lam_re[0, 0], ssm_lam_im[0, 0], ssm_log_step[0, 0], ssm_b_re[0, 0], ssm_b_im[0, 0],
                     ssm_c_re[0, 0], ssm_c_im[0, 0], reverse=False)
    wb = _s5_weights(ssm_lam_re[0, 1], ssm_lam_im[0, 1], ssm_log_step[0, 1], ssm_b_re[0, 1], ssm_b_im[0, 1],
                     ssm_c_re[0, 1], ssm_c_im[0, 1], reverse=True)
    d_tile = jnp.tile(ssm_d[0].astype(F32).reshape(nlb, 1, LANES), (1, 1, S5_CHUNK))
    sw2 = wf[1].shape[-1]
    zero_state = jnp.zeros((nlb, bsz, sw2), F32)
    uc4 = s_c.reshape(nlb, bsz, n_ctx, LANES)
    ul4 = s_l.reshape(nlb, bsz, seq_len, LANES)
    _, sf0 = _s5_scan(uc4, zero_state, wf, d_tile, reverse=False, cc=n_ctx // S5_CHUNK, add_skip=False)
    _, sb0 = _s5_scan(uc4, zero_state, wb, d_tile, reverse=True, cc=n_ctx // S5_CHUNK, add_skip=False)
    yf4, _ = _s5_scan(ul4, sf0, wf, d_tile, reverse=False, cc=128, add_skip=True)
    yb4, _ = _s5_scan(ul4, sb0, wb, d_tile, reverse=True, cc=128, add_skip=False)
    yf = yf4.reshape(nlb, t, LANES)
    yb = yb4.reshape(nlb, t, LANES)

    o_att = _attention(q_l, k_l, v_l, k_c, v_c, attn_sink[0].astype(F32), bsz=bsz, seq_len=seq_len, n_ctx=n_ctx,
                       qblocks=2)

    h1, top_i, gates, rank, counts = _merge(
        yf, yb, o_att, sgs, sga, x2, g_in, b_in, mod3, w_glu[0].astype(BF16), row2(b_glu[0]),
        w_ssm_out[0].astype(BF16), w_att_out[0].astype(BF16), w_o[0].astype(BF16), row2(ln1_g[0]), row2(ln1_b[0]),
        w_router[0], row2(b_router[0]), rows_per_batch=seq_len, tm=1024, sub=256, alpha=alpha)

    rows, sub_rows = 512, 256
    n_rows = t * TOP_K
    pos, tile_id, exp_id, lo, hi, nxt = _plan(counts[0].astype(jnp.int32), top_i, rank, rows=rows)
    pos_flat = pos.reshape(n_rows)
    xrows = _dispatch(pos_flat, h1, mod3, rows_per_batch=seq_len, tm=256)
    yrows = _experts((tile_id, exp_id, lo, hi, nxt), xrows, w_gate_up[0], b_gate_up[0], w_down[0], b_down[0],
                     rows=rows, sub_rows=sub_rows)
    out = _combine(pos_flat, h1, gates, mod3, row2(ln2_g[0]), row2(ln2_b[0]), yrows,
                   rows_per_batch=seq_len, tm=256, alpha=alpha)
    return out.reshape(bsz, seq_len, d)
```

```python
import functools
import math
from typing import NamedTuple

import jax
import jax.numpy as jnp
import numpy as np
from jax import lax
from jax.experimental import pallas as pl
from jax.experimental.pallas import tpu as pltpu

F32 = jnp.float32
BF16 = jnp.bfloat16
HIGHEST = lax.Precision.HIGHEST

HEAD_DIM = 64
N_Q_HEADS = 8
N_KV_HEADS = 2
Q_PER_KV = N_Q_HEADS // N_KV_HEADS
WINDOW = 128
GRID_W = 64
ROPE_BASE = 10000.0
ROPE_PAIRS = HEAD_DIM // 4
TOP_K = 4
SWIGLU_LIMIT = 7.0
SWIGLU_ALPHA = 1.702
LN_EPS = 1e-5
NEG_INF = -1e30

LANES = 128
SUBLANES = 8
MXU_DIM = 256
ROW_TILES = 8
VMEM_LIMIT = 56 * 1024 * 1024
N_DMA_PRIORITIES = 2
DMA_LOOP_UNROLL = 8

class _Tiles(NamedTuple):
    inproj_rows: int = 1024
    inproj_sub: int = 512
    s5_chunks: int = 128
    attn_qblocks: int = 4
    merge_rows: int = 1024
    merge_sub: int = 256
    moe_token_rows: int = 512
    expert_rows: int = 512
    expert_sub: int = 256


TILES = _Tiles()

S5_CHUNK = 8
S5_ROW_PAD = SUBLANES


def _cparams(sem):
    return pltpu.CompilerParams(dimension_semantics=sem, vmem_limit_bytes=VMEM_LIMIT)


def _sigmoid(x):
    return 1.0 / (1.0 + jnp.exp(-x))


def _layer_norm(x, g, b):
    mu = jnp.mean(x, axis=-1, keepdims=True)
    xc = x - mu
    var = jnp.mean(xc * xc, axis=-1, keepdims=True)
    return xc * lax.rsqrt(var + LN_EPS) * g + b


def _dot(a, b):
    return jnp.dot(a, b, preferred_element_type=F32)


def _dot_3pass(a, b):
    a_hi = a.astype(BF16)
    b_hi = b.astype(BF16)
    a_lo = (a - a_hi.astype(F32)).astype(BF16)
    b_lo = (b - b_hi.astype(F32)).astype(BF16)
    return _dot(a_hi, b_hi) + (_dot(a_hi, b_lo) + _dot(a_lo, b_hi))


def _mod_kernel(c_ref, w_ref, b_ref, o_ref):
    c = c_ref[...]
    a = c * _sigmoid(c)
    o_ref[...] = jnp.dot(a, w_ref[...], preferred_element_type=F32, precision=HIGHEST) + b_ref[...]


def _mod_vectors(c_rows, w_mod, b_mod):
    d = c_rows.shape[1]
    n = w_mod.shape[1]
    return pl.pallas_call(
        _mod_kernel,
        grid=(n // d,),
        in_specs=[
            pl.BlockSpec((SUBLANES, d), lambda i: (0, 0)),
            pl.BlockSpec((d, d), lambda i: (0, i)),
            pl.BlockSpec((1, d), lambda i: (0, i)),
        ],
        out_specs=pl.BlockSpec((SUBLANES, d), lambda i: (0, i)),
        out_shape=jax.ShapeDtypeStruct((SUBLANES, n), F32),
        compiler_params=_cparams(("arbitrary",)),
        name="mod",
    )(c_rows, w_mod, b_mod)


def _rope(t, cos, sin):
    n = t.shape[1]
    reps = n // LANES
    c = jnp.concatenate([cos] * reps, axis=1) if reps > 1 else cos
    s = jnp.concatenate([sin] * reps, axis=1) if reps > 1 else sin
    half = HEAD_DIM // 2
    upper = pltpu.roll(t, n - half, axis=1)
    lower = pltpu.roll(t, half, axis=1)
    lane = lax.broadcasted_iota(jnp.int32, t.shape, 1)
    partner = jnp.where((lane & half) == 0, upper, lower)
    return t * c + partner * s


def _dup_heads(t):
    assert t.shape[1] == LANES == N_KV_HEADS * HEAD_DIM
    swapped = pltpu.roll(t, HEAD_DIM, axis=1)
    low = lax.broadcasted_iota(jnp.int32, t.shape, 1) < HEAD_DIM
    return jnp.concatenate([jnp.where(low, t, swapped), jnp.where(low, swapped, t)], axis=1)


def _inproj_kernel(*refs, latent, ssm_w, attn_w, kv2_w, d_model, sub):
    if latent:
        (x_ref, g_ref, b_ref, sh_ref, sc_ref, w_ref, cos_ref, sin_ref,
         s_ref, q_ref, k_ref, v_ref, gs_ref, ga_ref) = refs
    else:
        x_ref, g_ref, b_ref, sh_ref, sc_ref, w_ref, s_ref, k_ref, v_ref = refs
    tm = x_ref.shape[0]
    parts = range(tm // sub)
    rs = [slice(p * sub, (p + 1) * sub) for p in parts]
    each = lambda f: [f(p) for p in parts]
    h = each(lambda p: _layer_norm(x_ref[rs[p], :], g_ref[...], b_ref[...]))
    u = each(lambda p: (h[p] * (1.0 + sc_ref[...]) + sh_ref[...]).astype(BF16))
    s = each(lambda p: _dot(u[p], w_ref[:, 0:ssm_w]))
    for p in parts:
        for lb in range(ssm_w // LANES):
            s_ref[lb, rs[p], :] = s[p][:, lb * LANES:(lb + 1) * LANES]
    if latent:
        q = each(lambda p: _dot(u[p], w_ref[:, ssm_w:ssm_w + attn_w]))
        q = each(lambda p: _rope(q[p], cos_ref[rs[p], :], sin_ref[rs[p], :]) * (HEAD_DIM ** -0.5))
        for p in parts:
            q_ref[rs[p], :] = q[p].astype(BF16)
    kv_w = kv2_w // 2
    k = each(lambda p: _dot(u[p], w_ref[:, ssm_w + attn_w:ssm_w + attn_w + kv_w]))
    if latent:
        k = each(lambda p: _rope(k[p], cos_ref[rs[p], :], sin_ref[rs[p], :]))
    v = each(lambda p: _dot(u[p], w_ref[:, ssm_w + attn_w + kv_w:ssm_w + attn_w + 2 * kv_w]))
    for p in parts:
        k_ref[rs[p], :] = _dup_heads(k[p]).astype(BF16)
        v_ref[rs[p], :] = _dup_heads(v[p]).astype(BF16)
    if latent:
        col = ssm_w + attn_w + kv2_w
        gs = each(lambda p: _dot(u[p], w_ref[:, col:col + d_model]))
        ga = each(lambda p: _dot(u[p], w_ref[:, col + d_model:col + 2 * d_model]))
        for p in parts:
            gs_ref[rs[p], :] = _sigmoid(gs[p]).astype(BF16)
            ga_ref[rs[p], :] = _sigmoid(ga[p]).astype(BF16)


def _inproj(x2, ln_g, ln_b, mod3, w_all, cos_t, sin_t, *, latent, rows_per_batch, tm, sub, ctx_mod_row,
            ssm_w, attn_w, kv2_w):
    t, d = x2.shape
    tiles_per_batch = rows_per_batch // tm
    nlb = ssm_w // LANES

    def brow(i):
        return i // tiles_per_batch if ctx_mod_row is None else ctx_mod_row

    in_specs = [
        pl.BlockSpec((tm, d), lambda i: (i, 0)),
        pl.BlockSpec((1, d), lambda i: (0, 0)),
        pl.BlockSpec((1, d), lambda i: (0, 0)),
        pl.BlockSpec((None, 1, d), lambda i: (brow(i), 0, 0)),
        pl.BlockSpec((None, 1, d), lambda i: (brow(i), 0, 1)),
        pl.BlockSpec(w_all.shape, lambda i: (0, 0)),
    ]
    args = [x2, ln_g, ln_b, mod3, mod3, w_all]
    out_specs = [pl.BlockSpec((nlb, tm, LANES), lambda i: (0, i, 0))]
    out_shape = [jax.ShapeDtypeStruct((nlb, t, LANES), F32)]
    if latent:
        in_specs += [
            pl.BlockSpec((tm, LANES), lambda i: (i % tiles_per_batch, 0)),
            pl.BlockSpec((tm, LANES), lambda i: (i % tiles_per_batch, 0)),
        ]
        args += [cos_t, sin_t]
        out_specs.append(pl.BlockSpec((tm, attn_w), lambda i: (i, 0)))
        out_shape.append(jax.ShapeDtypeStruct((t, attn_w), BF16))
    out_specs += [pl.BlockSpec((tm, kv2_w), lambda i: (i, 0))] * 2
    out_shape += [jax.ShapeDtypeStruct((t, kv2_w), BF16)] * 2
    if latent:
        out_specs += [pl.BlockSpec((tm, d), lambda i: (i, 0))] * 2
        out_shape += [jax.ShapeDtypeStruct((t, d), BF16)] * 2
    return pl.pallas_call(
        functools.partial(_inproj_kernel, latent=latent, ssm_w=ssm_w, attn_w=attn_w, kv2_w=kv2_w,
                          d_model=d, sub=sub),
        grid=(t // tm,),
        in_specs=in_specs,
        out_specs=out_specs,
        out_shape=out_shape,
        compiler_params=_cparams(("arbitrary",)),
        name="inproj_latent" if latent else "inproj_ctx",
    )(*args)


def _lane_expand(tbl, n_copies, rows_per_group, cols_per_group):
    r, w = tbl.shape
    wide_w = n_copies * w
    log2 = lambda v: int(v).bit_length() - 1
    assert all(v == 1 << log2(v) for v in (w, rows_per_group, cols_per_group))
    sel = ((lax.broadcasted_iota(jnp.int32, (w, wide_w), 1) & (w - 1))
           == lax.broadcasted_iota(jnp.int32, (w, wide_w), 0)).astype(BF16)
    wide = jnp.dot(tbl.astype(BF16), sel, preferred_element_type=F32)
    keep = ((lax.broadcasted_iota(jnp.int32, (r, wide_w), 0) >> log2(rows_per_group))
            == (lax.broadcasted_iota(jnp.int32, (r, wide_w), 1) >> log2(cols_per_group)))
    return jnp.where(keep, wide, 0.0)


def _s5_assemble_kernel(m_ref, ws_ref, wo_ref, kin_out, ws_out, wo_out, *, reverse):
    ch, _, hh = m_ref.shape
    p = ws_ref.shape[-1]
    gpb = LANES // hh
    sw = gpb * p
    bd = [_lane_expand(m_ref[j], gpb, hh, hh).astype(BF16) for j in range(ch)]
    zero = jnp.zeros((LANES, LANES), BF16)
    for k in range(ch):
        for i in range(ch):
            lag = (k - i) if reverse else (i - k)
            kin_out[k * LANES:(k + 1) * LANES, i * LANES:(i + 1) * LANES] = bd[lag] if lag >= 0 else zero
    for ri in range(2):
        for k in range(ch):
            ws_out[k * LANES:(k + 1) * LANES, ri * sw:(ri + 1) * sw] = (
                _lane_expand(ws_ref[ri, k], gpb, hh, p).astype(BF16))
        for i in range(ch):
            wo_out[ri * sw:(ri + 1) * sw, i * LANES:(i + 1) * LANES] = (
                _lane_expand(wo_ref[ri, i], gpb, p, hh).astype(BF16))


def _s5_assemble(m_tbl, ws_tbl, wo_tbl, *, reverse):
    ch, nlb, _, hh = m_tbl.shape
    p = ws_tbl.shape[-1]
    sw = (LANES // hh) * p
    width = ch * LANES
    return pl.pallas_call(
        functools.partial(_s5_assemble_kernel, reverse=reverse),
        grid=(nlb,),
        in_specs=[
            pl.BlockSpec((ch, None, LANES, hh), lambda l: (0, l, 0, 0)),
            pl.BlockSpec((2, ch, None, LANES, p), lambda l: (0, 0, l, 0, 0)),
            pl.BlockSpec((2, ch, None, sw, hh), lambda l: (0, 0, l, 0, 0)),
        ],
        out_specs=[
            pl.BlockSpec((None, width, width), lambda l: (l, 0, 0)),
            pl.BlockSpec((None, width, 2 * sw), lambda l: (l, 0, 0)),
            pl.BlockSpec((None, 2 * sw, width), lambda l: (l, 0, 0)),
        ],
        out_shape=[
            jax.ShapeDtypeStruct((nlb, width, width), BF16),
            jax.ShapeDtypeStruct((nlb, width, 2 * sw), BF16),
            jax.ShapeDtypeStruct((nlb, 2 * sw, width), BF16),
        ],
        compiler_params=_cparams(("arbitrary",)),
        name="s5_assemble_bwd" if reverse else "s5_assemble_fwd",
    )(m_tbl, ws_tbl, wo_tbl)


def _s5_weights(lam_re, lam_im, log_step, b_re, b_im, c_re, c_im, reverse):
    ch = S5_CHUNK
    g, p = lam_re.shape
    hh = b_re.shape[-1]
    gpb = LANES // hh
    nlb = g // gpb
    lr, li = lam_re.astype(F32), lam_im.astype(F32)
    dt = jnp.exp(log_step.astype(F32))[:, None]
    jj = jnp.arange(ch + 1, dtype=F32)[:, None, None]
    mag = jnp.exp(jj * lr * dt)
    pr = mag * jnp.cos(jj * li * dt)
    pi = mag * jnp.sin(jj * li * dt)
    ar, ai = pr[1], pi[1]
    den = lr * lr + li * li
    cr = ((ar - 1) * lr + ai * li) / den
    ci = (ai * lr - (ar - 1) * li) / den
    br, bi = b_re.astype(F32), b_im.astype(F32)
    bbr = cr[..., None] * br - ci[..., None] * bi
    bbi = cr[..., None] * bi + ci[..., None] * br
    ccr, cci = c_re.astype(F32), c_im.astype(F32)
    bbr_t = jnp.swapaxes(bbr, 1, 2)
    bbi_t = jnp.swapaxes(bbi, 1, 2)
    ccr_t = jnp.swapaxes(ccr, 1, 2)
    cci_t = jnp.swapaxes(cci, 1, 2)

    er = ccr[None] * pr[:ch, :, None, :] - cci[None] * pi[:ch, :, None, :]
    ei = ccr[None] * pi[:ch, :, None, :] + cci[None] * pr[:ch, :, None, :]
    m = jnp.sum(er[:, :, None, :, :] * bbr_t[None, :, :, None, :]
                - ei[:, :, None, :, :] * bbi_t[None, :, :, None, :], axis=-1)
    m_tbl = m.reshape(ch, nlb, LANES, hh)
    rk = np.arange(ch) if reverse else (ch - 1 - np.arange(ch))
    apr, api = pr[rk][:, :, None, :], pi[rk][:, :, None, :]
    sr = apr * bbr_t[None] - api * bbi_t[None]
    si = apr * bbi_t[None] + api * bbr_t[None]
    ws_tbl = jnp.stack([sr, si], axis=0).reshape(2, ch, nlb, LANES, p)
    ex = (ch - np.arange(ch)) if reverse else (np.arange(ch) + 1)
    epr, epi = pr[ex][:, :, :, None], pi[ex][:, :, :, None]
    wo_r = ccr_t[None] * epr - cci_t[None] * epi
    wo_i = -(ccr_t[None] * epi + cci_t[None] * epr)
    wo_tbl = jnp.stack([wo_r, wo_i], axis=0).reshape(2, ch, nlb, gpb * p, hh)
    kin, ws, wo = _s5_assemble(m_tbl, ws_tbl, wo_tbl, reverse=reverse)
    a_chunk = jnp.stack([pr[ch].reshape(nlb, gpb * p), pi[ch].reshape(nlb, gpb * p)], axis=1)
    return kin, ws, wo, a_chunk


def _s5_kernel(u_ref, s0_ref, a_ref, kin_ref, ws_ref, wo_ref, d_ref, y_ref, sfin_ref,
               z_scr, sin_scr, carry_scr, *, reverse, nb, cc, add_skip):
    j = pl.program_id(1)

    @pl.when(j == 0)
    def _():
        carry_scr[...] = s0_ref[...]

    ch = S5_CHUNK
    width = ch * LANES
    sw = a_ref.shape[-1]
    ns = sw // LANES
    u = jnp.concatenate(
        [jnp.concatenate([u_ref[b, pl.ds(k, cc, stride=ch), :] for b in range(nb)], axis=0)
         for k in range(ch)], axis=1)
    ub = u.astype(BF16)
    z = _dot(ub, ws_ref[...])
    ccp = cc + S5_ROW_PAD
    for s in range(2 * ns):
        for b in range(nb):
            z_scr[s, b * ccp:b * ccp + cc, :] = z[b * cc:(b + 1) * cc, s * LANES:(s + 1) * LANES]
    ar = [a_ref[0:1, s * LANES:(s + 1) * LANES] for s in range(ns)]
    ai = [a_ref[1:2, s * LANES:(s + 1) * LANES] for s in range(ns)]

    def body(c, st):
        cidx = (cc - 1 - c) if reverse else c
        rows = pl.ds(cidx, nb, stride=ccp)
        new = [None] * (2 * ns)
        for s in range(ns):
            sr, si = st[s], st[ns + s]
            sin_scr[s, rows, :] = sr
            sin_scr[ns + s, rows, :] = si
            new[s] = ar[s] * sr - ai[s] * si + z_scr[s, rows, :]
            new[ns + s] = ar[s] * si + ai[s] * sr + z_scr[ns + s, rows, :]
        return tuple(new)

    st0 = tuple(carry_scr[:, s * LANES:(s + 1) * LANES] for s in range(2 * ns))
    st = lax.fori_loop(0, cc, body, st0)
    s_fin = jnp.concatenate(st, axis=1)
    carry_scr[...] = s_fin
    sfin_ref[...] = s_fin
    s_in = jnp.concatenate(
        [jnp.concatenate([sin_scr[s, b * ccp:b * ccp + cc, :] for b in range(nb)], axis=0)
         for s in range(2 * ns)], axis=1)
    intra = []
    for c0 in range(0, width, MXU_DIM):
        ks = slice(c0, width) if reverse else slice(0, c0 + MXU_DIM)
        intra.append(_dot(ub[:, ks], kin_ref[ks, c0:c0 + MXU_DIM]))
    y = jnp.concatenate(intra, axis=1) + _dot(s_in.astype(BF16), wo_ref[...])
    if add_skip:
        y = y + u * d_ref[...]
    for b in range(nb):
        for i in range(ch):
            y_ref[b, pl.ds(i, cc, stride=ch), :] = y[b * cc:(b + 1) * cc, i * LANES:(i + 1) * LANES]


def _s5_scan(u4, s0, weights, d_tile, *, reverse, cc, add_skip):
    kin, ws, wo, a_chunk = weights
    nlb, nb, n_steps, _ = u4.shape
    width = S5_CHUNK * LANES
    sw2 = ws.shape[-1]
    nj = n_steps // (cc * S5_CHUNK)

    def jm(j):
        return (nj - 1 - j) if reverse else j

    return pl.pallas_call(
        functools.partial(_s5_kernel, reverse=reverse, nb=nb, cc=cc, add_skip=add_skip),
        grid=(nlb, nj),
        in_specs=[
            pl.BlockSpec((None, nb, cc * S5_CHUNK, LANES), lambda l, j: (l, 0, jm(j), 0)),
            pl.BlockSpec((None, nb, sw2), lambda l, j: (l, 0, 0)),
            pl.BlockSpec((None, 2, sw2 // 2), lambda l, j: (l, 0, 0)),
            pl.BlockSpec((None, width, width), lambda l, j: (l, 0, 0)),
            pl.BlockSpec((None, width, sw2), lambda l, j: (l, 0, 0)),
            pl.BlockSpec((None, sw2, width), lambda l, j: (l, 0, 0)),
            pl.BlockSpec((None, 1, width), lambda l, j: (l, 0, 0)),
        ],
        out_specs=[
            pl.BlockSpec((None, nb, cc * S5_CHUNK, LANES), lambda l, j: (l, 0, jm(j), 0)),
            pl.BlockSpec((None, nb, sw2), lambda l, j: (l, 0, 0)),
        ],
        out_shape=[
            jax.ShapeDtypeStruct(u4.shape, F32),
            jax.ShapeDtypeStruct((nlb, nb, sw2), F32),
        ],
        scratch_shapes=[
            pltpu.VMEM((sw2 // LANES, nb * (cc + S5_ROW_PAD), LANES), F32),
            pltpu.VMEM((sw2 // LANES, nb * (cc + S5_ROW_PAD), LANES), F32),
            pltpu.VMEM((nb, sw2), F32),
        ],
        compiler_params=_cparams(("arbitrary", "arbitrary")),
        name="s5_bwd" if reverse else "s5_fwd",
    )(u4, s0, a_chunk, kin, ws, wo, d_tile)


def _attn_kernel(sink_ref, q_ref, kp_ref, ko_ref, kn_ref, vp_ref, vo_ref, vn_ref, kc_ref, vc_ref,
                 o_ref, *, nblk, blk, qblocks):
    n = pl.program_id(1)
    n_ctx = kc_ref.shape[0]
    n_loc = 3 * blk
    nrow = Q_PER_KV * blk
    row = lax.broadcasted_iota(jnp.int32, (nrow, n_loc), 0)
    qi = row & (blk - 1)
    kj = lax.broadcasted_iota(jnp.int32, (nrow, n_loc), 1)
    rel = kj - qi
    in_window = (rel >= blk - WINDOW) & (rel <= blk + WINDOW)
    lane = lax.broadcasted_iota(jnp.int32, (blk, LANES), 1)
    low_half = lane < HEAD_DIM
    row1 = lax.broadcasted_iota(jnp.int32, (nrow, 1), 0)

    def valid_mask(j):
        gblk = n * qblocks + j
        lo = jnp.where(gblk == 0, blk, 0)
        hi = jnp.where(gblk == nblk - 1, 2 * blk, n_loc)
        return in_window & (kj >= lo) & (kj < hi)

    valid = [valid_mask(j) for j in range(qblocks)]
    parts = [(j, g) for j in range(qblocks) for g in range(N_KV_HEADS)]
    each = lambda f: [f(j, g) for j, g in parts]
    at = {pg: i for i, pg in enumerate(parts)}

    def key_blocks(p_ref, own_ref, n_ref, c_ref, j, g):
        cs = slice(g * LANES, (g + 1) * LANES)
        blocks = [p_ref[:, cs]] + [own_ref[i * blk:(i + 1) * blk, cs] for i in range(qblocks)] + [n_ref[:, cs]]
        return jnp.concatenate(blocks[j:j + 3] + [c_ref[:, cs]], axis=0)

    kcat = each(lambda j, g: key_blocks(kp_ref, ko_ref, kn_ref, kc_ref, j, g))
    vcat = each(lambda j, g: key_blocks(vp_ref, vo_ref, vn_ref, vc_ref, j, g))

    def stacked_q(j, g):
        qms = []
        for hq in range(Q_PER_KV):
            h = g * Q_PER_KV + hq
            qb = q_ref[j * blk:(j + 1) * blk, (h // 2) * LANES:(h // 2 + 1) * LANES]
            qms.append(jnp.where(low_half if h % 2 == 0 else jnp.logical_not(low_half), qb, jnp.zeros_like(qb)))
        return jnp.concatenate(qms, axis=0)

    def stacked_sink(j, g):
        snk = jnp.zeros((nrow, 1), F32)
        for hq in range(Q_PER_KV):
            snk = jnp.where((row1 >= hq * blk) & (row1 < (hq + 1) * blk), sink_ref[g * Q_PER_KV + hq], snk)
        return snk

    qs = each(stacked_q)
    snk = each(stacked_sink)
    s = each(lambda j, g: lax.dot_general(qs[at[j, g]], kcat[at[j, g]], (((1,), (1,)), ((), ())),
                                          preferred_element_type=F32))
    s = each(lambda j, g: jnp.concatenate(
        [jnp.where(valid[j], s[at[j, g]][:, :n_loc], NEG_INF), s[at[j, g]][:, n_loc:]], axis=1))
    mx = each(lambda j, g: jnp.maximum(jnp.max(s[at[j, g]], axis=1, keepdims=True), snk[at[j, g]]))
    p = each(lambda j, g: jnp.exp(s[at[j, g]] - mx[at[j, g]]))
    den = each(lambda j, g: jnp.sum(p[at[j, g]], axis=1, keepdims=True) + jnp.exp(snk[at[j, g]] - mx[at[j, g]]))
    o = each(lambda j, g: _dot(p[at[j, g]].astype(BF16), vcat[at[j, g]]) / den[at[j, g]])
    for j in range(qblocks):
        outs = [o[at[j, g]][hq * blk:(hq + 1) * blk] for g in range(N_KV_HEADS) for hq in range(Q_PER_KV)]
        blocks = [jnp.where(low_half, outs[2 * m], outs[2 * m + 1]) for m in range(N_Q_HEADS // 2)]
        o_ref[j * blk:(j + 1) * blk, :] = jnp.concatenate(blocks, axis=1).astype(BF16)


def _attention(q, kd, vd, kcd, vcd, sink, *, bsz, seq_len, n_ctx, qblocks):
    blk = WINDOW
    nblk = seq_len // blk
    nstep = nblk // qblocks
    aw = q.shape[1]
    kw = kd.shape[1]

    def qmap(b, n):
        return (b * nstep + n, 0)

    def pmap(b, n):
        return (b * nblk + jnp.maximum(n * qblocks - 1, 0), 0)

    def nmap(b, n):
        return (b * nblk + jnp.minimum((n + 1) * qblocks, nblk - 1), 0)

    edge = lambda f: pl.BlockSpec((blk, kw), f)
    own = pl.BlockSpec((qblocks * blk, kw), qmap)
    return pl.pallas_call(
        functools.partial(_attn_kernel, nblk=nblk, blk=blk, qblocks=qblocks),
        grid=(bsz, nstep),
        in_specs=[
            pl.BlockSpec(memory_space=pltpu.SMEM),
            pl.BlockSpec((qblocks * blk, aw), qmap),
            edge(pmap), own, edge(nmap),
            edge(pmap), own, edge(nmap),
            pl.BlockSpec((n_ctx, kw), lambda b, n: (b, 0)),
            pl.BlockSpec((n_ctx, kw), lambda b, n: (b, 0)),
        ],
        out_specs=pl.BlockSpec((qblocks * blk, aw), qmap),
        out_shape=jax.ShapeDtypeStruct(q.shape, BF16),
        compiler_params=_cparams(("arbitrary", "arbitrary")),
        name="attn",
    )(sink, q, kd, kd, kd, vd, vd, vd, kcd, vcd)


def _gelu_tanh(x):
    return 0.5 * x * (1.0 + jnp.tanh(math.sqrt(2.0 / math.pi) * (x + 0.044715 * (x * x * x))))


def _flatten_slots(v, max_value):
    tm, nk = v.shape
    per_row = LANES // nk
    log2 = lambda x: int(x).bit_length() - 1
    assert nk == 1 << log2(nk) and tm % per_row == 0
    spread = ((lax.broadcasted_iota(jnp.int32, (nk, LANES), 1) & (nk - 1))
              == lax.broadcasted_iota(jnp.int32, (nk, LANES), 0)).astype(BF16)
    t_id = lax.broadcasted_iota(jnp.int32, (tm, LANES), 0)
    c_id = lax.broadcasted_iota(jnp.int32, (tm, LANES), 1)
    own = (c_id >> log2(nk)) == (t_id & (per_row - 1))
    group = ((lax.broadcasted_iota(jnp.int32, (tm // per_row, tm), 1) >> log2(per_row))
             == lax.broadcasted_iota(jnp.int32, (tm // per_row, tm), 0)).astype(BF16)
    digit_bits = 7
    out = jnp.zeros((tm // per_row, LANES), jnp.int32)
    for shift in range(0, max(int(max_value).bit_length(), 1), digit_bits):
        digit = ((v >> shift) & ((1 << digit_bits) - 1)).astype(F32).astype(BF16)
        wide = jnp.dot(digit, spread, preferred_element_type=F32)
        wide = jnp.where(own, wide, 0.0).astype(BF16)
        out = out + (jnp.dot(group, wide, preferred_element_type=F32).astype(jnp.int32) << shift)
    return out


def _merge_kernel(yf_ref, yb_ref, o_ref, gs_ref, ga_ref, x_ref, lng_ref, lnb_ref, g1_ref, sh2_ref, sc2_ref,
                  wglu_ref, bglu_ref, wso_ref, wao_ref, wo_ref, l1g_ref, l1b_ref, wr_ref, br_ref,
                  h1_ref, idx_ref, gate_ref, rank_ref, cnt_ref, cnt_scr, *, alpha, n_exp, sub, n_tokens):
    i = pl.program_id(0)

    @pl.when(i == 0)
    def _():
        cnt_scr[...] = jnp.zeros_like(cnt_scr)

    nlb = yf_ref.shape[0]
    tm = x_ref.shape[0]
    lane = lax.broadcasted_iota(jnp.int32, (sub, n_exp), 1)
    lane_k = lax.broadcasted_iota(jnp.int32, (sub, TOP_K), 1)
    ri = lax.broadcasted_iota(jnp.int32, (sub, sub), 0)
    ci = lax.broadcasted_iota(jnp.int32, (sub, sub), 1)
    tri = jnp.where(ci < ri, 1.0, 0.0).astype(BF16)
    slots = sub * TOP_K // LANES
    cnt = cnt_scr[...]
    parts = range(tm // sub)
    rs = [slice(p * sub, (p + 1) * sub) for p in parts]
    each = lambda f: [f(p) for p in parts]
    y = each(lambda p: jnp.concatenate([yf_ref[lb, rs[p], :] + yb_ref[lb, rs[p], :] for lb in range(nlb)], axis=1))
    z = each(lambda p: _gelu_tanh(y[p]))
    zg = each(lambda p: _dot(z[p].astype(BF16), wglu_ref[...]) + bglu_ref[...])
    z = each(lambda p: z[p] * _sigmoid(zg[p]))
    ms = each(lambda p: _dot(z[p].astype(BF16), wso_ref[...]))
    ma = each(lambda p: _dot(o_ref[rs[p], :], wao_ref[...]))
    m = each(lambda p: gs_ref[rs[p], :].astype(F32) * ms[p] + ga_ref[rs[p], :].astype(F32) * ma[p])
    mix = each(lambda p: _dot(m[p].astype(BF16), wo_ref[...]))
    h = each(lambda p: _layer_norm(x_ref[rs[p], :], lng_ref[...], lnb_ref[...]))
    h1 = each(lambda p: _layer_norm(alpha * h[p] + g1_ref[...] * mix[p], l1g_ref[...], l1b_ref[...]))
    for p in parts:
        h1_ref[rs[p], :] = h1[p]
    xm = each(lambda p: h1[p] * (1.0 + sc2_ref[...]) + sh2_ref[...])
    work = each(lambda p: _dot_3pass(xm[p], wr_ref[...]) + br_ref[...])
    vals, sels = [], []
    for _ in range(TOP_K):
        mx = each(lambda p: jnp.max(work[p], axis=1, keepdims=True))
        sel = each(lambda p: jnp.min(jnp.where(work[p] == mx[p], lane, n_exp), axis=1, keepdims=True))
        work = each(lambda p: jnp.where(lane == sel[p], -jnp.inf, work[p]))
        vals.append(mx)
        sels.append(sel)
    exps = [each(lambda p: jnp.exp(vals[k][p] - vals[0][p])) for k in range(TOP_K)]
    den = each(lambda p: exps[0][p] + exps[1][p] + exps[2][p] + exps[3][p])
    onehot = each(lambda p: sum((lane == sels[k][p]).astype(F32) for k in range(TOP_K)))
    prefix = each(lambda p: _dot(tri, onehot[p].astype(BF16)))
    for p in parts:
        rank = prefix[p] + cnt
        idx_o = jnp.zeros((sub, TOP_K), jnp.int32)
        gate_o = jnp.zeros((sub, TOP_K), F32)
        rank_o = jnp.zeros((sub, TOP_K), jnp.int32)
        for k in range(TOP_K):
            rk = jnp.sum(jnp.where(lane == sels[k][p], rank, 0.0), axis=1, keepdims=True).astype(jnp.int32)
            idx_o = jnp.where(lane_k == k, sels[k][p], idx_o)
            gate_o = jnp.where(lane_k == k, exps[k][p] / den[p], gate_o)
            rank_o = jnp.where(lane_k == k, rk, rank_o)
        gate_ref[rs[p], :] = gate_o
        idx_ref[p * slots:(p + 1) * slots, :] = _flatten_slots(idx_o, n_exp - 1)
        rank_ref[p * slots:(p + 1) * slots, :] = _flatten_slots(rank_o, n_tokens - 1)
        cnt = cnt + jnp.sum(onehot[p], axis=0, keepdims=True)
    cnt_scr[...] = cnt
    cnt_ref[...] = cnt


def _merge(yf, yb, o_att, sgs, sga, x2, ln_g, ln_b, mod3, w_glu, b_glu, w_so, w_ao, w_o, l1g, l1b,
           w_r, b_r, *, rows_per_batch, tm, sub, alpha):
    t, d = x2.shape
    nlb = yf.shape[0]
    sw = nlb * LANES
    n_exp = w_r.shape[1]
    tpb = rows_per_batch // tm
    row = lambda i: (i, 0)
    const = lambda i: (0, 0)

    def modspec(chunk):
        return pl.BlockSpec((None, 1, d), lambda i: (i // tpb, 0, chunk))

    return pl.pallas_call(
        functools.partial(_merge_kernel, alpha=alpha, n_exp=n_exp, sub=sub, n_tokens=t),
        grid=(t // tm,),
        in_specs=[
            pl.BlockSpec((nlb, tm, LANES), lambda i: (0, i, 0)),
            pl.BlockSpec((nlb, tm, LANES), lambda i: (0, i, 0)),
            pl.BlockSpec((tm, o_att.shape[1]), row),
            pl.BlockSpec((tm, d), row),
            pl.BlockSpec((tm, d), row),
            pl.BlockSpec((tm, d), row),
            pl.BlockSpec((1, d), const),
            pl.BlockSpec((1, d), const),
            modspec(2), modspec(3), modspec(4),
            pl.BlockSpec(w_glu.shape, const),
            pl.BlockSpec((1, sw), const),
            pl.BlockSpec(w_so.shape, const),
            pl.BlockSpec(w_ao.shape, const),
            pl.BlockSpec(w_o.shape, const),
            pl.BlockSpec((1, d), const),
            pl.BlockSpec((1, d), const),
            pl.BlockSpec(w_r.shape, const),
            pl.BlockSpec((1, n_exp), const),
        ],
        out_specs=[
            pl.BlockSpec((tm, d), row),
            pl.BlockSpec((tm * TOP_K // LANES, LANES), row),
            pl.BlockSpec((tm, TOP_K), row),
            pl.BlockSpec((tm * TOP_K // LANES, LANES), row),
            pl.BlockSpec((1, n_exp), const),
        ],
        out_shape=[
            jax.ShapeDtypeStruct((t, d), F32),
            jax.ShapeDtypeStruct((t * TOP_K // LANES, LANES), jnp.int32),
            jax.ShapeDtypeStruct((t, TOP_K), F32),
            jax.ShapeDtypeStruct((t * TOP_K // LANES, LANES), jnp.int32),
            jax.ShapeDtypeStruct((1, n_exp), F32),
        ],
        scratch_shapes=[pltpu.VMEM((1, n_exp), F32)],
        compiler_params=_cparams(("arbitrary",)),
        name="merge",
    )(yf, yb, o_att, sgs, sga, x2, ln_g, ln_b, mod3, mod3, mod3, w_glu, b_glu, w_so, w_ao, w_o,
      l1g, l1b, w_r, b_r)


def _row_tile(ref, r):
    return ref.at[pl.ds(pl.multiple_of(r * ROW_TILES, ROW_TILES), ROW_TILES)]


def _row_tile_at(ref, first_line):
    return ref.at[pl.ds(pl.multiple_of(first_line, ROW_TILES), ROW_TILES)]


def _store_row_tiles(ref, val):
    rows = val.shape[0]
    for l in range(ROW_TILES):
        ref[pl.ds(l, rows, stride=ROW_TILES), :] = val[:, l * LANES:(l + 1) * LANES]


def _load_row_tiles(ref, row0, rows):
    return jnp.concatenate(
        [ref[pl.ds(row0 * ROW_TILES + l, rows, stride=ROW_TILES), :] for l in range(ROW_TILES)], axis=1)


def _row_copy_out(src, dst, pos_ref, sem, t, k):
    return pltpu.make_async_copy(_row_tile(src, t), _row_tile_at(dst, pos_ref[t * TOP_K + k]), sem)


def _dispatch_kernel(pos_ref, prev_pos_ref, h1_ref, sh2_ref, sc2_ref, xrows_ref, xm_scr, sems):
    i = pl.program_id(0)
    n = pl.num_programs(0)
    tm = h1_ref.shape[0]
    slot = i % 2
    xm = h1_ref[...] * (1.0 + sc2_ref[...]) + sh2_ref[...]
    _store_row_tiles(xm_scr.at[slot], xm)

    def copies(slot_, idx_ref, op):
        def body(t, carry):
            for k in range(TOP_K):
                op(_row_copy_out(xm_scr.at[slot_], xrows_ref, idx_ref, sems.at[slot_], t, k), k)
            return carry
        lax.fori_loop(0, tm, body, 0, unroll=DMA_LOOP_UNROLL)

    start = lambda cp, k: cp.start(priority=k % N_DMA_PRIORITIES)
    wait = lambda cp, k: cp.wait()
    for s in range(2):
        @pl.when(slot == s)
        def _():
            copies(s, pos_ref, start)

    for s in range(2):
        @pl.when((slot == 1 - s) & (i > 0))
        def _():
            copies(s, prev_pos_ref, wait)

        @pl.when((slot == s) & (i == n - 1))
        def _():
            copies(s, pos_ref, wait)


def _dispatch(pos_flat, h1, mod3, *, rows_per_batch, tm):
    t, d = h1.shape
    tpb = rows_per_batch // tm
    return pl.pallas_call(
        _dispatch_kernel,
        grid=(t // tm,),
        in_specs=[
            pl.BlockSpec((tm * TOP_K,), lambda i: (i,), memory_space=pltpu.SMEM),
            pl.BlockSpec((tm * TOP_K,), lambda i: (jnp.maximum(i - 1, 0),), memory_space=pltpu.SMEM),
            pl.BlockSpec((tm, d), lambda i: (i, 0)),
            pl.BlockSpec((None, 1, d), lambda i: (i // tpb, 0, 3)),
            pl.BlockSpec((None, 1, d), lambda i: (i // tpb, 0, 4)),
        ],
        out_specs=pl.BlockSpec(memory_space=pl.ANY),
        out_shape=jax.ShapeDtypeStruct((t * TOP_K * ROW_TILES, LANES), F32),
        scratch_shapes=[pltpu.VMEM((2, tm * ROW_TILES, LANES), F32), pltpu.SemaphoreType.DMA((2,))],
        compiler_params=_cparams(("arbitrary",)),
        name="dispatch",
    )(pos_flat, pos_flat, h1, mod3, mod3)


def _experts_kernel(tile_ref, exp_ref, lo_ref, hi_ref, nxt_ref, x_ref, wgu_hbm, bgu_ref, wd_hbm, bd_ref, y_ref,
                    wgu_f32, wd_f32, wgu_scr, wd_scr, sems, *, rows, sub_rows):
    w = pl.program_id(0)
    prev = jnp.maximum(w - 1, 0)
    e_new = (w == 0) | (exp_ref[w] != exp_ref[prev])
    t_new = (w == 0) | (tile_ref[w] != tile_ref[prev])
    lo = lo_ref[w]
    hi = hi_ref[w]
    f = wd_scr.shape[0]

    def weight_copies(e):
        return (pltpu.make_async_copy(wgu_hbm.at[e], wgu_f32, sems.at[0]),
                pltpu.make_async_copy(wd_hbm.at[e], wd_f32, sems.at[1]))

    @pl.when(w == 0)
    def _():
        for cp in weight_copies(exp_ref[0]):
            cp.start()

    @pl.when(e_new)
    def _():
        for cp in weight_copies(exp_ref[w]):
            cp.wait()
        wgu_scr[...] = wgu_f32[...].astype(BF16)
        wd_scr[...] = wd_f32[...].astype(BF16)

        @pl.when(nxt_ref[w] >= 0)
        def _():
            for cp in weight_copies(nxt_ref[w]):
                cp.start()

    row0 = tile_ref[w] * rows
    whole = (lo <= row0) & (hi >= row0 + rows)

    @pl.when(t_new & jnp.logical_not(whole))
    def _():
        y_ref[...] = jnp.zeros_like(y_ref)

    def expert_mlp(x):
        gu = _dot(x.astype(BF16), wgu_scr[...]) + bgu_ref[...]
        glu = jnp.minimum(gu[:, :f], SWIGLU_LIMIT)
        lin = jnp.clip(gu[:, f:], -SWIGLU_LIMIT, SWIGLU_LIMIT)
        act = glu * _sigmoid(SWIGLU_ALPHA * glu) * (lin + 1.0)
        return _dot(act.astype(BF16), wd_scr[...]) + bd_ref[...]

    def sub_block(sb):
        s_lo = row0 + sb * sub_rows
        s_hi = s_lo + sub_rows
        y_sub = y_ref.at[pl.ds(sb * sub_rows * ROW_TILES, sub_rows * ROW_TILES)]

        @pl.when((hi > s_lo) & (lo < s_hi))
        def _():
            y = expert_mlp(_load_row_tiles(x_ref, sb * sub_rows, sub_rows))
            sub_whole = (lo <= s_lo) & (hi >= s_hi)

            @pl.when(sub_whole)
            def _():
                _store_row_tiles(y_sub, y)

            @pl.when(jnp.logical_not(sub_whole))
            def _():
                r = s_lo + lax.broadcasted_iota(jnp.int32, (sub_rows, 1), 0)
                _store_row_tiles(y_sub, jnp.where((r >= lo) & (r < hi), y, _load_row_tiles(y_sub, 0, sub_rows)))

    @pl.when(whole)
    def _():
        _store_row_tiles(y_ref, expert_mlp(_load_row_tiles(x_ref, 0, rows)))

    @pl.when(jnp.logical_not(whole))
    def _():
        for sb in range(rows // sub_rows):
            sub_block(sb)


def _experts(work, xrows, w_gu, b_gu, w_d, b_d, *, rows, sub_rows):
    tile_id, exp_id, lo, hi, nxt = work
    n_exp, d, f2 = w_gu.shape
    f = w_d.shape[1]
    grid_spec = pltpu.PrefetchScalarGridSpec(
        num_scalar_prefetch=5,
        grid=(tile_id.shape[0],),
        in_specs=[
            pl.BlockSpec((rows * ROW_TILES, LANES), lambda w, ti, ex, lo, hi, nx: (ti[w], 0)),
            pl.BlockSpec(memory_space=pl.ANY),
            pl.BlockSpec((None, 1, f2), lambda w, ti, ex, lo, hi, nx: (ex[w], 0, 0)),
            pl.BlockSpec(memory_space=pl.ANY),
            pl.BlockSpec((None, 1, d), lambda w, ti, ex, lo, hi, nx: (ex[w], 0, 0)),
        ],
        out_specs=pl.BlockSpec((rows * ROW_TILES, LANES), lambda w, ti, ex, lo, hi, nx: (ti[w], 0)),
        scratch_shapes=[
            pltpu.VMEM((d, f2), F32), pltpu.VMEM((f, d), F32),
            pltpu.VMEM((d, f2), BF16), pltpu.VMEM((f, d), BF16),
            pltpu.SemaphoreType.DMA((2,)),
        ],
    )
    return pl.pallas_call(
        functools.partial(_experts_kernel, rows=rows, sub_rows=sub_rows),
        grid_spec=grid_spec,
        out_shape=jax.ShapeDtypeStruct(xrows.shape, F32),
        compiler_params=_cparams(("arbitrary",)),
        name="experts",
    )(tile_id, exp_id, lo, hi, nxt, xrows, w_gu, b_gu.reshape(n_exp, 1, f2), w_d, b_d.reshape(n_exp, 1, d))


def _row_copy_in(src, dst, pos_ref, sem, t, k, tm):
    return pltpu.make_async_copy(_row_tile_at(src, pos_ref[t * TOP_K + k]), _row_tile(dst, k * tm + t), sem)


def _combine_kernel(pos_ref, next_pos_ref, h1_ref, gate_ref, g2_ref, lg_ref, lb_ref, yrows_ref, o_ref,
                    buf, sems, *, alpha):
    i = pl.program_id(0)
    n = pl.num_programs(0)
    tm = h1_ref.shape[0]
    slot = i % 2

    def copies(slot_, idx_ref, op):
        def body(t, carry):
            for k in range(TOP_K):
                op(_row_copy_in(yrows_ref, buf.at[slot_], idx_ref, sems.at[slot_], t, k, tm), k)
            return carry
        lax.fori_loop(0, tm, body, 0, unroll=DMA_LOOP_UNROLL)

    start = lambda cp, k: cp.start(priority=k % N_DMA_PRIORITIES)
    wait = lambda cp, k: cp.wait()

    @pl.when(i == 0)
    def _():
        copies(0, pos_ref, start)

    for s in range(2):
        @pl.when((slot == 1 - s) & (i + 1 < n))
        def _():
            copies(s, next_pos_ref, start)

    def reduce(s):
        copies(s, pos_ref, wait)
        gates = gate_ref[...]
        ffn = gates[:, 0:1] * _load_row_tiles(buf.at[s], 0, tm)
        for k in range(1, TOP_K):
            ffn = ffn + gates[:, k:k + 1] * _load_row_tiles(buf.at[s], k * tm, tm)
        o_ref[...] = _layer_norm(alpha * h1_ref[...] + g2_ref[...] * ffn, lg_ref[...], lb_ref[...])

    for s in range(2):
        @pl.when(slot == s)
        def _():
            reduce(s)


def _combine(pos_flat, h1, gates, mod3, l2g, l2b, yrows, *, rows_per_batch, tm, alpha):
    t, d = h1.shape
    tpb = rows_per_batch // tm
    n_tiles = t // tm
    return pl.pallas_call(
        functools.partial(_combine_kernel, alpha=alpha),
        grid=(n_tiles,),
        in_specs=[
            pl.BlockSpec((tm * TOP_K,), lambda i: (i,), memory_space=pltpu.SMEM),
            pl.BlockSpec((tm * TOP_K,), lambda i: (jnp.minimum(i + 1, n_tiles - 1),), memory_space=pltpu.SMEM),
            pl.BlockSpec((tm, d), lambda i: (i, 0)),
            pl.BlockSpec((tm, TOP_K), lambda i: (i, 0)),
            pl.BlockSpec((None, 1, d), lambda i: (i // tpb, 0, 5)),
            pl.BlockSpec((1, d), lambda i: (0, 0)),
            pl.BlockSpec((1, d), lambda i: (0, 0)),
            pl.BlockSpec(memory_space=pl.ANY),
        ],
        out_specs=pl.BlockSpec((tm, d), lambda i: (i, 0)),
        out_shape=jax.ShapeDtypeStruct((t, d), F32),
        scratch_shapes=[pltpu.VMEM((2, TOP_K * tm * ROW_TILES, LANES), F32), pltpu.SemaphoreType.DMA((2,))],
        compiler_params=_cparams(("arbitrary",)),
        name="combine",
    )(pos_flat, pos_flat, h1, gates, mod3, l2g, l2b, yrows)


def _plan_kernel(cnt_ref, idx_ref, rank_ref, pos_ref, tile_out, exp_out, lo_out, hi_out, nxt_out,
                 start_s, end_s, next_s, *, rows, n_tiles):
    n_exp = cnt_ref.shape[0]
    n_work = tile_out.shape[0]

    def cumulate(e, acc):
        start_s[e] = acc
        end_s[e] = acc + cnt_ref[e]
        return acc + cnt_ref[e]

    lax.fori_loop(0, n_exp, cumulate, 0)

    def next_nonempty(i, cur):
        e = n_exp - 1 - i
        next_s[e] = cur
        return jnp.where(cnt_ref[e] > 0, e, cur)

    first = lax.fori_loop(0, n_exp, next_nonempty, -1)

    def item(w, carry):
        tile, e = carry
        live = tile < n_tiles
        tl = jnp.minimum(tile, n_tiles - 1)
        tile_lo = tl * rows
        tile_hi = tile_lo + rows
        lo = jnp.maximum(start_s[e], tile_lo)
        hi = jnp.where(live, jnp.minimum(end_s[e], tile_hi), lo)
        tile_out[w] = tl
        exp_out[w] = e
        lo_out[w] = lo
        hi_out[w] = hi
        nxt_out[w] = next_s[e]
        tile_done = live & (end_s[e] >= tile_hi)
        expert_done = live & (end_s[e] <= tile_hi) & (next_s[e] >= 0)
        return jnp.where(tile_done, tile + 1, tile), jnp.where(expert_done, next_s[e], e)

    lax.fori_loop(0, n_work, item, (0, first))

    idx = idx_ref[...]
    base = jnp.zeros(idx.shape, jnp.int32)
    for e in range(n_exp):
        base = jnp.where(idx == e, start_s[e], base)
    pos_ref[...] = (base + rank_ref[...]) * ROW_TILES


def _plan(counts, idx_flat, rank_flat, *, rows):
    n_exp = counts.shape[0]
    n_rows = idx_flat.shape[0] * idx_flat.shape[1]
    n_tiles = n_rows // rows
    n_work = n_tiles + n_exp - 1
    smem = pl.BlockSpec(memory_space=pltpu.SMEM)
    vmem = pl.BlockSpec(memory_space=pltpu.VMEM)
    return pl.pallas_call(
        functools.partial(_plan_kernel, rows=rows, n_tiles=n_tiles),
        in_specs=[smem, vmem, vmem],
        out_specs=[vmem] + [smem] * 5,
        out_shape=[jax.ShapeDtypeStruct(idx_flat.shape, jnp.int32)]
        + [jax.ShapeDtypeStruct((n_work,), jnp.int32)] * 5,
        scratch_shapes=[pltpu.SMEM((n_exp,), jnp.int32)] * 3,
        name="plan",
    )(counts, idx_flat, rank_flat)


def kernel(x, c, ctx, c_ctx, ln_in_g, ln_in_b, w_mod, b_mod, w_in, ssm_lam_re, ssm_lam_im, ssm_log_step, ssm_b_re, ssm_b_im, ssm_c_re, ssm_c_im, ssm_d, w_glu, b_glu, attn_sink, w_ssm_out, w_att_out, w_o, ln1_g, ln1_b, w_router, b_router, w_gate_up, b_gate_up, w_down, b_down, ln2_g, ln2_b):
    bsz, seq_len, d = x.shape
    n_ctx = ctx.shape[1]
    depth = w_mod.shape[0]
    assert depth == 1, "single-layer kernel"
    g_ssm, h_ssm = ssm_d.shape[1:]
    ssm_w = g_ssm * h_ssm
    attn_w = N_Q_HEADS * HEAD_DIM
    kv_w = N_KV_HEADS * HEAD_DIM
    kv2_w = 2 * kv_w
    nlb = ssm_w // LANES
    alpha = (2.0 * depth) ** 0.25
    t = bsz * seq_len
    assert bsz + 1 <= SUBLANES
    assert d == ROW_TILES * LANES and ROW_TILES == SUBLANES, "a token row must be exactly one (8, 128) tile"

    row2 = lambda a: a.reshape(1, -1)

    c_rows = jnp.concatenate([c, c_ctx[None], jnp.zeros((SUBLANES - bsz - 1, d), F32)], axis=0)
    mod = _mod_vectors(c_rows, w_mod[0], row2(b_mod[0]))
    mod3 = mod.reshape(SUBLANES, 1, 6 * d)

    assert w_in.shape[2] == ssm_w + attn_w + 2 * kv_w + 2 * d
    w_all = w_in[0].astype(BF16)

    pos = jnp.arange(seq_len)
    inv = ROPE_BASE ** (-jnp.arange(ROPE_PAIRS, dtype=F32) / ROPE_PAIRS)
    ang = jnp.concatenate([(pos // GRID_W).astype(F32)[:, None] * inv,
                           (pos % GRID_W).astype(F32)[:, None] * inv], axis=-1)
    cos_t = jnp.tile(jnp.cos(ang), (1, 2 * LANES // HEAD_DIM))
    sin_h = jnp.sin(ang)
    sin_t = jnp.tile(jnp.concatenate([-sin_h, sin_h], axis=-1), (1, LANES // HEAD_DIM))

    g_in, b_in = row2(ln_in_g), row2(ln_in_b)
    x2 = x.reshape(t, d)
    ctx2 = ctx.reshape(bsz * n_ctx, d)
    s_c, k_c, v_c = _inproj(ctx2, g_in, b_in, mod3, w_all, None, None, latent=False,
                            rows_per_batch=n_ctx, tm=n_ctx, sub=n_ctx, ctx_mod_row=bsz,
                            ssm_w=ssm_w, attn_w=attn_w, kv2_w=kv2_w)
    s_l, q_l, k_l, v_l, sgs, sga = _inproj(x2, g_in, b_in, mod3, w_all, cos_t, sin_t, latent=True,
                                           rows_per_batch=seq_len, tm=TILES.inproj_rows, sub=TILES.inproj_sub, ctx_mod_row=None,
                                           ssm_w=ssm_w, attn_w=attn_w, kv2_w=kv2_w)

    wf = _s5_weights(ssm_lam_re[0, 0], ssm_lam_im[0, 0], ssm_log_step[0, 0], ssm_b_re[0, 0], ssm_b_im[0, 0],
                     ssm_c_re[0, 0], ssm_c_im[0, 0], reverse=False)
    wb = _s5_weights(ssm_lam_re[0, 1], ssm_lam_im[0, 1], ssm_log_step[0, 1], ssm_b_re[0, 1], ssm_b_im[0, 1],
                     ssm_c_re[0, 1], ssm_c_im[0, 1], reverse=True)
    d_tile = jnp.tile(ssm_d[0].astype(F32).reshape(nlb, 1, LANES), (1, 1, S5_CHUNK))
    sw2 = wf[1].shape[-1]
    zero_state = jnp.zeros((nlb, bsz, sw2), F32)
    uc4 = s_c.reshape(nlb, bsz, n_ctx, LANES)
    ul4 = s_l.reshape(nlb, bsz, seq_len, LANES)
    _, sf0 = _s5_scan(uc4, zero_state, wf, d_tile, reverse=False, cc=n_ctx // S5_CHUNK, add_skip=False)
    _, sb0 = _s5_scan(uc4, zero_state, wb, d_tile, reverse=True, cc=n_ctx // S5_CHUNK, add_skip=False)
    yf4, _ = _s5_scan(ul4, sf0, wf, d_tile, reverse=False, cc=TILES.s5_chunks, add_skip=True)
    yb4, _ = _s5_scan(ul4, sb0, wb, d_tile, reverse=True, cc=TILES.s5_chunks, add_skip=False)
    yf = yf4.reshape(nlb, t, LANES)
    yb = yb4.reshape(nlb, t, LANES)

    o_att = _attention(q_l, k_l, v_l, k_c, v_c, attn_sink[0].astype(F32), bsz=bsz, seq_len=seq_len, n_ctx=n_ctx,
                       qblocks=TILES.attn_qblocks)

    h1, top_i, gates, rank, counts = _merge(
        yf, yb, o_att, sgs, sga, x2, g_in, b_in, mod3, w_glu[0].astype(BF16), row2(b_glu[0]),
        w_ssm_out[0].astype(BF16), w_att_out[0].astype(BF16), w_o[0].astype(BF16), row2(ln1_g[0]), row2(ln1_b[0]),
        w_router[0], row2(b_router[0]), rows_per_batch=seq_len, tm=TILES.merge_rows, sub=TILES.merge_sub, alpha=alpha)

    rows, sub_rows = TILES.expert_rows, TILES.expert_sub
    n_rows = t * TOP_K
    pos, tile_id, exp_id, lo, hi, nxt = _plan(counts[0].astype(jnp.int32), top_i, rank, rows=rows)
    pos_flat = pos.reshape(n_rows)
    xrows = _dispatch(pos_flat, h1, mod3, rows_per_batch=seq_len, tm=TILES.moe_token_rows)
    yrows = _experts((tile_id, exp_id, lo, hi, nxt), xrows, w_gate_up[0], b_gate_up[0], w_down[0], b_down[0],
                     rows=rows, sub_rows=sub_rows)
    out = _combine(pos_flat, h1, gates, mod3, row2(ln2_g[0]), row2(ln2_b[0]), yrows,
                   rows_per_batch=seq_len, tm=TILES.moe_token_rows, alpha=alpha)
    return out.reshape(bsz, seq_len, d)
```

```python
import functools
import math
from typing import NamedTuple

import jax
import jax.numpy as jnp
import numpy as np
from jax import lax
from jax.experimental import pallas as pl
from jax.experimental.pallas import tpu as pltpu

F32 = jnp.float32
BF16 = jnp.bfloat16
HIGHEST = lax.Precision.HIGHEST

HEAD_DIM = 64
N_Q_HEADS = 8
N_KV_HEADS = 2
Q_PER_KV = N_Q_HEADS // N_KV_HEADS
WINDOW = 128
GRID_W = 64
ROPE_BASE = 10000.0
ROPE_PAIRS = HEAD_DIM // 4
TOP_K = 4
SWIGLU_LIMIT = 7.0
SWIGLU_ALPHA = 1.702
LN_EPS = 1e-5
NEG_INF = -1e30

LANES = 128
SUBLANES = 8
MXU_DIM = 256
ROW_TILES = 8
VMEM_LIMIT = 56 * 1024 * 1024
N_DMA_PRIORITIES = 2
DMA_LOOP_UNROLL = 8

class _Tiles(NamedTuple):
    inproj_rows: int = 1024
    inproj_sub: int = 512
    s5_chunks: int = 128
    attn_qblocks: int = 4
    merge_rows: int = 1024
    merge_sub: int = 256
    moe_token_rows: int = 512
    expert_rows: int = 512
    expert_sub: int = 256


TILES = _Tiles()

S5_CHUNK = 8
S5_ROW_PAD = SUBLANES


def _cparams(sem):
    return pltpu.CompilerParams(dimension_semantics=sem, vmem_limit_bytes=VMEM_LIMIT)


def _sigmoid(x):
    return 1.0 / (1.0 + jnp.exp(-x))


def _layer_norm(x, g, b):
    mu = jnp.mean(x, axis=-1, keepdims=True)
    xc = x - mu
    var = jnp.mean(xc * xc, axis=-1, keepdims=True)
    return xc * lax.rsqrt(var + LN_EPS) * g + b


def _dot(a, b):
    return jnp.dot(a, b, preferred_element_type=F32)


def _dot_3pass(a, b):
    a_hi = a.astype(BF16)
    b_hi = b.astype(BF16)
    a_lo = (a - a_hi.astype(F32)).astype(BF16)
    b_lo = (b - b_hi.astype(F32)).astype(BF16)
    return _dot(a_hi, b_hi) + (_dot(a_hi, b_lo) + _dot(a_lo, b_hi))


def _mod_kernel(c_ref, w_ref, b_ref, o_ref):
    c = c_ref[...]
    a = c * _sigmoid(c)
    o_ref[...] = jnp.dot(a, w_ref[...], preferred_element_type=F32, precision=HIGHEST) + b_ref[...]


def _mod_vectors(c_rows, w_mod, b_mod):
    d = c_rows.shape[1]
    n = w_mod.shape[1]
    return pl.pallas_call(
        _mod_kernel,
        grid=(n // d,),
        in_specs=[
            pl.BlockSpec((SUBLANES, d), lambda i: (0, 0)),
            pl.BlockSpec((d, d), lambda i: (0, i)),
            pl.BlockSpec((1, d), lambda i: (0, i)),
        ],
        out_specs=pl.BlockSpec((SUBLANES, d), lambda i: (0, i)),
        out_shape=jax.ShapeDtypeStruct((SUBLANES, n), F32),
        compiler_params=_cparams(("arbitrary",)),
        name="mod",
    )(c_rows, w_mod, b_mod)


def _rope(t, cos, sin):
    n = t.shape[1]
    reps = n // LANES
    c = jnp.concatenate([cos] * reps, axis=1) if reps > 1 else cos
    s = jnp.concatenate([sin] * reps, axis=1) if reps > 1 else sin
    half = HEAD_DIM // 2
    upper = pltpu.roll(t, n - half, axis=1)
    lower = pltpu.roll(t, half, axis=1)
    lane = lax.broadcasted_iota(jnp.int32, t.shape, 1)
    partner = jnp.where((lane & half) == 0, upper, lower)
    return t * c + partner * s


def _dup_heads(t):
    assert t.shape[1] == LANES == N_KV_HEADS * HEAD_DIM
    swapped = pltpu.roll(t, HEAD_DIM, axis=1)
    low = lax.broadcasted_iota(jnp.int32, t.shape, 1) < HEAD_DIM
    return jnp.concatenate([jnp.where(low, t, swapped), jnp.where(low, swapped, t)], axis=1)


def _inproj_kernel(*refs, latent, ssm_w, attn_w, kv2_w, d_model, sub):
    if latent:
        (x_ref, g_ref, b_ref, sh_ref, sc_ref, w_ref, cos_ref, sin_ref,
         s_ref, q_ref, k_ref, v_ref, gs_ref, ga_ref, h_ref) = refs
    else:
        x_ref, g_ref, b_ref, sh_ref, sc_ref, w_ref, s_ref, k_ref, v_ref = refs
    tm = x_ref.shape[0]
    parts = range(tm // sub)
    rs = [slice(p * sub, (p + 1) * sub) for p in parts]
    each = lambda f: [f(p) for p in parts]
    h = each(lambda p: _layer_norm(x_ref[rs[p], :], g_ref[...], b_ref[...]))
    u = each(lambda p: (h[p] * (1.0 + sc_ref[...]) + sh_ref[...]).astype(BF16))
    s = each(lambda p: _dot(u[p], w_ref[:, 0:ssm_w]))
    for p in parts:
        for lb in range(ssm_w // LANES):
            s_ref[lb, rs[p], :] = s[p][:, lb * LANES:(lb + 1) * LANES]
    if latent:
        q = each(lambda p: _dot(u[p], w_ref[:, ssm_w:ssm_w + attn_w]))
        q = each(lambda p: _rope(q[p], cos_ref[rs[p], :], sin_ref[rs[p], :]) * (HEAD_DIM ** -0.5))
        for p in parts:
            q_ref[rs[p], :] = q[p].astype(BF16)
    kv_w = kv2_w // 2
    k = each(lambda p: _dot(u[p], w_ref[:, ssm_w + attn_w:ssm_w + attn_w + kv_w]))
    if latent:
        k = each(lambda p: _rope(k[p], cos_ref[rs[p], :], sin_ref[rs[p], :]))
    v = each(lambda p: _dot(u[p], w_ref[:, ssm_w + attn_w + kv_w:ssm_w + attn_w + 2 * kv_w]))
    for p in parts:
        k_ref[rs[p], :] = _dup_heads(k[p]).astype(BF16)
        v_ref[rs[p], :] = _dup_heads(v[p]).astype(BF16)
    if latent:
        col = ssm_w + attn_w + kv2_w
        gs = each(lambda p: _dot(u[p], w_ref[:, col:col + d_model]))
        ga = each(lambda p: _dot(u[p], w_ref[:, col + d_model:col + 2 * d_model]))
        for p in parts:
            gs_ref[rs[p], :] = _sigmoid(gs[p]).astype(BF16)
            ga_ref[rs[p], :] = _sigmoid(ga[p]).astype(BF16)
            h_ref[rs[p], :] = h[p]


def _inproj(x2, ln_g, ln_b, mod3, w_all, cos_t, sin_t, *, latent, rows_per_batch, tm, sub, ctx_mod_row,
            ssm_w, attn_w, kv2_w):
    t, d = x2.shape
    tiles_per_batch = rows_per_batch // tm
    nlb = ssm_w // LANES

    def brow(i):
        return i // tiles_per_batch if ctx_mod_row is None else ctx_mod_row

    in_specs = [
        pl.BlockSpec((tm, d), lambda i: (i, 0)),
        pl.BlockSpec((1, d), lambda i: (0, 0)),
        pl.BlockSpec((1, d), lambda i: (0, 0)),
        pl.BlockSpec((None, 1, d), lambda i: (brow(i), 0, 0)),
        pl.BlockSpec((None, 1, d), lambda i: (brow(i), 0, 1)),
        pl.BlockSpec(w_all.shape, lambda i: (0, 0)),
    ]
    args = [x2, ln_g, ln_b, mod3, mod3, w_all]
    out_specs = [pl.BlockSpec((nlb, tm, LANES), lambda i: (0, i, 0))]
    out_shape = [jax.ShapeDtypeStruct((nlb, t, LANES), F32)]
    if latent:
        in_specs += [
            pl.BlockSpec((tm, LANES), lambda i: (i % tiles_per_batch, 0)),
            pl.BlockSpec((tm, LANES), lambda i: (i % tiles_per_batch, 0)),
        ]
        args += [cos_t, sin_t]
        out_specs.append(pl.BlockSpec((tm, attn_w), lambda i: (i, 0)))
        out_shape.append(jax.ShapeDtypeStruct((t, attn_w), BF16))
    out_specs += [pl.BlockSpec((tm, kv2_w), lambda i: (i, 0))] * 2
    out_shape += [jax.ShapeDtypeStruct((t, kv2_w), BF16)] * 2
    if latent:
        out_specs += [pl.BlockSpec((tm, d), lambda i: (i, 0))] * 3
        out_shape += [jax.ShapeDtypeStruct((t, d), BF16)] * 2 + [jax.ShapeDtypeStruct((t, d), F32)]
    return pl.pallas_call(
        functools.partial(_inproj_kernel, latent=latent, ssm_w=ssm_w, attn_w=attn_w, kv2_w=kv2_w,
                          d_model=d, sub=sub),
        grid=(t // tm,),
        in_specs=in_specs,
        out_specs=out_specs,
        out_shape=out_shape,
        compiler_params=_cparams(("arbitrary",)),
        name="inproj_latent" if latent else "inproj_ctx",
    )(*args)


def _lane_expand(tbl, n_copies, rows_per_group, cols_per_group):
    r, w = tbl.shape
    wide_w = n_copies * w
    log2 = lambda v: int(v).bit_length() - 1
    assert all(v == 1 << log2(v) for v in (w, rows_per_group, cols_per_group))
    sel = ((lax.broadcasted_iota(jnp.int32, (w, wide_w), 1) & (w - 1))
           == lax.broadcasted_iota(jnp.int32, (w, wide_w), 0)).astype(BF16)
    wide = jnp.dot(tbl.astype(BF16), sel, preferred_element_type=F32)
    keep = ((lax.broadcasted_iota(jnp.int32, (r, wide_w), 0) >> log2(rows_per_group))
            == (lax.broadcasted_iota(jnp.int32, (r, wide_w), 1) >> log2(cols_per_group)))
    return jnp.where(keep, wide, 0.0)


def _s5_assemble_kernel(m_ref, ws_ref, wo_ref, kin_out, ws_out, wo_out, *, reverse):
    ch, _, hh = m_ref.shape
    p = ws_ref.shape[-1]
    gpb = LANES // hh
    sw = gpb * p
    bd = [_lane_expand(m_ref[j], gpb, hh, hh).astype(BF16) for j in range(ch)]
    zero = jnp.zeros((LANES, LANES), BF16)
    for k in range(ch):
        for i in range(ch):
            lag = (k - i) if reverse else (i - k)
            kin_out[k * LANES:(k + 1) * LANES, i * LANES:(i + 1) * LANES] = bd[lag] if lag >= 0 else zero
    for ri in range(2):
        for k in range(ch):
            ws_out[k * LANES:(k + 1) * LANES, ri * sw:(ri + 1) * sw] = (
                _lane_expand(ws_ref[ri, k], gpb, hh, p).astype(BF16))
        for i in range(ch):
            wo_out[ri * sw:(ri + 1) * sw, i * LANES:(i + 1) * LANES] = (
                _lane_expand(wo_ref[ri, i], gpb, p, hh).astype(BF16))


def _s5_assemble(m_tbl, ws_tbl, wo_tbl, *, reverse):
    ch, nlb, _, hh = m_tbl.shape
    p = ws_tbl.shape[-1]
    sw = (LANES // hh) * p
    width = ch * LANES
    return pl.pallas_call(
        functools.partial(_s5_assemble_kernel, reverse=reverse),
        grid=(nlb,),
        in_specs=[
            pl.BlockSpec((ch, None, LANES, hh), lambda l: (0, l, 0, 0)),
            pl.BlockSpec((2, ch, None, LANES, p), lambda l: (0, 0, l, 0, 0)),
            pl.BlockSpec((2, ch, None, sw, hh), lambda l: (0, 0, l, 0, 0)),
        ],
        out_specs=[
            pl.BlockSpec((None, width, width), lambda l: (l, 0, 0)),
            pl.BlockSpec((None, width, 2 * sw), lambda l: (l, 0, 0)),
            pl.BlockSpec((None, 2 * sw, width), lambda l: (l, 0, 0)),
        ],
        out_shape=[
            jax.ShapeDtypeStruct((nlb, width, width), BF16),
            jax.ShapeDtypeStruct((nlb, width, 2 * sw), BF16),
            jax.ShapeDtypeStruct((nlb, 2 * sw, width), BF16),
        ],
        compiler_params=_cparams(("arbitrary",)),
        name="s5_assemble_bwd" if reverse else "s5_assemble_fwd",
    )(m_tbl, ws_tbl, wo_tbl)


def _s5_weights(lam_re, lam_im, log_step, b_re, b_im, c_re, c_im, reverse):
    ch = S5_CHUNK
    g, p = lam_re.shape
    hh = b_re.shape[-1]
    gpb = LANES // hh
    nlb = g // gpb
    lr, li = lam_re.astype(F32), lam_im.astype(F32)
    dt = jnp.exp(log_step.astype(F32))[:, None]
    jj = jnp.arange(ch + 1, dtype=F32)[:, None, None]
    mag = jnp.exp(jj * lr * dt)
    pr = mag * jnp.cos(jj * li * dt)
    pi = mag * jnp.sin(jj * li * dt)
    ar, ai = pr[1], pi[1]
    den = lr * lr + li * li
    cr = ((ar - 1) * lr + ai * li) / den
    ci = (ai * lr - (ar - 1) * li) / den
    br, bi = b_re.astype(F32), b_im.astype(F32)
    bbr = cr[..., None] * br - ci[..., None] * bi
    bbi = cr[..., None] * bi + ci[..., None] * br
    ccr, cci = c_re.astype(F32), c_im.astype(F32)
    bbr_t = jnp.swapaxes(bbr, 1, 2)
    bbi_t = jnp.swapaxes(bbi, 1, 2)
    ccr_t = jnp.swapaxes(ccr, 1, 2)
    cci_t = jnp.swapaxes(cci, 1, 2)

    er = ccr[None] * pr[:ch, :, None, :] - cci[None] * pi[:ch, :, None, :]
    ei = ccr[None] * pi[:ch, :, None, :] + cci[None] * pr[:ch, :, None, :]
    m = jnp.sum(er[:, :, None, :, :] * bbr_t[None, :, :, None, :]
                - ei[:, :, None, :, :] * bbi_t[None, :, :, None, :], axis=-1)
    m_tbl = m.reshape(ch, nlb, LANES, hh)
    rk = np.arange(ch) if reverse else (ch - 1 - np.arange(ch))
    apr, api = pr[rk][:, :, None, :], pi[rk][:, :, None, :]
    sr = apr * bbr_t[None] - api * bbi_t[None]
    si = apr * bbi_t[None] + api * bbr_t[None]
    ws_tbl = jnp.stack([sr, si], axis=0).reshape(2, ch, nlb, LANES, p)
    ex = (ch - np.arange(ch)) if reverse else (np.arange(ch) + 1)
    epr, epi = pr[ex][:, :, :, None], pi[ex][:, :, :, None]
    wo_r = ccr_t[None] * epr - cci_t[None] * epi
    wo_i = -(ccr_t[None] * epi + cci_t[None] * epr)
    wo_tbl = jnp.stack([wo_r, wo_i], axis=0).reshape(2, ch, nlb, gpb * p, hh)
    kin, ws, wo = _s5_assemble(m_tbl, ws_tbl, wo_tbl, reverse=reverse)
    a_chunk = jnp.stack([pr[ch].reshape(nlb, gpb * p), pi[ch].reshape(nlb, gpb * p)], axis=1)
    return kin, ws, wo, a_chunk


def _s5_kernel(u_ref, s0_ref, a_ref, kin_ref, ws_ref, wo_ref, d_ref, y_ref, sfin_ref,
               z_scr, sin_scr, carry_scr, *, reverse, nb, cc, add_skip):
    j = pl.program_id(1)

    @pl.when(j == 0)
    def _():
        carry_scr[...] = s0_ref[...]

    ch = S5_CHUNK
    width = ch * LANES
    sw = a_ref.shape[-1]
    ns = sw // LANES
    u = jnp.concatenate(
        [jnp.concatenate([u_ref[b, pl.ds(k, cc, stride=ch), :] for b in range(nb)], axis=0)
         for k in range(ch)], axis=1)
    ub = u.astype(BF16)
    z = _dot(ub, ws_ref[...])
    ccp = cc + S5_ROW_PAD
    for s in range(2 * ns):
        for b in range(nb):
            z_scr[s, b * ccp:b * ccp + cc, :] = z[b * cc:(b + 1) * cc, s * LANES:(s + 1) * LANES]
    ar = [a_ref[0:1, s * LANES:(s + 1) * LANES] for s in range(ns)]
    ai = [a_ref[1:2, s * LANES:(s + 1) * LANES] for s in range(ns)]

    def body(c, st):
        cidx = (cc - 1 - c) if reverse else c
        rows = pl.ds(cidx, nb, stride=ccp)
        new = [None] * (2 * ns)
        for s in range(ns):
            sr, si = st[s], st[ns + s]
            sin_scr[s, rows, :] = sr
            sin_scr[ns + s, rows, :] = si
            new[s] = ar[s] * sr - ai[s] * si + z_scr[s, rows, :]
            new[ns + s] = ar[s] * si + ai[s] * sr + z_scr[ns + s, rows, :]
        return tuple(new)

    st0 = tuple(carry_scr[:, s * LANES:(s + 1) * LANES] for s in range(2 * ns))
    st = lax.fori_loop(0, cc, body, st0)
    s_fin = jnp.concatenate(st, axis=1)
    carry_scr[...] = s_fin
    sfin_ref[...] = s_fin
    s_in = jnp.concatenate(
        [jnp.concatenate([sin_scr[s, b * ccp:b * ccp + cc, :] for b in range(nb)], axis=0)
         for s in range(2 * ns)], axis=1)
    intra = []
    for c0 in range(0, width, MXU_DIM):
        ks = slice(c0, width) if reverse else slice(0, c0 + MXU_DIM)
        intra.append(_dot(ub[:, ks], kin_ref[ks, c0:c0 + MXU_DIM]))
    y = jnp.concatenate(intra, axis=1) + _dot(s_in.astype(BF16), wo_ref[...])
    if add_skip:
        y = y + u * d_ref[...]
    for b in range(nb):
        for i in range(ch):
            y_ref[b, pl.ds(i, cc, stride=ch), :] = y[b * cc:(b + 1) * cc, i * LANES:(i + 1) * LANES]


def _s5_scan(u4, s0, weights, d_tile, *, reverse, cc, add_skip):
    kin, ws, wo, a_chunk = weights
    nlb, nb, n_steps, _ = u4.shape
    width = S5_CHUNK * LANES
    sw2 = ws.shape[-1]
    nj = n_steps // (cc * S5_CHUNK)

    def jm(j):
        return (nj - 1 - j) if reverse else j

    return pl.pallas_call(
        functools.partial(_s5_kernel, reverse=reverse, nb=nb, cc=cc, add_skip=add_skip),
        grid=(nlb, nj),
        in_specs=[
            pl.BlockSpec((None, nb, cc * S5_CHUNK, LANES), lambda l, j: (l, 0, jm(j), 0)),
            pl.BlockSpec((None, nb, sw2), lambda l, j: (l, 0, 0)),
            pl.BlockSpec((None, 2, sw2 // 2), lambda l, j: (l, 0, 0)),
            pl.BlockSpec((None, width, width), lambda l, j: (l, 0, 0)),
            pl.BlockSpec((None, width, sw2), lambda l, j: (l, 0, 0)),
            pl.BlockSpec((None, sw2, width), lambda l, j: (l, 0, 0)),
            pl.BlockSpec((None, 1, width), lambda l, j: (l, 0, 0)),
        ],
        out_specs=[
            pl.BlockSpec((None, nb, cc * S5_CHUNK, LANES), lambda l, j: (l, 0, jm(j), 0)),
            pl.BlockSpec((None, nb, sw2), lambda l, j: (l, 0, 0)),
        ],
        out_shape=[
            jax.ShapeDtypeStruct(u4.shape, F32),
            jax.ShapeDtypeStruct((nlb, nb, sw2), F32),
        ],
        scratch_shapes=[
            pltpu.VMEM((sw2 // LANES, nb * (cc + S5_ROW_PAD), LANES), F32),
            pltpu.VMEM((sw2 // LANES, nb * (cc + S5_ROW_PAD), LANES), F32),
            pltpu.VMEM((nb, sw2), F32),
        ],
        compiler_params=_cparams(("arbitrary", "arbitrary")),
        name="s5_bwd" if reverse else "s5_fwd",
    )(u4, s0, a_chunk, kin, ws, wo, d_tile)


def _attn_kernel(sink_ref, q_ref, kp_ref, ko_ref, kn_ref, vp_ref, vo_ref, vn_ref, kc_ref, vc_ref,
                 o_ref, *, nblk, blk, qblocks):
    n = pl.program_id(1)
    n_ctx = kc_ref.shape[0]
    n_loc = 3 * blk
    nrow = Q_PER_KV * blk
    row = lax.broadcasted_iota(jnp.int32, (nrow, n_loc), 0)
    qi = row & (blk - 1)
    kj = lax.broadcasted_iota(jnp.int32, (nrow, n_loc), 1)
    rel = kj - qi
    in_window = (rel >= blk - WINDOW) & (rel <= blk + WINDOW)
    lane = lax.broadcasted_iota(jnp.int32, (blk, LANES), 1)
    low_half = lane < HEAD_DIM
    row1 = lax.broadcasted_iota(jnp.int32, (nrow, 1), 0)

    key1 = lax.broadcasted_iota(jnp.int32, (1, n_loc), 1)

    def edge_bias(j):
        gblk = n * qblocks + j
        lo = jnp.where(gblk == 0, blk, 0)
        hi = jnp.where(gblk == nblk - 1, 2 * blk, n_loc)
        return jnp.where((key1 >= lo) & (key1 < hi), 0.0, NEG_INF)

    bias = [edge_bias(j) for j in range(qblocks)]
    parts = [(j, g) for j in range(qblocks) for g in range(N_KV_HEADS)]
    each = lambda f: [f(j, g) for j, g in parts]
    at = {pg: i for i, pg in enumerate(parts)}

    def key_blocks(p_ref, own_ref, n_ref, c_ref, j, g):
        cs = slice(g * LANES, (g + 1) * LANES)
        blocks = [p_ref[:, cs]] + [own_ref[i * blk:(i + 1) * blk, cs] for i in range(qblocks)] + [n_ref[:, cs]]
        return jnp.concatenate(blocks[j:j + 3] + [c_ref[:, cs]], axis=0)

    kcat = each(lambda j, g: key_blocks(kp_ref, ko_ref, kn_ref, kc_ref, j, g))
    vcat = each(lambda j, g: key_blocks(vp_ref, vo_ref, vn_ref, vc_ref, j, g))

    def stacked_q(j, g):
        qms = []
        for hq in range(Q_PER_KV):
            h = g * Q_PER_KV + hq
            qb = q_ref[j * blk:(j + 1) * blk, (h // 2) * LANES:(h // 2 + 1) * LANES]
            qms.append(jnp.where(low_half if h % 2 == 0 else jnp.logical_not(low_half), qb, jnp.zeros_like(qb)))
        return jnp.concatenate(qms, axis=0)

    def stacked_sink(j, g):
        snk = jnp.zeros((nrow, 1), F32)
        for hq in range(Q_PER_KV):
            snk = jnp.where((row1 >= hq * blk) & (row1 < (hq + 1) * blk), sink_ref[g * Q_PER_KV + hq], snk)
        return snk

    qs = each(stacked_q)
    snk = each(stacked_sink)
    s = each(lambda j, g: lax.dot_general(qs[at[j, g]], kcat[at[j, g]], (((1,), (1,)), ((), ())),
                                          preferred_element_type=F32))
    s = each(lambda j, g: jnp.concatenate(
        [jnp.where(in_window, s[at[j, g]][:, :n_loc], NEG_INF) + bias[j], s[at[j, g]][:, n_loc:]], axis=1))
    mx = each(lambda j, g: jnp.maximum(jnp.max(s[at[j, g]], axis=1, keepdims=True), snk[at[j, g]]))
    p = each(lambda j, g: jnp.exp(s[at[j, g]] - mx[at[j, g]]))
    den = each(lambda j, g: jnp.sum(p[at[j, g]], axis=1, keepdims=True) + jnp.exp(snk[at[j, g]] - mx[at[j, g]]))
    o = each(lambda j, g: _dot(p[at[j, g]].astype(BF16), vcat[at[j, g]]) / den[at[j, g]])
    for j in range(qblocks):
        outs = [o[at[j, g]][hq * blk:(hq + 1) * blk] for g in range(N_KV_HEADS) for hq in range(Q_PER_KV)]
        blocks = [jnp.where(low_half, outs[2 * m], outs[2 * m + 1]) for m in range(N_Q_HEADS // 2)]
        o_ref[j * blk:(j + 1) * blk, :] = jnp.concatenate(blocks, axis=1).astype(BF16)


def _attention(q, kd, vd, kcd, vcd, sink, *, bsz, seq_len, n_ctx, qblocks):
    blk = WINDOW
    nblk = seq_len // blk
    nstep = nblk // qblocks
    aw = q.shape[1]
    kw = kd.shape[1]

    def qmap(b, n):
        return (b * nstep + n, 0)

    def pmap(b, n):
        return (b * nblk + jnp.maximum(n * qblocks - 1, 0), 0)

    def nmap(b, n):
        return (b * nblk + jnp.minimum((n + 1) * qblocks, nblk - 1), 0)

    edge = lambda f: pl.BlockSpec((blk, kw), f)
    own = pl.BlockSpec((qblocks * blk, kw), qmap)
    return pl.pallas_call(
        functools.partial(_attn_kernel, nblk=nblk, blk=blk, qblocks=qblocks),
        grid=(bsz, nstep),
        in_specs=[
            pl.BlockSpec(memory_space=pltpu.SMEM),
            pl.BlockSpec((qblocks * blk, aw), qmap),
            edge(pmap), own, edge(nmap),
            edge(pmap), own, edge(nmap),
            pl.BlockSpec((n_ctx, kw), lambda b, n: (b, 0)),
            pl.BlockSpec((n_ctx, kw), lambda b, n: (b, 0)),
        ],
        out_specs=pl.BlockSpec((qblocks * blk, aw), qmap),
        out_shape=jax.ShapeDtypeStruct(q.shape, BF16),
        compiler_params=_cparams(("arbitrary", "arbitrary")),
        name="attn",
    )(sink, q, kd, kd, kd, vd, vd, vd, kcd, vcd)


def _gelu_tanh(x):
    return 0.5 * x * (1.0 + jnp.tanh(math.sqrt(2.0 / math.pi) * (x + 0.044715 * (x * x * x))))


def _flatten_slots(v, max_value):
    tm, nk = v.shape
    per_row = LANES // nk
    log2 = lambda x: int(x).bit_length() - 1
    assert nk == 1 << log2(nk) and tm % per_row == 0
    spread = ((lax.broadcasted_iota(jnp.int32, (nk, LANES), 1) & (nk - 1))
              == lax.broadcasted_iota(jnp.int32, (nk, LANES), 0)).astype(BF16)
    t_id = lax.broadcasted_iota(jnp.int32, (tm, LANES), 0)
    c_id = lax.broadcasted_iota(jnp.int32, (tm, LANES), 1)
    own = (c_id >> log2(nk)) == (t_id & (per_row - 1))
    group = ((lax.broadcasted_iota(jnp.int32, (tm // per_row, tm), 1) >> log2(per_row))
             == lax.broadcasted_iota(jnp.int32, (tm // per_row, tm), 0)).astype(BF16)
    digit_bits = 7
    out = jnp.zeros((tm // per_row, LANES), jnp.int32)
    for shift in range(0, max(int(max_value).bit_length(), 1), digit_bits):
        digit = ((v >> shift) & ((1 << digit_bits) - 1)).astype(F32).astype(BF16)
        wide = jnp.dot(digit, spread, preferred_element_type=F32)
        wide = jnp.where(own, wide, 0.0).astype(BF16)
        out = out + (jnp.dot(group, wide, preferred_element_type=F32).astype(jnp.int32) << shift)
    return out


def _merge_kernel(yf_ref, yb_ref, o_ref, gs_ref, ga_ref, h_ref, g1_ref, sh2_ref, sc2_ref,
                  wglu_ref, bglu_ref, wso_ref, wao_ref, wo_ref, l1g_ref, l1b_ref, wr_ref, br_ref,
                  h1_ref, idx_ref, gate_ref, rank_ref, cnt_ref, cnt_scr, *, alpha, n_exp, sub, n_tokens):
    i = pl.program_id(0)

    @pl.when(i == 0)
    def _():
        cnt_scr[...] = jnp.zeros_like(cnt_scr)

    nlb = yf_ref.shape[0]
    tm = h_ref.shape[0]
    lane = lax.broadcasted_iota(jnp.int32, (sub, n_exp), 1)
    lane_k = lax.broadcasted_iota(jnp.int32, (sub, TOP_K), 1)
    ri = lax.broadcasted_iota(jnp.int32, (sub, sub), 0)
    ci = lax.broadcasted_iota(jnp.int32, (sub, sub), 1)
    tri = jnp.where(ci < ri, 1.0, 0.0).astype(BF16)
    slots = sub * TOP_K // LANES
    cnt = cnt_scr[...]
    parts = range(tm // sub)
    rs = [slice(p * sub, (p + 1) * sub) for p in parts]
    each = lambda f: [f(p) for p in parts]
    y = each(lambda p: jnp.concatenate([yf_ref[lb, rs[p], :] + yb_ref[lb, rs[p], :] for lb in range(nlb)], axis=1))
    z = each(lambda p: _gelu_tanh(y[p]))
    zg = each(lambda p: _dot(z[p].astype(BF16), wglu_ref[...]) + bglu_ref[...])
    z = each(lambda p: z[p] * _sigmoid(zg[p]))
    ms = each(lambda p: _dot(z[p].astype(BF16), wso_ref[...]))
    ma = each(lambda p: _dot(o_ref[rs[p], :], wao_ref[...]))
    m = each(lambda p: gs_ref[rs[p], :].astype(F32) * ms[p] + ga_ref[rs[p], :].astype(F32) * ma[p])
    mix = each(lambda p: _dot(m[p].astype(BF16), wo_ref[...]))
    h1 = each(lambda p: _layer_norm(alpha * h_ref[rs[p], :] + g1_ref[...] * mix[p], l1g_ref[...], l1b_ref[...]))
    for p in parts:
        h1_ref[rs[p], :] = h1[p]
    xm = each(lambda p: h1[p] * (1.0 + sc2_ref[...]) + sh2_ref[...])
    work = each(lambda p: _dot_3pass(xm[p], wr_ref[...]) + br_ref[...])
    vals, sels = [], []
    for _ in range(TOP_K):
        mx = each(lambda p: jnp.max(work[p], axis=1, keepdims=True))
        sel = each(lambda p: jnp.min(jnp.where(work[p] == mx[p], lane, n_exp), axis=1, keepdims=True))
        work = each(lambda p: jnp.where(lane == sel[p], -jnp.inf, work[p]))
        vals.append(mx)
        sels.append(sel)
    exps = [each(lambda p: jnp.exp(vals[k][p] - vals[0][p])) for k in range(TOP_K)]
    den = each(lambda p: exps[0][p] + exps[1][p] + exps[2][p] + exps[3][p])
    onehot = each(lambda p: sum((lane == sels[k][p]).astype(F32) for k in range(TOP_K)))
    prefix = each(lambda p: _dot(tri, onehot[p].astype(BF16)))
    for p in parts:
        rank = prefix[p] + cnt
        idx_o = jnp.zeros((sub, TOP_K), jnp.int32)
        gate_o = jnp.zeros((sub, TOP_K), F32)
        rank_o = jnp.zeros((sub, TOP_K), jnp.int32)
        for k in range(TOP_K):
            rk = jnp.sum(jnp.where(lane == sels[k][p], rank, 0.0), axis=1, keepdims=True).astype(jnp.int32)
            idx_o = jnp.where(lane_k == k, sels[k][p], idx_o)
            gate_o = jnp.where(lane_k == k, exps[k][p] / den[p], gate_o)
            rank_o = jnp.where(lane_k == k, rk, rank_o)
        gate_ref[rs[p], :] = gate_o
        idx_ref[p * slots:(p + 1) * slots, :] = _flatten_slots(idx_o, n_exp - 1)
        rank_ref[p * slots:(p + 1) * slots, :] = _flatten_slots(rank_o, n_tokens - 1)
        cnt = cnt + jnp.sum(onehot[p], axis=0, keepdims=True)
    cnt_scr[...] = cnt
    cnt_ref[...] = cnt


def _merge(yf, yb, o_att, sgs, sga, h_in, mod3, w_glu, b_glu, w_so, w_ao, w_o, l1g, l1b,
           w_r, b_r, *, rows_per_batch, tm, sub, alpha):
    t, d = h_in.shape
    nlb = yf.shape[0]
    sw = nlb * LANES
    n_exp = w_r.shape[1]
    tpb = rows_per_batch // tm
    row = lambda i: (i, 0)
    const = lambda i: (0, 0)

    def modspec(chunk):
        return pl.BlockSpec((None, 1, d), lambda i: (i // tpb, 0, chunk))

    return pl.pallas_call(
        functools.partial(_merge_kernel, alpha=alpha, n_exp=n_exp, sub=sub, n_tokens=t),
        grid=(t // tm,),
        in_specs=[
            pl.BlockSpec((nlb, tm, LANES), lambda i: (0, i, 0)),
            pl.BlockSpec((nlb, tm, LANES), lambda i: (0, i, 0)),
            pl.BlockSpec((tm, o_att.shape[1]), row),
            pl.BlockSpec((tm, d), row),
            pl.BlockSpec((tm, d), row),
            pl.BlockSpec((tm, d), row),
            modspec(2), modspec(3), modspec(4),
            pl.BlockSpec(w_glu.shape, const),
            pl.BlockSpec((1, sw), const),
            pl.BlockSpec(w_so.shape, const),
            pl.BlockSpec(w_ao.shape, const),
            pl.BlockSpec(w_o.shape, const),
            pl.BlockSpec((1, d), const),
            pl.BlockSpec((1, d), const),
            pl.BlockSpec(w_r.shape, const),
            pl.BlockSpec((1, n_exp), const),
        ],
        out_specs=[
            pl.BlockSpec((tm, d), row),
            pl.BlockSpec((tm * TOP_K // LANES, LANES), row),
            pl.BlockSpec((tm, TOP_K), row),
            pl.BlockSpec((tm * TOP_K // LANES, LANES), row),
            pl.BlockSpec((1, n_exp), const),
        ],
        out_shape=[
            jax.ShapeDtypeStruct((t, d), F32),
            jax.ShapeDtypeStruct((t * TOP_K // LANES, LANES), jnp.int32),
            jax.ShapeDtypeStruct((t, TOP_K), F32),
            jax.ShapeDtypeStruct((t * TOP_K // LANES, LANES), jnp.int32),
            jax.ShapeDtypeStruct((1, n_exp), F32),
        ],
        scratch_shapes=[pltpu.VMEM((1, n_exp), F32)],
        compiler_params=_cparams(("arbitrary",)),
        name="merge",
    )(yf, yb, o_att, sgs, sga, h_in, mod3, mod3, mod3, w_glu, b_glu, w_so, w_ao, w_o,
      l1g, l1b, w_r, b_r)


def _row_tile(ref, r):
    return ref.at[pl.ds(pl.multiple_of(r * ROW_TILES, ROW_TILES), ROW_TILES)]


def _row_tile_at(ref, first_line):
    return ref.at[pl.ds(pl.multiple_of(first_line, ROW_TILES), ROW_TILES)]


def _store_row_tiles(ref, val):
    rows = val.shape[0]
    for l in range(ROW_TILES):
        ref[pl.ds(l, rows, stride=ROW_TILES), :] = val[:, l * LANES:(l + 1) * LANES]


def _load_row_tiles(ref, row0, rows):
    return jnp.concatenate(
        [ref[pl.ds(row0 * ROW_TILES + l, rows, stride=ROW_TILES), :] for l in range(ROW_TILES)], axis=1)


def _row_copy_out(src, dst, pos_ref, sem, t, k):
    return pltpu.make_async_copy(_row_tile(src, t), _row_tile_at(dst, pos_ref[t * TOP_K + k]), sem)


def _dispatch_kernel(pos_ref, prev_pos_ref, h1_ref, sh2_ref, sc2_ref, xrows_ref, xm_scr, sems):
    i = pl.program_id(0)
    n = pl.num_programs(0)
    tm = h1_ref.shape[0]
    slot = i % 2
    xm = h1_ref[...] * (1.0 + sc2_ref[...]) + sh2_ref[...]
    _store_row_tiles(xm_scr.at[slot], xm)

    def copies(slot_, idx_ref, op):
        def body(t, carry):
            for k in range(TOP_K):
                op(_row_copy_out(xm_scr.at[slot_], xrows_ref, idx_ref, sems.at[slot_], t, k), k)
            return carry
        lax.fori_loop(0, tm, body, 0, unroll=DMA_LOOP_UNROLL)

    start = lambda cp, k: cp.start(priority=k % N_DMA_PRIORITIES)
    wait = lambda cp, k: cp.wait()
    for s in range(2):
        @pl.when(slot == s)
        def _():
            copies(s, pos_ref, start)

    for s in range(2):
        @pl.when((slot == 1 - s) & (i > 0))
        def _():
            copies(s, prev_pos_ref, wait)

        @pl.when((slot == s) & (i == n - 1))
        def _():
            copies(s, pos_ref, wait)


def _dispatch(pos_flat, h1, mod3, *, rows_per_batch, tm):
    t, d = h1.shape
    tpb = rows_per_batch // tm
    return pl.pallas_call(
        _dispatch_kernel,
        grid=(t // tm,),
        in_specs=[
            pl.BlockSpec((tm * TOP_K,), lambda i: (i,), memory_space=pltpu.SMEM),
            pl.BlockSpec((tm * TOP_K,), lambda i: (jnp.maximum(i - 1, 0),), memory_space=pltpu.SMEM),
            pl.BlockSpec((tm, d), lambda i: (i, 0)),
            pl.BlockSpec((None, 1, d), lambda i: (i // tpb, 0, 3)),
            pl.BlockSpec((None, 1, d), lambda i: (i // tpb, 0, 4)),
        ],
        out_specs=pl.BlockSpec(memory_space=pl.ANY),
        out_shape=jax.ShapeDtypeStruct((t * TOP_K * ROW_TILES, LANES), F32),
        scratch_shapes=[pltpu.VMEM((2, tm * ROW_TILES, LANES), F32), pltpu.SemaphoreType.DMA((2,))],
        compiler_params=_cparams(("arbitrary",)),
        name="dispatch",
    )(pos_flat, pos_flat, h1, mod3, mod3)


def _experts_kernel(tile_ref, exp_ref, lo_ref, hi_ref, nxt_ref, x_ref, wgu_hbm, bgu_ref, wd_hbm, bd_ref, y_ref,
                    wgu_f32, wd_f32, wgu_scr, wd_scr, sems, *, rows, sub_rows):
    w = pl.program_id(0)
    prev = jnp.maximum(w - 1, 0)
    e_new = (w == 0) | (exp_ref[w] != exp_ref[prev])
    t_new = (w == 0) | (tile_ref[w] != tile_ref[prev])
    lo = lo_ref[w]
    hi = hi_ref[w]
    f = wd_scr.shape[0]

    def weight_copies(e):
        return (pltpu.make_async_copy(wgu_hbm.at[e], wgu_f32, sems.at[0]),
                pltpu.make_async_copy(wd_hbm.at[e], wd_f32, sems.at[1]))

    @pl.when(w == 0)
    def _():
        for cp in weight_copies(exp_ref[0]):
            cp.start()

    @pl.when(e_new)
    def _():
        for cp in weight_copies(exp_ref[w]):
            cp.wait()
        wgu_scr[...] = wgu_f32[...].astype(BF16)
        wd_scr[...] = wd_f32[...].astype(BF16)

        @pl.when(nxt_ref[w] >= 0)
        def _():
            for cp in weight_copies(nxt_ref[w]):
                cp.start()

    row0 = tile_ref[w] * rows
    whole = (lo <= row0) & (hi >= row0 + rows)

    @pl.when(t_new & jnp.logical_not(whole))
    def _():
        y_ref[...] = jnp.zeros_like(y_ref)

    def expert_mlp(x):
        gu = _dot(x.astype(BF16), wgu_scr[...]) + bgu_ref[...]
        glu = jnp.minimum(gu[:, :f], SWIGLU_LIMIT)
        lin = jnp.clip(gu[:, f:], -SWIGLU_LIMIT, SWIGLU_LIMIT)
        act = glu * _sigmoid(SWIGLU_ALPHA * glu) * (lin + 1.0)
        return _dot(act.astype(BF16), wd_scr[...]) + bd_ref[...]

    def sub_block(sb):
        s_lo = row0 + sb * sub_rows
        s_hi = s_lo + sub_rows
        y_sub = y_ref.at[pl.ds(sb * sub_rows * ROW_TILES, sub_rows * ROW_TILES)]

        @pl.when((hi > s_lo) & (lo < s_hi))
        def _():
            y = expert_mlp(_load_row_tiles(x_ref, sb * sub_rows, sub_rows))
            sub_whole = (lo <= s_lo) & (hi >= s_hi)

            @pl.when(sub_whole)
            def _():
                _store_row_tiles(y_sub, y)

            @pl.when(jnp.logical_not(sub_whole))
            def _():
                r = s_lo + lax.broadcasted_iota(jnp.int32, (sub_rows, 1), 0)
                _store_row_tiles(y_sub, jnp.where((r >= lo) & (r < hi), y, _load_row_tiles(y_sub, 0, sub_rows)))

    @pl.when(whole)
    def _():
        _store_row_tiles(y_ref, expert_mlp(_load_row_tiles(x_ref, 0, rows)))

    @pl.when(jnp.logical_not(whole))
    def _():
        for sb in range(rows // sub_rows):
            sub_block(sb)


def _experts(work, xrows, w_gu, b_gu, w_d, b_d, *, rows, sub_rows):
    tile_id, exp_id, lo, hi, nxt = work
    n_exp, d, f2 = w_gu.shape
    f = w_d.shape[1]
    grid_spec = pltpu.PrefetchScalarGridSpec(
        num_scalar_prefetch=5,
        grid=(tile_id.shape[0],),
        in_specs=[
            pl.BlockSpec((rows * ROW_TILES, LANES), lambda w, ti, ex, lo, hi, nx: (ti[w], 0)),
            pl.BlockSpec(memory_space=pl.ANY),
            pl.BlockSpec((None, 1, f2), lambda w, ti, ex, lo, hi, nx: (ex[w], 0, 0)),
            pl.BlockSpec(memory_space=pl.ANY),
            pl.BlockSpec((None, 1, d), lambda w, ti, ex, lo, hi, nx: (ex[w], 0, 0)),
        ],
        out_specs=pl.BlockSpec((rows * ROW_TILES, LANES), lambda w, ti, ex, lo, hi, nx: (ti[w], 0)),
        scratch_shapes=[
            pltpu.VMEM((d, f2), F32), pltpu.VMEM((f, d), F32),
            pltpu.VMEM((d, f2), BF16), pltpu.VMEM((f, d), BF16),
            pltpu.SemaphoreType.DMA((2,)),
        ],
    )
    return pl.pallas_call(
        functools.partial(_experts_kernel, rows=rows, sub_rows=sub_rows),
        grid_spec=grid_spec,
        out_shape=jax.ShapeDtypeStruct(xrows.shape, F32),
        compiler_params=_cparams(("arbitrary",)),
        name="experts",
    )(tile_id, exp_id, lo, hi, nxt, xrows, w_gu, b_gu.reshape(n_exp, 1, f2), w_d, b_d.reshape(n_exp, 1, d))


def _row_copy_in(src, dst, pos_ref, sem, t, k, tm):
    return pltpu.make_async_copy(_row_tile_at(src, pos_ref[t * TOP_K + k]), _row_tile(dst, k * tm + t), sem)


def _combine_kernel(pos_ref, next_pos_ref, h1_ref, gate_ref, g2_ref, lg_ref, lb_ref, yrows_ref, o_ref,
                    buf, sems, *, alpha):
    i = pl.program_id(0)
    n = pl.num_programs(0)
    tm = h1_ref.shape[0]
    slot = i % 2

    def copies(slot_, idx_ref, op):
        def body(t, carry):
            for k in range(TOP_K):
                op(_row_copy_in(yrows_ref, buf.at[slot_], idx_ref, sems.at[slot_], t, k, tm), k)
            return carry
        lax.fori_loop(0, tm, body, 0, unroll=DMA_LOOP_UNROLL)

    start = lambda cp, k: cp.start(priority=k % N_DMA_PRIORITIES)
    wait = lambda cp, k: cp.wait()

    @pl.when(i == 0)
    def _():
        copies(0, pos_ref, start)

    for s in range(2):
        @pl.when((slot == 1 - s) & (i + 1 < n))
        def _():
            copies(s, next_pos_ref, start)

    def reduce(s):
        copies(s, pos_ref, wait)
        gates = gate_ref[...]
        ffn = gates[:, 0:1] * _load_row_tiles(buf.at[s], 0, tm)
        for k in range(1, TOP_K):
            ffn = ffn + gates[:, k:k + 1] * _load_row_tiles(buf.at[s], k * tm, tm)
        o_ref[...] = _layer_norm(alpha * h1_ref[...] + g2_ref[...] * ffn, lg_ref[...], lb_ref[...])

    for s in range(2):
        @pl.when(slot == s)
        def _():
            reduce(s)


def _combine(pos_flat, h1, gates, mod3, l2g, l2b, yrows, *, rows_per_batch, tm, alpha):
    t, d = h1.shape
    tpb = rows_per_batch // tm
    n_tiles = t // tm
    return pl.pallas_call(
        functools.partial(_combine_kernel, alpha=alpha),
        grid=(n_tiles,),
        in_specs=[
            pl.BlockSpec((tm * TOP_K,), lambda i: (i,), memory_space=pltpu.SMEM),
            pl.BlockSpec((tm * TOP_K,), lambda i: (jnp.minimum(i + 1, n_tiles - 1),), memory_space=pltpu.SMEM),
            pl.BlockSpec((tm, d), lambda i: (i, 0)),
            pl.BlockSpec((tm, TOP_K), lambda i: (i, 0)),
            pl.BlockSpec((None, 1, d), lambda i: (i // tpb, 0, 5)),
            pl.BlockSpec((1, d), lambda i: (0, 0)),
            pl.BlockSpec((1, d), lambda i: (0, 0)),
            pl.BlockSpec(memory_space=pl.ANY),
        ],
        out_specs=pl.BlockSpec((tm, d), lambda i: (i, 0)),
        out_shape=jax.ShapeDtypeStruct((t, d), F32),
        scratch_shapes=[pltpu.VMEM((2, TOP_K * tm * ROW_TILES, LANES), F32), pltpu.SemaphoreType.DMA((2,))],
        compiler_params=_cparams(("arbitrary",)),
        name="combine",
    )(pos_flat, pos_flat, h1, gates, mod3, l2g, l2b, yrows)


def _plan_kernel(cnt_ref, idx_ref, rank_ref, pos_ref, tile_out, exp_out, lo_out, hi_out, nxt_out,
                 start_s, end_s, next_s, *, rows, n_tiles):
    n_exp = cnt_ref.shape[0]
    n_work = tile_out.shape[0]

    def cumulate(e, acc):
        start_s[e] = acc
        end_s[e] = acc + cnt_ref[e]
        return acc + cnt_ref[e]

    lax.fori_loop(0, n_exp, cumulate, 0)

    def next_nonempty(i, cur):
        e = n_exp - 1 - i
        next_s[e] = cur
        return jnp.where(cnt_ref[e] > 0, e, cur)

    first = lax.fori_loop(0, n_exp, next_nonempty, -1)

    def item(w, carry):
        tile, e = carry
        live = tile < n_tiles
        tl = jnp.minimum(tile, n_tiles - 1)
        tile_lo = tl * rows
        tile_hi = tile_lo + rows
        lo = jnp.maximum(start_s[e], tile_lo)
        hi = jnp.where(live, jnp.minimum(end_s[e], tile_hi), lo)
        tile_out[w] = tl
        exp_out[w] = e
        lo_out[w] = lo
        hi_out[w] = hi
        nxt_out[w] = next_s[e]
        tile_done = live & (end_s[e] >= tile_hi)
        expert_done = live & (end_s[e] <= tile_hi) & (next_s[e] >= 0)
        return jnp.where(tile_done, tile + 1, tile), jnp.where(expert_done, next_s[e], e)

    lax.fori_loop(0, n_work, item, (0, first))

    idx = idx_ref[...]
    base = jnp.zeros(idx.shape, jnp.int32)
    for e in range(n_exp):
        base = jnp.where(idx == e, start_s[e], base)
    pos_ref[...] = (base + rank_ref[...]) * ROW_TILES


def _plan(counts, idx_flat, rank_flat, *, rows):
    n_exp = counts.shape[0]
    n_rows = idx_flat.shape[0] * idx_flat.shape[1]
    n_tiles = n_rows // rows
    n_work = n_tiles + n_exp - 1
    smem = pl.BlockSpec(memory_space=pltpu.SMEM)
    vmem = pl.BlockSpec(memory_space=pltpu.VMEM)
    return pl.pallas_call(
        functools.partial(_plan_kernel, rows=rows, n_tiles=n_tiles),
        in_specs=[smem, vmem, vmem],
        out_specs=[vmem] + [smem] * 5,
        out_shape=[jax.ShapeDtypeStruct(idx_flat.shape, jnp.int32)]
        + [jax.ShapeDtypeStruct((n_work,), jnp.int32)] * 5,
        scratch_shapes=[pltpu.SMEM((n_exp,), jnp.int32)] * 3,
        name="plan",
    )(counts, idx_flat, rank_flat)


def kernel(x, c, ctx, c_ctx, ln_in_g, ln_in_b, w_mod, b_mod, w_in, ssm_lam_re, ssm_lam_im, ssm_log_step, ssm_b_re, ssm_b_im, ssm_c_re, ssm_c_im, ssm_d, w_glu, b_glu, attn_sink, w_ssm_out, w_att_out, w_o, ln1_g, ln1_b, w_router, b_router, w_gate_up, b_gate_up, w_down, b_down, ln2_g, ln2_b):
    bsz, seq_len, d = x.shape
    n_ctx = ctx.shape[1]
    depth = w_mod.shape[0]
    assert depth == 1, "single-layer kernel"
    g_ssm, h_ssm = ssm_d.shape[1:]
    ssm_w = g_ssm * h_ssm
    attn_w = N_Q_HEADS * HEAD_DIM
    kv_w = N_KV_HEADS * HEAD_DIM
    kv2_w = 2 * kv_w
    nlb = ssm_w // LANES
    alpha = (2.0 * depth) ** 0.25
    t = bsz * seq_len
    assert bsz + 1 <= SUBLANES
    assert d == ROW_TILES * LANES and ROW_TILES == SUBLANES, "a token row must be exactly one (8, 128) tile"

    row2 = lambda a: a.reshape(1, -1)

    c_rows = jnp.concatenate([c, c_ctx[None], jnp.zeros((SUBLANES - bsz - 1, d), F32)], axis=0)
    mod = _mod_vectors(c_rows, w_mod[0], row2(b_mod[0]))
    mod3 = mod.reshape(SUBLANES, 1, 6 * d)

    assert w_in.shape[2] == ssm_w + attn_w + 2 * kv_w + 2 * d
    w_all = w_in[0].astype(BF16)

    pos = jnp.arange(seq_len)
    inv = ROPE_BASE ** (-jnp.arange(ROPE_PAIRS, dtype=F32) / ROPE_PAIRS)
    ang = jnp.concatenate([(pos // GRID_W).astype(F32)[:, None] * inv,
                           (pos % GRID_W).astype(F32)[:, None] * inv], axis=-1)
    cos_t = jnp.tile(jnp.cos(ang), (1, 2 * LANES // HEAD_DIM))
    sin_h = jnp.sin(ang)
    sin_t = jnp.tile(jnp.concatenate([-sin_h, sin_h], axis=-1), (1, LANES // HEAD_DIM))

    g_in, b_in = row2(ln_in_g), row2(ln_in_b)
    x2 = x.reshape(t, d)
    ctx2 = ctx.reshape(bsz * n_ctx, d)
    s_c, k_c, v_c = _inproj(ctx2, g_in, b_in, mod3, w_all, None, None, latent=False,
                            rows_per_batch=n_ctx, tm=n_ctx, sub=n_ctx, ctx_mod_row=bsz,
                            ssm_w=ssm_w, attn_w=attn_w, kv2_w=kv2_w)
    s_l, q_l, k_l, v_l, sgs, sga, h_l = _inproj(x2, g_in, b_in, mod3, w_all, cos_t, sin_t, latent=True,
                                           rows_per_batch=seq_len, tm=TILES.inproj_rows, sub=TILES.inproj_sub, ctx_mod_row=None,
                                           ssm_w=ssm_w, attn_w=attn_w, kv2_w=kv2_w)

    wf = _s5_weights(ssm_lam_re[0, 0], ssm_lam_im[0, 0], ssm_log_step[0, 0], ssm_b_re[0, 0], ssm_b_im[0, 0],
                     ssm_c_re[0, 0], ssm_c_im[0, 0], reverse=False)
    wb = _s5_weights(ssm_lam_re[0, 1], ssm_lam_im[0, 1], ssm_log_step[0, 1], ssm_b_re[0, 1], ssm_b_im[0, 1],
                     ssm_c_re[0, 1], ssm_c_im[0, 1], reverse=True)
    d_tile = jnp.tile(ssm_d[0].astype(F32).reshape(nlb, 1, LANES), (1, 1, S5_CHUNK))
    sw2 = wf[1].shape[-1]
    zero_state = jnp.zeros((nlb, bsz, sw2), F32)
    uc4 = s_c.reshape(nlb, bsz, n_ctx, LANES)
    ul4 = s_l.reshape(nlb, bsz, seq_len, LANES)
    _, sf0 = _s5_scan(uc4, zero_state, wf, d_tile, reverse=False, cc=n_ctx // S5_CHUNK, add_skip=False)
    _, sb0 = _s5_scan(uc4, zero_state, wb, d_tile, reverse=True, cc=n_ctx // S5_CHUNK, add_skip=False)
    yf4, _ = _s5_scan(ul4, sf0, wf, d_tile, reverse=False, cc=TILES.s5_chunks, add_skip=True)
    yb4, _ = _s5_scan(ul4, sb0, wb, d_tile, reverse=True, cc=TILES.s5_chunks, add_skip=False)
    yf = yf4.reshape(nlb, t, LANES)
    yb = yb4.reshape(nlb, t, LANES)

    o_att = _attention(q_l, k_l, v_l, k_c, v_c, attn_sink[0].astype(F32), bsz=bsz, seq_len=seq_len, n_ctx=n_ctx,
                       qblocks=TILES.attn_qblocks)

    h1, top_i, gates, rank, counts = _merge(
        yf, yb, o_att, sgs, sga, h_l, mod3, w_glu[0].astype(BF16), row2(b_glu[0]),
        w_ssm_out[0].astype(BF16), w_att_out[0].astype(BF16), w_o[0].astype(BF16), row2(ln1_g[0]), row2(ln1_b[0]),
        w_router[0], row2(b_router[0]), rows_per_batch=seq_len, tm=TILES.merge_rows, sub=TILES.merge_sub, alpha=alpha)

    rows, sub_rows = TILES.expert_rows, TILES.expert_sub
    n_rows = t * TOP_K
    pos, tile_id, exp_id, lo, hi, nxt = _plan(counts[0].astype(jnp.int32), top_i, rank, rows=rows)
    pos_flat = pos.reshape(n_rows)
    xrows = _dispatch(pos_flat, h1, mod3, rows_per_batch=seq_len, tm=TILES.moe_token_rows)
    yrows = _experts((tile_id, exp_id, lo, hi, nxt), xrows, w_gate_up[0], b_gate_up[0], w_down[0], b_down[0],
                     rows=rows, sub_rows=sub_rows)
    out = _combine(pos_flat, h1, gates, mod3, row2(ln2_g[0]), row2(ln2_b[0]), yrows,
                   rows_per_batch=seq_len, tm=TILES.moe_token_rows, alpha=alpha)
    return out.reshape(bsz, seq_len, d)
```

```python
import functools
import math
from typing import NamedTuple

import jax
import jax.numpy as jnp
import numpy as np
from jax import lax
from jax.experimental import pallas as pl
from jax.experimental.pallas import tpu as pltpu

F32 = jnp.float32
BF16 = jnp.bfloat16
HIGHEST = lax.Precision.HIGHEST

HEAD_DIM = 64
N_Q_HEADS = 8
N_KV_HEADS = 2
Q_PER_KV = N_Q_HEADS // N_KV_HEADS
WINDOW = 128
GRID_W = 64
ROPE_BASE = 10000.0
ROPE_PAIRS = HEAD_DIM // 4
TOP_K = 4
SWIGLU_LIMIT = 7.0
SWIGLU_ALPHA = 1.702
LN_EPS = 1e-5
NEG_INF = -1e30

LANES = 128
SUBLANES = 8
MXU_DIM = 256
ROW_TILES = 8
VMEM_LIMIT = 56 * 1024 * 1024
N_DMA_PRIORITIES = 2
DMA_LOOP_UNROLL = 8

class _Tiles(NamedTuple):
    inproj_rows: int = 1024
    inproj_sub: int = 256
    s5_chunks: int = 128
    attn_qblocks: int = 4
    merge_rows: int = 1024
    merge_sub: int = 512
    moe_token_rows: int = 256
    expert_rows: int = 512
    expert_sub: int = 256


TILES = _Tiles()

S5_CHUNK = 8
S5_ROW_PAD = SUBLANES


def _cparams(sem):
    return pltpu.CompilerParams(dimension_semantics=sem, vmem_limit_bytes=VMEM_LIMIT)


def _sigmoid(x):
    return 1.0 / (1.0 + jnp.exp(-x))


def _layer_norm(x, g, b):
    mu = jnp.mean(x, axis=-1, keepdims=True)
    xc = x - mu
    var = jnp.mean(xc * xc, axis=-1, keepdims=True)
    return xc * lax.rsqrt(var + LN_EPS) * g + b


def _dot(a, b):
    return jnp.dot(a, b, preferred_element_type=F32)


def _dot_3pass(a, b):
    a_hi = a.astype(BF16)
    b_hi = b.astype(BF16)
    a_lo = (a - a_hi.astype(F32)).astype(BF16)
    b_lo = (b - b_hi.astype(F32)).astype(BF16)
    return _dot(a_hi, b_hi) + (_dot(a_hi, b_lo) + _dot(a_lo, b_hi))


def _mod_kernel(c_ref, w_ref, b_ref, o_ref):
    c = c_ref[...]
    a = c * _sigmoid(c)
    o_ref[...] = jnp.dot(a, w_ref[...], preferred_element_type=F32, precision=HIGHEST) + b_ref[...]


def _mod_vectors(c_rows, w_mod, b_mod):
    d = c_rows.shape[1]
    n = w_mod.shape[1]
    return pl.pallas_call(
        _mod_kernel,
        grid=(n // d,),
        in_specs=[
            pl.BlockSpec((SUBLANES, d), lambda i: (0, 0)),
            pl.BlockSpec((d, d), lambda i: (0, i)),
            pl.BlockSpec((1, d), lambda i: (0, i)),
        ],
        out_specs=pl.BlockSpec((SUBLANES, d), lambda i: (0, i)),
        out_shape=jax.ShapeDtypeStruct((SUBLANES, n), F32),
        compiler_params=_cparams(("arbitrary",)),
        name="mod",
    )(c_rows, w_mod, b_mod)


def _rope(t, cos, sin):
    n = t.shape[1]
    reps = n // LANES
    c = jnp.concatenate([cos] * reps, axis=1) if reps > 1 else cos
    s = jnp.concatenate([sin] * reps, axis=1) if reps > 1 else sin
    half = HEAD_DIM // 2
    upper = pltpu.roll(t, n - half, axis=1)
    lower = pltpu.roll(t, half, axis=1)
    lane = lax.broadcasted_iota(jnp.int32, t.shape, 1)
    partner = jnp.where((lane & half) == 0, upper, lower)
    return t * c + partner * s


def _dup_heads(t):
    assert t.shape[1] == LANES == N_KV_HEADS * HEAD_DIM
    swapped = pltpu.roll(t, HEAD_DIM, axis=1)
    low = lax.broadcasted_iota(jnp.int32, t.shape, 1) < HEAD_DIM
    return jnp.concatenate([jnp.where(low, t, swapped), jnp.where(low, swapped, t)], axis=1)


def _inproj_kernel(*refs, latent, ssm_w, attn_w, kv2_w, d_model, sub):
    if latent:
        (x_ref, g_ref, b_ref, sh_ref, sc_ref, w_ref, cos_ref, sin_ref,
         s_ref, q_ref, k_ref, v_ref, gs_ref, ga_ref, h_ref) = refs
    else:
        x_ref, g_ref, b_ref, sh_ref, sc_ref, w_ref, s_ref, k_ref, v_ref = refs
    tm = x_ref.shape[0]
    parts = range(tm // sub)
    rs = [slice(p * sub, (p + 1) * sub) for p in parts]
    each = lambda f: [f(p) for p in parts]
    h = each(lambda p: _layer_norm(x_ref[rs[p], :], g_ref[...], b_ref[...]))
    u = each(lambda p: (h[p] * (1.0 + sc_ref[...]) + sh_ref[...]).astype(BF16))
    s = each(lambda p: _dot(u[p], w_ref[:, 0:ssm_w]))
    for p in parts:
        for lb in range(ssm_w // LANES):
            s_ref[lb, rs[p], :] = s[p][:, lb * LANES:(lb + 1) * LANES]
    if latent:
        q = each(lambda p: _dot(u[p], w_ref[:, ssm_w:ssm_w + attn_w]))
        q = each(lambda p: _rope(q[p], cos_ref[rs[p], :], sin_ref[rs[p], :]) * (HEAD_DIM ** -0.5))
        for p in parts:
            q_ref[rs[p], :] = q[p].astype(BF16)
    kv_w = kv2_w // 2
    k = each(lambda p: _dot(u[p], w_ref[:, ssm_w + attn_w:ssm_w + attn_w + kv_w]))
    if latent:
        k = each(lambda p: _rope(k[p], cos_ref[rs[p], :], sin_ref[rs[p], :]))
    v = each(lambda p: _dot(u[p], w_ref[:, ssm_w + attn_w + kv_w:ssm_w + attn_w + 2 * kv_w]))
    for p in parts:
        k_ref[rs[p], :] = _dup_heads(k[p]).astype(BF16)
        v_ref[rs[p], :] = _dup_heads(v[p]).astype(BF16)
    if latent:
        col = ssm_w + attn_w + kv2_w
        gs = each(lambda p: _dot(u[p], w_ref[:, col:col + d_model]))
        ga = each(lambda p: _dot(u[p], w_ref[:, col + d_model:col + 2 * d_model]))
        for p in parts:
            gs_ref[rs[p], :] = _sigmoid(gs[p]).astype(BF16)
            ga_ref[rs[p], :] = _sigmoid(ga[p]).astype(BF16)
            h_ref[rs[p], :] = h[p]


def _inproj(x2, ln_g, ln_b, mod3, w_all, cos_t, sin_t, *, latent, rows_per_batch, tm, sub, ctx_mod_row,
            ssm_w, attn_w, kv2_w):
    t, d = x2.shape
    tiles_per_batch = rows_per_batch // tm
    nlb = ssm_w // LANES

    def brow(i):
        return i // tiles_per_batch if ctx_mod_row is None else ctx_mod_row

    in_specs = [
        pl.BlockSpec((tm, d), lambda i: (i, 0)),
        pl.BlockSpec((1, d), lambda i: (0, 0)),
        pl.BlockSpec((1, d), lambda i: (0, 0)),
        pl.BlockSpec((None, 1, d), lambda i: (brow(i), 0, 0)),
        pl.BlockSpec((None, 1, d), lambda i: (brow(i), 0, 1)),
        pl.BlockSpec(w_all.shape, lambda i: (0, 0)),
    ]
    args = [x2, ln_g, ln_b, mod3, mod3, w_all]
    out_specs = [pl.BlockSpec((nlb, tm, LANES), lambda i: (0, i, 0))]
    out_shape = [jax.ShapeDtypeStruct((nlb, t, LANES), F32)]
    if latent:
        in_specs += [
            pl.BlockSpec((tm, LANES), lambda i: (i % tiles_per_batch, 0)),
            pl.BlockSpec((tm, LANES), lambda i: (i % tiles_per_batch, 0)),
        ]
        args += [cos_t, sin_t]
        out_specs.append(pl.BlockSpec((tm, attn_w), lambda i: (i, 0)))
        out_shape.append(jax.ShapeDtypeStruct((t, attn_w), BF16))
    out_specs += [pl.BlockSpec((tm, kv2_w), lambda i: (i, 0))] * 2
    out_shape += [jax.ShapeDtypeStruct((t, kv2_w), BF16)] * 2
    if latent:
        out_specs += [pl.BlockSpec((tm, d), lambda i: (i, 0))] * 3
        out_shape += [jax.ShapeDtypeStruct((t, d), BF16)] * 2 + [jax.ShapeDtypeStruct((t, d), F32)]
    return pl.pallas_call(
        functools.partial(_inproj_kernel, latent=latent, ssm_w=ssm_w, attn_w=attn_w, kv2_w=kv2_w,
                          d_model=d, sub=sub),
        grid=(t // tm,),
        in_specs=in_specs,
        out_specs=out_specs,
        out_shape=out_shape,
        compiler_params=_cparams(("arbitrary",)),
        name="inproj_latent" if latent else "inproj_ctx",
    )(*args)


def _lane_expand(tbl, n_copies, rows_per_group, cols_per_group):
    r, w = tbl.shape
    wide_w = n_copies * w
    log2 = lambda v: int(v).bit_length() - 1
    assert all(v == 1 << log2(v) for v in (w, rows_per_group, cols_per_group))
    sel = ((lax.broadcasted_iota(jnp.int32, (w, wide_w), 1) & (w - 1))
           == lax.broadcasted_iota(jnp.int32, (w, wide_w), 0)).astype(BF16)
    wide = jnp.dot(tbl.astype(BF16), sel, preferred_element_type=F32)
    keep = ((lax.broadcasted_iota(jnp.int32, (r, wide_w), 0) >> log2(rows_per_group))
            == (lax.broadcasted_iota(jnp.int32, (r, wide_w), 1) >> log2(cols_per_group)))
    return jnp.where(keep, wide, 0.0)


def _s5_assemble_kernel(m_ref, ws_ref, wo_ref, kin_out, ws_out, wo_out, *, reverse):
    ch, _, hh = m_ref.shape
    p = ws_ref.shape[-1]
    gpb = LANES // hh
    sw = gpb * p
    bd = [_lane_expand(m_ref[j], gpb, hh, hh).astype(BF16) for j in range(ch)]
    zero = jnp.zeros((LANES, LANES), BF16)
    for k in range(ch):
        for i in range(ch):
            lag = (k - i) if reverse else (i - k)
            kin_out[k * LANES:(k + 1) * LANES, i * LANES:(i + 1) * LANES] = bd[lag] if lag >= 0 else zero
    for ri in range(2):
        for k in range(ch):
            ws_out[k * LANES:(k + 1) * LANES, ri * sw:(ri + 1) * sw] = (
                _lane_expand(ws_ref[ri, k], gpb, hh, p).astype(BF16))
        for i in range(ch):
            wo_out[ri * sw:(ri + 1) * sw, i * LANES:(i + 1) * LANES] = (
                _lane_expand(wo_ref[ri, i], gpb, p, hh).astype(BF16))


def _s5_assemble(m_tbl, ws_tbl, wo_tbl, *, reverse):
    ch, nlb, _, hh = m_tbl.shape
    p = ws_tbl.shape[-1]
    sw = (LANES // hh) * p
    width = ch * LANES
    return pl.pallas_call(
        functools.partial(_s5_assemble_kernel, reverse=reverse),
        grid=(nlb,),
        in_specs=[
            pl.BlockSpec((ch, None, LANES, hh), lambda l: (0, l, 0, 0)),
            pl.BlockSpec((2, ch, None, LANES, p), lambda l: (0, 0, l, 0, 0)),
            pl.BlockSpec((2, ch, None, sw, hh), lambda l: (0, 0, l, 0, 0)),
        ],
        out_specs=[
            pl.BlockSpec((None, width, width), lambda l: (l, 0, 0)),
            pl.BlockSpec((None, width, 2 * sw), lambda l: (l, 0, 0)),
            pl.BlockSpec((None, 2 * sw, width), lambda l: (l, 0, 0)),
        ],
        out_shape=[
            jax.ShapeDtypeStruct((nlb, width, width), BF16),
            jax.ShapeDtypeStruct((nlb, width, 2 * sw), BF16),
            jax.ShapeDtypeStruct((nlb, 2 * sw, width), BF16),
        ],
        compiler_params=_cparams(("arbitrary",)),
        name="s5_assemble_bwd" if reverse else "s5_assemble_fwd",
    )(m_tbl, ws_tbl, wo_tbl)


def _s5_weights(lam_re, lam_im, log_step, b_re, b_im, c_re, c_im, reverse):
    ch = S5_CHUNK
    g, p = lam_re.shape
    hh = b_re.shape[-1]
    gpb = LANES // hh
    nlb = g // gpb
    lr, li = lam_re.astype(F32), lam_im.astype(F32)
    dt = jnp.exp(log_step.astype(F32))[:, None]
    jj = jnp.arange(ch + 1, dtype=F32)[:, None, None]
    mag = jnp.exp(jj * lr * dt)
    pr = mag * jnp.cos(jj * li * dt)
    pi = mag * jnp.sin(jj * li * dt)
    ar, ai = pr[1], pi[1]
    den = lr * lr + li * li
    cr = ((ar - 1) * lr + ai * li) / den
    ci = (ai * lr - (ar - 1) * li) / den
    br, bi = b_re.astype(F32), b_im.astype(F32)
    bbr = cr[..., None] * br - ci[..., None] * bi
    bbi = cr[..., None] * bi + ci[..., None] * br
    ccr, cci = c_re.astype(F32), c_im.astype(F32)
    bbr_t = jnp.swapaxes(bbr, 1, 2)
    bbi_t = jnp.swapaxes(bbi, 1, 2)
    ccr_t = jnp.swapaxes(ccr, 1, 2)
    cci_t = jnp.swapaxes(cci, 1, 2)

    er = ccr[None] * pr[:ch, :, None, :] - cci[None] * pi[:ch, :, None, :]
    ei = ccr[None] * pi[:ch, :, None, :] + cci[None] * pr[:ch, :, None, :]
    m = jnp.sum(er[:, :, None, :, :] * bbr_t[None, :, :, None, :]
                - ei[:, :, None, :, :] * bbi_t[None, :, :, None, :], axis=-1)
    m_tbl = m.reshape(ch, nlb, LANES, hh)
    rk = np.arange(ch) if reverse else (ch - 1 - np.arange(ch))
    apr, api = pr[rk][:, :, None, :], pi[rk][:, :, None, :]
    sr = apr * bbr_t[None] - api * bbi_t[None]
    si = apr * bbi_t[None] + api * bbr_t[None]
    ws_tbl = jnp.stack([sr, si], axis=0).reshape(2, ch, nlb, LANES, p)
    ex = (ch - np.arange(ch)) if reverse else (np.arange(ch) + 1)
    epr, epi = pr[ex][:, :, :, None], pi[ex][:, :, :, None]
    wo_r = ccr_t[None] * epr - cci_t[None] * epi
    wo_i = -(ccr_t[None] * epi + cci_t[None] * epr)
    wo_tbl = jnp.stack([wo_r, wo_i], axis=0).reshape(2, ch, nlb, gpb * p, hh)
    kin, ws, wo = _s5_assemble(m_tbl, ws_tbl, wo_tbl, reverse=reverse)
    a_chunk = jnp.stack([pr[ch].reshape(nlb, gpb * p), pi[ch].reshape(nlb, gpb * p)], axis=1)
    return kin, ws, wo, a_chunk


def _s5_kernel(u_ref, s0_ref, a_ref, kin_ref, ws_ref, wo_ref, d_ref, y_ref, sfin_ref,
               z_scr, sin_scr, carry_scr, *, reverse, nb, cc, add_skip):
    j = pl.program_id(1)

    @pl.when(j == 0)
    def _():
        carry_scr[...] = s0_ref[...]

    ch = S5_CHUNK
    width = ch * LANES
    sw = a_ref.shape[-1]
    ns = sw // LANES
    u = jnp.concatenate(
        [jnp.concatenate([u_ref[b, pl.ds(k, cc, stride=ch), :] for b in range(nb)], axis=0)
         for k in range(ch)], axis=1)
    ub = u.astype(BF16)
    z = _dot(ub, ws_ref[...])
    ccp = cc + S5_ROW_PAD
    for s in range(2 * ns):
        for b in range(nb):
            z_scr[s, b * ccp:b * ccp + cc, :] = z[b * cc:(b + 1) * cc, s * LANES:(s + 1) * LANES]
    ar = [a_ref[0:1, s * LANES:(s + 1) * LANES] for s in range(ns)]
    ai = [a_ref[1:2, s * LANES:(s + 1) * LANES] for s in range(ns)]

    def body(c, st):
        cidx = (cc - 1 - c) if reverse else c
        rows = pl.ds(cidx, nb, stride=ccp)
        new = [None] * (2 * ns)
        for s in range(ns):
            sr, si = st[s], st[ns + s]
            sin_scr[s, rows, :] = sr
            sin_scr[ns + s, rows, :] = si
            new[s] = ar[s] * sr - ai[s] * si + z_scr[s, rows, :]
            new[ns + s] = ar[s] * si + ai[s] * sr + z_scr[ns + s, rows, :]
        return tuple(new)

    st0 = tuple(carry_scr[:, s * LANES:(s + 1) * LANES] for s in range(2 * ns))
    st = lax.fori_loop(0, cc, body, st0)
    s_fin = jnp.concatenate(st, axis=1)
    carry_scr[...] = s_fin
    sfin_ref[...] = s_fin
    s_in = jnp.concatenate(
        [jnp.concatenate([sin_scr[s, b * ccp:b * ccp + cc, :] for b in range(nb)], axis=0)
         for s in range(2 * ns)], axis=1)
    intra = []
    for c0 in range(0, width, MXU_DIM):
        ks = slice(c0, width) if reverse else slice(0, c0 + MXU_DIM)
        intra.append(_dot(ub[:, ks], kin_ref[ks, c0:c0 + MXU_DIM]))
    y = jnp.concatenate(intra, axis=1) + _dot(s_in.astype(BF16), wo_ref[...])
    if add_skip:
        y = y + u * d_ref[...]
    for b in range(nb):
        for i in range(ch):
            y_ref[b, pl.ds(i, cc, stride=ch), :] = y[b * cc:(b + 1) * cc, i * LANES:(i + 1) * LANES]


def _s5_scan(u4, s0, weights, d_tile, *, reverse, cc, add_skip):
    kin, ws, wo, a_chunk = weights
    nlb, nb, n_steps, _ = u4.shape
    width = S5_CHUNK * LANES
    sw2 = ws.shape[-1]
    nj = n_steps // (cc * S5_CHUNK)

    def jm(j):
        return (nj - 1 - j) if reverse else j

    return pl.pallas_call(
        functools.partial(_s5_kernel, reverse=reverse, nb=nb, cc=cc, add_skip=add_skip),
        grid=(nlb, nj),
        in_specs=[
            pl.BlockSpec((None, nb, cc * S5_CHUNK, LANES), lambda l, j: (l, 0, jm(j), 0)),
            pl.BlockSpec((None, nb, sw2), lambda l, j: (l, 0, 0)),
            pl.BlockSpec((None, 2, sw2 // 2), lambda l, j: (l, 0, 0)),
            pl.BlockSpec((None, width, width), lambda l, j: (l, 0, 0)),
            pl.BlockSpec((None, width, sw2), lambda l, j: (l, 0, 0)),
            pl.BlockSpec((None, sw2, width), lambda l, j: (l, 0, 0)),
            pl.BlockSpec((None, 1, width), lambda l, j: (l, 0, 0)),
        ],
        out_specs=[
            pl.BlockSpec((None, nb, cc * S5_CHUNK, LANES), lambda l, j: (l, 0, jm(j), 0)),
            pl.BlockSpec((None, nb, sw2), lambda l, j: (l, 0, 0)),
        ],
        out_shape=[
            jax.ShapeDtypeStruct(u4.shape, F32),
            jax.ShapeDtypeStruct((nlb, nb, sw2), F32),
        ],
        scratch_shapes=[
            pltpu.VMEM((sw2 // LANES, nb * (cc + S5_ROW_PAD), LANES), F32),
            pltpu.VMEM((sw2 // LANES, nb * (cc + S5_ROW_PAD), LANES), F32),
            pltpu.VMEM((nb, sw2), F32),
        ],
        compiler_params=_cparams(("arbitrary", "arbitrary")),
        name="s5_bwd" if reverse else "s5_fwd",
    )(u4, s0, a_chunk, kin, ws, wo, d_tile)


def _attn_kernel(sink_ref, q_ref, kp_ref, ko_ref, kn_ref, vp_ref, vo_ref, vn_ref, kc_ref, vc_ref,
                 o_ref, *, nblk, blk, qblocks):
    n = pl.program_id(1)
    n_ctx = kc_ref.shape[0]
    n_loc = 3 * blk
    nrow = Q_PER_KV * blk
    row = lax.broadcasted_iota(jnp.int32, (nrow, n_loc), 0)
    qi = row & (blk - 1)
    kj = lax.broadcasted_iota(jnp.int32, (nrow, n_loc), 1)
    rel = kj - qi
    in_window = (rel >= blk - WINDOW) & (rel <= blk + WINDOW)
    lane = lax.broadcasted_iota(jnp.int32, (blk, LANES), 1)
    low_half = lane < HEAD_DIM
    row1 = lax.broadcasted_iota(jnp.int32, (nrow, 1), 0)

    key1 = lax.broadcasted_iota(jnp.int32, (1, n_loc), 1)

    def edge_bias(j):
        gblk = n * qblocks + j
        lo = jnp.where(gblk == 0, blk, 0)
        hi = jnp.where(gblk == nblk - 1, 2 * blk, n_loc)
        return jnp.where((key1 >= lo) & (key1 < hi), 0.0, NEG_INF)

    bias = [edge_bias(j) for j in range(qblocks)]
    parts = [(j, g) for j in range(qblocks) for g in range(N_KV_HEADS)]
    each = lambda f: [f(j, g) for j, g in parts]
    at = {pg: i for i, pg in enumerate(parts)}

    def key_blocks(p_ref, own_ref, n_ref, c_ref, j, g):
        cs = slice(g * LANES, (g + 1) * LANES)
        blocks = [p_ref[:, cs]] + [own_ref[i * blk:(i + 1) * blk, cs] for i in range(qblocks)] + [n_ref[:, cs]]
        return jnp.concatenate(blocks[j:j + 3] + [c_ref[:, cs]], axis=0)

    kcat = each(lambda j, g: key_blocks(kp_ref, ko_ref, kn_ref, kc_ref, j, g))
    vcat = each(lambda j, g: key_blocks(vp_ref, vo_ref, vn_ref, vc_ref, j, g))

    def stacked_q(j, g):
        qms = []
        for hq in range(Q_PER_KV):
            h = g * Q_PER_KV + hq
            qb = q_ref[j * blk:(j + 1) * blk, (h // 2) * LANES:(h // 2 + 1) * LANES]
            qms.append(jnp.where(low_half if h % 2 == 0 else jnp.logical_not(low_half), qb, jnp.zeros_like(qb)))
        return jnp.concatenate(qms, axis=0)

    def stacked_sink(j, g):
        snk = jnp.zeros((nrow, 1), F32)
        for hq in range(Q_PER_KV):
            snk = jnp.where((row1 >= hq * blk) & (row1 < (hq + 1) * blk), sink_ref[g * Q_PER_KV + hq], snk)
        return snk

    qs = each(stacked_q)
    snk = each(stacked_sink)
    s = each(lambda j, g: lax.dot_general(qs[at[j, g]], kcat[at[j, g]], (((1,), (1,)), ((), ())),
                                          preferred_element_type=F32))
    s = each(lambda j, g: jnp.concatenate(
        [jnp.where(in_window, s[at[j, g]][:, :n_loc], NEG_INF) + bias[j], s[at[j, g]][:, n_loc:]], axis=1))
    mx = each(lambda j, g: jnp.maximum(jnp.max(s[at[j, g]], axis=1, keepdims=True), snk[at[j, g]]))
    p = each(lambda j, g: jnp.exp(s[at[j, g]] - mx[at[j, g]]))
    den = each(lambda j, g: jnp.sum(p[at[j, g]], axis=1, keepdims=True) + jnp.exp(snk[at[j, g]] - mx[at[j, g]]))
    o = each(lambda j, g: _dot(p[at[j, g]].astype(BF16), vcat[at[j, g]]) / den[at[j, g]])
    for j in range(qblocks):
        outs = [o[at[j, g]][hq * blk:(hq + 1) * blk] for g in range(N_KV_HEADS) for hq in range(Q_PER_KV)]
        blocks = [jnp.where(low_half, outs[2 * m], outs[2 * m + 1]) for m in range(N_Q_HEADS // 2)]
        o_ref[j * blk:(j + 1) * blk, :] = jnp.concatenate(blocks, axis=1).astype(BF16)


def _attention(q, kd, vd, kcd, vcd, sink, *, bsz, seq_len, n_ctx, qblocks):
    blk = WINDOW
    nblk = seq_len // blk
    nstep = nblk // qblocks
    aw = q.shape[1]
    kw = kd.shape[1]

    def qmap(b, n):
        return (b * nstep + n, 0)

    def pmap(b, n):
        return (b * nblk + jnp.maximum(n * qblocks - 1, 0), 0)

    def nmap(b, n):
        return (b * nblk + jnp.minimum((n + 1) * qblocks, nblk - 1), 0)

    edge = lambda f: pl.BlockSpec((blk, kw), f)
    own = pl.BlockSpec((qblocks * blk, kw), qmap)
    return pl.pallas_call(
        functools.partial(_attn_kernel, nblk=nblk, blk=blk, qblocks=qblocks),
        grid=(bsz, nstep),
        in_specs=[
            pl.BlockSpec(memory_space=pltpu.SMEM),
            pl.BlockSpec((qblocks * blk, aw), qmap),
            edge(pmap), own, edge(nmap),
            edge(pmap), own, edge(nmap),
            pl.BlockSpec((n_ctx, kw), lambda b, n: (b, 0)),
            pl.BlockSpec((n_ctx, kw), lambda b, n: (b, 0)),
        ],
        out_specs=pl.BlockSpec((qblocks * blk, aw), qmap),
        out_shape=jax.ShapeDtypeStruct(q.shape, BF16),
        compiler_params=_cparams(("arbitrary", "arbitrary")),
        name="attn",
    )(sink, q, kd, kd, kd, vd, vd, vd, kcd, vcd)


def _gelu_tanh(x):
    return 0.5 * x * (1.0 + jnp.tanh(math.sqrt(2.0 / math.pi) * (x + 0.044715 * (x * x * x))))


def _flatten_slots(v, max_value):
    tm, nk = v.shape
    per_row = LANES // nk
    log2 = lambda x: int(x).bit_length() - 1
    assert nk == 1 << log2(nk) and tm % per_row == 0
    spread = ((lax.broadcasted_iota(jnp.int32, (nk, LANES), 1) & (nk - 1))
              == lax.broadcasted_iota(jnp.int32, (nk, LANES), 0)).astype(BF16)
    t_id = lax.broadcasted_iota(jnp.int32, (tm, LANES), 0)
    c_id = lax.broadcasted_iota(jnp.int32, (tm, LANES), 1)
    own = (c_id >> log2(nk)) == (t_id & (per_row - 1))
    group = ((lax.broadcasted_iota(jnp.int32, (tm // per_row, tm), 1) >> log2(per_row))
             == lax.broadcasted_iota(jnp.int32, (tm // per_row, tm), 0)).astype(BF16)
    digit_bits = 7
    out = jnp.zeros((tm // per_row, LANES), jnp.int32)
    for shift in range(0, max(int(max_value).bit_length(), 1), digit_bits):
        digit = ((v >> shift) & ((1 << digit_bits) - 1)).astype(F32).astype(BF16)
        wide = jnp.dot(digit, spread, preferred_element_type=F32)
        wide = jnp.where(own, wide, 0.0).astype(BF16)
        out = out + (jnp.dot(group, wide, preferred_element_type=F32).astype(jnp.int32) << shift)
    return out


def _merge_kernel(yf_ref, yb_ref, o_ref, gs_ref, ga_ref, h_ref, g1_ref, sh2_ref, sc2_ref,
                  wglu_ref, bglu_ref, wso_ref, wao_ref, wo_ref, l1g_ref, l1b_ref, wr_ref, br_ref,
                  h1_ref, idx_ref, gate_ref, rank_ref, cnt_ref, cnt_scr, *, alpha, n_exp, sub, n_tokens):
    i = pl.program_id(0)

    @pl.when(i == 0)
    def _():
        cnt_scr[...] = jnp.zeros_like(cnt_scr)

    nlb = yf_ref.shape[0]
    tm = h_ref.shape[0]
    lane = lax.broadcasted_iota(jnp.int32, (sub, n_exp), 1)
    lane_k = lax.broadcasted_iota(jnp.int32, (sub, TOP_K), 1)
    ri = lax.broadcasted_iota(jnp.int32, (sub, sub), 0)
    ci = lax.broadcasted_iota(jnp.int32, (sub, sub), 1)
    tri = jnp.where(ci < ri, 1.0, 0.0).astype(BF16)
    slots = sub * TOP_K // LANES
    cnt = cnt_scr[...]
    parts = range(tm // sub)
    rs = [slice(p * sub, (p + 1) * sub) for p in parts]
    each = lambda f: [f(p) for p in parts]
    y = each(lambda p: jnp.concatenate([yf_ref[lb, rs[p], :] + yb_ref[lb, rs[p], :] for lb in range(nlb)], axis=1))
    z = each(lambda p: _gelu_tanh(y[p]))
    zg = each(lambda p: _dot(z[p].astype(BF16), wglu_ref[...]) + bglu_ref[...])
    z = each(lambda p: z[p] * _sigmoid(zg[p]))
    ms = each(lambda p: _dot(z[p].astype(BF16), wso_ref[...]))
    ma = each(lambda p: _dot(o_ref[rs[p], :], wao_ref[...]))
    m = each(lambda p: gs_ref[rs[p], :].astype(F32) * ms[p] + ga_ref[rs[p], :].astype(F32) * ma[p])
    mix = each(lambda p: _dot(m[p].astype(BF16), wo_ref[...]))
    h1 = each(lambda p: _layer_norm(alpha * h_ref[rs[p], :] + g1_ref[...] * mix[p], l1g_ref[...], l1b_ref[...]))
    for p in parts:
        h1_ref[rs[p], :] = h1[p]
    xm = each(lambda p: h1[p] * (1.0 + sc2_ref[...]) + sh2_ref[...])
    work = each(lambda p: _dot_3pass(xm[p], wr_ref[...]) + br_ref[...])
    vals, sels = [], []
    for _ in range(TOP_K):
        mx = each(lambda p: jnp.max(work[p], axis=1, keepdims=True))
        sel = each(lambda p: jnp.min(jnp.where(work[p] == mx[p], lane, n_exp), axis=1, keepdims=True))
        work = each(lambda p: jnp.where(lane == sel[p], -jnp.inf, work[p]))
        vals.append(mx)
        sels.append(sel)
    exps = [each(lambda p: jnp.exp(vals[k][p] - vals[0][p])) for k in range(TOP_K)]
    den = each(lambda p: exps[0][p] + exps[1][p] + exps[2][p] + exps[3][p])
    onehot = each(lambda p: sum((lane == sels[k][p]).astype(F32) for k in range(TOP_K)))
    prefix = each(lambda p: _dot(tri, onehot[p].astype(BF16)))
    for p in parts:
        rank = prefix[p] + cnt
        idx_o = jnp.zeros((sub, TOP_K), jnp.int32)
        gate_o = jnp.zeros((sub, TOP_K), F32)
        rank_o = jnp.zeros((sub, TOP_K), jnp.int32)
        for k in range(TOP_K):
            rk = jnp.sum(jnp.where(lane == sels[k][p], rank, 0.0), axis=1, keepdims=True).astype(jnp.int32)
            idx_o = jnp.where(lane_k == k, sels[k][p], idx_o)
            gate_o = jnp.where(lane_k == k, exps[k][p] / den[p], gate_o)
            rank_o = jnp.where(lane_k == k, rk, rank_o)
        gate_ref[rs[p], :] = gate_o
        idx_ref[p * slots:(p + 1) * slots, :] = _flatten_slots(idx_o, n_exp - 1)
        rank_ref[p * slots:(p + 1) * slots, :] = _flatten_slots(rank_o, n_tokens - 1)
        cnt = cnt + jnp.sum(onehot[p], axis=0, keepdims=True)
    cnt_scr[...] = cnt
    cnt_ref[...] = cnt


def _merge(yf, yb, o_att, sgs, sga, h_in, mod3, w_glu, b_glu, w_so, w_ao, w_o, l1g, l1b,
           w_r, b_r, *, rows_per_batch, tm, sub, alpha):
    t, d = h_in.shape
    nlb = yf.shape[0]
    sw = nlb * LANES
    n_exp = w_r.shape[1]
    tpb = rows_per_batch // tm
    row = lambda i: (i, 0)
    const = lambda i: (0, 0)

    def modspec(chunk):
        return pl.BlockSpec((None, 1, d), lambda i: (i // tpb, 0, chunk))

    return pl.pallas_call(
        functools.partial(_merge_kernel, alpha=alpha, n_exp=n_exp, sub=sub, n_tokens=t),
        grid=(t // tm,),
        in_specs=[
            pl.BlockSpec((nlb, tm, LANES), lambda i: (0, i, 0)),
            pl.BlockSpec((nlb, tm, LANES), lambda i: (0, i, 0)),
            pl.BlockSpec((tm, o_att.shape[1]), row),
            pl.BlockSpec((tm, d), row),
            pl.BlockSpec((tm, d), row),
            pl.BlockSpec((tm, d), row),
            modspec(2), modspec(3), modspec(4),
            pl.BlockSpec(w_glu.shape, const),
            pl.BlockSpec((1, sw), const),
            pl.BlockSpec(w_so.shape, const),
            pl.BlockSpec(w_ao.shape, const),
            pl.BlockSpec(w_o.shape, const),
            pl.BlockSpec((1, d), const),
            pl.BlockSpec((1, d), const),
            pl.BlockSpec(w_r.shape, const),
            pl.BlockSpec((1, n_exp), const),
        ],
        out_specs=[
            pl.BlockSpec((tm, d), row),
            pl.BlockSpec((tm * TOP_K // LANES, LANES), row),
            pl.BlockSpec((tm, TOP_K), row),
            pl.BlockSpec((tm * TOP_K // LANES, LANES), row),
            pl.BlockSpec((1, n_exp), const),
        ],
        out_shape=[
            jax.ShapeDtypeStruct((t, d), F32),
            jax.ShapeDtypeStruct((t * TOP_K // LANES, LANES), jnp.int32),
            jax.ShapeDtypeStruct((t, TOP_K), F32),
            jax.ShapeDtypeStruct((t * TOP_K // LANES, LANES), jnp.int32),
            jax.ShapeDtypeStruct((1, n_exp), F32),
        ],
        scratch_shapes=[pltpu.VMEM((1, n_exp), F32)],
        compiler_params=_cparams(("arbitrary",)),
        name="merge",
    )(yf, yb, o_att, sgs, sga, h_in, mod3, mod3, mod3, w_glu, b_glu, w_so, w_ao, w_o,
      l1g, l1b, w_r, b_r)


def _row_tile(ref, r):
    return ref.at[pl.ds(pl.multiple_of(r * ROW_TILES, ROW_TILES), ROW_TILES)]


def _row_tile_at(ref, first_line):
    return ref.at[pl.ds(pl.multiple_of(first_line, ROW_TILES), ROW_TILES)]


def _store_row_tiles(ref, val):
    rows = val.shape[0]
    for l in range(ROW_TILES):
        ref[pl.ds(l, rows, stride=ROW_TILES), :] = val[:, l * LANES:(l + 1) * LANES]


def _load_row_tiles(ref, row0, rows):
    return jnp.concatenate(
        [ref[pl.ds(row0 * ROW_TILES + l, rows, stride=ROW_TILES), :] for l in range(ROW_TILES)], axis=1)


def _row_copy_out(src, dst, pos_ref, sem, t, k):
    return pltpu.make_async_copy(_row_tile(src, t), _row_tile_at(dst, pos_ref[t * TOP_K + k]), sem)


def _dispatch_kernel(pos_ref, prev_pos_ref, h1_ref, sh2_ref, sc2_ref, xrows_ref, xm_scr, sems):
    i = pl.program_id(0)
    n = pl.num_programs(0)
    tm = h1_ref.shape[0]
    slot = i % 2
    xm = h1_ref[...] * (1.0 + sc2_ref[...]) + sh2_ref[...]
    _store_row_tiles(xm_scr.at[slot], xm)

    def copies(slot_, idx_ref, op):
        def body(t, carry):
            for k in range(TOP_K):
                op(_row_copy_out(xm_scr.at[slot_], xrows_ref, idx_ref, sems.at[slot_], t, k), k)
            return carry
        lax.fori_loop(0, tm, body, 0, unroll=DMA_LOOP_UNROLL)

    start = lambda cp, k: cp.start(priority=k % N_DMA_PRIORITIES)
    wait = lambda cp, k: cp.wait()
    for s in range(2):
        @pl.when(slot == s)
        def _():
            copies(s, pos_ref, start)

    for s in range(2):
        @pl.when((slot == 1 - s) & (i > 0))
        def _():
            copies(s, prev_pos_ref, wait)

        @pl.when((slot == s) & (i == n - 1))
        def _():
            copies(s, pos_ref, wait)


def _dispatch(pos_flat, h1, mod3, *, rows_per_batch, tm):
    t, d = h1.shape
    tpb = rows_per_batch // tm
    return pl.pallas_call(
        _dispatch_kernel,
        grid=(t // tm,),
        in_specs=[
            pl.BlockSpec((tm * TOP_K,), lambda i: (i,), memory_space=pltpu.SMEM),
            pl.BlockSpec((tm * TOP_K,), lambda i: (jnp.maximum(i - 1, 0),), memory_space=pltpu.SMEM),
            pl.BlockSpec((tm, d), lambda i: (i, 0)),
            pl.BlockSpec((None, 1, d), lambda i: (i // tpb, 0, 3)),
            pl.BlockSpec((None, 1, d), lambda i: (i // tpb, 0, 4)),
        ],
        out_specs=pl.BlockSpec(memory_space=pl.ANY),
        out_shape=jax.ShapeDtypeStruct((t * TOP_K * ROW_TILES, LANES), F32),
        scratch_shapes=[pltpu.VMEM((2, tm * ROW_TILES, LANES), F32), pltpu.SemaphoreType.DMA((2,))],
        compiler_params=_cparams(("arbitrary",)),
        name="dispatch",
    )(pos_flat, pos_flat, h1, mod3, mod3)


def _experts_kernel(tile_ref, exp_ref, lo_ref, hi_ref, nxt_ref, x_ref, wgu_hbm, bgu_ref, wd_hbm, bd_ref, y_ref,
                    wgu_f32, wd_f32, wgu_scr, wd_scr, sems, *, rows, sub_rows):
    w = pl.program_id(0)
    prev = jnp.maximum(w - 1, 0)
    e_new = (w == 0) | (exp_ref[w] != exp_ref[prev])
    t_new = (w == 0) | (tile_ref[w] != tile_ref[prev])
    lo = lo_ref[w]
    hi = hi_ref[w]
    f = wd_scr.shape[0]

    def weight_copies(e):
        return (pltpu.make_async_copy(wgu_hbm.at[e], wgu_f32, sems.at[0]),
                pltpu.make_async_copy(wd_hbm.at[e], wd_f32, sems.at[1]))

    @pl.when(w == 0)
    def _():
        for cp in weight_copies(exp_ref[0]):
            cp.start()

    @pl.when(e_new)
    def _():
        for cp in weight_copies(exp_ref[w]):
            cp.wait()
        wgu_scr[...] = wgu_f32[...].astype(BF16)
        wd_scr[...] = wd_f32[...].astype(BF16)

        @pl.when(nxt_ref[w] >= 0)
        def _():
            for cp in weight_copies(nxt_ref[w]):
                cp.start()

    row0 = tile_ref[w] * rows
    whole = (lo <= row0) & (hi >= row0 + rows)

    @pl.when(t_new & jnp.logical_not(whole))
    def _():
        y_ref[...] = jnp.zeros_like(y_ref)

    def expert_mlp(x):
        gu = _dot(x.astype(BF16), wgu_scr[...]) + bgu_ref[...]
        glu = jnp.minimum(gu[:, :f], SWIGLU_LIMIT)
        lin = jnp.clip(gu[:, f:], -SWIGLU_LIMIT, SWIGLU_LIMIT)
        act = glu * _sigmoid(SWIGLU_ALPHA * glu) * (lin + 1.0)
        return _dot(act.astype(BF16), wd_scr[...]) + bd_ref[...]

    def sub_block(sb):
        s_lo = row0 + sb * sub_rows
        s_hi = s_lo + sub_rows
        y_sub = y_ref.at[pl.ds(sb * sub_rows * ROW_TILES, sub_rows * ROW_TILES)]

        @pl.when((hi > s_lo) & (lo < s_hi))
        def _():
            y = expert_mlp(_load_row_tiles(x_ref, sb * sub_rows, sub_rows))
            sub_whole = (lo <= s_lo) & (hi >= s_hi)

            @pl.when(sub_whole)
            def _():
                _store_row_tiles(y_sub, y)

            @pl.when(jnp.logical_not(sub_whole))
            def _():
                r = s_lo + lax.broadcasted_iota(jnp.int32, (sub_rows, 1), 0)
                _store_row_tiles(y_sub, jnp.where((r >= lo) & (r < hi), y, _load_row_tiles(y_sub, 0, sub_rows)))

    @pl.when(whole)
    def _():
        _store_row_tiles(y_ref, expert_mlp(_load_row_tiles(x_ref, 0, rows)))

    @pl.when(jnp.logical_not(whole))
    def _():
        for sb in range(rows // sub_rows):
            sub_block(sb)


def _experts(work, xrows, w_gu, b_gu, w_d, b_d, *, rows, sub_rows):
    tile_id, exp_id, lo, hi, nxt = work
    n_exp, d, f2 = w_gu.shape
    f = w_d.shape[1]
    grid_spec = pltpu.PrefetchScalarGridSpec(
        num_scalar_prefetch=5,
        grid=(tile_id.shape[0],),
        in_specs=[
            pl.BlockSpec((rows * ROW_TILES, LANES), lambda w, ti, ex, lo, hi, nx: (ti[w], 0)),
            pl.BlockSpec(memory_space=pl.ANY),
            pl.BlockSpec((None, 1, f2), lambda w, ti, ex, lo, hi, nx: (ex[w], 0, 0)),
            pl.BlockSpec(memory_space=pl.ANY),
            pl.BlockSpec((None, 1, d), lambda w, ti, ex, lo, hi, nx: (ex[w], 0, 0)),
        ],
        out_specs=pl.BlockSpec((rows * ROW_TILES, LANES), lambda w, ti, ex, lo, hi, nx: (ti[w], 0)),
        scratch_shapes=[
            pltpu.VMEM((d, f2), F32), pltpu.VMEM((f, d), F32),
            pltpu.VMEM((d, f2), BF16), pltpu.VMEM((f, d), BF16),
            pltpu.SemaphoreType.DMA((2,)),
        ],
    )
    return pl.pallas_call(
        functools.partial(_experts_kernel, rows=rows, sub_rows=sub_rows),
        grid_spec=grid_spec,
        out_shape=jax.ShapeDtypeStruct(xrows.shape, F32),
        compiler_params=_cparams(("arbitrary",)),
        name="experts",
    )(tile_id, exp_id, lo, hi, nxt, xrows, w_gu, b_gu.reshape(n_exp, 1, f2), w_d, b_d.reshape(n_exp, 1, d))


def _row_copy_in(src, dst, pos_ref, sem, t, k, tm):
    return pltpu.make_async_copy(_row_tile_at(src, pos_ref[t * TOP_K + k]), _row_tile(dst, k * tm + t), sem)


def _combine_kernel(pos_ref, next_pos_ref, h1_ref, gate_ref, g2_ref, lg_ref, lb_ref, yrows_ref, o_ref,
                    buf, sems, *, alpha):
    i = pl.program_id(0)
    n = pl.num_programs(0)
    tm = h1_ref.shape[0]
    slot = i % 2

    def copies(slot_, idx_ref, op):
        def body(t, carry):
            for k in range(TOP_K):
                op(_row_copy_in(yrows_ref, buf.at[slot_], idx_ref, sems.at[slot_], t, k, tm), k)
            return carry
        lax.fori_loop(0, tm, body, 0, unroll=DMA_LOOP_UNROLL)

    start = lambda cp, k: cp.start(priority=k % N_DMA_PRIORITIES)
    wait = lambda cp, k: cp.wait()

    @pl.when(i == 0)
    def _():
        copies(0, pos_ref, start)

    for s in range(2):
        @pl.when((slot == 1 - s) & (i + 1 < n))
        def _():
            copies(s, next_pos_ref, start)

    def reduce(s):
        copies(s, pos_ref, wait)
        gates = gate_ref[...]
        ffn = gates[:, 0:1] * _load_row_tiles(buf.at[s], 0, tm)
        for k in range(1, TOP_K):
            ffn = ffn + gates[:, k:k + 1] * _load_row_tiles(buf.at[s], k * tm, tm)
        o_ref[...] = _layer_norm(alpha * h1_ref[...] + g2_ref[...] * ffn, lg_ref[...], lb_ref[...])

    for s in range(2):
        @pl.when(slot == s)
        def _():
            reduce(s)


def _combine(pos_flat, h1, gates, mod3, l2g, l2b, yrows, *, rows_per_batch, tm, alpha):
    t, d = h1.shape
    tpb = rows_per_batch // tm
    n_tiles = t // tm
    return pl.pallas_call(
        functools.partial(_combine_kernel, alpha=alpha),
        grid=(n_tiles,),
        in_specs=[
            pl.BlockSpec((tm * TOP_K,), lambda i: (i,), memory_space=pltpu.SMEM),
            pl.BlockSpec((tm * TOP_K,), lambda i: (jnp.minimum(i + 1, n_tiles - 1),), memory_space=pltpu.SMEM),
            pl.BlockSpec((tm, d), lambda i: (i, 0)),
            pl.BlockSpec((tm, TOP_K), lambda i: (i, 0)),
            pl.BlockSpec((None, 1, d), lambda i: (i // tpb, 0, 5)),
            pl.BlockSpec((1, d), lambda i: (0, 0)),
            pl.BlockSpec((1, d), lambda i: (0, 0)),
            pl.BlockSpec(memory_space=pl.ANY),
        ],
        out_specs=pl.BlockSpec((tm, d), lambda i: (i, 0)),
        out_shape=jax.ShapeDtypeStruct((t, d), F32),
        scratch_shapes=[pltpu.VMEM((2, TOP_K * tm * ROW_TILES, LANES), F32), pltpu.SemaphoreType.DMA((2,))],
        compiler_params=_cparams(("arbitrary",)),
        name="combine",
    )(pos_flat, pos_flat, h1, gates, mod3, l2g, l2b, yrows)


def _plan_kernel(cnt_ref, idx_ref, rank_ref, pos_ref, tile_out, exp_out, lo_out, hi_out, nxt_out,
                 start_s, end_s, next_s, *, rows, n_tiles):
    n_exp = cnt_ref.shape[0]
    n_work = tile_out.shape[0]

    def cumulate(e, acc):
        start_s[e] = acc
        end_s[e] = acc + cnt_ref[e]
        return acc + cnt_ref[e]

    lax.fori_loop(0, n_exp, cumulate, 0)

    def next_nonempty(i, cur):
        e = n_exp - 1 - i
        next_s[e] = cur
        return jnp.where(cnt_ref[e] > 0, e, cur)

    first = lax.fori_loop(0, n_exp, next_nonempty, -1)

    def item(w, carry):
        tile, e = carry
        live = tile < n_tiles
        tl = jnp.minimum(tile, n_tiles - 1)
        tile_lo = tl * rows
        tile_hi = tile_lo + rows
        lo = jnp.maximum(start_s[e], tile_lo)
        hi = jnp.where(live, jnp.minimum(end_s[e], tile_hi), lo)
        tile_out[w] = tl
        exp_out[w] = e
        lo_out[w] = lo
        hi_out[w] = hi
        nxt_out[w] = next_s[e]
        tile_done = live & (end_s[e] >= tile_hi)
        expert_done = live & (end_s[e] <= tile_hi) & (next_s[e] >= 0)
        return jnp.where(tile_done, tile + 1, tile), jnp.where(expert_done, next_s[e], e)

    lax.fori_loop(0, n_work, item, (0, first))

    idx = idx_ref[...]
    base = jnp.zeros(idx.shape, jnp.int32)
    for e in range(n_exp):
        base = jnp.where(idx == e, start_s[e], base)
    pos_ref[...] = (base + rank_ref[...]) * ROW_TILES


def _plan(counts, idx_flat, rank_flat, *, rows):
    n_exp = counts.shape[0]
    n_rows = idx_flat.shape[0] * idx_flat.shape[1]
    n_tiles = n_rows // rows
    n_work = n_tiles + n_exp - 1
    smem = pl.BlockSpec(memory_space=pltpu.SMEM)
    vmem = pl.BlockSpec(memory_space=pltpu.VMEM)
    return pl.pallas_call(
        functools.partial(_plan_kernel, rows=rows, n_tiles=n_tiles),
        in_specs=[smem, vmem, vmem],
        out_specs=[vmem] + [smem] * 5,
        out_shape=[jax.ShapeDtypeStruct(idx_flat.shape, jnp.int32)]
        + [jax.ShapeDtypeStruct((n_work,), jnp.int32)] * 5,
        scratch_shapes=[pltpu.SMEM((n_exp,), jnp.int32)] * 3,
        name="plan",
    )(counts, idx_flat, rank_flat)


def kernel(x, c, ctx, c_ctx, ln_in_g, ln_in_b, w_mod, b_mod, w_in, ssm_lam_re, ssm_lam_im, ssm_log_step, ssm_b_re, ssm_b_im, ssm_c_re, ssm_c_im, ssm_d, w_glu, b_glu, attn_sink, w_ssm_out, w_att_out, w_o, ln1_g, ln1_b, w_router, b_router, w_gate_up, b_gate_up, w_down, b_down, ln2_g, ln2_b):
    bsz, seq_len, d = x.shape
    n_ctx = ctx.shape[1]
    depth = w_mod.shape[0]
    assert depth == 1, "single-layer kernel"
    g_ssm, h_ssm = ssm_d.shape[1:]
    ssm_w = g_ssm * h_ssm
    attn_w = N_Q_HEADS * HEAD_DIM
    kv_w = N_KV_HEADS * HEAD_DIM
    kv2_w = 2 * kv_w
    nlb = ssm_w // LANES
    alpha = (2.0 * depth) ** 0.25
    t = bsz * seq_len
    assert bsz + 1 <= SUBLANES
    assert d == ROW_TILES * LANES and ROW_TILES == SUBLANES, "a token row must be exactly one (8, 128) tile"

    row2 = lambda a: a.reshape(1, -1)

    c_rows = jnp.concatenate([c, c_ctx[None], jnp.zeros((SUBLANES - bsz - 1, d), F32)], axis=0)
    mod = _mod_vectors(c_rows, w_mod[0], row2(b_mod[0]))
    mod3 = mod.reshape(SUBLANES, 1, 6 * d)

    assert w_in.shape[2] == ssm_w + attn_w + 2 * kv_w + 2 * d
    w_all = w_in[0].astype(BF16)

    pos = jnp.arange(seq_len)
    inv = ROPE_BASE ** (-jnp.arange(ROPE_PAIRS, dtype=F32) / ROPE_PAIRS)
    ang = jnp.concatenate([(pos // GRID_W).astype(F32)[:, None] * inv,
                           (pos % GRID_W).astype(F32)[:, None] * inv], axis=-1)
    cos_t = jnp.tile(jnp.cos(ang), (1, 2 * LANES // HEAD_DIM))
    sin_h = jnp.sin(ang)
    sin_t = jnp.tile(jnp.concatenate([-sin_h, sin_h], axis=-1), (1, LANES // HEAD_DIM))

    g_in, b_in = row2(ln_in_g), row2(ln_in_b)
    x2 = x.reshape(t, d)
    ctx2 = ctx.reshape(bsz * n_ctx, d)
    s_c, k_c, v_c = _inproj(ctx2, g_in, b_in, mod3, w_all, None, None, latent=False,
                            rows_per_batch=n_ctx, tm=n_ctx, sub=n_ctx, ctx_mod_row=bsz,
                            ssm_w=ssm_w, attn_w=attn_w, kv2_w=kv2_w)
    s_l, q_l, k_l, v_l, sgs, sga, h_l = _inproj(x2, g_in, b_in, mod3, w_all, cos_t, sin_t, latent=True,
                                           rows_per_batch=seq_len, tm=TILES.inproj_rows, sub=TILES.inproj_sub, ctx_mod_row=None,
                                           ssm_w=ssm_w, attn_w=attn_w, kv2_w=kv2_w)

    wf = _s5_weights(ssm_lam_re[0, 0], ssm_lam_im[0, 0], ssm_log_step[0, 0], ssm_b_re[0, 0], ssm_b_im[0, 0],
                     ssm_c_re[0, 0], ssm_c_im[0, 0], reverse=False)
    wb = _s5_weights(ssm_lam_re[0, 1], ssm_lam_im[0, 1], ssm_log_step[0, 1], ssm_b_re[0, 1], ssm_b_im[0, 1],
                     ssm_c_re[0, 1], ssm_c_im[0, 1], reverse=True)
    d_tile = jnp.tile(ssm_d[0].astype(F32).reshape(nlb, 1, LANES), (1, 1, S5_CHUNK))
    sw2 = wf[1].shape[-1]
    zero_state = jnp.zeros((nlb, bsz, sw2), F32)
    uc4 = s_c.reshape(nlb, bsz, n_ctx, LANES)
    ul4 = s_l.reshape(nlb, bsz, seq_len, LANES)
    _, sf0 = _s5_scan(uc4, zero_state, wf, d_tile, reverse=False, cc=n_ctx // S5_CHUNK, add_skip=False)
    _, sb0 = _s5_scan(uc4, zero_state, wb, d_tile, reverse=True, cc=n_ctx // S5_CHUNK, add_skip=False)
    yf4, _ = _s5_scan(ul4, sf0, wf, d_tile, reverse=False, cc=TILES.s5_chunks, add_skip=True)
    yb4, _ = _s5_scan(ul4, sb0, wb, d_tile, reverse=True, cc=TILES.s5_chunks, add_skip=False)
    yf = yf4.reshape(nlb, t, LANES)
    yb = yb4.reshape(nlb, t, LANES)

    o_att = _attention(q_l, k_l, v_l, k_c, v_c, attn_sink[0].astype(F32), bsz=bsz, seq_len=seq_len, n_ctx=n_ctx,
                       qblocks=TILES.attn_qblocks)

    h1, top_i, gates, rank, counts = _merge(
        yf, yb, o_att, sgs, sga, h_l, mod3, w_glu[0].astype(BF16), row2(b_glu[0]),
        w_ssm_out[0].astype(BF16), w_att_out[0].astype(BF16), w_o[0].astype(BF16), row2(ln1_g[0]), row2(ln1_b[0]),
        w_router[0], row2(b_router[0]), rows_per_batch=seq_len, tm=TILES.merge_rows, sub=TILES.merge_sub, alpha=alpha)

    rows, sub_rows = TILES.expert_rows, TILES.expert_sub
    n_rows = t * TOP_K
    pos, tile_id, exp_id, lo, hi, nxt = _plan(counts[0].astype(jnp.int32), top_i, rank, rows=rows)
    pos_flat = pos.reshape(n_rows)
    xrows = _dispatch(pos_flat, h1, mod3, rows_per_batch=seq_len, tm=TILES.moe_token_rows)
    yrows = _experts((tile_id, exp_id, lo, hi, nxt), xrows, w_gate_up[0], b_gate_up[0], w_down[0], b_down[0],
                     rows=rows, sub_rows=sub_rows)
    out = _combine(pos_flat, h1, gates, mod3, row2(ln2_g[0]), row2(ln2_b[0]), yrows,
                   rows_per_batch=seq_len, tm=TILES.moe_token_rows, alpha=alpha)
    return out.reshape(bsz, seq_len, d)
```

```python
import functools
import math
from typing import NamedTuple

import jax
import jax.numpy as jnp
import numpy as np
from jax import lax
from jax.experimental import pallas as pl
from jax.experimental.pallas import tpu as pltpu

F32 = jnp.float32
BF16 = jnp.bfloat16
HIGHEST = lax.Precision.HIGHEST

HEAD_DIM = 64
N_Q_HEADS = 8
N_KV_HEADS = 2
Q_PER_KV = N_Q_HEADS // N_KV_HEADS
WINDOW = 128
GRID_W = 64
ROPE_BASE = 10000.0
ROPE_PAIRS = HEAD_DIM // 4
TOP_K = 4
SWIGLU_LIMIT = 7.0
SWIGLU_ALPHA = 1.702
LN_EPS = 1e-5
NEG_INF = -1e30

LANES = 128
SUBLANES = 8
MXU_DIM = 256
ROW_TILES = 8
VMEM_LIMIT = 56 * 1024 * 1024
N_DMA_PRIORITIES = 2
DMA_LOOP_UNROLL = 8

class _Tiles(NamedTuple):
    inproj_rows: int = 1024
    inproj_sub: int = 256
    s5_chunks: int = 256
    attn_qblocks: int = 8
    merge_rows: int = 1024
    merge_sub: int = 512
    moe_token_rows: int = 256
    expert_rows: int = 512
    expert_sub: int = 256


TILES = _Tiles()

S5_CHUNK = 8
S5_ROW_PAD = SUBLANES


def _cparams(sem):
    return pltpu.CompilerParams(dimension_semantics=sem, vmem_limit_bytes=VMEM_LIMIT)


def _sigmoid(x):
    return 1.0 / (1.0 + jnp.exp(-x))


def _layer_norm(x, g, b):
    mu = jnp.mean(x, axis=-1, keepdims=True)
    xc = x - mu
    var = jnp.mean(xc * xc, axis=-1, keepdims=True)
    return xc * lax.rsqrt(var + LN_EPS) * g + b


def _dot(a, b):
    return jnp.dot(a, b, preferred_element_type=F32)


def _dot_3pass(a, b):
    a_hi = a.astype(BF16)
    b_hi = b.astype(BF16)
    a_lo = (a - a_hi.astype(F32)).astype(BF16)
    b_lo = (b - b_hi.astype(F32)).astype(BF16)
    return _dot(a_hi, b_hi) + (_dot(a_hi, b_lo) + _dot(a_lo, b_hi))


def _mod_kernel(c_ref, w_ref, b_ref, o_ref):
    c = c_ref[...]
    a = c * _sigmoid(c)
    o_ref[...] = jnp.dot(a, w_ref[...], preferred_element_type=F32, precision=HIGHEST) + b_ref[...]


def _mod_vectors(c_rows, w_mod, b_mod):
    d = c_rows.shape[1]
    n = w_mod.shape[1]
    return pl.pallas_call(
        _mod_kernel,
        grid=(n // d,),
        in_specs=[
            pl.BlockSpec((SUBLANES, d), lambda i: (0, 0)),
            pl.BlockSpec((d, d), lambda i: (0, i)),
            pl.BlockSpec((1, d), lambda i: (0, i)),
        ],
        out_specs=pl.BlockSpec((SUBLANES, d), lambda i: (0, i)),
        out_shape=jax.ShapeDtypeStruct((SUBLANES, n), F32),
        compiler_params=_cparams(("arbitrary",)),
        name="mod",
    )(c_rows, w_mod, b_mod)


def _rope(t, cos, sin):
    n = t.shape[1]
    reps = n // LANES
    c = jnp.concatenate([cos] * reps, axis=1) if reps > 1 else cos
    s = jnp.concatenate([sin] * reps, axis=1) if reps > 1 else sin
    half = HEAD_DIM // 2
    upper = pltpu.roll(t, n - half, axis=1)
    lower = pltpu.roll(t, half, axis=1)
    lane = lax.broadcasted_iota(jnp.int32, t.shape, 1)
    partner = jnp.where((lane & half) == 0, upper, lower)
    return t * c + partner * s


def _dup_heads(t):
    assert t.shape[1] == LANES == N_KV_HEADS * HEAD_DIM
    swapped = pltpu.roll(t, HEAD_DIM, axis=1)
    low = lax.broadcasted_iota(jnp.int32, t.shape, 1) < HEAD_DIM
    return jnp.concatenate([jnp.where(low, t, swapped), jnp.where(low, swapped, t)], axis=1)


def _inproj_kernel(*refs, latent, ssm_w, attn_w, kv2_w, d_model, sub):
    if latent:
        (x_ref, g_ref, b_ref, sh_ref, sc_ref, w_ref, cos_ref, sin_ref,
         s_ref, q_ref, k_ref, v_ref, gs_ref, ga_ref, h_ref) = refs
    else:
        x_ref, g_ref, b_ref, sh_ref, sc_ref, w_ref, s_ref, k_ref, v_ref = refs
    tm = x_ref.shape[0]
    parts = range(tm // sub)
    rs = [slice(p * sub, (p + 1) * sub) for p in parts]
    each = lambda f: [f(p) for p in parts]
    h = each(lambda p: _layer_norm(x_ref[rs[p], :], g_ref[...], b_ref[...]))
    u = each(lambda p: (h[p] * (1.0 + sc_ref[...]) + sh_ref[...]).astype(BF16))
    s = each(lambda p: _dot(u[p], w_ref[:, 0:ssm_w]))
    for p in parts:
        for lb in range(ssm_w // LANES):
            s_ref[lb, rs[p], :] = s[p][:, lb * LANES:(lb + 1) * LANES]
    if latent:
        q = each(lambda p: _dot(u[p], w_ref[:, ssm_w:ssm_w + attn_w]))
        q = each(lambda p: _rope(q[p], cos_ref[rs[p], :], sin_ref[rs[p], :]) * (HEAD_DIM ** -0.5))
        for p in parts:
            q_ref[rs[p], :] = q[p].astype(BF16)
    kv_w = kv2_w // 2
    k = each(lambda p: _dot(u[p], w_ref[:, ssm_w + attn_w:ssm_w + attn_w + kv_w]))
    if latent:
        k = each(lambda p: _rope(k[p], cos_ref[rs[p], :], sin_ref[rs[p], :]))
    v = each(lambda p: _dot(u[p], w_ref[:, ssm_w + attn_w + kv_w:ssm_w + attn_w + 2 * kv_w]))
    for p in parts:
        k_ref[rs[p], :] = _dup_heads(k[p]).astype(BF16)
        v_ref[rs[p], :] = _dup_heads(v[p]).astype(BF16)
    if latent:
        col = ssm_w + attn_w + kv2_w
        gs = each(lambda p: _dot(u[p], w_ref[:, col:col + d_model]))
        ga = each(lambda p: _dot(u[p], w_ref[:, col + d_model:col + 2 * d_model]))
        for p in parts:
            gs_ref[rs[p], :] = _sigmoid(gs[p]).astype(BF16)
            ga_ref[rs[p], :] = _sigmoid(ga[p]).astype(BF16)
            h_ref[rs[p], :] = h[p]


def _inproj(x2, ln_g, ln_b, mod3, w_all, cos_t, sin_t, *, latent, rows_per_batch, tm, sub, ctx_mod_row,
            ssm_w, attn_w, kv2_w):
    t, d = x2.shape
    tiles_per_batch = rows_per_batch // tm
    nlb = ssm_w // LANES

    def brow(i):
        return i // tiles_per_batch if ctx_mod_row is None else ctx_mod_row

    in_specs = [
        pl.BlockSpec((tm, d), lambda i: (i, 0)),
        pl.BlockSpec((1, d), lambda i: (0, 0)),
        pl.BlockSpec((1, d), lambda i: (0, 0)),
        pl.BlockSpec((None, 1, d), lambda i: (brow(i), 0, 0)),
        pl.BlockSpec((None, 1, d), lambda i: (brow(i), 0, 1)),
        pl.BlockSpec(w_all.shape, lambda i: (0, 0)),
    ]
    args = [x2, ln_g, ln_b, mod3, mod3, w_all]
    out_specs = [pl.BlockSpec((nlb, tm, LANES), lambda i: (0, i, 0))]
    out_shape = [jax.ShapeDtypeStruct((nlb, t, LANES), F32)]
    if latent:
        in_specs += [
            pl.BlockSpec((tm, LANES), lambda i: (i % tiles_per_batch, 0)),
            pl.BlockSpec((tm, LANES), lambda i: (i % tiles_per_batch, 0)),
        ]
        args += [cos_t, sin_t]
        out_specs.append(pl.BlockSpec((tm, attn_w), lambda i: (i, 0)))
        out_shape.append(jax.ShapeDtypeStruct((t, attn_w), BF16))
    out_specs += [pl.BlockSpec((tm, kv2_w), lambda i: (i, 0))] * 2
    out_shape += [jax.ShapeDtypeStruct((t, kv2_w), BF16)] * 2
    if latent:
        out_specs += [pl.BlockSpec((tm, d), lambda i: (i, 0))] * 3
        out_shape += [jax.ShapeDtypeStruct((t, d), BF16)] * 2 + [jax.ShapeDtypeStruct((t, d), F32)]
    return pl.pallas_call(
        functools.partial(_inproj_kernel, latent=latent, ssm_w=ssm_w, attn_w=attn_w, kv2_w=kv2_w,
                          d_model=d, sub=sub),
        grid=(t // tm,),
        in_specs=in_specs,
        out_specs=out_specs,
        out_shape=out_shape,
        compiler_params=_cparams(("arbitrary",)),
        name="inproj_latent" if latent else "inproj_ctx",
    )(*args)


def _lane_expand(tbl, n_copies, rows_per_group, cols_per_group):
    r, w = tbl.shape
    wide_w = n_copies * w
    log2 = lambda v: int(v).bit_length() - 1
    assert all(v == 1 << log2(v) for v in (w, rows_per_group, cols_per_group))
    sel = ((lax.broadcasted_iota(jnp.int32, (w, wide_w), 1) & (w - 1))
           == lax.broadcasted_iota(jnp.int32, (w, wide_w), 0)).astype(BF16)
    wide = jnp.dot(tbl.astype(BF16), sel, preferred_element_type=F32)
    keep = ((lax.broadcasted_iota(jnp.int32, (r, wide_w), 0) >> log2(rows_per_group))
            == (lax.broadcasted_iota(jnp.int32, (r, wide_w), 1) >> log2(cols_per_group)))
    return jnp.where(keep, wide, 0.0)


def _s5_assemble_kernel(m_ref, ws_ref, wo_ref, kin_out, ws_out, wo_out, *, reverse):
    ch, _, hh = m_ref.shape
    p = ws_ref.shape[-1]
    gpb = LANES // hh
    sw = gpb * p
    bd = [_lane_expand(m_ref[j], gpb, hh, hh).astype(BF16) for j in range(ch)]
    zero = jnp.zeros((LANES, LANES), BF16)
    for k in range(ch):
        for i in range(ch):
            lag = (k - i) if reverse else (i - k)
            kin_out[k * LANES:(k + 1) * LANES, i * LANES:(i + 1) * LANES] = bd[lag] if lag >= 0 else zero
    for ri in range(2):
        for k in range(ch):
            ws_out[k * LANES:(k + 1) * LANES, ri * sw:(ri + 1) * sw] = (
                _lane_expand(ws_ref[ri, k], gpb, hh, p).astype(BF16))
        for i in range(ch):
            wo_out[ri * sw:(ri + 1) * sw, i * LANES:(i + 1) * LANES] = (
                _lane_expand(wo_ref[ri, i], gpb, p, hh).astype(BF16))


def _s5_assemble(m_tbl, ws_tbl, wo_tbl, *, reverse):
    ch, nlb, _, hh = m_tbl.shape
    p = ws_tbl.shape[-1]
    sw = (LANES // hh) * p
    width = ch * LANES
    return pl.pallas_call(
        functools.partial(_s5_assemble_kernel, reverse=reverse),
        grid=(nlb,),
        in_specs=[
            pl.BlockSpec((ch, None, LANES, hh), lambda l: (0, l, 0, 0)),
            pl.BlockSpec((2, ch, None, LANES, p), lambda l: (0, 0, l, 0, 0)),
            pl.BlockSpec((2, ch, None, sw, hh), lambda l: (0, 0, l, 0, 0)),
        ],
        out_specs=[
            pl.BlockSpec((None, width, width), lambda l: (l, 0, 0)),
            pl.BlockSpec((None, width, 2 * sw), lambda l: (l, 0, 0)),
            pl.BlockSpec((None, 2 * sw, width), lambda l: (l, 0, 0)),
        ],
        out_shape=[
            jax.ShapeDtypeStruct((nlb, width, width), BF16),
            jax.ShapeDtypeStruct((nlb, width, 2 * sw), BF16),
            jax.ShapeDtypeStruct((nlb, 2 * sw, width), BF16),
        ],
        compiler_params=_cparams(("arbitrary",)),
        name="s5_assemble_bwd" if reverse else "s5_assemble_fwd",
    )(m_tbl, ws_tbl, wo_tbl)


def _s5_weights(lam_re, lam_im, log_step, b_re, b_im, c_re, c_im, reverse):
    ch = S5_CHUNK
    g, p = lam_re.shape
    hh = b_re.shape[-1]
    gpb = LANES // hh
    nlb = g // gpb
    lr, li = lam_re.astype(F32), lam_im.astype(F32)
    dt = jnp.exp(log_step.astype(F32))[:, None]
    jj = jnp.arange(ch + 1, dtype=F32)[:, None, None]
    mag = jnp.exp(jj * lr * dt)
    pr = mag * jnp.cos(jj * li * dt)
    pi = mag * jnp.sin(jj * li * dt)
    ar, ai = pr[1], pi[1]
    den = lr * lr + li * li
    cr = ((ar - 1) * lr + ai * li) / den
    ci = (ai * lr - (ar - 1) * li) / den
    br, bi = b_re.astype(F32), b_im.astype(F32)
    bbr = cr[..., None] * br - ci[..., None] * bi
    bbi = cr[..., None] * bi + ci[..., None] * br
    ccr, cci = c_re.astype(F32), c_im.astype(F32)
    bbr_t = jnp.swapaxes(bbr, 1, 2)
    bbi_t = jnp.swapaxes(bbi, 1, 2)
    ccr_t = jnp.swapaxes(ccr, 1, 2)
    cci_t = jnp.swapaxes(cci, 1, 2)

    er = ccr[None] * pr[:ch, :, None, :] - cci[None] * pi[:ch, :, None, :]
    ei = ccr[None] * pi[:ch, :, None, :] + cci[None] * pr[:ch, :, None, :]
    m = jnp.sum(er[:, :, None, :, :] * bbr_t[None, :, :, None, :]
                - ei[:, :, None, :, :] * bbi_t[None, :, :, None, :], axis=-1)
    m_tbl = m.reshape(ch, nlb, LANES, hh)
    rk = np.arange(ch) if reverse else (ch - 1 - np.arange(ch))
    apr, api = pr[rk][:, :, None, :], pi[rk][:, :, None, :]
    sr = apr * bbr_t[None] - api * bbi_t[None]
    si = apr * bbi_t[None] + api * bbr_t[None]
    ws_tbl = jnp.stack([sr, si], axis=0).reshape(2, ch, nlb, LANES, p)
    ex = (ch - np.arange(ch)) if reverse else (np.arange(ch) + 1)
    epr, epi = pr[ex][:, :, :, None], pi[ex][:, :, :, None]
    wo_r = ccr_t[None] * epr - cci_t[None] * epi
    wo_i = -(ccr_t[None] * epi + cci_t[None] * epr)
    wo_tbl = jnp.stack([wo_r, wo_i], axis=0).reshape(2, ch, nlb, gpb * p, hh)
    kin, ws, wo = _s5_assemble(m_tbl, ws_tbl, wo_tbl, reverse=reverse)
    a_chunk = jnp.stack([pr[ch].reshape(nlb, gpb * p), pi[ch].reshape(nlb, gpb * p)], axis=1)
    return kin, ws, wo, a_chunk


def _s5_kernel(u_ref, s0_ref, a_ref, kin_ref, ws_ref, wo_ref, d_ref, y_ref, sfin_ref,
               z_scr, sin_scr, carry_scr, *, reverse, nb, cc, add_skip):
    j = pl.program_id(1)

    @pl.when(j == 0)
    def _():
        carry_scr[...] = s0_ref[...]

    ch = S5_CHUNK
    width = ch * LANES
    sw = a_ref.shape[-1]
    ns = sw // LANES
    u = jnp.concatenate(
        [jnp.concatenate([u_ref[b, pl.ds(k, cc, stride=ch), :] for b in range(nb)], axis=0)
         for k in range(ch)], axis=1)
    ub = u.astype(BF16)
    z = _dot(ub, ws_ref[...])
    ccp = cc + S5_ROW_PAD
    for s in range(2 * ns):
        for b in range(nb):
            z_scr[s, b * ccp:b * ccp + cc, :] = z[b * cc:(b + 1) * cc, s * LANES:(s + 1) * LANES]
    ar = [a_ref[0:1, s * LANES:(s + 1) * LANES] for s in range(ns)]
    ai = [a_ref[1:2, s * LANES:(s + 1) * LANES] for s in range(ns)]

    def body(c, st):
        cidx = (cc - 1 - c) if reverse else c
        rows = pl.ds(cidx, nb, stride=ccp)
        new = [None] * (2 * ns)
        for s in range(ns):
            sr, si = st[s], st[ns + s]
            sin_scr[s, rows, :] = sr
            sin_scr[ns + s, rows, :] = si
            new[s] = ar[s] * sr - ai[s] * si + z_scr[s, rows, :]
            new[ns + s] = ar[s] * si + ai[s] * sr + z_scr[ns + s, rows, :]
        return tuple(new)

    st0 = tuple(carry_scr[:, s * LANES:(s + 1) * LANES] for s in range(2 * ns))
    st = lax.fori_loop(0, cc, body, st0)
    s_fin = jnp.concatenate(st, axis=1)
    carry_scr[...] = s_fin
    sfin_ref[...] = s_fin
    s_in = jnp.concatenate(
        [jnp.concatenate([sin_scr[s, b * ccp:b * ccp + cc, :] for b in range(nb)], axis=0)
         for s in range(2 * ns)], axis=1)
    intra = []
    for c0 in range(0, width, MXU_DIM):
        ks = slice(c0, width) if reverse else slice(0, c0 + MXU_DIM)
        intra.append(_dot(ub[:, ks], kin_ref[ks, c0:c0 + MXU_DIM]))
    y = jnp.concatenate(intra, axis=1) + _dot(s_in.astype(BF16), wo_ref[...])
    if add_skip:
        y = y + u * d_ref[...]
    for b in range(nb):
        for i in range(ch):
            y_ref[b, pl.ds(i, cc, stride=ch), :] = y[b * cc:(b + 1) * cc, i * LANES:(i + 1) * LANES]


def _s5_scan(u4, s0, weights, d_tile, *, reverse, cc, add_skip):
    kin, ws, wo, a_chunk = weights
    nlb, nb, n_steps, _ = u4.shape
    width = S5_CHUNK * LANES
    sw2 = ws.shape[-1]
    nj = n_steps // (cc * S5_CHUNK)

    def jm(j):
        return (nj - 1 - j) if reverse else j

    return pl.pallas_call(
        functools.partial(_s5_kernel, reverse=reverse, nb=nb, cc=cc, add_skip=add_skip),
        grid=(nlb, nj),
        in_specs=[
            pl.BlockSpec((None, nb, cc * S5_CHUNK, LANES), lambda l, j: (l, 0, jm(j), 0)),
            pl.BlockSpec((None, nb, sw2), lambda l, j: (l, 0, 0)),
            pl.BlockSpec((None, 2, sw2 // 2), lambda l, j: (l, 0, 0)),
            pl.BlockSpec((None, width, width), lambda l, j: (l, 0, 0)),
            pl.BlockSpec((None, width, sw2), lambda l, j: (l, 0, 0)),
            pl.BlockSpec((None, sw2, width), lambda l, j: (l, 0, 0)),
            pl.BlockSpec((None, 1, width), lambda l, j: (l, 0, 0)),
        ],
        out_specs=[
            pl.BlockSpec((None, nb, cc * S5_CHUNK, LANES), lambda l, j: (l, 0, jm(j), 0)),
            pl.BlockSpec((None, nb, sw2), lambda l, j: (l, 0, 0)),
        ],
        out_shape=[
            jax.ShapeDtypeStruct(u4.shape, F32),
            jax.ShapeDtypeStruct((nlb, nb, sw2), F32),
        ],
        scratch_shapes=[
            pltpu.VMEM((sw2 // LANES, nb * (cc + S5_ROW_PAD), LANES), F32),
            pltpu.VMEM((sw2 // LANES, nb * (cc + S5_ROW_PAD), LANES), F32),
            pltpu.VMEM((nb, sw2), F32),
        ],
        compiler_params=_cparams(("arbitrary", "arbitrary")),
        name="s5_bwd" if reverse else "s5_fwd",
    )(u4, s0, a_chunk, kin, ws, wo, d_tile)


def _attn_kernel(sink_ref, q_ref, kp_ref, ko_ref, kn_ref, vp_ref, vo_ref, vn_ref, kc_ref, vc_ref,
                 o_ref, *, nblk, blk, qblocks):
    n = pl.program_id(1)
    n_ctx = kc_ref.shape[0]
    n_loc = 3 * blk
    nrow = Q_PER_KV * blk
    row = lax.broadcasted_iota(jnp.int32, (nrow, n_loc), 0)
    qi = row & (blk - 1)
    kj = lax.broadcasted_iota(jnp.int32, (nrow, n_loc), 1)
    rel = kj - qi
    in_window = (rel >= blk - WINDOW) & (rel <= blk + WINDOW)
    lane = lax.broadcasted_iota(jnp.int32, (blk, LANES), 1)
    low_half = lane < HEAD_DIM
    row1 = lax.broadcasted_iota(jnp.int32, (nrow, 1), 0)

    key1 = lax.broadcasted_iota(jnp.int32, (1, n_loc), 1)

    def edge_bias(j):
        gblk = n * qblocks + j
        lo = jnp.where(gblk == 0, blk, 0)
        hi = jnp.where(gblk == nblk - 1, 2 * blk, n_loc)
        return jnp.where((key1 >= lo) & (key1 < hi), 0.0, NEG_INF)

    bias = [edge_bias(j) for j in range(qblocks)]
    parts = [(j, g) for j in range(qblocks) for g in range(N_KV_HEADS)]
    each = lambda f: [f(j, g) for j, g in parts]
    at = {pg: i for i, pg in enumerate(parts)}

    def key_blocks(p_ref, own_ref, n_ref, c_ref, j, g):
        cs = slice(g * LANES, (g + 1) * LANES)
        blocks = [p_ref[:, cs]] + [own_ref[i * blk:(i + 1) * blk, cs] for i in range(qblocks)] + [n_ref[:, cs]]
        return jnp.concatenate(blocks[j:j + 3] + [c_ref[:, cs]], axis=0)

    kcat = each(lambda j, g: key_blocks(kp_ref, ko_ref, kn_ref, kc_ref, j, g))
    vcat = each(lambda j, g: key_blocks(vp_ref, vo_ref, vn_ref, vc_ref, j, g))

    def stacked_q(j, g):
        qms = []
        for hq in range(Q_PER_KV):
            h = g * Q_PER_KV + hq
            qb = q_ref[j * blk:(j + 1) * blk, (h // 2) * LANES:(h // 2 + 1) * LANES]
            qms.append(jnp.where(low_half if h % 2 == 0 else jnp.logical_not(low_half), qb, jnp.zeros_like(qb)))
        return jnp.concatenate(qms, axis=0)

    def stacked_sink(j, g):
        snk = jnp.zeros((nrow, 1), F32)
        for hq in range(Q_PER_KV):
            snk = jnp.where((row1 >= hq * blk) & (row1 < (hq + 1) * blk), sink_ref[g * Q_PER_KV + hq], snk)
        return snk

    qs = each(stacked_q)
    snk = each(stacked_sink)
    s = each(lambda j, g: lax.dot_general(qs[at[j, g]], kcat[at[j, g]], (((1,), (1,)), ((), ())),
                                          preferred_element_type=F32))
    s = each(lambda j, g: jnp.concatenate(
        [jnp.where(in_window, s[at[j, g]][:, :n_loc], NEG_INF) + bias[j], s[at[j, g]][:, n_loc:]], axis=1))
    mx = each(lambda j, g: jnp.maximum(jnp.max(s[at[j, g]], axis=1, keepdims=True), snk[at[j, g]]))
    p = each(lambda j, g: jnp.exp(s[at[j, g]] - mx[at[j, g]]))
    den = each(lambda j, g: jnp.sum(p[at[j, g]], axis=1, keepdims=True) + jnp.exp(snk[at[j, g]] - mx[at[j, g]]))
    o = each(lambda j, g: _dot(p[at[j, g]].astype(BF16), vcat[at[j, g]]) / den[at[j, g]])
    for j in range(qblocks):
        outs = [o[at[j, g]][hq * blk:(hq + 1) * blk] for g in range(N_KV_HEADS) for hq in range(Q_PER_KV)]
        blocks = [jnp.where(low_half, outs[2 * m], outs[2 * m + 1]) for m in range(N_Q_HEADS // 2)]
        o_ref[j * blk:(j + 1) * blk, :] = jnp.concatenate(blocks, axis=1).astype(BF16)


def _attention(q, kd, vd, kcd, vcd, sink, *, bsz, seq_len, n_ctx, qblocks):
    blk = WINDOW
    nblk = seq_len // blk
    nstep = nblk // qblocks
    aw = q.shape[1]
    kw = kd.shape[1]

    def qmap(b, n):
        return (b * nstep + n, 0)

    def pmap(b, n):
        return (b * nblk + jnp.maximum(n * qblocks - 1, 0), 0)

    def nmap(b, n):
        return (b * nblk + jnp.minimum((n + 1) * qblocks, nblk - 1), 0)

    edge = lambda f: pl.BlockSpec((blk, kw), f)
    own = pl.BlockSpec((qblocks * blk, kw), qmap)
    return pl.pallas_call(
        functools.partial(_attn_kernel, nblk=nblk, blk=blk, qblocks=qblocks),
        grid=(bsz, nstep),
        in_specs=[
            pl.BlockSpec(memory_space=pltpu.SMEM),
            pl.BlockSpec((qblocks * blk, aw), qmap),
            edge(pmap), own, edge(nmap),
            edge(pmap), own, edge(nmap),
            pl.BlockSpec((n_ctx, kw), lambda b, n: (b, 0)),
            pl.BlockSpec((n_ctx, kw), lambda b, n: (b, 0)),
        ],
        out_specs=pl.BlockSpec((qblocks * blk, aw), qmap),
        out_shape=jax.ShapeDtypeStruct(q.shape, BF16),
        compiler_params=_cparams(("arbitrary", "arbitrary")),
        name="attn",
    )(sink, q, kd, kd, kd, vd, vd, vd, kcd, vcd)


def _gelu_tanh(x):
    return 0.5 * x * (1.0 + jnp.tanh(math.sqrt(2.0 / math.pi) * (x + 0.044715 * (x * x * x))))


def _flatten_slots(v, max_value):
    tm, nk = v.shape
    per_row = LANES // nk
    log2 = lambda x: int(x).bit_length() - 1
    assert nk == 1 << log2(nk) and tm % per_row == 0
    spread = ((lax.broadcasted_iota(jnp.int32, (nk, LANES), 1) & (nk - 1))
              == lax.broadcasted_iota(jnp.int32, (nk, LANES), 0)).astype(BF16)
    t_id = lax.broadcasted_iota(jnp.int32, (tm, LANES), 0)
    c_id = lax.broadcasted_iota(jnp.int32, (tm, LANES), 1)
    own = (c_id >> log2(nk)) == (t_id & (per_row - 1))
    group = ((lax.broadcasted_iota(jnp.int32, (tm // per_row, tm), 1) >> log2(per_row))
             == lax.broadcasted_iota(jnp.int32, (tm // per_row, tm), 0)).astype(BF16)
    digit_bits = 7
    out = jnp.zeros((tm // per_row, LANES), jnp.int32)
    for shift in range(0, max(int(max_value).bit_length(), 1), digit_bits):
        digit = ((v >> shift) & ((1 << digit_bits) - 1)).astype(F32).astype(BF16)
        wide = jnp.dot(digit, spread, preferred_element_type=F32)
        wide = jnp.where(own, wide, 0.0).astype(BF16)
        out = out + (jnp.dot(group, wide, preferred_element_type=F32).astype(jnp.int32) << shift)
    return out


def _merge_kernel(yf_ref, yb_ref, o_ref, gs_ref, ga_ref, h_ref, g1_ref, sh2_ref, sc2_ref,
                  wglu_ref, bglu_ref, wso_ref, wao_ref, wo_ref, l1g_ref, l1b_ref, wr_ref, br_ref,
                  h1_ref, idx_ref, gate_ref, rank_ref, cnt_ref, cnt_scr, *, alpha, n_exp, sub, n_tokens):
    i = pl.program_id(0)

    @pl.when(i == 0)
    def _():
        cnt_scr[...] = jnp.zeros_like(cnt_scr)

    nlb = yf_ref.shape[0]
    tm = h_ref.shape[0]
    lane = lax.broadcasted_iota(jnp.int32, (sub, n_exp), 1)
    lane_k = lax.broadcasted_iota(jnp.int32, (sub, TOP_K), 1)
    ri = lax.broadcasted_iota(jnp.int32, (sub, sub), 0)
    ci = lax.broadcasted_iota(jnp.int32, (sub, sub), 1)
    tri = jnp.where(ci < ri, 1.0, 0.0).astype(BF16)
    slots = sub * TOP_K // LANES
    cnt = cnt_scr[...]
    parts = range(tm // sub)
    rs = [slice(p * sub, (p + 1) * sub) for p in parts]
    each = lambda f: [f(p) for p in parts]
    y = each(lambda p: jnp.concatenate([yf_ref[lb, rs[p], :] + yb_ref[lb, rs[p], :] for lb in range(nlb)], axis=1))
    z = each(lambda p: _gelu_tanh(y[p]))
    zg = each(lambda p: _dot(z[p].astype(BF16), wglu_ref[...]) + bglu_ref[...])
    z = each(lambda p: z[p] * _sigmoid(zg[p]))
    ms = each(lambda p: _dot(z[p].astype(BF16), wso_ref[...]))
    ma = each(lambda p: _dot(o_ref[rs[p], :], wao_ref[...]))
    m = each(lambda p: gs_ref[rs[p], :].astype(F32) * ms[p] + ga_ref[rs[p], :].astype(F32) * ma[p])
    mix = each(lambda p: _dot(m[p].astype(BF16), wo_ref[...]))
    h1 = each(lambda p: _layer_norm(alpha * h_ref[rs[p], :] + g1_ref[...] * mix[p], l1g_ref[...], l1b_ref[...]))
    for p in parts:
        h1_ref[rs[p], :] = h1[p]
    xm = each(lambda p: h1[p] * (1.0 + sc2_ref[...]) + sh2_ref[...])
    work = each(lambda p: _dot_3pass(xm[p], wr_ref[...]) + br_ref[...])
    vals, sels = [], []
    for _ in range(TOP_K):
        mx = each(lambda p: jnp.max(work[p], axis=1, keepdims=True))
        sel = each(lambda p: jnp.min(jnp.where(work[p] == mx[p], lane, n_exp), axis=1, keepdims=True))
        work = each(lambda p: jnp.where(lane == sel[p], -jnp.inf, work[p]))
        vals.append(mx)
        sels.append(sel)
    exps = [each(lambda p: jnp.exp(vals[k][p] - vals[0][p])) for k in range(TOP_K)]
    den = each(lambda p: exps[0][p] + exps[1][p] + exps[2][p] + exps[3][p])
    onehot = each(lambda p: sum((lane == sels[k][p]).astype(F32) for k in range(TOP_K)))
    prefix = each(lambda p: _dot(tri, onehot[p].astype(BF16)))
    for p in parts:
        rank = prefix[p] + cnt
        idx_o = jnp.zeros((sub, TOP_K), jnp.int32)
        gate_o = jnp.zeros((sub, TOP_K), F32)
        rank_o = jnp.zeros((sub, TOP_K), jnp.int32)
        for k in range(TOP_K):
            rk = jnp.sum(jnp.where(lane == sels[k][p], rank, 0.0), axis=1, keepdims=True).astype(jnp.int32)
            idx_o = jnp.where(lane_k == k, sels[k][p], idx_o)
            gate_o = jnp.where(lane_k == k, exps[k][p] / den[p], gate_o)
            rank_o = jnp.where(lane_k == k, rk, rank_o)
        gate_ref[rs[p], :] = gate_o
        idx_ref[p * slots:(p + 1) * slots, :] = _flatten_slots(idx_o, n_exp - 1)
        rank_ref[p * slots:(p + 1) * slots, :] = _flatten_slots(rank_o, n_tokens - 1)
        cnt = cnt + jnp.sum(onehot[p], axis=0, keepdims=True)
    cnt_scr[...] = cnt
    cnt_ref[...] = cnt


def _merge(yf, yb, o_att, sgs, sga, h_in, mod3, w_glu, b_glu, w_so, w_ao, w_o, l1g, l1b,
           w_r, b_r, *, rows_per_batch, tm, sub, alpha):
    t, d = h_in.shape
    nlb = yf.shape[0]
    sw = nlb * LANES
    n_exp = w_r.shape[1]
    tpb = rows_per_batch // tm
    row = lambda i: (i, 0)
    const = lambda i: (0, 0)

    def modspec(chunk):
        return pl.BlockSpec((None, 1, d), lambda i: (i // tpb, 0, chunk))

    return pl.pallas_call(
        functools.partial(_merge_kernel, alpha=alpha, n_exp=n_exp, sub=sub, n_tokens=t),
        grid=(t // tm,),
        in_specs=[
            pl.BlockSpec((nlb, tm, LANES), lambda i: (0, i, 0)),
            pl.BlockSpec((nlb, tm, LANES), lambda i: (0, i, 0)),
            pl.BlockSpec((tm, o_att.shape[1]), row),
            pl.BlockSpec((tm, d), row),
            pl.BlockSpec((tm, d), row),
            pl.BlockSpec((tm, d), row),
            modspec(2), modspec(3), modspec(4),
            pl.BlockSpec(w_glu.shape, const),
            pl.BlockSpec((1, sw), const),
            pl.BlockSpec(w_so.shape, const),
            pl.BlockSpec(w_ao.shape, const),
            pl.BlockSpec(w_o.shape, const),
            pl.BlockSpec((1, d), const),
            pl.BlockSpec((1, d), const),
            pl.BlockSpec(w_r.shape, const),
            pl.BlockSpec((1, n_exp), const),
        ],
        out_specs=[
            pl.BlockSpec((tm, d), row),
            pl.BlockSpec((tm * TOP_K // LANES, LANES), row),
            pl.BlockSpec((tm, TOP_K), row),
            pl.BlockSpec((tm * TOP_K // LANES, LANES), row),
            pl.BlockSpec((1, n_exp), const),
        ],
        out_shape=[
            jax.ShapeDtypeStruct((t, d), F32),
            jax.ShapeDtypeStruct((t * TOP_K // LANES, LANES), jnp.int32),
            jax.ShapeDtypeStruct((t, TOP_K), F32),
            jax.ShapeDtypeStruct((t * TOP_K // LANES, LANES), jnp.int32),
            jax.ShapeDtypeStruct((1, n_exp), F32),
        ],
        scratch_shapes=[pltpu.VMEM((1, n_exp), F32)],
        compiler_params=_cparams(("arbitrary",)),
        name="merge",
    )(yf, yb, o_att, sgs, sga, h_in, mod3, mod3, mod3, w_glu, b_glu, w_so, w_ao, w_o,
      l1g, l1b, w_r, b_r)


def _row_tile(ref, r):
    return ref.at[pl.ds(pl.multiple_of(r * ROW_TILES, ROW_TILES), ROW_TILES)]


def _row_tile_at(ref, first_line):
    return ref.at[pl.ds(pl.multiple_of(first_line, ROW_TILES), ROW_TILES)]


def _store_row_tiles(ref, val):
    rows = val.shape[0]
    for l in range(ROW_TILES):
        ref[pl.ds(l, rows, stride=ROW_TILES), :] = val[:, l * LANES:(l + 1) * LANES]


def _load_row_tiles(ref, row0, rows):
    return jnp.concatenate(
        [ref[pl.ds(row0 * ROW_TILES + l, rows, stride=ROW_TILES), :] for l in range(ROW_TILES)], axis=1)


def _row_copy_out(src, dst, pos_ref, sem, t, k):
    return pltpu.make_async_copy(_row_tile(src, t), _row_tile_at(dst, pos_ref[t * TOP_K + k]), sem)


def _dispatch_kernel(pos_ref, prev_pos_ref, h1_ref, sh2_ref, sc2_ref, xrows_ref, xm_scr, sems):
    i = pl.program_id(0)
    n = pl.num_programs(0)
    tm = h1_ref.shape[0]
    slot = i % 2
    xm = h1_ref[...] * (1.0 + sc2_ref[...]) + sh2_ref[...]
    _store_row_tiles(xm_scr.at[slot], xm)

    def copies(slot_, idx_ref, op):
        def body(t, carry):
            for k in range(TOP_K):
                op(_row_copy_out(xm_scr.at[slot_], xrows_ref, idx_ref, sems.at[slot_], t, k), k)
            return carry
        lax.fori_loop(0, tm, body, 0, unroll=DMA_LOOP_UNROLL)

    start = lambda cp, k: cp.start(priority=k % N_DMA_PRIORITIES)
    wait = lambda cp, k: cp.wait()
    for s in range(2):
        @pl.when(slot == s)
        def _():
            copies(s, pos_ref, start)

    for s in range(2):
        @pl.when((slot == 1 - s) & (i > 0))
        def _():
            copies(s, prev_pos_ref, wait)

        @pl.when((slot == s) & (i == n - 1))
        def _():
            copies(s, pos_ref, wait)


def _dispatch(pos_flat, h1, mod3, *, rows_per_batch, tm):
    t, d = h1.shape
    tpb = rows_per_batch // tm
    return pl.pallas_call(
        _dispatch_kernel,
        grid=(t // tm,),
        in_specs=[
            pl.BlockSpec((tm * TOP_K,), lambda i: (i,), memory_space=pltpu.SMEM),
            pl.BlockSpec((tm * TOP_K,), lambda i: (jnp.maximum(i - 1, 0),), memory_space=pltpu.SMEM),
            pl.BlockSpec((tm, d), lambda i: (i, 0)),
            pl.BlockSpec((None, 1, d), lambda i: (i // tpb, 0, 3)),
            pl.BlockSpec((None, 1, d), lambda i: (i // tpb, 0, 4)),
        ],
        out_specs=pl.BlockSpec(memory_space=pl.ANY),
        out_shape=jax.ShapeDtypeStruct((t * TOP_K * ROW_TILES, LANES), F32),
        scratch_shapes=[pltpu.VMEM((2, tm * ROW_TILES, LANES), F32), pltpu.SemaphoreType.DMA((2,))],
        compiler_params=_cparams(("arbitrary",)),
        name="dispatch",
    )(pos_flat, pos_flat, h1, mod3, mod3)


def _experts_kernel(tile_ref, exp_ref, lo_ref, hi_ref, nxt_ref, x_ref, wgu_hbm, bgu_ref, wd_hbm, bd_ref, y_ref,
                    wgu_f32, wd_f32, wgu_scr, wd_scr, sems, *, rows, sub_rows):
    w = pl.program_id(0)
    prev = jnp.maximum(w - 1, 0)
    e_new = (w == 0) | (exp_ref[w] != exp_ref[prev])
    t_new = (w == 0) | (tile_ref[w] != tile_ref[prev])
    lo = lo_ref[w]
    hi = hi_ref[w]
    f = wd_scr.shape[0]

    def weight_copies(e):
        return (pltpu.make_async_copy(wgu_hbm.at[e], wgu_f32, sems.at[0]),
                pltpu.make_async_copy(wd_hbm.at[e], wd_f32, sems.at[1]))

    @pl.when(w == 0)
    def _():
        for cp in weight_copies(exp_ref[0]):
            cp.start()

    @pl.when(e_new)
    def _():
        for cp in weight_copies(exp_ref[w]):
            cp.wait()
        wgu_scr[...] = wgu_f32[...].astype(BF16)
        wd_scr[...] = wd_f32[...].astype(BF16)

        @pl.when(nxt_ref[w] >= 0)
        def _():
            for cp in weight_copies(nxt_ref[w]):
                cp.start()

    row0 = tile_ref[w] * rows
    whole = (lo <= row0) & (hi >= row0 + rows)

    @pl.when(t_new & jnp.logical_not(whole))
    def _():
        y_ref[...] = jnp.zeros_like(y_ref)

    def expert_mlp(x):
        gu = _dot(x.astype(BF16), wgu_scr[...]) + bgu_ref[...]
        glu = jnp.minimum(gu[:, :f], SWIGLU_LIMIT)
        lin = jnp.clip(gu[:, f:], -SWIGLU_LIMIT, SWIGLU_LIMIT)
        act = glu * _sigmoid(SWIGLU_ALPHA * glu) * (lin + 1.0)
        return _dot(act.astype(BF16), wd_scr[...]) + bd_ref[...]

    def sub_block(sb):
        s_lo = row0 + sb * sub_rows
        s_hi = s_lo + sub_rows
        y_sub = y_ref.at[pl.ds(sb * sub_rows * ROW_TILES, sub_rows * ROW_TILES)]

        @pl.when((hi > s_lo) & (lo < s_hi))
        def _():
            y = expert_mlp(_load_row_tiles(x_ref, sb * sub_rows, sub_rows))
            sub_whole = (lo <= s_lo) & (hi >= s_hi)

            @pl.when(sub_whole)
            def _():
                _store_row_tiles(y_sub, y)

            @pl.when(jnp.logical_not(sub_whole))
            def _():
                r = s_lo + lax.broadcasted_iota(jnp.int32, (sub_rows, 1), 0)
                _store_row_tiles(y_sub, jnp.where((r >= lo) & (r < hi), y, _load_row_tiles(y_sub, 0, sub_rows)))

    @pl.when(whole)
    def _():
        _store_row_tiles(y_ref, expert_mlp(_load_row_tiles(x_ref, 0, rows)))

    @pl.when(jnp.logical_not(whole))
    def _():
        for sb in range(rows // sub_rows):
            sub_block(sb)


def _experts(work, xrows, w_gu, b_gu, w_d, b_d, *, rows, sub_rows):
    tile_id, exp_id, lo, hi, nxt = work
    n_exp, d, f2 = w_gu.shape
    f = w_d.shape[1]
    grid_spec = pltpu.PrefetchScalarGridSpec(
        num_scalar_prefetch=5,
        grid=(tile_id.shape[0],),
        in_specs=[
            pl.BlockSpec((rows * ROW_TILES, LANES), lambda w, ti, ex, lo, hi, nx: (ti[w], 0)),
            pl.BlockSpec(memory_space=pl.ANY),
            pl.BlockSpec((None, 1, f2), lambda w, ti, ex, lo, hi, nx: (ex[w], 0, 0)),
            pl.BlockSpec(memory_space=pl.ANY),
            pl.BlockSpec((None, 1, d), lambda w, ti, ex, lo, hi, nx: (ex[w], 0, 0)),
        ],
        out_specs=pl.BlockSpec((rows * ROW_TILES, LANES), lambda w, ti, ex, lo, hi, nx: (ti[w], 0)),
        scratch_shapes=[
            pltpu.VMEM((d, f2), F32), pltpu.VMEM((f, d), F32),
            pltpu.VMEM((d, f2), BF16), pltpu.VMEM((f, d), BF16),
            pltpu.SemaphoreType.DMA((2,)),
        ],
    )
    return pl.pallas_call(
        functools.partial(_experts_kernel, rows=rows, sub_rows=sub_rows),
        grid_spec=grid_spec,
        out_shape=jax.ShapeDtypeStruct(xrows.shape, F32),
        compiler_params=_cparams(("arbitrary",)),
        name="experts",
    )(tile_id, exp_id, lo, hi, nxt, xrows, w_gu, b_gu.reshape(n_exp, 1, f2), w_d, b_d.reshape(n_exp, 1, d))


def _row_copy_in(src, dst, pos_ref, sem, t, k, tm):
    return pltpu.make_async_copy(_row_tile_at(src, pos_ref[t * TOP_K + k]), _row_tile(dst, k * tm + t), sem)


def _combine_kernel(pos_ref, next_pos_ref, h1_ref, gate_ref, g2_ref, lg_ref, lb_ref, yrows_ref, o_ref,
                    buf, sems, *, alpha):
    i = pl.program_id(0)
    n = pl.num_programs(0)
    tm = h1_ref.shape[0]
    slot = i % 2

    def copies(slot_, idx_ref, op):
        def body(t, carry):
            for k in range(TOP_K):
                op(_row_copy_in(yrows_ref, buf.at[slot_], idx_ref, sems.at[slot_], t, k, tm), k)
            return carry
        lax.fori_loop(0, tm, body, 0, unroll=DMA_LOOP_UNROLL)

    start = lambda cp, k: cp.start(priority=k % N_DMA_PRIORITIES)
    wait = lambda cp, k: cp.wait()

    @pl.when(i == 0)
    def _():
        copies(0, pos_ref, start)

    for s in range(2):
        @pl.when((slot == 1 - s) & (i + 1 < n))
        def _():
            copies(s, next_pos_ref, start)

    def reduce(s):
        copies(s, pos_ref, wait)
        gates = gate_ref[...]
        ffn = gates[:, 0:1] * _load_row_tiles(buf.at[s], 0, tm)
        for k in range(1, TOP_K):
            ffn = ffn + gates[:, k:k + 1] * _load_row_tiles(buf.at[s], k * tm, tm)
        o_ref[...] = _layer_norm(alpha * h1_ref[...] + g2_ref[...] * ffn, lg_ref[...], lb_ref[...])

    for s in range(2):
        @pl.when(slot == s)
        def _():
            reduce(s)


def _combine(pos_flat, h1, gates, mod3, l2g, l2b, yrows, *, rows_per_batch, tm, alpha):
    t, d = h1.shape
    tpb = rows_per_batch // tm
    n_tiles = t // tm
    return pl.pallas_call(
        functools.partial(_combine_kernel, alpha=alpha),
        grid=(n_tiles,),
        in_specs=[
            pl.BlockSpec((tm * TOP_K,), lambda i: (i,), memory_space=pltpu.SMEM),
            pl.BlockSpec((tm * TOP_K,), lambda i: (jnp.minimum(i + 1, n_tiles - 1),), memory_space=pltpu.SMEM),
            pl.BlockSpec((tm, d), lambda i: (i, 0)),
            pl.BlockSpec((tm, TOP_K), lambda i: (i, 0)),
            pl.BlockSpec((None, 1, d), lambda i: (i // tpb, 0, 5)),
            pl.BlockSpec((1, d), lambda i: (0, 0)),
            pl.BlockSpec((1, d), lambda i: (0, 0)),
            pl.BlockSpec(memory_space=pl.ANY),
        ],
        out_specs=pl.BlockSpec((tm, d), lambda i: (i, 0)),
        out_shape=jax.ShapeDtypeStruct((t, d), F32),
        scratch_shapes=[pltpu.VMEM((2, TOP_K * tm * ROW_TILES, LANES), F32), pltpu.SemaphoreType.DMA((2,))],
        compiler_params=_cparams(("arbitrary",)),
        name="combine",
    )(pos_flat, pos_flat, h1, gates, mod3, l2g, l2b, yrows)


def _plan_kernel(cnt_ref, idx_ref, rank_ref, pos_ref, tile_out, exp_out, lo_out, hi_out, nxt_out,
                 start_s, end_s, next_s, *, rows, n_tiles):
    n_exp = cnt_ref.shape[0]
    n_work = tile_out.shape[0]

    def cumulate(e, acc):
        start_s[e] = acc
        end_s[e] = acc + cnt_ref[e]
        return acc + cnt_ref[e]

    lax.fori_loop(0, n_exp, cumulate, 0)

    def next_nonempty(i, cur):
        e = n_exp - 1 - i
        next_s[e] = cur
        return jnp.where(cnt_ref[e] > 0, e, cur)

    first = lax.fori_loop(0, n_exp, next_nonempty, -1)

    def item(w, carry):
        tile, e = carry
        live = tile < n_tiles
        tl = jnp.minimum(tile, n_tiles - 1)
        tile_lo = tl * rows
        tile_hi = tile_lo + rows
        lo = jnp.maximum(start_s[e], tile_lo)
        hi = jnp.where(live, jnp.minimum(end_s[e], tile_hi), lo)
        tile_out[w] = tl
        exp_out[w] = e
        lo_out[w] = lo
        hi_out[w] = hi
        nxt_out[w] = next_s[e]
        tile_done = live & (end_s[e] >= tile_hi)
        expert_done = live & (end_s[e] <= tile_hi) & (next_s[e] >= 0)
        return jnp.where(tile_done, tile + 1, tile), jnp.where(expert_done, next_s[e], e)

    lax.fori_loop(0, n_work, item, (0, first))

    idx = idx_ref[...]
    base = jnp.zeros(idx.shape, jnp.int32)
    for e in range(n_exp):
        base = jnp.where(idx == e, start_s[e], base)
    pos_ref[...] = (base + rank_ref[...]) * ROW_TILES


def _plan(counts, idx_flat, rank_flat, *, rows):
    n_exp = counts.shape[0]
    n_rows = idx_flat.shape[0] * idx_flat.shape[1]
    n_tiles = n_rows // rows
    n_work = n_tiles + n_exp - 1
    smem = pl.BlockSpec(memory_space=pltpu.SMEM)
    vmem = pl.BlockSpec(memory_space=pltpu.VMEM)
    return pl.pallas_call(
        functools.partial(_plan_kernel, rows=rows, n_tiles=n_tiles),
        in_specs=[smem, vmem, vmem],
        out_specs=[vmem] + [smem] * 5,
        out_shape=[jax.ShapeDtypeStruct(idx_flat.shape, jnp.int32)]
        + [jax.ShapeDtypeStruct((n_work,), jnp.int32)] * 5,
        scratch_shapes=[pltpu.SMEM((n_exp,), jnp.int32)] * 3,
        name="plan",
    )(counts, idx_flat, rank_flat)


def kernel(x, c, ctx, c_ctx, ln_in_g, ln_in_b, w_mod, b_mod, w_in, ssm_lam_re, ssm_lam_im, ssm_log_step, ssm_b_re, ssm_b_im, ssm_c_re, ssm_c_im, ssm_d, w_glu, b_glu, attn_sink, w_ssm_out, w_att_out, w_o, ln1_g, ln1_b, w_router, b_router, w_gate_up, b_gate_up, w_down, b_down, ln2_g, ln2_b):
    bsz, seq_len, d = x.shape
    n_ctx = ctx.shape[1]
    depth = w_mod.shape[0]
    assert depth == 1, "single-layer kernel"
    g_ssm, h_ssm = ssm_d.shape[1:]
    ssm_w = g_ssm * h_ssm
    attn_w = N_Q_HEADS * HEAD_DIM
    kv_w = N_KV_HEADS * HEAD_DIM
    kv2_w = 2 * kv_w
    nlb = ssm_w // LANES
    alpha = (2.0 * depth) ** 0.25
    t = bsz * seq_len
    assert bsz + 1 <= SUBLANES
    assert d == ROW_TILES * LANES and ROW_TILES == SUBLANES, "a token row must be exactly one (8, 128) tile"

    row2 = lambda a: a.reshape(1, -1)

    c_rows = jnp.concatenate([c, c_ctx[None], jnp.zeros((SUBLANES - bsz - 1, d), F32)], axis=0)
    mod = _mod_vectors(c_rows, w_mod[0], row2(b_mod[0]))
    mod3 = mod.reshape(SUBLANES, 1, 6 * d)

    assert w_in.shape[2] == ssm_w + attn_w + 2 * kv_w + 2 * d
    w_all = w_in[0].astype(BF16)

    pos = jnp.arange(seq_len)
    inv = ROPE_BASE ** (-jnp.arange(ROPE_PAIRS, dtype=F32) / ROPE_PAIRS)
    ang = jnp.concatenate([(pos // GRID_W).astype(F32)[:, None] * inv,
                           (pos % GRID_W).astype(F32)[:, None] * inv], axis=-1)
    cos_t = jnp.tile(jnp.cos(ang), (1, 2 * LANES // HEAD_DIM))
    sin_h = jnp.sin(ang)
    sin_t = jnp.tile(jnp.concatenate([-sin_h, sin_h], axis=-1), (1, LANES // HEAD_DIM))

    g_in, b_in = row2(ln_in_g), row2(ln_in_b)
    x2 = x.reshape(t, d)
    ctx2 = ctx.reshape(bsz * n_ctx, d)
    s_c, k_c, v_c = _inproj(ctx2, g_in, b_in, mod3, w_all, None, None, latent=False,
                            rows_per_batch=n_ctx, tm=n_ctx, sub=n_ctx, ctx_mod_row=bsz,
                            ssm_w=ssm_w, attn_w=attn_w, kv2_w=kv2_w)
    s_l, q_l, k_l, v_l, sgs, sga, h_l = _inproj(x2, g_in, b_in, mod3, w_all, cos_t, sin_t, latent=True,
                                           rows_per_batch=seq_len, tm=TILES.inproj_rows, sub=TILES.inproj_sub, ctx_mod_row=None,
                                           ssm_w=ssm_w, attn_w=attn_w, kv2_w=kv2_w)

    wf = _s5_weights(ssm_lam_re[0, 0], ssm_lam_im[0, 0], ssm_log_step[0, 0], ssm_b_re[0, 0], ssm_b_im[0, 0],
                     ssm_c_re[0, 0], ssm_c_im[0, 0], reverse=False)
    wb = _s5_weights(ssm_lam_re[0, 1], ssm_lam_im[0, 1], ssm_log_step[0, 1], ssm_b_re[0, 1], ssm_b_im[0, 1],
                     ssm_c_re[0, 1], ssm_c_im[0, 1], reverse=True)
    d_tile = jnp.tile(ssm_d[0].astype(F32).reshape(nlb, 1, LANES), (1, 1, S5_CHUNK))
    sw2 = wf[1].shape[-1]
    zero_state = jnp.zeros((nlb, bsz, sw2), F32)
    uc4 = s_c.reshape(nlb, bsz, n_ctx, LANES)
    ul4 = s_l.reshape(nlb, bsz, seq_len, LANES)
    _, sf0 = _s5_scan(uc4, zero_state, wf, d_tile, reverse=False, cc=n_ctx // S5_CHUNK, add_skip=False)
    _, sb0 = _s5_scan(uc4, zero_state, wb, d_tile, reverse=True, cc=n_ctx // S5_CHUNK, add_skip=False)
    yf4, _ = _s5_scan(ul4, sf0, wf, d_tile, reverse=False, cc=TILES.s5_chunks, add_skip=True)
    yb4, _ = _s5_scan(ul4, sb0, wb, d_tile, reverse=True, cc=TILES.s5_chunks, add_skip=False)
    yf = yf4.reshape(nlb, t, LANES)
    yb = yb4.reshape(nlb, t, LANES)

    o_att = _attention(q_l, k_l, v_l, k_c, v_c, attn_sink[0].astype(F32), bsz=bsz, seq_len=seq_len, n_ctx=n_ctx,
                       qblocks=TILES.attn_qblocks)

    h1, top_i, gates, rank, counts = _merge(
        yf, yb, o_att, sgs, sga, h_l, mod3, w_glu[0].astype(BF16), row2(b_glu[0]),
        w_ssm_out[0].astype(BF16), w_att_out[0].astype(BF16), w_o[0].astype(BF16), row2(ln1_g[0]), row2(ln1_b[0]),
        w_router[0], row2(b_router[0]), rows_per_batch=seq_len, tm=TILES.merge_rows, sub=TILES.merge_sub, alpha=alpha)

    rows, sub_rows = TILES.expert_rows, TILES.expert_sub
    n_rows = t * TOP_K
    pos, tile_id, exp_id, lo, hi, nxt = _plan(counts[0].astype(jnp.int32), top_i, rank, rows=rows)
    pos_flat = pos.reshape(n_rows)
    xrows = _dispatch(pos_flat, h1, mod3, rows_per_batch=seq_len, tm=TILES.moe_token_rows)
    yrows = _experts((tile_id, exp_id, lo, hi, nxt), xrows, w_gate_up[0], b_gate_up[0], w_down[0], b_down[0],
                     rows=rows, sub_rows=sub_rows)
    out = _combine(pos_flat, h1, gates, mod3, row2(ln2_g[0]), row2(ln2_b[0]), yrows,
                   rows_per_batch=seq_len, tm=TILES.moe_token_rows, alpha=alpha)
    return out.reshape(bsz, seq_len, d)
```

```python
import functools
import math
from typing import NamedTuple

import jax
import jax.numpy as jnp
import numpy as np
from jax import lax
from jax.experimental import pallas as pl
from jax.experimental.pallas import tpu as pltpu

F32 = jnp.float32
BF16 = jnp.bfloat16
HIGHEST = lax.Precision.HIGHEST

HEAD_DIM = 64
N_Q_HEADS = 8
N_KV_HEADS = 2
Q_PER_KV = N_Q_HEADS // N_KV_HEADS
WINDOW = 128
GRID_W = 64
ROPE_BASE = 10000.0
ROPE_PAIRS = HEAD_DIM // 4
TOP_K = 4
SWIGLU_LIMIT = 7.0
SWIGLU_ALPHA = 1.702
LN_EPS = 1e-5
NEG_INF = -1e30

LANES = 128
SUBLANES = 8
MXU_DIM = 256
ROW_TILES = 8
VMEM_LIMIT = 56 * 1024 * 1024
N_DMA_PRIORITIES = 2
DMA_LOOP_UNROLL = 8

class _Tiles(NamedTuple):
    inproj_rows: int = 1024
    inproj_sub: int = 256
    s5_chunks: int = 256
    attn_qblocks: int = 8
    merge_rows: int = 1024
    merge_sub: int = 512
    moe_token_rows: int = 256
    expert_rows: int = 512
    expert_sub: int = 256


TILES = _Tiles()

S5_CHUNK = 8
S5_ROW_PAD = SUBLANES


def _cparams(sem):
    return pltpu.CompilerParams(dimension_semantics=sem, vmem_limit_bytes=VMEM_LIMIT)


def _sigmoid(x):
    return 1.0 / (1.0 + jnp.exp(-x))


def _layer_norm(x, g, b):
    mu = jnp.mean(x, axis=-1, keepdims=True)
    xc = x - mu
    var = jnp.mean(xc * xc, axis=-1, keepdims=True)
    return xc * lax.rsqrt(var + LN_EPS) * g + b


def _dot(a, b):
    return jnp.dot(a, b, preferred_element_type=F32)


def _dot_3pass(a, b):
    a_hi = a.astype(BF16)
    b_hi = b.astype(BF16)
    a_lo = (a - a_hi.astype(F32)).astype(BF16)
    b_lo = (b - b_hi.astype(F32)).astype(BF16)
    return _dot(a_hi, b_hi) + (_dot(a_hi, b_lo) + _dot(a_lo, b_hi))


def _mod_kernel(c_ref, w_ref, b_ref, o_ref):
    c = c_ref[...]
    a = c * _sigmoid(c)
    o_ref[...] = jnp.dot(a, w_ref[...], preferred_element_type=F32, precision=HIGHEST) + b_ref[...]


def _mod_vectors(c_rows, w_mod, b_mod):
    d = c_rows.shape[1]
    n = w_mod.shape[1]
    return pl.pallas_call(
        _mod_kernel,
        grid=(n // d,),
        in_specs=[
            pl.BlockSpec((SUBLANES, d), lambda i: (0, 0)),
            pl.BlockSpec((d, d), lambda i: (0, i)),
            pl.BlockSpec((1, d), lambda i: (0, i)),
        ],
        out_specs=pl.BlockSpec((SUBLANES, d), lambda i: (0, i)),
        out_shape=jax.ShapeDtypeStruct((SUBLANES, n), F32),
        compiler_params=_cparams(("arbitrary",)),
        name="mod",
    )(c_rows, w_mod, b_mod)


def _rope(t, cos, sin):
    n = t.shape[1]
    reps = n // LANES
    c = jnp.concatenate([cos] * reps, axis=1) if reps > 1 else cos
    s = jnp.concatenate([sin] * reps, axis=1) if reps > 1 else sin
    half = HEAD_DIM // 2
    upper = pltpu.roll(t, n - half, axis=1)
    lower = pltpu.roll(t, half, axis=1)
    lane = lax.broadcasted_iota(jnp.int32, t.shape, 1)
    partner = jnp.where((lane & half) == 0, upper, lower)
    return t * c + partner * s


def _dup_heads(t):
    assert t.shape[1] == LANES == N_KV_HEADS * HEAD_DIM
    swapped = pltpu.roll(t, HEAD_DIM, axis=1)
    low = lax.broadcasted_iota(jnp.int32, t.shape, 1) < HEAD_DIM
    return jnp.concatenate([jnp.where(low, t, swapped), jnp.where(low, swapped, t)], axis=1)


def _inproj_kernel(*refs, latent, ssm_w, attn_w, kv2_w, d_model, sub):
    if latent:
        (x_ref, g_ref, b_ref, sh_ref, sc_ref, w_ref, cos_ref, sin_ref,
         s_ref, q_ref, k_ref, v_ref, gs_ref, ga_ref, h_ref) = refs
    else:
        x_ref, g_ref, b_ref, sh_ref, sc_ref, w_ref, s_ref, k_ref, v_ref = refs
    tm = x_ref.shape[0]
    parts = range(tm // sub)
    rs = [slice(p * sub, (p + 1) * sub) for p in parts]
    each = lambda f: [f(p) for p in parts]
    h = each(lambda p: _layer_norm(x_ref[rs[p], :], g_ref[...], b_ref[...]))
    u = each(lambda p: (h[p] * (1.0 + sc_ref[...]) + sh_ref[...]).astype(BF16))
    s = each(lambda p: _dot(u[p], w_ref[:, 0:ssm_w]))
    for p in parts:
        for lb in range(ssm_w // LANES):
            s_ref[lb, rs[p], :] = s[p][:, lb * LANES:(lb + 1) * LANES]
    if latent:
        q = each(lambda p: _dot(u[p], w_ref[:, ssm_w:ssm_w + attn_w]))
        q = each(lambda p: _rope(q[p], cos_ref[rs[p], :], sin_ref[rs[p], :]) * (HEAD_DIM ** -0.5))
        for p in parts:
            q_ref[rs[p], :] = q[p].astype(BF16)
    kv_w = kv2_w // 2
    k = each(lambda p: _dot(u[p], w_ref[:, ssm_w + attn_w:ssm_w + attn_w + kv_w]))
    if latent:
        k = each(lambda p: _rope(k[p], cos_ref[rs[p], :], sin_ref[rs[p], :]))
    v = each(lambda p: _dot(u[p], w_ref[:, ssm_w + attn_w + kv_w:ssm_w + attn_w + 2 * kv_w]))
    for p in parts:
        k_ref[rs[p], :] = _dup_heads(k[p]).astype(BF16)
        v_ref[rs[p], :] = _dup_heads(v[p]).astype(BF16)
    if latent:
        col = ssm_w + attn_w + kv2_w
        gs = each(lambda p: _dot(u[p], w_ref[:, col:col + d_model]))
        ga = each(lambda p: _dot(u[p], w_ref[:, col + d_model:col + 2 * d_model]))
        for p in parts:
            gs_ref[rs[p], :] = _sigmoid(gs[p]).astype(BF16)
            ga_ref[rs[p], :] = _sigmoid(ga[p]).astype(BF16)
            h_ref[rs[p], :] = h[p]


def _inproj(x2, ln_g, ln_b, mod3, w_all, cos_t, sin_t, *, latent, rows_per_batch, tm, sub, ctx_mod_row,
            ssm_w, attn_w, kv2_w):
    t, d = x2.shape
    tiles_per_batch = rows_per_batch // tm
    nlb = ssm_w // LANES

    def brow(i):
        return i // tiles_per_batch if ctx_mod_row is None else ctx_mod_row

    in_specs = [
        pl.BlockSpec((tm, d), lambda i: (i, 0)),
        pl.BlockSpec((1, d), lambda i: (0, 0)),
        pl.BlockSpec((1, d), lambda i: (0, 0)),
        pl.BlockSpec((None, 1, d), lambda i: (brow(i), 0, 0)),
        pl.BlockSpec((None, 1, d), lambda i: (brow(i), 0, 1)),
        pl.BlockSpec(w_all.shape, lambda i: (0, 0)),
    ]
    args = [x2, ln_g, ln_b, mod3, mod3, w_all]
    out_specs = [pl.BlockSpec((nlb, tm, LANES), lambda i: (0, i, 0))]
    out_shape = [jax.ShapeDtypeStruct((nlb, t, LANES), F32)]
    if latent:
        in_specs += [
            pl.BlockSpec((tm, LANES), lambda i: (i % tiles_per_batch, 0)),
            pl.BlockSpec((tm, LANES), lambda i: (i % tiles_per_batch, 0)),
        ]
        args += [cos_t, sin_t]
        out_specs.append(pl.BlockSpec((tm, attn_w), lambda i: (i, 0)))
        out_shape.append(jax.ShapeDtypeStruct((t, attn_w), BF16))
    out_specs += [pl.BlockSpec((tm, kv2_w), lambda i: (i, 0))] * 2
    out_shape += [jax.ShapeDtypeStruct((t, kv2_w), BF16)] * 2
    if latent:
        out_specs += [pl.BlockSpec((tm, d), lambda i: (i, 0))] * 3
        out_shape += [jax.ShapeDtypeStruct((t, d), BF16)] * 2 + [jax.ShapeDtypeStruct((t, d), F32)]
    return pl.pallas_call(
        functools.partial(_inproj_kernel, latent=latent, ssm_w=ssm_w, attn_w=attn_w, kv2_w=kv2_w,
                          d_model=d, sub=sub),
        grid=(t // tm,),
        in_specs=in_specs,
        out_specs=out_specs,
        out_shape=out_shape,
        compiler_params=_cparams(("arbitrary",)),
        name="inproj_latent" if latent else "inproj_ctx",
    )(*args)


def _lane_expand(tbl, n_copies, rows_per_group, cols_per_group):
    r, w = tbl.shape
    wide_w = n_copies * w
    log2 = lambda v: int(v).bit_length() - 1
    assert all(v == 1 << log2(v) for v in (w, rows_per_group, cols_per_group))
    sel = ((lax.broadcasted_iota(jnp.int32, (w, wide_w), 1) & (w - 1))
           == lax.broadcasted_iota(jnp.int32, (w, wide_w), 0)).astype(BF16)
    wide = jnp.dot(tbl.astype(BF16), sel, preferred_element_type=F32)
    keep = ((lax.broadcasted_iota(jnp.int32, (r, wide_w), 0) >> log2(rows_per_group))
            == (lax.broadcasted_iota(jnp.int32, (r, wide_w), 1) >> log2(cols_per_group)))
    return jnp.where(keep, wide, 0.0)


def _s5_assemble_kernel(m_ref, ws_ref, wo_ref, kin_out, ws_out, wo_out, *, reverse):
    ch, _, hh = m_ref.shape
    p = ws_ref.shape[-1]
    gpb = LANES // hh
    sw = gpb * p
    bd = [_lane_expand(m_ref[j], gpb, hh, hh).astype(BF16) for j in range(ch)]
    zero = jnp.zeros((LANES, LANES), BF16)
    for k in range(ch):
        for i in range(ch):
            lag = (k - i) if reverse else (i - k)
            kin_out[k * LANES:(k + 1) * LANES, i * LANES:(i + 1) * LANES] = bd[lag] if lag >= 0 else zero
    for ri in range(2):
        for k in range(ch):
            ws_out[k * LANES:(k + 1) * LANES, ri * sw:(ri + 1) * sw] = (
                _lane_expand(ws_ref[ri, k], gpb, hh, p).astype(BF16))
        for i in range(ch):
            wo_out[ri * sw:(ri + 1) * sw, i * LANES:(i + 1) * LANES] = (
                _lane_expand(wo_ref[ri, i], gpb, p, hh).astype(BF16))


def _s5_assemble(m_tbl, ws_tbl, wo_tbl, *, reverse):
    ch, nlb, _, hh = m_tbl.shape
    p = ws_tbl.shape[-1]
    sw = (LANES // hh) * p
    width = ch * LANES
    return pl.pallas_call(
        functools.partial(_s5_assemble_kernel, reverse=reverse),
        grid=(nlb,),
        in_specs=[
            pl.BlockSpec((ch, None, LANES, hh), lambda l: (0, l, 0, 0)),
            pl.BlockSpec((2, ch, None, LANES, p), lambda l: (0, 0, l, 0, 0)),
            pl.BlockSpec((2, ch, None, sw, hh), lambda l: (0, 0, l, 0, 0)),
        ],
        out_specs=[
            pl.BlockSpec((None, width, width), lambda l: (l, 0, 0)),
            pl.BlockSpec((None, width, 2 * sw), lambda l: (l, 0, 0)),
            pl.BlockSpec((None, 2 * sw, width), lambda l: (l, 0, 0)),
        ],
        out_shape=[
            jax.ShapeDtypeStruct((nlb, width, width), BF16),
            jax.ShapeDtypeStruct((nlb, width, 2 * sw), BF16),
            jax.ShapeDtypeStruct((nlb, 2 * sw, width), BF16),
        ],
        compiler_params=_cparams(("arbitrary",)),
        name="s5_assemble_bwd" if reverse else "s5_assemble_fwd",
    )(m_tbl, ws_tbl, wo_tbl)


def _s5_weights(lam_re, lam_im, log_step, b_re, b_im, c_re, c_im, reverse):
    ch = S5_CHUNK
    g, p = lam_re.shape
    hh = b_re.shape[-1]
    gpb = LANES // hh
    nlb = g // gpb
    lr, li = lam_re.astype(F32), lam_im.astype(F32)
    dt = jnp.exp(log_step.astype(F32))[:, None]
    jj = jnp.arange(ch + 1, dtype=F32)[:, None, None]
    mag = jnp.exp(jj * lr * dt)
    pr = mag * jnp.cos(jj * li * dt)
    pi = mag * jnp.sin(jj * li * dt)
    ar, ai = pr[1], pi[1]
    den = lr * lr + li * li
    cr = ((ar - 1) * lr + ai * li) / den
    ci = (ai * lr - (ar - 1) * li) / den
    br, bi = b_re.astype(F32), b_im.astype(F32)
    bbr = cr[..., None] * br - ci[..., None] * bi
    bbi = cr[..., None] * bi + ci[..., None] * br
    ccr, cci = c_re.astype(F32), c_im.astype(F32)
    bbr_t = jnp.swapaxes(bbr, 1, 2)
    bbi_t = jnp.swapaxes(bbi, 1, 2)
    ccr_t = jnp.swapaxes(ccr, 1, 2)
    cci_t = jnp.swapaxes(cci, 1, 2)

    er = ccr[None] * pr[:ch, :, None, :] - cci[None] * pi[:ch, :, None, :]
    ei = ccr[None] * pi[:ch, :, None, :] + cci[None] * pr[:ch, :, None, :]
    m = jnp.sum(er[:, :, None, :, :] * bbr_t[None, :, :, None, :]
                - ei[:, :, None, :, :] * bbi_t[None, :, :, None, :], axis=-1)
    m_tbl = m.reshape(ch, nlb, LANES, hh)
    rk = np.arange(ch) if reverse else (ch - 1 - np.arange(ch))
    apr, api = pr[rk][:, :, None, :], pi[rk][:, :, None, :]
    sr = apr * bbr_t[None] - api * bbi_t[None]
    si = apr * bbi_t[None] + api * bbr_t[None]
    ws_tbl = jnp.stack([sr, si], axis=0).reshape(2, ch, nlb, LANES, p)
    ex = (ch - np.arange(ch)) if reverse else (np.arange(ch) + 1)
    epr, epi = pr[ex][:, :, :, None], pi[ex][:, :, :, None]
    wo_r = ccr_t[None] * epr - cci_t[None] * epi
    wo_i = -(ccr_t[None] * epi + cci_t[None] * epr)
    wo_tbl = jnp.stack([wo_r, wo_i], axis=0).reshape(2, ch, nlb, gpb * p, hh)
    kin, ws, wo = _s5_assemble(m_tbl, ws_tbl, wo_tbl, reverse=reverse)
    a_chunk = jnp.stack([pr[ch].reshape(nlb, gpb * p), pi[ch].reshape(nlb, gpb * p)], axis=1)
    return kin, ws, wo, a_chunk


def _s5_kernel(u_ref, s0_ref, a_ref, kin_ref, ws_ref, wo_ref, d_ref, y_ref, sfin_ref,
               z_scr, sin_scr, carry_scr, *, reverse, nb, cc, add_skip):
    j = pl.program_id(1)

    @pl.when(j == 0)
    def _():
        carry_scr[...] = s0_ref[...]

    ch = S5_CHUNK
    width = ch * LANES
    sw = a_ref.shape[-1]
    ns = sw // LANES
    u = jnp.concatenate(
        [jnp.concatenate([u_ref[b, pl.ds(k, cc, stride=ch), :] for b in range(nb)], axis=0)
         for k in range(ch)], axis=1)
    ub = u.astype(BF16)
    z = _dot(ub, ws_ref[...])
    ccp = cc + S5_ROW_PAD
    for s in range(2 * ns):
        for b in range(nb):
            z_scr[s, b * ccp:b * ccp + cc, :] = z[b * cc:(b + 1) * cc, s * LANES:(s + 1) * LANES]
    ar = [a_ref[0:1, s * LANES:(s + 1) * LANES] for s in range(ns)]
    ai = [a_ref[1:2, s * LANES:(s + 1) * LANES] for s in range(ns)]

    def body(c, st):
        cidx = (cc - 1 - c) if reverse else c
        rows = pl.ds(cidx, nb, stride=ccp)
        new = [None] * (2 * ns)
        for s in range(ns):
            sr, si = st[s], st[ns + s]
            sin_scr[s, rows, :] = sr
            sin_scr[ns + s, rows, :] = si
            new[s] = ar[s] * sr - ai[s] * si + z_scr[s, rows, :]
            new[ns + s] = ar[s] * si + ai[s] * sr + z_scr[ns + s, rows, :]
        return tuple(new)

    st0 = tuple(carry_scr[:, s * LANES:(s + 1) * LANES] for s in range(2 * ns))
    st = lax.fori_loop(0, cc, body, st0)
    s_fin = jnp.concatenate(st, axis=1)
    carry_scr[...] = s_fin
    sfin_ref[...] = s_fin
    s_in = jnp.concatenate(
        [jnp.concatenate([sin_scr[s, b * ccp:b * ccp + cc, :] for b in range(nb)], axis=0)
         for s in range(2 * ns)], axis=1)
    intra = []
    for c0 in range(0, width, MXU_DIM):
        ks = slice(c0, width) if reverse else slice(0, c0 + MXU_DIM)
        intra.append(_dot(ub[:, ks], kin_ref[ks, c0:c0 + MXU_DIM]))
    y = jnp.concatenate(intra, axis=1) + _dot(s_in.astype(BF16), wo_ref[...])
    if add_skip:
        y = y + u * d_ref[...]
    for b in range(nb):
        for i in range(ch):
            y_ref[b, pl.ds(i, cc, stride=ch), :] = y[b * cc:(b + 1) * cc, i * LANES:(i + 1) * LANES]


def _s5_scan(u4, s0, weights, d_tile, *, reverse, cc, add_skip):
    kin, ws, wo, a_chunk = weights
    nlb, nb, n_steps, _ = u4.shape
    width = S5_CHUNK * LANES
    sw2 = ws.shape[-1]
    nj = n_steps // (cc * S5_CHUNK)

    def jm(j):
        return (nj - 1 - j) if reverse else j

    return pl.pallas_call(
        functools.partial(_s5_kernel, reverse=reverse, nb=nb, cc=cc, add_skip=add_skip),
        grid=(nlb, nj),
        in_specs=[
            pl.BlockSpec((None, nb, cc * S5_CHUNK, LANES), lambda l, j: (l, 0, jm(j), 0)),
            pl.BlockSpec((None, nb, sw2), lambda l, j: (l, 0, 0)),
            pl.BlockSpec((None, 2, sw2 // 2), lambda l, j: (l, 0, 0)),
            pl.BlockSpec((None, width, width), lambda l, j: (l, 0, 0)),
            pl.BlockSpec((None, width, sw2), lambda l, j: (l, 0, 0)),
            pl.BlockSpec((None, sw2, width), lambda l, j: (l, 0, 0)),
            pl.BlockSpec((None, 1, width), lambda l, j: (l, 0, 0)),
        ],
        out_specs=[
            pl.BlockSpec((None, nb, cc * S5_CHUNK, LANES), lambda l, j: (l, 0, jm(j), 0)),
            pl.BlockSpec((None, nb, sw2), lambda l, j: (l, 0, 0)),
        ],
        out_shape=[
            jax.ShapeDtypeStruct(u4.shape, F32),
            jax.ShapeDtypeStruct((nlb, nb, sw2), F32),
        ],
        scratch_shapes=[
            pltpu.VMEM((sw2 // LANES, nb * (cc + S5_ROW_PAD), LANES), F32),
            pltpu.VMEM((sw2 // LANES, nb * (cc + S5_ROW_PAD), LANES), F32),
            pltpu.VMEM((nb, sw2), F32),
        ],
        compiler_params=_cparams(("arbitrary", "arbitrary")),
        name="s5_bwd" if reverse else "s5_fwd",
    )(u4, s0, a_chunk, kin, ws, wo, d_tile)


def _attn_kernel(sink_ref, q_ref, kp_ref, ko_ref, kn_ref, vp_ref, vo_ref, vn_ref, kc_ref, vc_ref,
                 o_ref, *, nblk, blk, qblocks):
    n = pl.program_id(1)
    n_ctx = kc_ref.shape[0]
    n_loc = 3 * blk
    nrow = Q_PER_KV * blk
    row = lax.broadcasted_iota(jnp.int32, (nrow, n_loc), 0)
    qi = row & (blk - 1)
    kj = lax.broadcasted_iota(jnp.int32, (nrow, n_loc), 1)
    rel = kj - qi
    in_window = (rel >= blk - WINDOW) & (rel <= blk + WINDOW)
    lane = lax.broadcasted_iota(jnp.int32, (blk, LANES), 1)
    low_half = lane < HEAD_DIM
    row1 = lax.broadcasted_iota(jnp.int32, (nrow, 1), 0)

    key1 = lax.broadcasted_iota(jnp.int32, (1, n_loc), 1)

    def edge_bias(j):
        gblk = n * qblocks + j
        lo = jnp.where(gblk == 0, blk, 0)
        hi = jnp.where(gblk == nblk - 1, 2 * blk, n_loc)
        return jnp.where((key1 >= lo) & (key1 < hi), 0.0, NEG_INF)

    bias = [edge_bias(j) for j in range(qblocks)]
    parts = [(j, g) for j in range(qblocks) for g in range(N_KV_HEADS)]
    each = lambda f: [f(j, g) for j, g in parts]
    at = {pg: i for i, pg in enumerate(parts)}

    def key_blocks(p_ref, own_ref, n_ref, c_ref, j, g):
        cs = slice(g * LANES, (g + 1) * LANES)
        blocks = [p_ref[:, cs]] + [own_ref[i * blk:(i + 1) * blk, cs] for i in range(qblocks)] + [n_ref[:, cs]]
        return jnp.concatenate(blocks[j:j + 3] + [c_ref[:, cs]], axis=0)

    kcat = each(lambda j, g: key_blocks(kp_ref, ko_ref, kn_ref, kc_ref, j, g))
    vcat = each(lambda j, g: key_blocks(vp_ref, vo_ref, vn_ref, vc_ref, j, g))

    def stacked_q(j, g):
        qms = []
        for hq in range(Q_PER_KV):
            h = g * Q_PER_KV + hq
            qb = q_ref[j * blk:(j + 1) * blk, (h // 2) * LANES:(h // 2 + 1) * LANES]
            qms.append(jnp.where(low_half if h % 2 == 0 else jnp.logical_not(low_half), qb, jnp.zeros_like(qb)))
        return jnp.concatenate(qms, axis=0)

    def stacked_sink(j, g):
        snk = jnp.zeros((nrow, 1), F32)
        for hq in range(Q_PER_KV):
            snk = jnp.where((row1 >= hq * blk) & (row1 < (hq + 1) * blk), sink_ref[g * Q_PER_KV + hq], snk)
        return snk

    qs = each(stacked_q)
    snk = each(stacked_sink)
    s = each(lambda j, g: lax.dot_general(qs[at[j, g]], kcat[at[j, g]], (((1,), (1,)), ((), ())),
                                          preferred_element_type=F32))
    s = each(lambda j, g: jnp.concatenate(
        [jnp.where(in_window, s[at[j, g]][:, :n_loc], NEG_INF) + bias[j], s[at[j, g]][:, n_loc:]], axis=1))
    mx = each(lambda j, g: jnp.maximum(jnp.max(s[at[j, g]], axis=1, keepdims=True), snk[at[j, g]]))
    p = each(lambda j, g: jnp.exp(s[at[j, g]] - mx[at[j, g]]))
    den = each(lambda j, g: jnp.sum(p[at[j, g]], axis=1, keepdims=True) + jnp.exp(snk[at[j, g]] - mx[at[j, g]]))
    inv = each(lambda j, g: 1.0 / den[at[j, g]])
    o = each(lambda j, g: _dot(p[at[j, g]].astype(BF16), vcat[at[j, g]]) * inv[at[j, g]])
    for j in range(qblocks):
        outs = [o[at[j, g]][hq * blk:(hq + 1) * blk] for g in range(N_KV_HEADS) for hq in range(Q_PER_KV)]
        blocks = [jnp.where(low_half, outs[2 * m], outs[2 * m + 1]) for m in range(N_Q_HEADS // 2)]
        o_ref[j * blk:(j + 1) * blk, :] = jnp.concatenate(blocks, axis=1).astype(BF16)


def _attention(q, kd, vd, kcd, vcd, sink, *, bsz, seq_len, n_ctx, qblocks):
    blk = WINDOW
    nblk = seq_len // blk
    nstep = nblk // qblocks
    aw = q.shape[1]
    kw = kd.shape[1]

    def qmap(b, n):
        return (b * nstep + n, 0)

    def pmap(b, n):
        return (b * nblk + jnp.maximum(n * qblocks - 1, 0), 0)

    def nmap(b, n):
        return (b * nblk + jnp.minimum((n + 1) * qblocks, nblk - 1), 0)

    edge = lambda f: pl.BlockSpec((blk, kw), f)
    own = pl.BlockSpec((qblocks * blk, kw), qmap)
    return pl.pallas_call(
        functools.partial(_attn_kernel, nblk=nblk, blk=blk, qblocks=qblocks),
        grid=(bsz, nstep),
        in_specs=[
            pl.BlockSpec(memory_space=pltpu.SMEM),
            pl.BlockSpec((qblocks * blk, aw), qmap),
            edge(pmap), own, edge(nmap),
            edge(pmap), own, edge(nmap),
            pl.BlockSpec((n_ctx, kw), lambda b, n: (b, 0)),
            pl.BlockSpec((n_ctx, kw), lambda b, n: (b, 0)),
        ],
        out_specs=pl.BlockSpec((qblocks * blk, aw), qmap),
        out_shape=jax.ShapeDtypeStruct(q.shape, BF16),
        compiler_params=_cparams(("arbitrary", "arbitrary")),
        name="attn",
    )(sink, q, kd, kd, kd, vd, vd, vd, kcd, vcd)


def _gelu_tanh(x):
    return 0.5 * x * (1.0 + jnp.tanh(math.sqrt(2.0 / math.pi) * (x + 0.044715 * (x * x * x))))


def _flatten_slots(v, max_value):
    tm, nk = v.shape
    per_row = LANES // nk
    log2 = lambda x: int(x).bit_length() - 1
    assert nk == 1 << log2(nk) and tm % per_row == 0
    spread = ((lax.broadcasted_iota(jnp.int32, (nk, LANES), 1) & (nk - 1))
              == lax.broadcasted_iota(jnp.int32, (nk, LANES), 0)).astype(BF16)
    t_id = lax.broadcasted_iota(jnp.int32, (tm, LANES), 0)
    c_id = lax.broadcasted_iota(jnp.int32, (tm, LANES), 1)
    own = (c_id >> log2(nk)) == (t_id & (per_row - 1))
    group = ((lax.broadcasted_iota(jnp.int32, (tm // per_row, tm), 1) >> log2(per_row))
             == lax.broadcasted_iota(jnp.int32, (tm // per_row, tm), 0)).astype(BF16)
    digit_bits = 7
    out = jnp.zeros((tm // per_row, LANES), jnp.int32)
    for shift in range(0, max(int(max_value).bit_length(), 1), digit_bits):
        digit = ((v >> shift) & ((1 << digit_bits) - 1)).astype(F32).astype(BF16)
        wide = jnp.dot(digit, spread, preferred_element_type=F32)
        wide = jnp.where(own, wide, 0.0).astype(BF16)
        out = out + (jnp.dot(group, wide, preferred_element_type=F32).astype(jnp.int32) << shift)
    return out


def _merge_kernel(yf_ref, yb_ref, o_ref, gs_ref, ga_ref, h_ref, g1_ref, sh2_ref, sc2_ref,
                  wglu_ref, bglu_ref, wso_ref, wao_ref, wo_ref, l1g_ref, l1b_ref, wr_ref, br_ref,
                  h1_ref, idx_ref, gate_ref, rank_ref, cnt_ref, cnt_scr, *, alpha, n_exp, sub, n_tokens):
    i = pl.program_id(0)

    @pl.when(i == 0)
    def _():
        cnt_scr[...] = jnp.zeros_like(cnt_scr)

    nlb = yf_ref.shape[0]
    tm = h_ref.shape[0]
    lane = lax.broadcasted_iota(jnp.int32, (sub, n_exp), 1)
    lane_k = lax.broadcasted_iota(jnp.int32, (sub, TOP_K), 1)
    ri = lax.broadcasted_iota(jnp.int32, (sub, sub), 0)
    ci = lax.broadcasted_iota(jnp.int32, (sub, sub), 1)
    tri = jnp.where(ci < ri, 1.0, 0.0).astype(BF16)
    slots = sub * TOP_K // LANES
    cnt = cnt_scr[...]
    parts = range(tm // sub)
    rs = [slice(p * sub, (p + 1) * sub) for p in parts]
    each = lambda f: [f(p) for p in parts]
    y = each(lambda p: jnp.concatenate([yf_ref[lb, rs[p], :] + yb_ref[lb, rs[p], :] for lb in range(nlb)], axis=1))
    z = each(lambda p: _gelu_tanh(y[p]))
    zg = each(lambda p: _dot(z[p].astype(BF16), wglu_ref[...]) + bglu_ref[...])
    z = each(lambda p: z[p] * _sigmoid(zg[p]))
    ms = each(lambda p: _dot(z[p].astype(BF16), wso_ref[...]))
    ma = each(lambda p: _dot(o_ref[rs[p], :], wao_ref[...]))
    m = each(lambda p: gs_ref[rs[p], :].astype(F32) * ms[p] + ga_ref[rs[p], :].astype(F32) * ma[p])
    mix = each(lambda p: _dot(m[p].astype(BF16), wo_ref[...]))
    h1 = each(lambda p: _layer_norm(alpha * h_ref[rs[p], :] + g1_ref[...] * mix[p], l1g_ref[...], l1b_ref[...]))
    for p in parts:
        h1_ref[rs[p], :] = h1[p]
    xm = each(lambda p: h1[p] * (1.0 + sc2_ref[...]) + sh2_ref[...])
    work = each(lambda p: _dot_3pass(xm[p], wr_ref[...]) + br_ref[...])
    vals, sels = [], []
    for _ in range(TOP_K):
        mx = each(lambda p: jnp.max(work[p], axis=1, keepdims=True))
        sel = each(lambda p: jnp.min(jnp.where(work[p] == mx[p], lane, n_exp), axis=1, keepdims=True))
        work = each(lambda p: jnp.where(lane == sel[p], -jnp.inf, work[p]))
        vals.append(mx)
        sels.append(sel)
    exps = [each(lambda p: jnp.exp(vals[k][p] - vals[0][p])) for k in range(TOP_K)]
    den = each(lambda p: exps[0][p] + exps[1][p] + exps[2][p] + exps[3][p])
    onehot = each(lambda p: sum((lane == sels[k][p]).astype(F32) for k in range(TOP_K)))
    prefix = each(lambda p: _dot(tri, onehot[p].astype(BF16)))
    for p in parts:
        rank = prefix[p] + cnt
        idx_o = jnp.zeros((sub, TOP_K), jnp.int32)
        gate_o = jnp.zeros((sub, TOP_K), F32)
        rank_o = jnp.zeros((sub, TOP_K), jnp.int32)
        for k in range(TOP_K):
            rk = jnp.sum(jnp.where(lane == sels[k][p], rank, 0.0), axis=1, keepdims=True).astype(jnp.int32)
            idx_o = jnp.where(lane_k == k, sels[k][p], idx_o)
            gate_o = jnp.where(lane_k == k, exps[k][p] / den[p], gate_o)
            rank_o = jnp.where(lane_k == k, rk, rank_o)
        gate_ref[rs[p], :] = gate_o
        idx_ref[p * slots:(p + 1) * slots, :] = _flatten_slots(idx_o, n_exp - 1)
        rank_ref[p * slots:(p + 1) * slots, :] = _flatten_slots(rank_o, n_tokens - 1)
        cnt = cnt + jnp.sum(onehot[p], axis=0, keepdims=True)
    cnt_scr[...] = cnt
    cnt_ref[...] = cnt


def _merge(yf, yb, o_att, sgs, sga, h_in, mod3, w_glu, b_glu, w_so, w_ao, w_o, l1g, l1b,
           w_r, b_r, *, rows_per_batch, tm, sub, alpha):
    t, d = h_in.shape
    nlb = yf.shape[0]
    sw = nlb * LANES
    n_exp = w_r.shape[1]
    tpb = rows_per_batch // tm
    row = lambda i: (i, 0)
    const = lambda i: (0, 0)

    def modspec(chunk):
        return pl.BlockSpec((None, 1, d), lambda i: (i // tpb, 0, chunk))

    return pl.pallas_call(
        functools.partial(_merge_kernel, alpha=alpha, n_exp=n_exp, sub=sub, n_tokens=t),
        grid=(t // tm,),
        in_specs=[
            pl.BlockSpec((nlb, tm, LANES), lambda i: (0, i, 0)),
            pl.BlockSpec((nlb, tm, LANES), lambda i: (0, i, 0)),
            pl.BlockSpec((tm, o_att.shape[1]), row),
            pl.BlockSpec((tm, d), row),
            pl.BlockSpec((tm, d), row),
            pl.BlockSpec((tm, d), row),
            modspec(2), modspec(3), modspec(4),
            pl.BlockSpec(w_glu.shape, const),
            pl.BlockSpec((1, sw), const),
            pl.BlockSpec(w_so.shape, const),
            pl.BlockSpec(w_ao.shape, const),
            pl.BlockSpec(w_o.shape, const),
            pl.BlockSpec((1, d), const),
            pl.BlockSpec((1, d), const),
            pl.BlockSpec(w_r.shape, const),
            pl.BlockSpec((1, n_exp), const),
        ],
        out_specs=[
            pl.BlockSpec((tm, d), row),
            pl.BlockSpec((tm * TOP_K // LANES, LANES), row),
            pl.BlockSpec((tm, TOP_K), row),
            pl.BlockSpec((tm * TOP_K // LANES, LANES), row),
            pl.BlockSpec((1, n_exp), const),
        ],
        out_shape=[
            jax.ShapeDtypeStruct((t, d), F32),
            jax.ShapeDtypeStruct((t * TOP_K // LANES, LANES), jnp.int32),
            jax.ShapeDtypeStruct((t, TOP_K), F32),
            jax.ShapeDtypeStruct((t * TOP_K // LANES, LANES), jnp.int32),
            jax.ShapeDtypeStruct((1, n_exp), F32),
        ],
        scratch_shapes=[pltpu.VMEM((1, n_exp), F32)],
        compiler_params=_cparams(("arbitrary",)),
        name="merge",
    )(yf, yb, o_att, sgs, sga, h_in, mod3, mod3, mod3, w_glu, b_glu, w_so, w_ao, w_o,
      l1g, l1b, w_r, b_r)


def _row_tile(ref, r):
    return ref.at[pl.ds(pl.multiple_of(r * ROW_TILES, ROW_TILES), ROW_TILES)]


def _row_tile_at(ref, first_line):
    return ref.at[pl.ds(pl.multiple_of(first_line, ROW_TILES), ROW_TILES)]


def _store_row_tiles(ref, val):
    rows = val.shape[0]
    for l in range(ROW_TILES):
        ref[pl.ds(l, rows, stride=ROW_TILES), :] = val[:, l * LANES:(l + 1) * LANES]


def _load_row_tiles(ref, row0, rows):
    return jnp.concatenate(
        [ref[pl.ds(row0 * ROW_TILES + l, rows, stride=ROW_TILES), :] for l in range(ROW_TILES)], axis=1)


def _row_copy_out(src, dst, pos_ref, sem, t, k):
    return pltpu.make_async_copy(_row_tile(src, t), _row_tile_at(dst, pos_ref[t * TOP_K + k]), sem)


def _dispatch_kernel(pos_ref, prev_pos_ref, h1_ref, sh2_ref, sc2_ref, xrows_ref, xm_scr, sems):
    i = pl.program_id(0)
    n = pl.num_programs(0)
    tm = h1_ref.shape[0]
    slot = i % 2
    xm = h1_ref[...] * (1.0 + sc2_ref[...]) + sh2_ref[...]
    _store_row_tiles(xm_scr.at[slot], xm)

    def copies(slot_, idx_ref, op):
        def body(t, carry):
            for k in range(TOP_K):
                op(_row_copy_out(xm_scr.at[slot_], xrows_ref, idx_ref, sems.at[slot_], t, k), k)
            return carry
        lax.fori_loop(0, tm, body, 0, unroll=DMA_LOOP_UNROLL)

    start = lambda cp, k: cp.start(priority=k % N_DMA_PRIORITIES)
    wait = lambda cp, k: cp.wait()
    for s in range(2):
        @pl.when(slot == s)
        def _():
            copies(s, pos_ref, start)

    for s in range(2):
        @pl.when((slot == 1 - s) & (i > 0))
        def _():
            copies(s, prev_pos_ref, wait)

        @pl.when((slot == s) & (i == n - 1))
        def _():
            copies(s, pos_ref, wait)


def _dispatch(pos_flat, h1, mod3, *, rows_per_batch, tm):
    t, d = h1.shape
    tpb = rows_per_batch // tm
    return pl.pallas_call(
        _dispatch_kernel,
        grid=(t // tm,),
        in_specs=[
            pl.BlockSpec((tm * TOP_K,), lambda i: (i,), memory_space=pltpu.SMEM),
            pl.BlockSpec((tm * TOP_K,), lambda i: (jnp.maximum(i - 1, 0),), memory_space=pltpu.SMEM),
            pl.BlockSpec((tm, d), lambda i: (i, 0)),
            pl.BlockSpec((None, 1, d), lambda i: (i // tpb, 0, 3)),
            pl.BlockSpec((None, 1, d), lambda i: (i // tpb, 0, 4)),
        ],
        out_specs=pl.BlockSpec(memory_space=pl.ANY),
        out_shape=jax.ShapeDtypeStruct((t * TOP_K * ROW_TILES, LANES), F32),
        scratch_shapes=[pltpu.VMEM((2, tm * ROW_TILES, LANES), F32), pltpu.SemaphoreType.DMA((2,))],
        compiler_params=_cparams(("arbitrary",)),
        name="dispatch",
    )(pos_flat, pos_flat, h1, mod3, mod3)


def _experts_kernel(tile_ref, exp_ref, lo_ref, hi_ref, nxt_ref, x_ref, wgu_hbm, bgu_ref, wd_hbm, bd_ref, y_ref,
                    wgu_f32, wd_f32, wgu_scr, wd_scr, sems, *, rows, sub_rows):
    w = pl.program_id(0)
    prev = jnp.maximum(w - 1, 0)
    e_new = (w == 0) | (exp_ref[w] != exp_ref[prev])
    t_new = (w == 0) | (tile_ref[w] != tile_ref[prev])
    lo = lo_ref[w]
    hi = hi_ref[w]
    f = wd_scr.shape[0]

    def weight_copies(e):
        return (pltpu.make_async_copy(wgu_hbm.at[e], wgu_f32, sems.at[0]),
                pltpu.make_async_copy(wd_hbm.at[e], wd_f32, sems.at[1]))

    @pl.when(w == 0)
    def _():
        for cp in weight_copies(exp_ref[0]):
            cp.start()

    @pl.when(e_new)
    def _():
        for cp in weight_copies(exp_ref[w]):
            cp.wait()
        wgu_scr[...] = wgu_f32[...].astype(BF16)
        wd_scr[...] = wd_f32[...].astype(BF16)

        @pl.when(nxt_ref[w] >= 0)
        def _():
            for cp in weight_copies(nxt_ref[w]):
                cp.start()

    row0 = tile_ref[w] * rows
    whole = (lo <= row0) & (hi >= row0 + rows)

    @pl.when(t_new & jnp.logical_not(whole))
    def _():
        y_ref[...] = jnp.zeros_like(y_ref)

    def expert_mlp(x):
        gu = _dot(x.astype(BF16), wgu_scr[...]) + bgu_ref[...]
        glu = jnp.minimum(gu[:, :f], SWIGLU_LIMIT)
        lin = jnp.clip(gu[:, f:], -SWIGLU_LIMIT, SWIGLU_LIMIT)
        act = glu * _sigmoid(SWIGLU_ALPHA * glu) * (lin + 1.0)
        return _dot(act.astype(BF16), wd_scr[...]) + bd_ref[...]

    def sub_block(sb):
        s_lo = row0 + sb * sub_rows
        s_hi = s_lo + sub_rows
        y_sub = y_ref.at[pl.ds(sb * sub_rows * ROW_TILES, sub_rows * ROW_TILES)]

        @pl.when((hi > s_lo) & (lo < s_hi))
        def _():
            y = expert_mlp(_load_row_tiles(x_ref, sb * sub_rows, sub_rows))
            sub_whole = (lo <= s_lo) & (hi >= s_hi)

            @pl.when(sub_whole)
            def _():
                _store_row_tiles(y_sub, y)

            @pl.when(jnp.logical_not(sub_whole))
            def _():
                r = s_lo + lax.broadcasted_iota(jnp.int32, (sub_rows, 1), 0)
                _store_row_tiles(y_sub, jnp.where((r >= lo) & (r < hi), y, _load_row_tiles(y_sub, 0, sub_rows)))

    @pl.when(whole)
    def _():
        _store_row_tiles(y_ref, expert_mlp(_load_row_tiles(x_ref, 0, rows)))

    @pl.when(jnp.logical_not(whole))
    def _():
        for sb in range(rows // sub_rows):
            sub_block(sb)


def _experts(work, xrows, w_gu, b_gu, w_d, b_d, *, rows, sub_rows):
    tile_id, exp_id, lo, hi, nxt = work
    n_exp, d, f2 = w_gu.shape
    f = w_d.shape[1]
    grid_spec = pltpu.PrefetchScalarGridSpec(
        num_scalar_prefetch=5,
        grid=(tile_id.shape[0],),
        in_specs=[
            pl.BlockSpec((rows * ROW_TILES, LANES), lambda w, ti, ex, lo, hi, nx: (ti[w], 0)),
            pl.BlockSpec(memory_space=pl.ANY),
            pl.BlockSpec((None, 1, f2), lambda w, ti, ex, lo, hi, nx: (ex[w], 0, 0)),
            pl.BlockSpec(memory_space=pl.ANY),
            pl.BlockSpec((None, 1, d), lambda w, ti, ex, lo, hi, nx: (ex[w], 0, 0)),
        ],
        out_specs=pl.BlockSpec((rows * ROW_TILES, LANES), lambda w, ti, ex, lo, hi, nx: (ti[w], 0)),
        scratch_shapes=[
            pltpu.VMEM((d, f2), F32), pltpu.VMEM((f, d), F32),
            pltpu.VMEM((d, f2), BF16), pltpu.VMEM((f, d), BF16),
            pltpu.SemaphoreType.DMA((2,)),
        ],
    )
    return pl.pallas_call(
        functools.partial(_experts_kernel, rows=rows, sub_rows=sub_rows),
        grid_spec=grid_spec,
        out_shape=jax.ShapeDtypeStruct(xrows.shape, F32),
        compiler_params=_cparams(("arbitrary",)),
        name="experts",
    )(tile_id, exp_id, lo, hi, nxt, xrows, w_gu, b_gu.reshape(n_exp, 1, f2), w_d, b_d.reshape(n_exp, 1, d))


def _row_copy_in(src, dst, pos_ref, sem, t, k, tm):
    return pltpu.make_async_copy(_row_tile_at(src, pos_ref[t * TOP_K + k]), _row_tile(dst, k * tm + t), sem)


def _combine_kernel(pos_ref, next_pos_ref, h1_ref, gate_ref, g2_ref, lg_ref, lb_ref, yrows_ref, o_ref,
                    buf, sems, *, alpha):
    i = pl.program_id(0)
    n = pl.num_programs(0)
    tm = h1_ref.shape[0]
    slot = i % 2

    def copies(slot_, idx_ref, op):
        def body(t, carry):
            for k in range(TOP_K):
                op(_row_copy_in(yrows_ref, buf.at[slot_], idx_ref, sems.at[slot_], t, k, tm), k)
            return carry
        lax.fori_loop(0, tm, body, 0, unroll=DMA_LOOP_UNROLL)

    start = lambda cp, k: cp.start(priority=k % N_DMA_PRIORITIES)
    wait = lambda cp, k: cp.wait()

    @pl.when(i == 0)
    def _():
        copies(0, pos_ref, start)

    for s in range(2):
        @pl.when((slot == 1 - s) & (i + 1 < n))
        def _():
            copies(s, next_pos_ref, start)

    def reduce(s):
        copies(s, pos_ref, wait)
        gates = gate_ref[...]
        ffn = gates[:, 0:1] * _load_row_tiles(buf.at[s], 0, tm)
        for k in range(1, TOP_K):
            ffn = ffn + gates[:, k:k + 1] * _load_row_tiles(buf.at[s], k * tm, tm)
        o_ref[...] = _layer_norm(alpha * h1_ref[...] + g2_ref[...] * ffn, lg_ref[...], lb_ref[...])

    for s in range(2):
        @pl.when(slot == s)
        def _():
            reduce(s)


def _combine(pos_flat, h1, gates, mod3, l2g, l2b, yrows, *, rows_per_batch, tm, alpha):
    t, d = h1.shape
    tpb = rows_per_batch // tm
    n_tiles = t // tm
    return pl.pallas_call(
        functools.partial(_combine_kernel, alpha=alpha),
        grid=(n_tiles,),
        in_specs=[
            pl.BlockSpec((tm * TOP_K,), lambda i: (i,), memory_space=pltpu.SMEM),
            pl.BlockSpec((tm * TOP_K,), lambda i: (jnp.minimum(i + 1, n_tiles - 1),), memory_space=pltpu.SMEM),
            pl.BlockSpec((tm, d), lambda i: (i, 0)),
            pl.BlockSpec((tm, TOP_K), lambda i: (i, 0)),
            pl.BlockSpec((None, 1, d), lambda i: (i // tpb, 0, 5)),
            pl.BlockSpec((1, d), lambda i: (0, 0)),
            pl.BlockSpec((1, d), lambda i: (0, 0)),
            pl.BlockSpec(memory_space=pl.ANY),
        ],
        out_specs=pl.BlockSpec((tm, d), lambda i: (i, 0)),
        out_shape=jax.ShapeDtypeStruct((t, d), F32),
        scratch_shapes=[pltpu.VMEM((2, TOP_K * tm * ROW_TILES, LANES), F32), pltpu.SemaphoreType.DMA((2,))],
        compiler_params=_cparams(("arbitrary",)),
        name="combine",
    )(pos_flat, pos_flat, h1, gates, mod3, l2g, l2b, yrows)


def _plan_kernel(cnt_ref, idx_ref, rank_ref, pos_ref, tile_out, exp_out, lo_out, hi_out, nxt_out,
                 start_s, end_s, next_s, *, rows, n_tiles):
    n_exp = cnt_ref.shape[0]
    n_work = tile_out.shape[0]

    def cumulate(e, acc):
        start_s[e] = acc
        end_s[e] = acc + cnt_ref[e]
        return acc + cnt_ref[e]

    lax.fori_loop(0, n_exp, cumulate, 0)

    def next_nonempty(i, cur):
        e = n_exp - 1 - i
        next_s[e] = cur
        return jnp.where(cnt_ref[e] > 0, e, cur)

    first = lax.fori_loop(0, n_exp, next_nonempty, -1)

    def item(w, carry):
        tile, e = carry
        live = tile < n_tiles
        tl = jnp.minimum(tile, n_tiles - 1)
        tile_lo = tl * rows
        tile_hi = tile_lo + rows
        lo = jnp.maximum(start_s[e], tile_lo)
        hi = jnp.where(live, jnp.minimum(end_s[e], tile_hi), lo)
        tile_out[w] = tl
        exp_out[w] = e
        lo_out[w] = lo
        hi_out[w] = hi
        nxt_out[w] = next_s[e]
        tile_done = live & (end_s[e] >= tile_hi)
        expert_done = live & (end_s[e] <= tile_hi) & (next_s[e] >= 0)
        return jnp.where(tile_done, tile + 1, tile), jnp.where(expert_done, next_s[e], e)

    lax.fori_loop(0, n_work, item, (0, first))

    idx = idx_ref[...]
    base = jnp.zeros(idx.shape, jnp.int32)
    for e in range(n_exp):
        base = jnp.where(idx == e, start_s[e], base)
    pos_ref[...] = (base + rank_ref[...]) * ROW_TILES


def _plan(counts, idx_flat, rank_flat, *, rows):
    n_exp = counts.shape[0]
    n_rows = idx_flat.shape[0] * idx_flat.shape[1]
    n_tiles = n_rows // rows
    n_work = n_tiles + n_exp - 1
    smem = pl.BlockSpec(memory_space=pltpu.SMEM)
    vmem = pl.BlockSpec(memory_space=pltpu.VMEM)
    return pl.pallas_call(
        functools.partial(_plan_kernel, rows=rows, n_tiles=n_tiles),
        in_specs=[smem, vmem, vmem],
        out_specs=[vmem] + [smem] * 5,
        out_shape=[jax.ShapeDtypeStruct(idx_flat.shape, jnp.int32)]
        + [jax.ShapeDtypeStruct((n_work,), jnp.int32)] * 5,
        scratch_shapes=[pltpu.SMEM((n_exp,), jnp.int32)] * 3,
        name="plan",
    )(counts, idx_flat, rank_flat)


def kernel(x, c, ctx, c_ctx, ln_in_g, ln_in_b, w_mod, b_mod, w_in, ssm_lam_re, ssm_lam_im, ssm_log_step, ssm_b_re, ssm_b_im, ssm_c_re, ssm_c_im, ssm_d, w_glu, b_glu, attn_sink, w_ssm_out, w_att_out, w_o, ln1_g, ln1_b, w_router, b_router, w_gate_up, b_gate_up, w_down, b_down, ln2_g, ln2_b):
    bsz, seq_len, d = x.shape
    n_ctx = ctx.shape[1]
    depth = w_mod.shape[0]
    assert depth == 1, "single-layer kernel"
    g_ssm, h_ssm = ssm_d.shape[1:]
    ssm_w = g_ssm * h_ssm
    attn_w = N_Q_HEADS * HEAD_DIM
    kv_w = N_KV_HEADS * HEAD_DIM
    kv2_w = 2 * kv_w
    nlb = ssm_w // LANES
    alpha = (2.0 * depth) ** 0.25
    t = bsz * seq_len
    assert bsz + 1 <= SUBLANES
    assert d == ROW_TILES * LANES and ROW_TILES == SUBLANES, "a token row must be exactly one (8, 128) tile"

    row2 = lambda a: a.reshape(1, -1)

    c_rows = jnp.concatenate([c, c_ctx[None], jnp.zeros((SUBLANES - bsz - 1, d), F32)], axis=0)
    mod = _mod_vectors(c_rows, w_mod[0], row2(b_mod[0]))
    mod3 = mod.reshape(SUBLANES, 1, 6 * d)

    assert w_in.shape[2] == ssm_w + attn_w + 2 * kv_w + 2 * d
    w_all = w_in[0].astype(BF16)

    pos = jnp.arange(seq_len)
    inv = ROPE_BASE ** (-jnp.arange(ROPE_PAIRS, dtype=F32) / ROPE_PAIRS)
    ang = jnp.concatenate([(pos // GRID_W).astype(F32)[:, None] * inv,
                           (pos % GRID_W).astype(F32)[:, None] * inv], axis=-1)
    cos_t = jnp.tile(jnp.cos(ang), (1, 2 * LANES // HEAD_DIM))
    sin_h = jnp.sin(ang)
    sin_t = jnp.tile(jnp.concatenate([-sin_h, sin_h], axis=-1), (1, LANES // HEAD_DIM))

    g_in, b_in = row2(ln_in_g), row2(ln_in_b)
    x2 = x.reshape(t, d)
    ctx2 = ctx.reshape(bsz * n_ctx, d)
    s_c, k_c, v_c = _inproj(ctx2, g_in, b_in, mod3, w_all, None, None, latent=False,
                            rows_per_batch=n_ctx, tm=n_ctx, sub=n_ctx, ctx_mod_row=bsz,
                            ssm_w=ssm_w, attn_w=attn_w, kv2_w=kv2_w)
    s_l, q_l, k_l, v_l, sgs, sga, h_l = _inproj(x2, g_in, b_in, mod3, w_all, cos_t, sin_t, latent=True,
                                           rows_per_batch=seq_len, tm=TILES.inproj_rows, sub=TILES.inproj_sub, ctx_mod_row=None,
                                           ssm_w=ssm_w, attn_w=attn_w, kv2_w=kv2_w)

    wf = _s5_weights(ssm_lam_re[0, 0], ssm_lam_im[0, 0], ssm_log_step[0, 0], ssm_b_re[0, 0], ssm_b_im[0, 0],
                     ssm_c_re[0, 0], ssm_c_im[0, 0], reverse=False)
    wb = _s5_weights(ssm_lam_re[0, 1], ssm_lam_im[0, 1], ssm_log_step[0, 1], ssm_b_re[0, 1], ssm_b_im[0, 1],
                     ssm_c_re[0, 1], ssm_c_im[0, 1], reverse=True)
    d_tile = jnp.tile(ssm_d[0].astype(F32).reshape(nlb, 1, LANES), (1, 1, S5_CHUNK))
    sw2 = wf[1].shape[-1]
    zero_state = jnp.zeros((nlb, bsz, sw2), F32)
    uc4 = s_c.reshape(nlb, bsz, n_ctx, LANES)
    ul4 = s_l.reshape(nlb, bsz, seq_len, LANES)
    _, sf0 = _s5_scan(uc4, zero_state, wf, d_tile, reverse=False, cc=n_ctx // S5_CHUNK, add_skip=False)
    _, sb0 = _s5_scan(uc4, zero_state, wb, d_tile, reverse=True, cc=n_ctx // S5_CHUNK, add_skip=False)
    yf4, _ = _s5_scan(ul4, sf0, wf, d_tile, reverse=False, cc=TILES.s5_chunks, add_skip=True)
    yb4, _ = _s5_scan(ul4, sb0, wb, d_tile, reverse=True, cc=TILES.s5_chunks, add_skip=False)
    yf = yf4.reshape(nlb, t, LANES)
    yb = yb4.reshape(nlb, t, LANES)

    o_att = _attention(q_l, k_l, v_l, k_c, v_c, attn_sink[0].astype(F32), bsz=bsz, seq_len=seq_len, n_ctx=n_ctx,
                       qblocks=TILES.attn_qblocks)

    h1, top_i, gates, rank, counts = _merge(
        yf, yb, o_att, sgs, sga, h_l, mod3, w_glu[0].astype(BF16), row2(b_glu[0]),
        w_ssm_out[0].astype(BF16), w_att_out[0].astype(BF16), w_o[0].astype(BF16), row2(ln1_g[0]), row2(ln1_b[0]),
        w_router[0], row2(b_router[0]), rows_per_batch=seq_len, tm=TILES.merge_rows, sub=TILES.merge_sub, alpha=alpha)

    rows, sub_rows = TILES.expert_rows, TILES.expert_sub
    n_rows = t * TOP_K
    pos, tile_id, exp_id, lo, hi, nxt = _plan(counts[0].astype(jnp.int32), top_i, rank, rows=rows)
    pos_flat = pos.reshape(n_rows)
    xrows = _dispatch(pos_flat, h1, mod3, rows_per_batch=seq_len, tm=TILES.moe_token_rows)
    yrows = _experts((tile_id, exp_id, lo, hi, nxt), xrows, w_gate_up[0], b_gate_up[0], w_down[0], b_down[0],
                     rows=rows, sub_rows=sub_rows)
    out = _combine(pos_flat, h1, gates, mod3, row2(ln2_g[0]), row2(ln2_b[0]), yrows,
                   rows_per_batch=seq_len, tm=TILES.moe_token_rows, alpha=alpha)
    return out.reshape(bsz, seq_len, d)
```
